```python
import jax, jax.numpy as jnp
from jax import lax
import numpy as np

D_MODEL = 1024
BATCH = 8
SEQ = 2048
DEPTH = 1
DEC_BATCH = 128
DEC_SEQ = 4
PAST_LEN = 16384
PAGE_SIZE = 128

N_HEADS_A = 8
N_KV_A = 2
Q_PER_KV = N_HEADS_A // N_KV_A
HEAD_DIM_A = 64
WINDOW = 128
ATTN_WIDTH = N_HEADS_A * HEAD_DIM_A
KV_WIDTH = N_KV_A * HEAD_DIM_A
D_INNER = D_MODEL
HEAD_DIM_M = 64
N_HEADS_M = D_INNER // HEAD_DIM_M
N_GROUPS_M = 2
HEADS_PER_GROUP = N_HEADS_M // N_GROUPS_M
D_STATE = 128
CONV_W = 4
CONV_DIM = D_INNER + 2 * N_GROUPS_M * D_STATE
CHUNK = 128
D_FF = 4 * D_MODEL
EPS = 1e-6
IN_WIDTH = ATTN_WIDTH + 2 * KV_WIDTH + D_INNER + CONV_DIM + N_HEADS_M + 2 * D_MODEL

kernel_name = "hybrid_swa_sink_mamba2_gated_step"


def _rmsnorm(x, w):
    xf = x.astype(jnp.float32)
    y = xf * lax.rsqrt(jnp.mean(xf * xf, axis=-1, keepdims=True) + EPS)
    return (y * w.astype(jnp.float32)).astype(x.dtype)


def _split(u):
    sizes = (ATTN_WIDTH, KV_WIDTH, KV_WIDTH, D_INNER, CONV_DIM, N_HEADS_M, D_MODEL, D_MODEL)
    out = []
    off = 0
    for s in sizes:
        out.append(u[..., off:off + s])
        off += s
    return out


def _sink_attend(q, k, v, mask, sinks):
    scale = HEAD_DIM_A ** -0.5
    s = jnp.einsum("bnqkgd,bnskd->bnkgqs", q, k).astype(jnp.float32) * scale
    s = jnp.where(mask[None, :, None, None], s, -jnp.inf)
    sk = jnp.broadcast_to(sinks.astype(jnp.float32).reshape(N_KV_A, Q_PER_KV, 1, 1), s.shape[:-1] + (1,))
    p = jax.nn.softmax(jnp.concatenate([s, sk], axis=-1), axis=-1)[..., :-1]
    return jnp.einsum("bnkgqs,bnskd->bnqkgd", p.astype(v.dtype), v)


def _swa_prompt(q, k, v, sinks):
    b, L = q.shape[:2]
    nb = L // WINDOW
    qb = q.reshape(b, nb, WINDOW, N_KV_A, Q_PER_KV, HEAD_DIM_A)
    kb = k.reshape(b, nb, WINDOW, N_KV_A, HEAD_DIM_A)
    vb = v.reshape(b, nb, WINDOW, N_KV_A, HEAD_DIM_A)
    padw = ((0, 0), (1, 0), (0, 0), (0, 0), (0, 0))
    kk = jnp.concatenate([jnp.pad(kb, padw)[:, :-1], kb], axis=2)
    vv = jnp.concatenate([jnp.pad(vb, padw)[:, :-1], vb], axis=2)
    qpos = jnp.arange(nb)[:, None] * WINDOW + jnp.arange(WINDOW)[None]
    kpos = (jnp.arange(nb)[:, None] - 1) * WINDOW + jnp.arange(2 * WINDOW)[None]
    d = qpos[:, :, None] - kpos[:, None, :]
    mask = (d >= 0) & (d <= WINDOW) & (kpos[:, None, :] >= 0)
    o = _sink_attend(qb, kk, vv, mask, sinks)
    return o.reshape(b, L, ATTN_WIDTH), k[:, -WINDOW:], v[:, -WINDOW:]


def _swa_sample(q, k, v, k_buf, v_buf, sinks):
    b, L = q.shape[:2]
    kk = jnp.concatenate([k_buf, k], axis=1)
    vv = jnp.concatenate([v_buf, v], axis=1)
    qpos = WINDOW + jnp.arange(L)
    kpos = jnp.arange(WINDOW + L)
    d = qpos[:, None] - kpos[None, :]
    mask = ((d >= 0) & (d <= WINDOW))[None]
    o = _sink_attend(q[:, None], kk[:, None], vv[:, None], mask, sinks)
    return o.reshape(b, L, ATTN_WIDTH), kk[:, -WINDOW:], vv[:, -WINDOW:]


def _causal_conv(xbc, prev, w, bias):
    L = xbc.shape[1]
    xp = jnp.concatenate([prev, xbc], axis=1)
    y = xp[:, 0:L] * w[0]
    for i in range(1, CONV_W):
        y = y + xp[:, i:i + L] * w[i]
    return jax.nn.silu(y + bias), xp[:, -(CONV_W - 1):]


def _segsum_exp(a_cs):
    T = a_cs.shape[-1]
    diff = a_cs[..., :, None] - a_cs[..., None, :]
    mask = jnp.tril(jnp.ones((T, T), dtype=bool))
    return jnp.exp(jnp.where(mask, diff, -jnp.inf))


def _ssd(xh, dt, a, bmat, cmat, h0):
    b, L = xh.shape[:2]
    T = CHUNK if L >= CHUNK else L
    Lp = -(-L // T) * T
    pad = Lp - L
    xf = xh.astype(jnp.float32)
    bf = bmat.astype(jnp.float32)
    cf = cmat.astype(jnp.float32)
    if pad:
        xf = jnp.pad(xf, ((0, 0), (0, pad), (0, 0), (0, 0)))
        dt = jnp.pad(dt, ((0, 0), (0, pad), (0, 0)))
        bf = jnp.pad(bf, ((0, 0), (0, pad), (0, 0), (0, 0)))
        cf = jnp.pad(cf, ((0, 0), (0, pad), (0, 0), (0, 0)))
    nc = Lp // T
    G, HG = N_GROUPS_M, HEADS_PER_GROUP
    x = (xf * dt[..., None]).reshape(b, nc, T, G, HG, HEAD_DIM_M)
    a_cs = jnp.cumsum((dt * a).reshape(b, nc, T, G, HG).transpose(0, 1, 3, 4, 2), axis=-1)
    Bc = bf.reshape(b, nc, T, G, D_STATE)
    Cc = cf.reshape(b, nc, T, G, D_STATE)
    cb = jnp.einsum("bclgn,bcsgn->bcgls", Cc, Bc)
    wmat = cb[:, :, :, None] * _segsum_exp(a_cs)
    y_diag = jnp.einsum("bcghls,bcsghp->bclghp", wmat, x)
    decay = jnp.exp(a_cs[..., -1:] - a_cs)
    st = jnp.einsum("bclgn,bcghl,bclghp->bcghpn", Bc, decay, x)
    chunk_decay = jnp.exp(a_cs[..., -1])

    def step(h, inp):
        s_c, d_c = inp
        return h * d_c[..., None, None] + s_c, h

    h_init = h0.astype(jnp.float32).reshape(b, G, HG, HEAD_DIM_M, D_STATE)
    h_last, h_in = lax.scan(step, h_init, (st.transpose(1, 0, 2, 3, 4, 5), chunk_decay.transpose(1, 0, 2, 3)))
    h_in = h_in.transpose(1, 0, 2, 3, 4, 5)
    y_off = jnp.einsum("bclgn,bcghpn,bcghl->bclghp", Cc, h_in, jnp.exp(a_cs))
    y = (y_diag + y_off).reshape(b, Lp, N_HEADS_M, HEAD_DIM_M)[:, :L]
    return y.astype(xh.dtype), h_last.reshape(b, N_HEADS_M, HEAD_DIM_M, D_STATE).astype(h0.dtype)


def _mamba_branch(z, xbc, dt_raw, conv_prev, ssm_prev, conv_w, conv_b, dt_bias, a_log, d_skip, ssm_norm):
    b, L = z.shape[:2]
    xbc, conv_new = _causal_conv(xbc, conv_prev, conv_w, conv_b)
    xs = xbc[..., :D_INNER]
    bm = xbc[..., D_INNER:D_INNER + N_GROUPS_M * D_STATE].reshape(b, L, N_GROUPS_M, D_STATE)
    cm = xbc[..., D_INNER + N_GROUPS_M * D_STATE:].reshape(b, L, N_GROUPS_M, D_STATE)
    xh = xs.reshape(b, L, N_HEADS_M, HEAD_DIM_M)
    dt = jax.nn.softplus(dt_raw.astype(jnp.float32) + dt_bias.astype(jnp.float32))
    a = -jnp.exp(a_log.astype(jnp.float32))
    y, ssm_new = _ssd(xh, dt, a, bm, cm, ssm_prev)
    y = (y + xh * d_skip[:, None]).reshape(b, L, D_INNER) * jax.nn.silu(z)
    yg = y.astype(jnp.float32).reshape(b, L, N_GROUPS_M, D_INNER // N_GROUPS_M)
    yg = yg * lax.rsqrt(jnp.mean(yg * yg, axis=-1, keepdims=True) + EPS)
    y = (yg.reshape(b, L, D_INNER) * ssm_norm.astype(jnp.float32)).astype(z.dtype)
    return y, conv_new, ssm_new


def _block(x, k_buf, v_buf, conv_prev, ssm_prev, norm1, w_in, sinks, conv_w, conv_b, dt_bias, a_log,
           d_skip, ssm_norm, w_oa, w_ob, w_o, norm2, w_up, w_down):
    b, L, _ = x.shape
    h = _rmsnorm(x, norm1)
    q, k, v, z, xbc, dt_raw, g_a, g_b = _split(h @ w_in)
    q = q.reshape(b, L, N_KV_A, Q_PER_KV, HEAD_DIM_A)
    k = k.reshape(b, L, N_KV_A, HEAD_DIM_A)
    v = v.reshape(b, L, N_KV_A, HEAD_DIM_A)
    if k_buf is None:
        a_out, k_new, v_new = _swa_prompt(q, k, v, sinks)
    else:
        a_out, k_new, v_new = _swa_sample(q, k, v, k_buf, v_buf, sinks)
    m_out, conv_new, ssm_new = _mamba_branch(z, xbc, dt_raw, conv_prev, ssm_prev, conv_w, conv_b,
                                             dt_bias, a_log, d_skip, ssm_norm)
    merged = jax.nn.sigmoid(g_a) * (a_out @ w_oa) + jax.nn.sigmoid(g_b) * (m_out @ w_ob)
    x = x + merged @ w_o
    hm = _rmsnorm(x, norm2)
    x = x + jnp.square(jax.nn.relu(hm @ w_up)) @ w_down
    return x, k_new, v_new, conv_new, ssm_new


def setup_inputs(seed: int = 0) -> dict:
    key = jax.random.key(seed)
    ks = jax.random.split(key, 24)
    f32 = jnp.float32
    nrm = lambda k, shape, s: jax.random.normal(k, shape, f32) * s
    dt0 = jnp.exp(jax.random.uniform(ks[10], (DEPTH, N_HEADS_M), f32, np.log(1e-3), np.log(1e-1)))
    return {
        "x_prompt": nrm(ks[0], (BATCH, SEQ, D_MODEL), 1.0),
        "x_sample": nrm(ks[1], (DEC_BATCH, DEC_SEQ, D_MODEL), 1.0),
        "cache_swa_k": nrm(ks[2], (DEPTH, DEC_BATCH, WINDOW, N_KV_A, HEAD_DIM_A), 1.0),
        "cache_swa_v": nrm(ks[3], (DEPTH, DEC_BATCH, WINDOW, N_KV_A, HEAD_DIM_A), 1.0),
        "state_conv": nrm(ks[4], (DEPTH, DEC_BATCH, CONV_W - 1, CONV_DIM), 1.0),
        "state_ssm": nrm(ks[5], (DEPTH, DEC_BATCH, N_HEADS_M, HEAD_DIM_M, D_STATE), 0.1),
        "norm1": 1.0 + nrm(ks[6], (DEPTH, D_MODEL), 0.05),
        "w_in": nrm(ks[7], (DEPTH, D_MODEL, IN_WIDTH), D_MODEL ** -0.5),
        "sinks": nrm(ks[8], (DEPTH, N_HEADS_A), 0.5),
        "conv_w": nrm(ks[9], (DEPTH, CONV_W, CONV_DIM), CONV_W ** -0.5),
        "conv_b": nrm(ks[11], (DEPTH, CONV_DIM), 0.01),
        "dt_bias": dt0 + jnp.log(-jnp.expm1(-dt0)),
        "a_log": jnp.log(jax.random.uniform(ks[12], (DEPTH, N_HEADS_M), f32, 1.0, 16.0)),
        "d_skip": 1.0 + nrm(ks[13], (DEPTH, N_HEADS_M), 0.1),
        "ssm_norm": 1.0 + nrm(ks[14], (DEPTH, D_INNER), 0.05),
        "w_oa": nrm(ks[15], (DEPTH, ATTN_WIDTH, D_MODEL), ATTN_WIDTH ** -0.5),
        "w_ob": nrm(ks[16], (DEPTH, D_INNER, D_MODEL), D_INNER ** -0.5),
        "w_o": nrm(ks[17], (DEPTH, D_MODEL, D_MODEL), D_MODEL ** -0.5),
        "norm2": 1.0 + nrm(ks[18], (DEPTH, D_MODEL), 0.05),
        "w_up": nrm(ks[19], (DEPTH, D_MODEL, D_FF), D_MODEL ** -0.5),
        "w_down": nrm(ks[20], (DEPTH, D_FF, D_MODEL), D_FF ** -0.5),
        "final_norm": 1.0 + nrm(ks[21], (D_MODEL,), 0.05),
    }


def reference(x_prompt, x_sample, cache_swa_k, cache_swa_v, state_conv, state_ssm, norm1, w_in, sinks,
              conv_w, conv_b, dt_bias, a_log, d_skip, ssm_norm, w_oa, w_ob, w_o, norm2, w_up, w_down,
              final_norm):
    xp, xs = x_prompt, x_sample
    pk, pv, pc, ps = [], [], [], []
    sk, sv, sc, ss = [], [], [], []
    bp = x_prompt.shape[0]
    for l in range(DEPTH):
        w = (norm1[l], w_in[l], sinks[l], conv_w[l], conv_b[l], dt_bias[l], a_log[l], d_skip[l],
             ssm_norm[l], w_oa[l], w_ob[l], w_o[l], norm2[l], w_up[l], w_down[l])
        conv0 = jnp.zeros((bp, CONV_W - 1, CONV_DIM), xp.dtype)
        ssm0 = jnp.zeros((bp, N_HEADS_M, HEAD_DIM_M, D_STATE), state_ssm.dtype)
        xp, k1, v1, c1, s1 = _block(xp, None, None, conv0, ssm0, *w)
        xs, k2, v2, c2, s2 = _block(xs, cache_swa_k[l], cache_swa_v[l], state_conv[l], state_ssm[l], *w)
        pk.append(k1); pv.append(v1); pc.append(c1); ps.append(s1)
        sk.append(k2); sv.append(v2); sc.append(c2); ss.append(s2)
    y_prompt = _rmsnorm(xp, final_norm)
    y_sample = _rmsnorm(xs, final_norm)
    return (y_prompt, y_sample, jnp.stack(pk), jnp.stack(pv), jnp.stack(pc), jnp.stack(ps),
            jnp.stack(sk), jnp.stack(sv), jnp.stack(sc), jnp.stack(ss))
```

```python
import functools

import jax
import jax.numpy as jnp
from jax import lax
from jax.experimental import pallas as pl
from jax.experimental.pallas import tpu as pltpu

F32 = jnp.float32
BF16 = jnp.bfloat16

D_MODEL = 1024
N_HEADS_A = 8
N_KV_A = 2
Q_PER_KV = N_HEADS_A // N_KV_A
HEAD_DIM_A = 64
WINDOW = 128
ATTN_WIDTH = N_HEADS_A * HEAD_DIM_A
KV_WIDTH = N_KV_A * HEAD_DIM_A
D_INNER = 1024
HEAD_DIM_M = 64
N_HEADS_M = D_INNER // HEAD_DIM_M
N_GROUPS_M = 2
GROUP_WIDTH = D_INNER // N_GROUPS_M
D_STATE = 128
CONV_W = 4
CONV_DIM = D_INNER + 2 * N_GROUPS_M * D_STATE
CHUNK = 128
D_FF = 4 * D_MODEL
EPS = 1e-6

LANES = 128
SUBLANES = 8
QKV_WIDTH = ATTN_WIDTH + 2 * KV_WIDTH
NEG_BIG = -1e30
VMEM_LIMIT = 56 * 1024 * 1024
SAMPLE_PAD = SUBLANES
SAMPLE_GROUP = 16


def _mm(a, b):
    return jnp.dot(a.astype(BF16), b.astype(BF16), preferred_element_type=F32)


def _mm_nt(a, b):
    return lax.dot_general(a.astype(BF16), b.astype(BF16), (((1,), (1,)), ((), ())),
                           preferred_element_type=F32)


def _mm_exact(a, b):
    return jnp.dot(a, b, preferred_element_type=F32, precision=lax.Precision.HIGHEST)


def _rms(x, w):
    return x * lax.rsqrt(jnp.mean(x * x, axis=-1, keepdims=True) + EPS) * w


def _sigmoid(x):
    return 1.0 / (1.0 + jnp.exp(-x))


def _silu(x):
    return x * _sigmoid(x)


def _softplus(x):
    return jnp.maximum(x, 0.0) + jnp.log(1.0 + jnp.exp(-jnp.abs(x)))


def _lane_lo(shape):
    return (lax.broadcasted_iota(jnp.int32, shape, len(shape) - 1) % LANES) < HEAD_DIM_A


def _dup_half(x, lo, first):
    xr = pltpu.roll(x, HEAD_DIM_A, axis=1)
    return jnp.where(lo, x, xr) if first else jnp.where(lo, xr, x)


def _stack_heads(q, kv, lo):
    qa = q[:, kv * 2 * LANES: kv * 2 * LANES + LANES]
    qb = q[:, kv * 2 * LANES + LANES: (kv + 1) * 2 * LANES]
    zero = jnp.zeros_like(qa)
    return jnp.concatenate([jnp.where(lo, qa, zero), jnp.where(lo, zero, qa),
                            jnp.where(lo, qb, zero), jnp.where(lo, zero, qb)], axis=0)


def _unstack_heads(o, rows, lo):
    return jnp.concatenate([jnp.where(lo, o[0:rows], o[rows:2 * rows]),
                            jnp.where(lo, o[2 * rows:3 * rows], o[3 * rows:4 * rows])], axis=1)


def _prompt_mixer_kernel(sinks_ref, x_ref, norm1_ref, wqkv_ref, wz_ref, wxbc_ref, wdt_ref, wg_ref,
                         convw_ref, convb_ref, dtb_ref, alog_ref, dskip_ref, ssmn_ref,
                         woa_ref, wob_ref, wo_ref,
                         x1_ref, nk_ref, nv_ref, nconv_ref, nssm_ref,
                         kprev_ref, vprev_ref, conv_buf, ht_ref):
    c = pl.program_id(1)
    last = pl.num_programs(1) - 1
    T = CHUNK

    @pl.when(c == 0)
    def _():
        kprev_ref[...] = jnp.zeros_like(kprev_ref)
        vprev_ref[...] = jnp.zeros_like(vprev_ref)
        conv_buf[0:SUBLANES, :] = jnp.zeros((SUBLANES, CONV_DIM), F32)
        ht_ref[...] = jnp.zeros_like(ht_ref)

    x = x_ref[0]
    hb = _rms(x, norm1_ref[...]).astype(BF16)

    qkv = _mm(hb, wqkv_ref[...])
    q = qkv[:, :ATTN_WIDTH] * (HEAD_DIM_A ** -0.5)
    k = qkv[:, ATTN_WIDTH:ATTN_WIDTH + KV_WIDTH]
    v = qkv[:, ATTN_WIDTH + KV_WIDTH:]
    kk = jnp.concatenate([kprev_ref[...], k], axis=0)
    vv = jnp.concatenate([vprev_ref[...], v], axis=0)
    kprev_ref[...] = k
    vprev_ref[...] = v
    nk_ref[0] = k
    nv_ref[0] = v

    lo = _lane_lo((T, LANES))
    lo2 = _lane_lo((2 * T, LANES))
    r = lax.broadcasted_iota(jnp.int32, (T, 2 * T), 0)
    col = lax.broadcasted_iota(jnp.int32, (T, 2 * T), 1)
    first_col = jnp.where(c == 0, T, 0)
    valid = (col >= r) & (col <= r + WINDOW) & (col >= first_col)
    a_slabs = []
    for kv in range(N_KV_A):
        kd = _dup_half(kk, lo2, kv == 0).astype(BF16)
        vd = _dup_half(vv, lo2, kv == 0).astype(BF16)
        s = _mm_nt(_stack_heads(q, kv, lo), kd)
        es, inv = [], []
        for g in range(Q_PER_KV):
            sk = sinks_ref[kv * Q_PER_KV + g]
            sg = jnp.where(valid, s[g * T:(g + 1) * T], NEG_BIG)
            m = jnp.maximum(jnp.max(sg, axis=-1, keepdims=True), sk)
            e = jnp.exp(sg - m)
            den = jnp.sum(e, axis=-1, keepdims=True) + jnp.exp(sk - m)
            es.append(e.astype(BF16))
            inv.append(1.0 / den)
        o = _mm(jnp.concatenate(es, axis=0), vd) * jnp.concatenate(inv, axis=0)
        a_slabs.append(_unstack_heads(o, T, lo))
    a_out = jnp.concatenate(a_slabs, axis=1)

    z = _mm(hb, wz_ref[...])
    xbc_raw = _mm(hb, wxbc_ref[...])
    dt_raw = _mm(hb, wdt_ref[...])
    conv_buf[SUBLANES:SUBLANES + T, :] = xbc_raw
    yc = xbc_raw * convw_ref[CONV_W - 1:CONV_W, :]
    for i in range(CONV_W - 1):
        off = SUBLANES - (CONV_W - 1) + i
        yc = yc + conv_buf[off:off + T, :] * convw_ref[i:i + 1, :]
    tail = conv_buf[T:T + SUBLANES, :]
    conv_buf[0:SUBLANES, :] = tail
    nconv_ref[0] = tail[SUBLANES - (CONV_W - 1):, :]
    xbc = _silu(yc + convb_ref[...])
    xs = xbc[:, :D_INNER]
    bm = xbc[:, D_INNER:D_INNER + N_GROUPS_M * D_STATE]
    cm = xbc[:, D_INNER + N_GROUPS_M * D_STATE:]

    dt = _softplus(dt_raw + dtb_ref[...])
    a_neg = -jnp.exp(alog_ref[...])
    rr = lax.broadcasted_iota(jnp.int32, (T, T), 0)
    cc = lax.broadcasted_iota(jnp.int32, (T, T), 1)
    tril = cc <= rr
    a_cs = _mm_exact(tril.astype(F32), dt * a_neg)
    a_cs_t = a_cs.T

    cbs, bts = [], []
    for g in range(N_GROUPS_M):
        cg = cm[:, g * D_STATE:(g + 1) * D_STATE]
        bg = bm[:, g * D_STATE:(g + 1) * D_STATE]
        cbs.append(_mm_nt(cg, bg))
        bts.append(bg.T.astype(BF16))

    ys = []
    for j in range(N_HEADS_M // 2):
        g = (2 * j) // (N_HEADS_M // N_GROUPS_M)
        sl = slice(j * LANES, (j + 1) * LANES)
        ws, colbs, dtbs = [], [], []
        for h in (2 * j, 2 * j + 1):
            colb = jnp.broadcast_to(a_cs[:, h:h + 1], (T, T))
            rowb = jnp.broadcast_to(a_cs_t[h:h + 1, :], (T, T))
            seg = jnp.where(tril, jnp.exp(colb - rowb), 0.0)
            ws.append((cbs[g] * seg).astype(BF16))
            colbs.append(colb)
            dtbs.append(jnp.broadcast_to(dt[:, h:h + 1], (T, LANES)))
        dt_e = jnp.where(lo, dtbs[0], dtbs[1])
        acs_e = jnp.where(lo, colbs[0], colbs[1])
        xs_j = xs[:, sl]
        xdt = xs_j * dt_e
        zero = jnp.zeros_like(xdt)
        rhs = jnp.concatenate([jnp.where(lo, xdt, zero), jnp.where(lo, zero, xdt)], axis=0)
        y_diag = _mm(jnp.concatenate(ws, axis=1), rhs)
        h_prev = ht_ref[:, sl]
        cg = cm[:, g * D_STATE:(g + 1) * D_STATE]
        y_off = _mm(cg, h_prev) * jnp.exp(acs_e)
        alast = acs_e[T - 1:T, :]
        st = _mm(bts[g], xdt * jnp.exp(alast - acs_e))
        ht_ref[:, sl] = h_prev * jnp.exp(alast) + st
        y = y_diag + y_off + xs_j * dskip_ref[:, sl]
        ys.append(y * _silu(z[:, sl]))

    m_slabs = []
    per_group = (N_HEADS_M // 2) // N_GROUPS_M
    for g in range(N_GROUPS_M):
        grp = ys[g * per_group:(g + 1) * per_group]
        ssq = grp[0] * grp[0]
        for y in grp[1:]:
            ssq = ssq + y * y
        scale = lax.rsqrt(jnp.sum(ssq, axis=-1, keepdims=True) * (1.0 / GROUP_WIDTH) + EPS)
        m_slabs.extend([y * scale for y in grp])
    m_out = jnp.concatenate(m_slabs, axis=1) * ssmn_ref[...]

    @pl.when(c == last)
    def _():
        nssm_ref[0] = ht_ref[...].T

    gates = _mm(hb, wg_ref[...])
    merged = (_sigmoid(gates[:, :D_MODEL]) * _mm(a_out, woa_ref[...])
              + _sigmoid(gates[:, D_MODEL:]) * _mm(m_out, wob_ref[...]))
    x1_ref[0] = x + _mm(merged, wo_ref[...])


def _const_spec(shape):
    return pl.BlockSpec(shape, lambda *_: (0,) * len(shape))


def _prompt_mixer(x, sinks, norm1, wqkv, wz, wxbc, wdt, wg, convw, convb, dtb, alog, dskip, ssmn,
                  woa, wob, wo):
    nb, seq, _ = x.shape
    assert seq % CHUNK == 0
    nc = seq // CHUNK
    consts = (norm1, wqkv, wz, wxbc, wdt, wg, convw, convb, dtb, alog, dskip, ssmn, woa, wob, wo)
    in_specs = ([pl.BlockSpec(memory_space=pltpu.SMEM),
                 pl.BlockSpec((1, CHUNK, D_MODEL), lambda b, c: (b, c, 0))]
                + [_const_spec(a.shape) for a in consts])
    out_shape = (jax.ShapeDtypeStruct((nb, seq, D_MODEL), F32),
                 jax.ShapeDtypeStruct((nb, WINDOW, KV_WIDTH), F32),
                 jax.ShapeDtypeStruct((nb, WINDOW, KV_WIDTH), F32),
                 jax.ShapeDtypeStruct((nb, CONV_W - 1, CONV_DIM), F32),
                 jax.ShapeDtypeStruct((nb, D_INNER, D_STATE), F32))
    out_specs = (pl.BlockSpec((1, CHUNK, D_MODEL), lambda b, c: (b, c, 0)),
                 pl.BlockSpec((1, WINDOW, KV_WIDTH), lambda b, c: (b, 0, 0)),
                 pl.BlockSpec((1, WINDOW, KV_WIDTH), lambda b, c: (b, 0, 0)),
                 pl.BlockSpec((1, CONV_W - 1, CONV_DIM), lambda b, c: (b, 0, 0)),
                 pl.BlockSpec((1, D_INNER, D_STATE), lambda b, c: (b, 0, 0)))
    scratch = [pltpu.VMEM((CHUNK, KV_WIDTH), F32), pltpu.VMEM((CHUNK, KV_WIDTH), F32),
               pltpu.VMEM((CHUNK + SUBLANES, CONV_DIM), F32), pltpu.VMEM((D_STATE, D_INNER), F32)]
    return pl.pallas_call(
        _prompt_mixer_kernel, grid=(nb, nc), in_specs=in_specs, out_specs=out_specs,
        out_shape=out_shape, scratch_shapes=scratch, name="prompt_mixer",
        compiler_params=pltpu.CompilerParams(dimension_semantics=("arbitrary", "arbitrary"),
                                             vmem_limit_bytes=VMEM_LIMIT),
    )(sinks, x, *consts)


MLP_TILE = 512
FF_BLOCK = 1024


def _mlp_kernel(x_ref, norm2_ref, wup_ref, wdown_ref, fnorm_ref, y_ref):
    x = x_ref[...]
    hm = _rms(x, norm2_ref[...]).astype(BF16)
    acc = x
    for j in range(D_FF // FF_BLOCK):
        h = _mm(hm, wup_ref[:, j * FF_BLOCK:(j + 1) * FF_BLOCK])
        h = jnp.square(jnp.maximum(h, 0.0))
        acc = acc + _mm(h, wdown_ref[j * FF_BLOCK:(j + 1) * FF_BLOCK, :])
    y_ref[...] = _rms(acc, fnorm_ref[...])


def _mlp(x, norm2, wup, wdown, fnorm):
    rows = x.shape[0]
    assert rows % MLP_TILE == 0
    consts = (norm2, wup, wdown, fnorm)
    return pl.pallas_call(
        _mlp_kernel, grid=(rows // MLP_TILE,),
        in_specs=[pl.BlockSpec((MLP_TILE, D_MODEL), lambda i: (i, 0))] + [_const_spec(a.shape) for a in consts],
        out_specs=pl.BlockSpec((MLP_TILE, D_MODEL), lambda i: (i, 0)),
        out_shape=jax.ShapeDtypeStruct((rows, D_MODEL), F32), name="mlp",
        compiler_params=pltpu.CompilerParams(dimension_semantics=("arbitrary",),
                                             vmem_limit_bytes=VMEM_LIMIT),
    )(x, *consts)


ROW_TILE = 256


def _in_proj_kernel(x_ref, norm1_ref, wqkv_ref, wz_ref, wxbc_ref, wdt_ref, wg_ref,
                    qkv_ref, z_ref, xbc_ref, dt_ref, g_ref):
    hb = _rms(x_ref[...], norm1_ref[...]).astype(BF16)
    qkv_ref[...] = _mm(hb, wqkv_ref[...])
    z_ref[...] = _mm(hb, wz_ref[...])
    xbc_ref[...] = _mm(hb, wxbc_ref[...])
    dt_ref[...] = _mm(hb, wdt_ref[...])
    g_ref[...] = _mm(hb, wg_ref[...])


def _in_proj(x, norm1, wqkv, wz, wxbc, wdt, wg):
    rows = x.shape[0]
    assert rows % ROW_TILE == 0
    consts = (norm1, wqkv, wz, wxbc, wdt, wg)
    widths = (QKV_WIDTH, D_INNER, CONV_DIM, LANES, 2 * D_MODEL)
    return pl.pallas_call(
        _in_proj_kernel, grid=(rows // ROW_TILE,),
        in_specs=[pl.BlockSpec((ROW_TILE, D_MODEL), lambda i: (i, 0))] + [_const_spec(a.shape) for a in consts],
        out_specs=tuple(pl.BlockSpec((ROW_TILE, w), lambda i: (i, 0)) for w in widths),
        out_shape=tuple(jax.ShapeDtypeStruct((rows, w), F32) for w in widths), name="sample_in_proj",
        compiler_params=pltpu.CompilerParams(dimension_semantics=("arbitrary",),
                                             vmem_limit_bytes=VMEM_LIMIT),
    )(x, *consts)


def _roll_rows(x, shift):
    return pltpu.roll(x, shift % x.shape[0], axis=0)


def _sample_mixer_kernel(sinks_ref, qkv_ref, z_ref, xbc_ref, dtraw_ref, cprev_ref, kc_ref, vc_ref, st_ref,
                         convw_ref, convb_ref, dtb_ref, alog_ref, dskip_ref, ssmn_ref,
                         aout_ref, mout_ref, nk_ref, nv_ref, nconv_ref, nst_ref,
                         q_s, kn_s, vn_s, c_s, acs_s, xdt_s, b_s, yoff_s):
    R = SAMPLE_GROUP * SAMPLE_PAD
    L = 4
    rm = lax.broadcasted_iota(jnp.int32, (R, 1), 0) % SAMPLE_PAD

    xr = xbc_raw = xbc_ref[...]
    with_prev = jnp.where(rm >= SAMPLE_PAD - (CONV_W - 1), cprev_ref[...], xr)
    yc = xr * convw_ref[CONV_W - 1:CONV_W, :]
    for kshift in range(1, CONV_W):
        sh = jnp.where(rm >= kshift, _roll_rows(xr, kshift), _roll_rows(with_prev, kshift - SAMPLE_PAD))
        yc = yc + sh * convw_ref[CONV_W - 1 - kshift:CONV_W - kshift, :]
    nconv_ref[...] = _roll_rows(xbc_raw, -1)
    xbc = _silu(yc + convb_ref[...])
    xs = xbc[:, :D_INNER]
    bm = xbc[:, D_INNER:D_INNER + N_GROUPS_M * D_STATE]
    cm = xbc[:, D_INNER + N_GROUPS_M * D_STATE:]

    dt = _softplus(dtraw_ref[...] + dtb_ref[...])
    d_a = dt * (-jnp.exp(alog_ref[...]))
    a_cs = d_a
    suf = jnp.zeros_like(d_a)
    for kshift in range(1, L):
        a_cs = a_cs + jnp.where(rm >= kshift, _roll_rows(d_a, kshift), 0.0)
        suf = suf + jnp.where(rm <= L - 1 - kshift, _roll_rows(d_a, -kshift), 0.0)
    acs_s[...] = a_cs

    expand = (lax.broadcasted_iota(jnp.int32, (LANES, D_INNER), 1) // HEAD_DIM_M
              == lax.broadcasted_iota(jnp.int32, (LANES, D_INNER), 0)).astype(F32)
    dt_e = _mm_exact(dt, expand)
    acs_e = _mm_exact(a_cs, expand)
    suf_e = _mm_exact(suf, expand)
    xdt = xs * dt_e
    real = rm < L
    xdt_s[...] = jnp.where(real, xdt * jnp.exp(suf_e), 0.0).T.astype(BF16)
    b_s[...] = bm
    c_s[...] = cm

    y = xs * dskip_ref[...]
    for kshift in range(L):
        bk = bm if kshift == 0 else _roll_rows(bm, kshift)
        cb = cm * bk
        cb_e = jnp.concatenate(
            [jnp.broadcast_to(jnp.sum(cb[:, g * D_STATE:(g + 1) * D_STATE], axis=-1, keepdims=True),
                              (R, GROUP_WIDTH)) for g in range(N_GROUPS_M)], axis=1)
        if kshift == 0:
            y = y + cb_e * xdt
        else:
            seg = jnp.exp(acs_e - _roll_rows(acs_e, kshift))
            y = y + jnp.where(rm >= kshift, cb_e * seg * _roll_rows(xdt, kshift), 0.0)

    qkv = qkv_ref[...]
    q_s[...] = qkv[:, :ATTN_WIDTH] * (HEAD_DIM_A ** -0.5)
    kn_s[...] = qkv[:, ATTN_WIDTH:ATTN_WIDTH + KV_WIDTH]
    vn_s[...] = qkv[:, ATTN_WIDTH + KV_WIDTH:]

    lo8 = _lane_lo((SAMPLE_PAD, LANES))
    lo_w = _lane_lo((WINDOW, LANES))
    r8 = lax.broadcasted_iota(jnp.int32, (SAMPLE_PAD, LANES), 0)
    tok = lax.broadcasted_iota(jnp.int32, (SAMPLE_PAD, WINDOW), 0)
    ccol = lax.broadcasted_iota(jnp.int32, (SAMPLE_PAD, WINDOW), 1)
    valid_c = jnp.concatenate([ccol >= tok] * Q_PER_KV, axis=0)
    ncol = lax.broadcasted_iota(jnp.int32, (SAMPLE_PAD, SAMPLE_PAD), 1)
    ntok = lax.broadcasted_iota(jnp.int32, (SAMPLE_PAD, SAMPLE_PAD), 0)
    valid_n = jnp.concatenate([(ncol <= ntok) & (ncol < L)] * Q_PER_KV, axis=0)
    row_r = lax.broadcasted_iota(jnp.int32, (R, D_STATE), 0)

    def per_seq(i, carry):
        base = pl.multiple_of(i * SAMPLE_PAD, SAMPLE_PAD)
        rows = pl.ds(base, SAMPLE_PAD)
        q = q_s[rows, :]
        kn = kn_s[rows, :]
        vn = vn_s[rows, :]
        kc = kc_ref[i]
        vc = vc_ref[i]
        slabs = []
        for kv in range(N_KV_A):
            lhs = _stack_heads(q, kv, lo8)
            s_c = jnp.where(valid_c, _mm_nt(lhs, _dup_half(kc, lo_w, kv == 0)), NEG_BIG)
            s_n = jnp.where(valid_n, _mm_nt(lhs, _dup_half(kn, lo8, kv == 0)), NEG_BIG)
            sk = jnp.concatenate([jnp.full((SAMPLE_PAD, 1), sinks_ref[kv * Q_PER_KV + g], F32)
                                  for g in range(Q_PER_KV)], axis=0)
            m = jnp.maximum(jnp.maximum(jnp.max(s_c, axis=-1, keepdims=True),
                                        jnp.max(s_n, axis=-1, keepdims=True)), sk)
            e_c = jnp.exp(s_c - m)
            e_n = jnp.exp(s_n - m)
            den = (jnp.sum(e_c, axis=-1, keepdims=True) + jnp.sum(e_n, axis=-1, keepdims=True)
                   + jnp.exp(sk - m))
            o = (_mm(e_c, _dup_half(vc, lo_w, kv == 0)) + _mm(e_n, _dup_half(vn, lo8, kv == 0))) / den
            slabs.append(_unstack_heads(o, SAMPLE_PAD, lo8))
        aout_ref[rows, :] = jnp.concatenate(slabs, axis=1)

        for cache, new, out_ref in ((kc, kn, nk_ref), (vc, vn, nv_ref)):
            shifted = _roll_rows(cache, -L)
            tail8 = jnp.where(r8 >= SAMPLE_PAD - L, _roll_rows(new, L), shifted[WINDOW - SAMPLE_PAD:])
            out_ref[i] = jnp.concatenate([shifted[:WINDOW - SAMPLE_PAD], tail8], axis=0)

        state = st_ref[i]
        state_b = state.astype(BF16)
        a_tot = acs_s[rows, :][L - 1:L, :]
        in_seq = (row_r >= base) & (row_r < base + SAMPLE_PAD)
        yo = []
        for g in range(N_GROUPS_M):
            gs = slice(g * GROUP_WIDTH, (g + 1) * GROUP_WIDTH)
            cg = c_s[rows, g * D_STATE:(g + 1) * D_STATE]
            yo.append(_mm_nt(cg, state_b[gs]))
            bsel = jnp.where(in_seq, b_s[:, g * D_STATE:(g + 1) * D_STATE], 0.0)
            upd = _mm(xdt_s[gs, :], bsel)
            for hh in range(N_HEADS_M // N_GROUPS_M):
                h = g * (N_HEADS_M // N_GROUPS_M) + hh
                hs = slice(h * HEAD_DIM_M, (h + 1) * HEAD_DIM_M)
                decay = jnp.exp(jnp.broadcast_to(a_tot[:, h:h + 1], (HEAD_DIM_M, D_STATE)))
                nst_ref[i, hs, :] = state[hs] * decay + upd[hh * HEAD_DIM_M:(hh + 1) * HEAD_DIM_M]
        yoff_s[rows, :] = jnp.concatenate(yo, axis=1)
        return carry

    lax.fori_loop(0, SAMPLE_GROUP, per_seq, 0)

    y = (y + yoff_s[...] * jnp.exp(acs_e)) * _silu(z_ref[...])
    outs = []
    for g in range(N_GROUPS_M):
        yg = y[:, g * GROUP_WIDTH:(g + 1) * GROUP_WIDTH]
        outs.append(yg * lax.rsqrt(jnp.mean(yg * yg, axis=-1, keepdims=True) + EPS))
    mout_ref[...] = jnp.concatenate(outs, axis=1) * ssmn_ref[...]


def _sample_mixer(sinks, qkv, z, xbc, dtraw, cprev, kc, vc, st, convw, convb, dtb, alog, dskip, ssmn):
    nseq = kc.shape[0]
    assert nseq % SAMPLE_GROUP == 0
    R = SAMPLE_GROUP * SAMPLE_PAD
    rows = nseq * SAMPLE_PAD
    consts = (convw, convb, dtb, alog, dskip, ssmn)
    row_spec = lambda w: pl.BlockSpec((R, w), lambda i: (i, 0))
    seq_spec = lambda a, b: pl.BlockSpec((SAMPLE_GROUP, a, b), lambda i: (i, 0, 0))
    in_specs = ([pl.BlockSpec(memory_space=pltpu.SMEM),
                 row_spec(QKV_WIDTH), row_spec(D_INNER), row_spec(CONV_DIM), row_spec(LANES), row_spec(CONV_DIM),
                 seq_spec(WINDOW, KV_WIDTH), seq_spec(WINDOW, KV_WIDTH), seq_spec(D_INNER, D_STATE)]
                + [_const_spec(a.shape) for a in consts])
    out_shape = (jax.ShapeDtypeStruct((rows, ATTN_WIDTH), F32), jax.ShapeDtypeStruct((rows, D_INNER), F32),
                 jax.ShapeDtypeStruct((nseq, WINDOW, KV_WIDTH), F32),
                 jax.ShapeDtypeStruct((nseq, WINDOW, KV_WIDTH), F32),
                 jax.ShapeDtypeStruct((rows, CONV_DIM), F32),
                 jax.ShapeDtypeStruct((nseq, D_INNER, D_STATE), F32))
    out_specs = (row_spec(ATTN_WIDTH), row_spec(D_INNER), seq_spec(WINDOW, KV_WIDTH), seq_spec(WINDOW, KV_WIDTH),
                 row_spec(CONV_DIM), seq_spec(D_INNER, D_STATE))
    scratch = [pltpu.VMEM((R, ATTN_WIDTH), F32), pltpu.VMEM((R, KV_WIDTH), F32), pltpu.VMEM((R, KV_WIDTH), F32),
               pltpu.VMEM((R, N_GROUPS_M * D_STATE), F32), pltpu.VMEM((R, LANES), F32),
               pltpu.VMEM((D_INNER, R), BF16), pltpu.VMEM((R, N_GROUPS_M * D_STATE), F32),
               pltpu.VMEM((R, D_INNER), F32)]
    return pl.pallas_call(
        _sample_mixer_kernel, grid=(nseq // SAMPLE_GROUP,), in_specs=in_specs, out_specs=out_specs,
        out_shape=out_shape, scratch_shapes=scratch, name="sample_mixer",
        compiler_params=pltpu.CompilerParams(dimension_semantics=("arbitrary",),
                                             vmem_limit_bytes=VMEM_LIMIT),
    )(sinks, qkv, z, xbc, dtraw, cprev, kc, vc, st, *consts)


def _out_proj_kernel(x_ref, a_ref, m_ref, g_ref, woa_ref, wob_ref, wo_ref, x1_ref):
    gates = g_ref[...]
    merged = (_sigmoid(gates[:, :D_MODEL]) * _mm(a_ref[...], woa_ref[...])
              + _sigmoid(gates[:, D_MODEL:]) * _mm(m_ref[...], wob_ref[...]))
    x1_ref[...] = x_ref[...] + _mm(merged, wo_ref[...])


def _out_proj(x, a_out, m_out, gates, woa, wob, wo):
    rows = x.shape[0]
    assert rows % ROW_TILE == 0
    consts = (woa, wob, wo)
    widths = (D_MODEL, ATTN_WIDTH, D_INNER, 2 * D_MODEL)
    return pl.pallas_call(
        _out_proj_kernel, grid=(rows // ROW_TILE,),
        in_specs=([pl.BlockSpec((ROW_TILE, w), lambda i: (i, 0)) for w in widths]
                  + [_const_spec(a.shape) for a in consts]),
        out_specs=pl.BlockSpec((ROW_TILE, D_MODEL), lambda i: (i, 0)),
        out_shape=jax.ShapeDtypeStruct((rows, D_MODEL), F32), name="sample_out_proj",
        compiler_params=pltpu.CompilerParams(dimension_semantics=("arbitrary",),
                                             vmem_limit_bytes=VMEM_LIMIT),
    )(x, a_out, m_out, gates, *consts)


def _layer_params(norm1, w_in, sinks, conv_w, conv_b, dt_bias, a_log, d_skip, ssm_norm, w_oa, w_ob, w_o,
                  norm2, w_up, w_down):
    off = [0]

    def take(width):
        piece = w_in[:, off[0]:off[0] + width]
        off[0] += width
        return piece

    wq, wk, wv = take(ATTN_WIDTH), take(KV_WIDTH), take(KV_WIDTH)
    wz, wxbc, wdt = take(D_INNER), take(CONV_DIM), take(N_HEADS_M)
    wga, wgb = take(D_MODEL), take(D_MODEL)
    pad_lanes = lambda a: jnp.pad(a, ((0, 0), (0, LANES - a.shape[1])))
    return dict(
        sinks=sinks.astype(F32), norm1=norm1[None, :],
        wqkv=jnp.concatenate([wq, wk, wv], axis=1).astype(BF16), wz=wz.astype(BF16),
        wxbc=wxbc.astype(BF16), wdt=pad_lanes(wdt).astype(BF16),
        wg=jnp.concatenate([wga, wgb], axis=1).astype(BF16),
        convw=conv_w, convb=conv_b[None, :], dtb=pad_lanes(dt_bias[None, :]), alog=pad_lanes(a_log[None, :]),
        dskip=jnp.repeat(d_skip, HEAD_DIM_M)[None, :], ssmn=ssm_norm[None, :],
        woa=w_oa.astype(BF16), wob=w_ob.astype(BF16), wo=w_o.astype(BF16),
        norm2=norm2[None, :], wup=w_up.astype(BF16), wdown=w_down.astype(BF16))


def kernel(x_prompt, x_sample, cache_swa_k, cache_swa_v, state_conv, state_ssm, norm1, w_in, sinks, conv_w,
           conv_b, dt_bias, a_log, d_skip, ssm_norm, w_oa, w_ob, w_o, norm2, w_up, w_down, final_norm):
    depth = w_in.shape[0]
    assert depth == 1
    nb, seq, _ = x_prompt.shape
    ns, ls, _ = x_sample.shape
    assert ls == 4
    p = _layer_params(norm1[0], w_in[0], sinks[0], conv_w[0], conv_b[0], dt_bias[0], a_log[0], d_skip[0],
                      ssm_norm[0], w_oa[0], w_ob[0], w_o[0], norm2[0], w_up[0], w_down[0])
    fnorm = final_norm[None, :]
    mixer_consts = (p["convw"], p["convb"], p["dtb"], p["alog"], p["dskip"], p["ssmn"])

    x1p, pk, pv, pc, pst = _prompt_mixer(
        x_prompt, p["sinks"], p["norm1"], p["wqkv"], p["wz"], p["wxbc"], p["wdt"], p["wg"],
        *mixer_consts, p["woa"], p["wob"], p["wo"])
    y_prompt = _mlp(x1p.reshape(nb * seq, D_MODEL), p["norm2"], p["wup"], p["wdown"], fnorm)
    y_prompt = y_prompt.reshape(nb, seq, D_MODEL)

    xs_pad = jnp.pad(x_sample, ((0, 0), (0, SAMPLE_PAD - ls), (0, 0))).reshape(ns * SAMPLE_PAD, D_MODEL)
    cprev = jnp.pad(state_conv[0], ((0, 0), (SAMPLE_PAD - (CONV_W - 1), 0), (0, 0)))
    cprev = cprev.reshape(ns * SAMPLE_PAD, CONV_DIM)
    qkv, z, xbc, dtraw, gates = _in_proj(xs_pad, p["norm1"], p["wqkv"], p["wz"], p["wxbc"], p["wdt"], p["wg"])
    a_out, m_out, sk, sv, sc, sst = _sample_mixer(
        p["sinks"], qkv, z, xbc, dtraw, cprev,
        cache_swa_k[0].reshape(ns, WINDOW, KV_WIDTH), cache_swa_v[0].reshape(ns, WINDOW, KV_WIDTH),
        state_ssm[0].reshape(ns, D_INNER, D_STATE), *mixer_consts)
    x1s = _out_proj(xs_pad, a_out, m_out, gates, p["woa"], p["wob"], p["wo"])
    y_sample = _mlp(x1s, p["norm2"], p["wup"], p["wdown"], fnorm)
    y_sample = y_sample.reshape(ns, SAMPLE_PAD, D_MODEL)[:, :ls]
    sc = sc.reshape(ns, SAMPLE_PAD, CONV_DIM)[:, :CONV_W - 1]

    kv_shape = (1, -1, WINDOW, N_KV_A, HEAD_DIM_A)
    ssm_shape = (1, -1, N_HEADS_M, HEAD_DIM_M, D_STATE)
    return (y_prompt, y_sample,
            pk.reshape(kv_shape), pv.reshape(kv_shape), pc[None], pst.reshape(ssm_shape),
            sk.reshape(kv_shape), sv.reshape(kv_shape), sc[None], sst.reshape(ssm_shape))
```

```python
import functools

import jax
import jax.numpy as jnp
from jax import lax
from jax.experimental import pallas as pl
from jax.experimental.pallas import tpu as pltpu

F32 = jnp.float32
BF16 = jnp.bfloat16

D_MODEL = 1024
N_HEADS_A = 8
N_KV_A = 2
Q_PER_KV = N_HEADS_A // N_KV_A
HEAD_DIM_A = 64
WINDOW = 128
ATTN_WIDTH = N_HEADS_A * HEAD_DIM_A
KV_WIDTH = N_KV_A * HEAD_DIM_A
D_INNER = 1024
HEAD_DIM_M = 64
N_HEADS_M = D_INNER // HEAD_DIM_M
N_GROUPS_M = 2
GROUP_WIDTH = D_INNER // N_GROUPS_M
D_STATE = 128
CONV_W = 4
CONV_DIM = D_INNER + 2 * N_GROUPS_M * D_STATE
CHUNK = 128
D_FF = 4 * D_MODEL
EPS = 1e-6

LANES = 128
SUBLANES = 8
QKV_WIDTH = ATTN_WIDTH + 2 * KV_WIDTH
NEG_BIG = -1e30
VMEM_LIMIT = 56 * 1024 * 1024
SAMPLE_PAD = SUBLANES
SAMPLE_GROUP = 16


def _mm(a, b):
    return jnp.dot(a.astype(BF16), b.astype(BF16), preferred_element_type=F32)


def _mm_nt(a, b):
    return lax.dot_general(a.astype(BF16), b.astype(BF16), (((1,), (1,)), ((), ())),
                           preferred_element_type=F32)


def _mm_exact(a, b):
    return jnp.dot(a, b, preferred_element_type=F32, precision=lax.Precision.HIGHEST)


def _rms(x, w):
    return x * lax.rsqrt(jnp.mean(x * x, axis=-1, keepdims=True) + EPS) * w


def _sigmoid(x):
    return 0.5 + 0.5 * jnp.tanh(0.5 * x)


def _silu(x):
    h = 0.5 * x
    return h + h * jnp.tanh(h)


def _softplus(x):
    return jnp.maximum(x, 0.0) + jnp.log(1.0 + jnp.exp(-jnp.abs(x)))


def _lane_lo(shape):
    return (lax.broadcasted_iota(jnp.int32, shape, len(shape) - 1) % LANES) < HEAD_DIM_A


def _dup_half(x, lo, first):
    xr = pltpu.roll(x, HEAD_DIM_A, axis=1)
    return jnp.where(lo, x, xr) if first else jnp.where(lo, xr, x)


def _stack_heads(q, kv, lo):
    qa = q[:, kv * 2 * LANES: kv * 2 * LANES + LANES]
    qb = q[:, kv * 2 * LANES + LANES: (kv + 1) * 2 * LANES]
    zero = jnp.zeros_like(qa)
    return jnp.concatenate([jnp.where(lo, qa, zero), jnp.where(lo, zero, qa),
                            jnp.where(lo, qb, zero), jnp.where(lo, zero, qb)], axis=0)


def _unstack_heads(o, rows, lo):
    return jnp.concatenate([jnp.where(lo, o[0:rows], o[rows:2 * rows]),
                            jnp.where(lo, o[2 * rows:3 * rows], o[3 * rows:4 * rows])], axis=1)


def _prompt_mixer_kernel(sinks_ref, x_ref, norm1_ref, wqkv_ref, wz_ref, wxbc_ref, wdt_ref, wg_ref,
                         convw_ref, convb_ref, dtb_ref, alog_ref, dskip_ref, ssmn_ref,
                         woa_ref, wob_ref, wo_ref,
                         x1_ref, nk_ref, nv_ref, nconv_ref, nssm_ref,
                         kprev_ref, vprev_ref, conv_buf, ht_ref):
    c = pl.program_id(1)
    last = pl.num_programs(1) - 1
    T = CHUNK

    @pl.when(c == 0)
    def _():
        kprev_ref[...] = jnp.zeros_like(kprev_ref)
        vprev_ref[...] = jnp.zeros_like(vprev_ref)
        conv_buf[0:SUBLANES, :] = jnp.zeros((SUBLANES, CONV_DIM), F32)
        ht_ref[...] = jnp.zeros_like(ht_ref)

    x = x_ref[0]
    hb = _rms(x, norm1_ref[...]).astype(BF16)

    qkv = _mm(hb, wqkv_ref[...])
    q = qkv[:, :ATTN_WIDTH] * (HEAD_DIM_A ** -0.5)
    k = qkv[:, ATTN_WIDTH:ATTN_WIDTH + KV_WIDTH]
    v = qkv[:, ATTN_WIDTH + KV_WIDTH:]
    kk = jnp.concatenate([kprev_ref[...], k], axis=0)
    vv = jnp.concatenate([vprev_ref[...], v], axis=0)
    kprev_ref[...] = k
    vprev_ref[...] = v
    nk_ref[0] = k
    nv_ref[0] = v

    lo = _lane_lo((T, LANES))
    lo2 = _lane_lo((2 * T, LANES))
    r = lax.broadcasted_iota(jnp.int32, (T, 2 * T), 0)
    col = lax.broadcasted_iota(jnp.int32, (T, 2 * T), 1)
    first_col = jnp.where(c == 0, T, 0)
    valid = (col >= r) & (col <= r + WINDOW) & (col >= first_col)
    a_slabs = []
    for kv in range(N_KV_A):
        kd = _dup_half(kk, lo2, kv == 0).astype(BF16)
        vd = _dup_half(vv, lo2, kv == 0).astype(BF16)
        s = _mm_nt(_stack_heads(q, kv, lo), kd)
        es, inv = [], []
        for g in range(Q_PER_KV):
            sk = sinks_ref[kv * Q_PER_KV + g]
            sg = jnp.where(valid, s[g * T:(g + 1) * T], NEG_BIG)
            m = jnp.maximum(jnp.max(sg, axis=-1, keepdims=True), sk)
            e = jnp.exp(sg - m)
            den = jnp.sum(e, axis=-1, keepdims=True) + jnp.exp(sk - m)
            es.append(e.astype(BF16))
            inv.append(1.0 / den)
        o = _mm(jnp.concatenate(es, axis=0), vd) * jnp.concatenate(inv, axis=0)
        a_slabs.append(_unstack_heads(o, T, lo))
    a_out = jnp.concatenate(a_slabs, axis=1)

    z = _mm(hb, wz_ref[...])
    xbc_raw = _mm(hb, wxbc_ref[...])
    dt_raw = _mm(hb, wdt_ref[...])
    conv_buf[SUBLANES:SUBLANES + T, :] = xbc_raw
    yc = xbc_raw * convw_ref[CONV_W - 1:CONV_W, :]
    for i in range(CONV_W - 1):
        off = SUBLANES - (CONV_W - 1) + i
        yc = yc + conv_buf[off:off + T, :] * convw_ref[i:i + 1, :]
    tail = conv_buf[T:T + SUBLANES, :]
    conv_buf[0:SUBLANES, :] = tail
    nconv_ref[0] = tail[SUBLANES - (CONV_W - 1):, :]
    xbc = _silu(yc + convb_ref[...])
    xs = xbc[:, :D_INNER]
    bm = xbc[:, D_INNER:D_INNER + N_GROUPS_M * D_STATE]
    cm = xbc[:, D_INNER + N_GROUPS_M * D_STATE:]

    dt = _softplus(dt_raw + dtb_ref[...])
    a_neg = -jnp.exp(alog_ref[...])
    rr = lax.broadcasted_iota(jnp.int32, (T, T), 0)
    cc = lax.broadcasted_iota(jnp.int32, (T, T), 1)
    tril = cc <= rr
    a_cs = _mm_exact(tril.astype(F32), dt * a_neg)
    a_cs_t = a_cs.T

    cbs, bts = [], []
    for g in range(N_GROUPS_M):
        cg = cm[:, g * D_STATE:(g + 1) * D_STATE]
        bg = bm[:, g * D_STATE:(g + 1) * D_STATE]
        cbs.append(_mm_nt(cg, bg))
        bts.append(bg.T.astype(BF16))

    ys = []
    for j in range(N_HEADS_M // 2):
        g = (2 * j) // (N_HEADS_M // N_GROUPS_M)
        sl = slice(j * LANES, (j + 1) * LANES)
        ws, colbs, dtbs = [], [], []
        for h in (2 * j, 2 * j + 1):
            colb = jnp.broadcast_to(a_cs[:, h:h + 1], (T, T))
            rowb = jnp.broadcast_to(a_cs_t[h:h + 1, :], (T, T))
            seg = jnp.where(tril, jnp.exp(colb - rowb), 0.0)
            ws.append((cbs[g] * seg).astype(BF16))
            colbs.append(colb)
            dtbs.append(jnp.broadcast_to(dt[:, h:h + 1], (T, LANES)))
        dt_e = jnp.where(lo, dtbs[0], dtbs[1])
        acs_e = jnp.where(lo, colbs[0], colbs[1])
        xs_j = xs[:, sl]
        xdt = xs_j * dt_e
        zero = jnp.zeros_like(xdt)
        rhs = jnp.concatenate([jnp.where(lo, xdt, zero), jnp.where(lo, zero, xdt)], axis=0)
        y_diag = _mm(jnp.concatenate(ws, axis=1), rhs)
        h_prev = ht_ref[:, sl]
        cg = cm[:, g * D_STATE:(g + 1) * D_STATE]
        y_off = _mm(cg, h_prev) * jnp.exp(acs_e)
        alast = acs_e[T - 1:T, :]
        st = _mm(bts[g], xdt * jnp.exp(alast - acs_e))
        ht_ref[:, sl] = h_prev * jnp.exp(alast) + st
        y = y_diag + y_off + xs_j * dskip_ref[:, sl]
        ys.append(y * _silu(z[:, sl]))

    m_slabs = []
    per_group = (N_HEADS_M // 2) // N_GROUPS_M
    for g in range(N_GROUPS_M):
        grp = ys[g * per_group:(g + 1) * per_group]
        ssq = grp[0] * grp[0]
        for y in grp[1:]:
            ssq = ssq + y * y
        scale = lax.rsqrt(jnp.sum(ssq, axis=-1, keepdims=True) * (1.0 / GROUP_WIDTH) + EPS)
        m_slabs.extend([y * scale for y in grp])
    m_out = jnp.concatenate(m_slabs, axis=1) * ssmn_ref[...]

    @pl.when(c == last)
    def _():
        nssm_ref[0] = ht_ref[...].T

    gates = _mm(hb, wg_ref[...])
    merged = (_sigmoid(gates[:, :D_MODEL]) * _mm(a_out, woa_ref[...])
              + _sigmoid(gates[:, D_MODEL:]) * _mm(m_out, wob_ref[...]))
    x1_ref[0] = x + _mm(merged, wo_ref[...])


def _const_spec(shape):
    return pl.BlockSpec(shape, lambda *_: (0,) * len(shape))


def _prompt_mixer(x, sinks, norm1, wqkv, wz, wxbc, wdt, wg, convw, convb, dtb, alog, dskip, ssmn,
                  woa, wob, wo):
    nb, seq, _ = x.shape
    assert seq % CHUNK == 0
    nc = seq // CHUNK
    consts = (norm1, wqkv, wz, wxbc, wdt, wg, convw, convb, dtb, alog, dskip, ssmn, woa, wob, wo)
    in_specs = ([pl.BlockSpec(memory_space=pltpu.SMEM),
                 pl.BlockSpec((1, CHUNK, D_MODEL), lambda b, c: (b, c, 0))]
                + [_const_spec(a.shape) for a in consts])
    out_shape = (jax.ShapeDtypeStruct((nb, seq, D_MODEL), F32),
                 jax.ShapeDtypeStruct((nb, WINDOW, KV_WIDTH), F32),
                 jax.ShapeDtypeStruct((nb, WINDOW, KV_WIDTH), F32),
                 jax.ShapeDtypeStruct((nb, CONV_W - 1, CONV_DIM), F32),
                 jax.ShapeDtypeStruct((nb, D_INNER, D_STATE), F32))
    out_specs = (pl.BlockSpec((1, CHUNK, D_MODEL), lambda b, c: (b, c, 0)),
                 pl.BlockSpec((1, WINDOW, KV_WIDTH), lambda b, c: (b, 0, 0)),
                 pl.BlockSpec((1, WINDOW, KV_WIDTH), lambda b, c: (b, 0, 0)),
                 pl.BlockSpec((1, CONV_W - 1, CONV_DIM), lambda b, c: (b, 0, 0)),
                 pl.BlockSpec((1, D_INNER, D_STATE), lambda b, c: (b, 0, 0)))
    scratch = [pltpu.VMEM((CHUNK, KV_WIDTH), F32), pltpu.VMEM((CHUNK, KV_WIDTH), F32),
               pltpu.VMEM((CHUNK + SUBLANES, CONV_DIM), F32), pltpu.VMEM((D_STATE, D_INNER), F32)]
    return pl.pallas_call(
        _prompt_mixer_kernel, grid=(nb, nc), in_specs=in_specs, out_specs=out_specs,
        out_shape=out_shape, scratch_shapes=scratch, name="prompt_mixer",
        compiler_params=pltpu.CompilerParams(dimension_semantics=("arbitrary", "arbitrary"),
                                             vmem_limit_bytes=VMEM_LIMIT),
    )(sinks, x, *consts)


MLP_TILE = 512
FF_BLOCK = 1024


def _mlp_kernel(x_ref, norm2_ref, wup_ref, wdown_ref, fnorm_ref, y_ref):
    x = x_ref[...]
    hm = _rms(x, norm2_ref[...]).astype(BF16)
    acc = x
    for j in range(D_FF // FF_BLOCK):
        h = _mm(hm, wup_ref[:, j * FF_BLOCK:(j + 1) * FF_BLOCK])
        h = jnp.square(jnp.maximum(h, 0.0))
        acc = acc + _mm(h, wdown_ref[j * FF_BLOCK:(j + 1) * FF_BLOCK, :])
    y_ref[...] = _rms(acc, fnorm_ref[...])


def _mlp(x, norm2, wup, wdown, fnorm):
    rows = x.shape[0]
    assert rows % MLP_TILE == 0
    consts = (norm2, wup, wdown, fnorm)
    return pl.pallas_call(
        _mlp_kernel, grid=(rows // MLP_TILE,),
        in_specs=[pl.BlockSpec((MLP_TILE, D_MODEL), lambda i: (i, 0))] + [_const_spec(a.shape) for a in consts],
        out_specs=pl.BlockSpec((MLP_TILE, D_MODEL), lambda i: (i, 0)),
        out_shape=jax.ShapeDtypeStruct((rows, D_MODEL), F32), name="mlp",
        compiler_params=pltpu.CompilerParams(dimension_semantics=("arbitrary",),
                                             vmem_limit_bytes=VMEM_LIMIT),
    )(x, *consts)


ROW_TILE = 256


def _in_proj_kernel(x_ref, norm1_ref, wqkv_ref, wz_ref, wxbc_ref, wdt_ref, wg_ref,
                    qkv_ref, z_ref, xbc_ref, dt_ref, g_ref):
    hb = _rms(x_ref[...], norm1_ref[...]).astype(BF16)
    qkv_ref[...] = _mm(hb, wqkv_ref[...])
    z_ref[...] = _mm(hb, wz_ref[...])
    xbc_ref[...] = _mm(hb, wxbc_ref[...])
    dt_ref[...] = _mm(hb, wdt_ref[...])
    g_ref[...] = _mm(hb, wg_ref[...])


def _in_proj(x, norm1, wqkv, wz, wxbc, wdt, wg):
    rows = x.shape[0]
    assert rows % ROW_TILE == 0
    consts = (norm1, wqkv, wz, wxbc, wdt, wg)
    widths = (QKV_WIDTH, D_INNER, CONV_DIM, LANES, 2 * D_MODEL)
    return pl.pallas_call(
        _in_proj_kernel, grid=(rows // ROW_TILE,),
        in_specs=[pl.BlockSpec((ROW_TILE, D_MODEL), lambda i: (i, 0))] + [_const_spec(a.shape) for a in consts],
        out_specs=tuple(pl.BlockSpec((ROW_TILE, w), lambda i: (i, 0)) for w in widths),
        out_shape=tuple(jax.ShapeDtypeStruct((rows, w), F32) for w in widths), name="sample_in_proj",
        compiler_params=pltpu.CompilerParams(dimension_semantics=("arbitrary",),
                                             vmem_limit_bytes=VMEM_LIMIT),
    )(x, *consts)


def _tile_roll(x, shift):
    rows, width = x.shape
    tiles = x.reshape(rows // SUBLANES, SUBLANES, width)
    return pltpu.roll(tiles, shift % SUBLANES, axis=1).reshape(rows, width)


def _expand_heads(cols, expand):
    terms = []
    for c in cols:
        hi = c.astype(BF16)
        r1 = c - hi.astype(F32)
        mid = r1.astype(BF16)
        lo = (r1 - mid.astype(F32)).astype(BF16)
        terms.extend([hi, mid, lo])
    rows = cols[0].shape[0]
    out = jnp.dot(jnp.concatenate(terms, axis=0), expand, preferred_element_type=F32)
    return [out[(3 * i) * rows:(3 * i + 1) * rows] + out[(3 * i + 1) * rows:(3 * i + 2) * rows]
            + out[(3 * i + 2) * rows:(3 * i + 3) * rows] for i in range(len(cols))]


def _sample_mixer_kernel(sinks_ref, qkv_ref, z_ref, xbc_ref, dtraw_ref, cprev_ref, kc_ref, vc_ref, st_ref,
                         convw_ref, convb_ref, dtb_ref, alog_ref, dskip_ref, ssmn_ref,
                         aout_ref, mout_ref, nk_ref, nv_ref, nconv_ref, nst_ref):
    R = SAMPLE_GROUP * SAMPLE_PAD
    L = 4
    rm = lax.broadcasted_iota(jnp.int32, (R, 1), 0) % SAMPLE_PAD

    xr = xbc_raw = xbc_ref[...]
    with_prev = jnp.where(rm >= SAMPLE_PAD - (CONV_W - 1), cprev_ref[...], xr)
    yc = xr * convw_ref[CONV_W - 1:CONV_W, :]
    for kshift in range(1, CONV_W):
        yc = yc + _tile_roll(with_prev, kshift) * convw_ref[CONV_W - 1 - kshift:CONV_W - kshift, :]
    nconv_ref[...] = _tile_roll(xbc_raw, -1)
    xbc = _silu(yc + convb_ref[...])
    xs = xbc[:, :D_INNER]
    bm = xbc[:, D_INNER:D_INNER + N_GROUPS_M * D_STATE]
    cm = xbc[:, D_INNER + N_GROUPS_M * D_STATE:]

    dt = _softplus(dtraw_ref[...] + dtb_ref[...])
    d_a = dt * (-jnp.exp(alog_ref[...]))
    a_cs = d_a
    suf = jnp.zeros_like(d_a)
    for kshift in range(1, L):
        a_cs = a_cs + jnp.where(rm >= kshift, _tile_roll(d_a, kshift), 0.0)
        suf = suf + jnp.where(rm <= L - 1 - kshift, _tile_roll(d_a, -kshift), 0.0)

    expand = (lax.broadcasted_iota(jnp.int32, (LANES, D_INNER), 1) // HEAD_DIM_M
              == lax.broadcasted_iota(jnp.int32, (LANES, D_INNER), 0)).astype(BF16)
    dt_e, acs_e, suf_e = _expand_heads([dt, a_cs, suf], expand)
    xdt = xs * dt_e
    real = rm < L
    xd_t = jnp.where(real, xdt * jnp.exp(suf_e), 0.0).T.astype(BF16)

    y = xs * dskip_ref[...]
    for kshift in range(L):
        bk = bm if kshift == 0 else _tile_roll(bm, kshift)
        cb = cm * bk
        cb_e = jnp.concatenate(
            [jnp.broadcast_to(jnp.sum(cb[:, g * D_STATE:(g + 1) * D_STATE], axis=-1, keepdims=True),
                              (R, GROUP_WIDTH)) for g in range(N_GROUPS_M)], axis=1)
        if kshift == 0:
            y = y + cb_e * xdt
        else:
            seg = jnp.exp(acs_e - _tile_roll(acs_e, kshift))
            y = y + jnp.where(rm >= kshift, cb_e * seg * _tile_roll(xdt, kshift), 0.0)

    qkv = qkv_ref[...]
    q = qkv[:, :ATTN_WIDTH] * (HEAD_DIM_A ** -0.5)
    q_swapped = jnp.concatenate(
        [pltpu.roll(q[:, s * LANES:(s + 1) * LANES], HEAD_DIM_A, axis=1) for s in range(ATTN_WIDTH // LANES)], axis=1)
    kn = qkv[:, ATTN_WIDTH:ATTN_WIDTH + KV_WIDTH]
    vn = qkv[:, ATTN_WIDTH + KV_WIDTH:]
    lo8 = _lane_lo((SAMPLE_PAD, LANES))
    zero8 = jnp.zeros((SAMPLE_PAD, LANES), F32)
    seq_rows = [slice(i * SAMPLE_PAD, (i + 1) * SAMPLE_PAD) for i in range(SAMPLE_GROUP)]
    HROWS = N_HEADS_A * SAMPLE_PAD

    s_c, s_n = [], []
    for rows in seq_rows:
        pieces = []
        for kv in range(N_KV_A):
            for g in range(Q_PER_KV):
                slab = kv * (Q_PER_KV // 2) + g // 2
                src = q if g % 2 == kv else q_swapped
                piece = src[rows, slab * LANES:(slab + 1) * LANES]
                pieces.append(jnp.where(lo8, piece, zero8) if kv == 0 else jnp.where(lo8, zero8, piece))
        lhs = jnp.concatenate(pieces, axis=0).astype(BF16)
        i = len(s_c)
        s_c.append(_mm_nt(lhs, kc_ref[i]))
        s_n.append(_mm_nt(lhs, kn[rows]))
    s_c = jnp.concatenate(s_c, axis=0)
    s_n = jnp.concatenate(s_n, axis=0)
    n_rows = SAMPLE_GROUP * HROWS
    tok_c = lax.broadcasted_iota(jnp.int32, (n_rows, WINDOW), 0) % SAMPLE_PAD
    s_c = jnp.where(lax.broadcasted_iota(jnp.int32, (n_rows, WINDOW), 1) >= tok_c, s_c, NEG_BIG)
    tok_n = lax.broadcasted_iota(jnp.int32, (n_rows, SAMPLE_PAD), 0) % SAMPLE_PAD
    col_n = lax.broadcasted_iota(jnp.int32, (n_rows, SAMPLE_PAD), 1)
    s_n = jnp.where((col_n <= tok_n) & (col_n < L), s_n, NEG_BIG)
    sk = jnp.concatenate([jnp.full((SAMPLE_PAD, 1), sinks_ref[h], F32) for h in range(N_HEADS_A)] * SAMPLE_GROUP,
                         axis=0)
    m = jnp.maximum(jnp.maximum(jnp.max(s_c, axis=-1, keepdims=True), jnp.max(s_n, axis=-1, keepdims=True)), sk)
    e_c = jnp.exp(s_c - m)
    e_n = jnp.exp(s_n - m)
    inv_den = 1.0 / (jnp.sum(e_c, axis=-1, keepdims=True) + jnp.sum(e_n, axis=-1, keepdims=True) + jnp.exp(sk - m))
    e_c = e_c.astype(BF16)
    e_n = e_n.astype(BF16)
    o = jnp.concatenate([_mm(e_c[i * HROWS:(i + 1) * HROWS], vc_ref[i]) + _mm(e_n[i * HROWS:(i + 1) * HROWS], vn[rows])
                         for i, rows in enumerate(seq_rows)], axis=0) * inv_den
    o_swapped = pltpu.roll(o, HEAD_DIM_A, axis=1)
    a_rows = []
    for i in range(SAMPLE_GROUP):
        blk = lambda arr, kv, g: arr[i * HROWS + (kv * Q_PER_KV + g) * SAMPLE_PAD:
                                     i * HROWS + (kv * Q_PER_KV + g + 1) * SAMPLE_PAD]
        slabs = []
        for kv in range(N_KV_A):
            for j in range(Q_PER_KV // 2):
                first = blk(o if kv == 0 else o_swapped, kv, 2 * j)
                second = blk(o_swapped if kv == 0 else o, kv, 2 * j + 1)
                slabs.append(jnp.where(lo8, first, second))
        a_rows.append(jnp.concatenate(slabs, axis=1))
    aout_ref[...] = jnp.concatenate(a_rows, axis=0)

    r_w = lax.broadcasted_iota(jnp.int32, (WINDOW, LANES), 0) % SUBLANES
    for i, rows in enumerate(seq_rows):
        for cache_ref, new, out_ref in ((kc_ref, kn, nk_ref), (vc_ref, vn, nv_ref)):
            cache = cache_ref[i]
            nxt = jnp.concatenate([cache[SUBLANES:], new[rows]], axis=0)
            out_ref[i] = _tile_roll(jnp.where(r_w >= L, cache, nxt), -L)

    row_r = lax.broadcasted_iota(jnp.int32, (R, D_STATE), 0)
    heads_per_group = N_HEADS_M // N_GROUPS_M
    yoff = []
    for i, rows in enumerate(seq_rows):
        state = st_ref[i]
        state_b = state.astype(BF16)
        a_tot = a_cs[i * SAMPLE_PAD + L - 1:i * SAMPLE_PAD + L, :]
        in_seq = (row_r >= i * SAMPLE_PAD) & (row_r < (i + 1) * SAMPLE_PAD)
        yo = []
        for g in range(N_GROUPS_M):
            gs = slice(g * GROUP_WIDTH, (g + 1) * GROUP_WIDTH)
            yo.append(_mm_nt(cm[rows, g * D_STATE:(g + 1) * D_STATE], state_b[gs]))
            bsel = jnp.where(in_seq, bm[:, g * D_STATE:(g + 1) * D_STATE], 0.0)
            upd = _mm(xd_t[gs, :], bsel)
            for hh in range(heads_per_group):
                h = g * heads_per_group + hh
                hs = slice(h * HEAD_DIM_M, (h + 1) * HEAD_DIM_M)
                decay = jnp.exp(jnp.broadcast_to(a_tot[:, h:h + 1], (HEAD_DIM_M, D_STATE)))
                nst_ref[i, hs, :] = state[hs] * decay + upd[hh * HEAD_DIM_M:(hh + 1) * HEAD_DIM_M]
        yoff.append(jnp.concatenate(yo, axis=1))
    yoff = jnp.concatenate(yoff, axis=0)

    y = (y + yoff * jnp.exp(acs_e)) * _silu(z_ref[...])
    outs = []
    for g in range(N_GROUPS_M):
        yg = y[:, g * GROUP_WIDTH:(g + 1) * GROUP_WIDTH]
        outs.append(yg * lax.rsqrt(jnp.mean(yg * yg, axis=-1, keepdims=True) + EPS))
    mout_ref[...] = jnp.concatenate(outs, axis=1) * ssmn_ref[...]


def _sample_mixer(sinks, qkv, z, xbc, dtraw, cprev, kc, vc, st, convw, convb, dtb, alog, dskip, ssmn):
    nseq = kc.shape[0]
    assert nseq % SAMPLE_GROUP == 0
    R = SAMPLE_GROUP * SAMPLE_PAD
    rows = nseq * SAMPLE_PAD
    consts = (convw, convb, dtb, alog, dskip, ssmn)
    row_spec = lambda w: pl.BlockSpec((R, w), lambda i: (i, 0))
    seq_spec = lambda a, b: pl.BlockSpec((SAMPLE_GROUP, a, b), lambda i: (i, 0, 0))
    in_specs = ([pl.BlockSpec(memory_space=pltpu.SMEM),
                 row_spec(QKV_WIDTH), row_spec(D_INNER), row_spec(CONV_DIM), row_spec(LANES), row_spec(CONV_DIM),
                 seq_spec(WINDOW, KV_WIDTH), seq_spec(WINDOW, KV_WIDTH), seq_spec(D_INNER, D_STATE)]
                + [_const_spec(a.shape) for a in consts])
    out_shape = (jax.ShapeDtypeStruct((rows, ATTN_WIDTH), F32), jax.ShapeDtypeStruct((rows, D_INNER), F32),
                 jax.ShapeDtypeStruct((nseq, WINDOW, KV_WIDTH), F32),
                 jax.ShapeDtypeStruct((nseq, WINDOW, KV_WIDTH), F32),
                 jax.ShapeDtypeStruct((rows, CONV_DIM), F32),
                 jax.ShapeDtypeStruct((nseq, D_INNER, D_STATE), F32))
    out_specs = (row_spec(ATTN_WIDTH), row_spec(D_INNER), seq_spec(WINDOW, KV_WIDTH), seq_spec(WINDOW, KV_WIDTH),
                 row_spec(CONV_DIM), seq_spec(D_INNER, D_STATE))
    return pl.pallas_call(
        _sample_mixer_kernel, grid=(nseq // SAMPLE_GROUP,), in_specs=in_specs, out_specs=out_specs,
        out_shape=out_shape, name="sample_mixer",
        compiler_params=pltpu.CompilerParams(dimension_semantics=("arbitrary",),
                                             vmem_limit_bytes=VMEM_LIMIT),
    )(sinks, qkv, z, xbc, dtraw, cprev, kc, vc, st, *consts)


def _out_proj_kernel(x_ref, a_ref, m_ref, g_ref, woa_ref, wob_ref, wo_ref, x1_ref):
    gates = g_ref[...]
    merged = (_sigmoid(gates[:, :D_MODEL]) * _mm(a_ref[...], woa_ref[...])
              + _sigmoid(gates[:, D_MODEL:]) * _mm(m_ref[...], wob_ref[...]))
    x1_ref[...] = x_ref[...] + _mm(merged, wo_ref[...])


def _out_proj(x, a_out, m_out, gates, woa, wob, wo):
    rows = x.shape[0]
    assert rows % ROW_TILE == 0
    consts = (woa, wob, wo)
    widths = (D_MODEL, ATTN_WIDTH, D_INNER, 2 * D_MODEL)
    return pl.pallas_call(
        _out_proj_kernel, grid=(rows // ROW_TILE,),
        in_specs=([pl.BlockSpec((ROW_TILE, w), lambda i: (i, 0)) for w in widths]
                  + [_const_spec(a.shape) for a in consts]),
        out_specs=pl.BlockSpec((ROW_TILE, D_MODEL), lambda i: (i, 0)),
        out_shape=jax.ShapeDtypeStruct((rows, D_MODEL), F32), name="sample_out_proj",
        compiler_params=pltpu.CompilerParams(dimension_semantics=("arbitrary",),
                                             vmem_limit_bytes=VMEM_LIMIT),
    )(x, a_out, m_out, gates, *consts)


def _layer_params(norm1, w_in, sinks, conv_w, conv_b, dt_bias, a_log, d_skip, ssm_norm, w_oa, w_ob, w_o,
                  norm2, w_up, w_down):
    off = [0]

    def take(width):
        piece = w_in[:, off[0]:off[0] + width]
        off[0] += width
        return piece

    wq, wk, wv = take(ATTN_WIDTH), take(KV_WIDTH), take(KV_WIDTH)
    wz, wxbc, wdt = take(D_INNER), take(CONV_DIM), take(N_HEADS_M)
    wga, wgb = take(D_MODEL), take(D_MODEL)
    pad_lanes = lambda a: jnp.pad(a, ((0, 0), (0, LANES - a.shape[1])))
    return dict(
        sinks=sinks.astype(F32), norm1=norm1[None, :],
        wqkv=jnp.concatenate([wq, wk, wv], axis=1).astype(BF16), wz=wz.astype(BF16),
        wxbc=wxbc.astype(BF16), wdt=pad_lanes(wdt).astype(BF16),
        wg=jnp.concatenate([wga, wgb], axis=1).astype(BF16),
        convw=conv_w, convb=conv_b[None, :], dtb=pad_lanes(dt_bias[None, :]), alog=pad_lanes(a_log[None, :]),
        dskip=jnp.repeat(d_skip, HEAD_DIM_M)[None, :], ssmn=ssm_norm[None, :],
        woa=w_oa.astype(BF16), wob=w_ob.astype(BF16), wo=w_o.astype(BF16),
        norm2=norm2[None, :], wup=w_up.astype(BF16), wdown=w_down.astype(BF16))


def kernel(x_prompt, x_sample, cache_swa_k, cache_swa_v, state_conv, state_ssm, norm1, w_in, sinks, conv_w,
           conv_b, dt_bias, a_log, d_skip, ssm_norm, w_oa, w_ob, w_o, norm2, w_up, w_down, final_norm):
    depth = w_in.shape[0]
    assert depth == 1
    nb, seq, _ = x_prompt.shape
    ns, ls, _ = x_sample.shape
    assert ls == 4
    p = _layer_params(norm1[0], w_in[0], sinks[0], conv_w[0], conv_b[0], dt_bias[0], a_log[0], d_skip[0],
                      ssm_norm[0], w_oa[0], w_ob[0], w_o[0], norm2[0], w_up[0], w_down[0])
    fnorm = final_norm[None, :]
    mixer_consts = (p["convw"], p["convb"], p["dtb"], p["alog"], p["dskip"], p["ssmn"])

    x1p, pk, pv, pc, pst = _prompt_mixer(
        x_prompt, p["sinks"], p["norm1"], p["wqkv"], p["wz"], p["wxbc"], p["wdt"], p["wg"],
        *mixer_consts, p["woa"], p["wob"], p["wo"])
    y_prompt = _mlp(x1p.reshape(nb * seq, D_MODEL), p["norm2"], p["wup"], p["wdown"], fnorm)
    y_prompt = y_prompt.reshape(nb, seq, D_MODEL)

    xs_pad = jnp.pad(x_sample, ((0, 0), (0, SAMPLE_PAD - ls), (0, 0))).reshape(ns * SAMPLE_PAD, D_MODEL)
    cprev = jnp.pad(state_conv[0], ((0, 0), (SAMPLE_PAD - (CONV_W - 1), 0), (0, 0)))
    cprev = cprev.reshape(ns * SAMPLE_PAD, CONV_DIM)
    qkv, z, xbc, dtraw, gates = _in_proj(xs_pad, p["norm1"], p["wqkv"], p["wz"], p["wxbc"], p["wdt"], p["wg"])
    a_out, m_out, sk, sv, sc, sst = _sample_mixer(
        p["sinks"], qkv, z, xbc, dtraw, cprev,
        cache_swa_k[0].reshape(ns, WINDOW, KV_WIDTH), cache_swa_v[0].reshape(ns, WINDOW, KV_WIDTH),
        state_ssm[0].reshape(ns, D_INNER, D_STATE), *mixer_consts)
    x1s = _out_proj(xs_pad, a_out, m_out, gates, p["woa"], p["wob"], p["wo"])
    y_sample = _mlp(x1s, p["norm2"], p["wup"], p["wdown"], fnorm)
    y_sample = y_sample.reshape(ns, SAMPLE_PAD, D_MODEL)[:, :ls]
    sc = sc.reshape(ns, SAMPLE_PAD, CONV_DIM)[:, :CONV_W - 1]

    kv_shape = (1, -1, WINDOW, N_KV_A, HEAD_DIM_A)
    ssm_shape = (1, -1, N_HEADS_M, HEAD_DIM_M, D_STATE)
    return (y_prompt, y_sample,
            pk.reshape(kv_shape), pv.reshape(kv_shape), pc[None], pst.reshape(ssm_shape),
            sk.reshape(kv_shape), sv.reshape(kv_shape), sc[None], sst.reshape(ssm_shape))
```

```python
import collections
import functools

import jax
import jax.numpy as jnp
from jax import lax
from jax.experimental import pallas as pl
from jax.experimental.pallas import tpu as pltpu

F32 = jnp.float32
BF16 = jnp.bfloat16

D_MODEL = 1024
N_HEADS_A = 8
N_KV_A = 2
Q_PER_KV = N_HEADS_A // N_KV_A
HEAD_DIM_A = 64
WINDOW = 128
ATTN_WIDTH = N_HEADS_A * HEAD_DIM_A
KV_WIDTH = N_KV_A * HEAD_DIM_A
D_INNER = 1024
HEAD_DIM_M = 64
N_HEADS_M = D_INNER // HEAD_DIM_M
N_GROUPS_M = 2
GROUP_WIDTH = D_INNER // N_GROUPS_M
D_STATE = 128
CONV_W = 4
CONV_DIM = D_INNER + 2 * N_GROUPS_M * D_STATE
CHUNK = 128
D_FF = 4 * D_MODEL
EPS = 1e-6

LANES = 128
SUBLANES = 8
QKV_WIDTH = ATTN_WIDTH + 2 * KV_WIDTH
NEG_BIG = -1e30
VMEM_LIMIT = 56 * 1024 * 1024
SAMPLE_PAD = SUBLANES
SAMPLE_GROUP = 16


def _mm(a, b):
    return jnp.dot(a.astype(BF16), b.astype(BF16), preferred_element_type=F32)


def _mm_nt(a, b):
    return lax.dot_general(a.astype(BF16), b.astype(BF16), (((1,), (1,)), ((), ())),
                           preferred_element_type=F32)


def _rms(x, w):
    return x * lax.rsqrt(jnp.mean(x * x, axis=-1, keepdims=True) + EPS) * w


def _sigmoid(x):
    return 0.5 + 0.5 * jnp.tanh(0.5 * x)


def _silu(x):
    h = 0.5 * x
    return h + h * jnp.tanh(h)


def _softplus(x):
    return jnp.maximum(x, 0.0) + jnp.log(1.0 + jnp.exp(-jnp.abs(x)))


def _lane_lo(shape):
    return (lax.broadcasted_iota(jnp.int32, shape, len(shape) - 1) % LANES) < HEAD_DIM_A


def _dup_half(x, lo, first):
    xr = pltpu.roll(x, HEAD_DIM_A, axis=1)
    return jnp.where(lo, x, xr) if first else jnp.where(lo, xr, x)


def _stack_heads(q, kv, lo):
    qa = q[:, kv * 2 * LANES: kv * 2 * LANES + LANES]
    qb = q[:, kv * 2 * LANES + LANES: (kv + 1) * 2 * LANES]
    zero = jnp.zeros_like(qa)
    return jnp.concatenate([jnp.where(lo, qa, zero), jnp.where(lo, zero, qa),
                            jnp.where(lo, qb, zero), jnp.where(lo, zero, qb)], axis=0)


def _unstack_heads(o, rows, lo):
    return jnp.concatenate([jnp.where(lo, o[0:rows], o[rows:2 * rows]),
                            jnp.where(lo, o[2 * rows:3 * rows], o[3 * rows:4 * rows])], axis=1)


ProjBufs = collections.namedtuple("ProjBufs", "qkv z xbc dt")
MixedBufs = collections.namedtuple("MixedBufs", "attn ssd")
Carry = collections.namedtuple("Carry", "kprev vprev conv_tail ht")
MixerWeights = collections.namedtuple(
    "MixerWeights", "norm1 wqkv wz wxbc wdt wg convw convb dtb alog dskip ssmn woa wob wo")
StateOuts = collections.namedtuple("StateOuts", "k v conv ssm")


def _split3_bf16(x):
    hi = x.astype(BF16)
    r1 = x - hi.astype(F32)
    mid = r1.astype(BF16)
    return hi, mid, (r1 - mid.astype(F32)).astype(BF16)


DENSE_COLS = 512


def _col_pieces(width):
    return [slice(c, min(c + DENSE_COLS, width)) for c in range(0, width, DENSE_COLS)]


def _stage_in_proj(x_ref, rows, w, proj):
    hb = _rms(x_ref[rows, :], w.norm1[...]).astype(BF16)
    yield
    for w_ref, dst in ((w.wqkv, proj.qkv), (w.wz, proj.z), (w.wxbc, proj.xbc), (w.wdt, proj.dt)):
        for cols in _col_pieces(dst.shape[1]):
            dst[:, cols] = _mm(hb, w_ref[:, cols])
            yield


def _stage_out_proj(x_ref, rows, mixed, w, x1_ref):
    x = x_ref[rows, :]
    hb = _rms(x, w.norm1[...]).astype(BF16)
    yield
    merged = []
    for half, (src, w_ref) in enumerate(((mixed.attn, w.woa), (mixed.ssd, w.wob))):
        parts = []
        for cols in _col_pieces(D_MODEL):
            gate_cols = slice(half * D_MODEL + cols.start, half * D_MODEL + cols.stop)
            gate = _sigmoid(_mm(hb, w.wg[:, gate_cols]))
            yield
            parts.append(gate * _mm(src[...], w_ref[:, cols]))
            yield
        merged.append(jnp.concatenate(parts, axis=1))
    merged = (merged[0] + merged[1]).astype(BF16)
    for cols in _col_pieces(D_MODEL):
        x1_ref[rows, cols] = x[:, cols] + _mm(merged, w.wo[:, cols])
        yield


def _alternate(*gens):
    gens = list(gens)
    while gens:
        for g in list(gens):
            try:
                next(g)
                yield
            except StopIteration:
                gens.remove(g)


def _trace_interleaved(primary, filler):
    for _ in primary:
        next(filler, None)
    for _ in filler:
        pass


def _stage_mix(proj, mixed, carry, w, sinks_ref, first, outs):
    T = CHUNK
    reset = (lambda a: a) if first is None else (lambda a: jnp.where(first, 0.0, a))

    qkv = proj.qkv[...]
    q = qkv[:, :ATTN_WIDTH] * (HEAD_DIM_A ** -0.5)
    k = qkv[:, ATTN_WIDTH:ATTN_WIDTH + KV_WIDTH]
    v = qkv[:, ATTN_WIDTH + KV_WIDTH:]
    kk = jnp.concatenate([carry.kprev[...], k], axis=0)
    vv = jnp.concatenate([carry.vprev[...], v], axis=0)
    carry.kprev[...] = k
    carry.vprev[...] = v

    lo = _lane_lo((T, LANES))
    lo2 = _lane_lo((2 * T, LANES))
    r = lax.broadcasted_iota(jnp.int32, (T, 2 * T), 0)
    col = lax.broadcasted_iota(jnp.int32, (T, 2 * T), 1)
    valid = (col >= r) & (col <= r + WINDOW)
    if first is not None:
        valid = valid & (col >= jnp.where(first, T, 0))
    a_slabs = []
    for kv in range(N_KV_A):
        kd = _dup_half(kk, lo2, kv == 0).astype(BF16)
        vd = _dup_half(vv, lo2, kv == 0).astype(BF16)
        s = _mm_nt(_stack_heads(q, kv, lo), kd)
        es, inv = [], []
        for g in range(Q_PER_KV):
            sk = sinks_ref[kv * Q_PER_KV + g]
            sg = jnp.where(valid, s[g * T:(g + 1) * T], NEG_BIG)
            m = jnp.maximum(jnp.max(sg, axis=-1, keepdims=True), sk)
            e = jnp.exp(sg - m)
            den = jnp.sum(e, axis=-1, keepdims=True) + jnp.exp(sk - m)
            es.append(e.astype(BF16))
            inv.append(1.0 / den)
        yield
        o = _mm(jnp.concatenate(es, axis=0), vd) * jnp.concatenate(inv, axis=0)
        a_slabs.append(_unstack_heads(o, T, lo))
        yield
    mixed.attn[...] = jnp.concatenate(a_slabs, axis=1).astype(BF16)

    xr = proj.xbc[...]
    prev_tile = jnp.concatenate([reset(carry.conv_tail[...]), xr[:T - SUBLANES]], axis=0)
    carry.conv_tail[...] = xr[T - SUBLANES:]
    rm = lax.broadcasted_iota(jnp.int32, (T, 1), 0) % SUBLANES
    yc = xr * w.convw[CONV_W - 1:CONV_W, :]
    for shift in range(1, CONV_W):
        shifted = _tile_roll(jnp.where(rm >= SUBLANES - shift, prev_tile, xr), shift)
        yc = yc + shifted * w.convw[CONV_W - 1 - shift:CONV_W - shift, :]
        yield
    xbc = _silu(yc + w.convb[...])
    xs = xbc[:, :D_INNER]
    bm = xbc[:, D_INNER:D_INNER + N_GROUPS_M * D_STATE]
    cm = xbc[:, D_INNER + N_GROUPS_M * D_STATE:]

    z = proj.z[...]
    dt = _softplus(proj.dt[...] + w.dtb[...])
    rr = lax.broadcasted_iota(jnp.int32, (T, T), 0)
    cc = lax.broadcasted_iota(jnp.int32, (T, T), 1)
    tril = cc <= rr
    cs = jnp.dot(jnp.where(tril, 1.0, 0.0).astype(BF16),
                 jnp.concatenate(_split3_bf16(dt * -jnp.exp(w.alog[...])), axis=1), preferred_element_type=F32)
    a_cs = cs[:, :LANES] + cs[:, LANES:2 * LANES] + cs[:, 2 * LANES:]
    a_cs_t = a_cs.T
    yield

    ys = []
    pairs_per_group = N_HEADS_M // 2 // N_GROUPS_M
    for g in range(N_GROUPS_M):
        gcols = slice(g * GROUP_WIDTH, (g + 1) * GROUP_WIDTH)
        cg = cm[:, g * D_STATE:(g + 1) * D_STATE]
        bg = bm[:, g * D_STATE:(g + 1) * D_STATE]
        cb = _mm_nt(cg, bg)
        h_prev = reset(carry.ht[:, gcols])
        y_off = _mm(cg, h_prev)
        xds, decays = [], []
        for jj in range(pairs_per_group):
            j = g * pairs_per_group + jj
            sl = slice(j * LANES, (j + 1) * LANES)
            ws, colbs, dtbs = [], [], []
            for h in (2 * j, 2 * j + 1):
                colb = jnp.broadcast_to(a_cs[:, h:h + 1], (T, T))
                rowb = jnp.broadcast_to(a_cs_t[h:h + 1, :], (T, T))
                seg = jnp.where(tril, jnp.exp(colb - rowb), 0.0)
                ws.append((cb * seg).astype(BF16))
                colbs.append(colb)
                dtbs.append(jnp.broadcast_to(dt[:, h:h + 1], (T, LANES)))
            dt_e = jnp.where(lo, dtbs[0], dtbs[1])
            acs_e = jnp.where(lo, colbs[0], colbs[1])
            xs_j = xs[:, sl]
            xdt = xs_j * dt_e
            zero = jnp.zeros_like(xdt)
            rhs = jnp.concatenate([jnp.where(lo, xdt, zero), jnp.where(lo, zero, xdt)], axis=0)
            y_diag = _mm(jnp.concatenate(ws, axis=1), rhs)
            alast = acs_e[T - 1:T, :]
            xds.append(xdt * jnp.exp(alast - acs_e))
            decays.append(jnp.exp(alast))
            y = y_diag + y_off[:, jj * LANES:(jj + 1) * LANES] * jnp.exp(acs_e) + xs_j * w.dskip[:, sl]
            ys.append(y * _silu(z[:, sl]))
            yield
        carry.ht[:, gcols] = (h_prev * jnp.concatenate(decays, axis=1)
                              + _mm(bg.T, jnp.concatenate(xds, axis=1)))
        yield

    m_slabs = []
    for g in range(N_GROUPS_M):
        grp = ys[g * pairs_per_group:(g + 1) * pairs_per_group]
        ssq = grp[0] * grp[0]
        for y in grp[1:]:
            ssq = ssq + y * y
        scale = lax.rsqrt(jnp.sum(ssq, axis=-1, keepdims=True) * (1.0 / GROUP_WIDTH) + EPS)
        m_slabs.extend([y * scale for y in grp])
    mixed.ssd[...] = (jnp.concatenate(m_slabs, axis=1) * w.ssmn[...]).astype(BF16)

    if outs is not None:
        outs.k[0] = k
        outs.v[0] = v
        outs.conv[0] = xr[T - (CONV_W - 1):]
        outs.ssm[0] = carry.ht[...].T


def _prompt_mixer_kernel(sinks_ref, x_in_ref, x_res_ref, *refs, chunks_per_seq):
    w = MixerWeights(*refs[:15])
    x1_ref = refs[15]
    outs = StateOuts(*refs[16:20])
    proj = (ProjBufs(*refs[20:24]), ProjBufs(*refs[24:28]))
    mixed = (MixedBufs(*refs[28:30]), MixedBufs(*refs[30:32]))
    carry = Carry(*refs[32:36])
    T = CHUNK
    s = pl.program_id(0)

    @pl.when(s == 0)
    def _():
        for ref in proj[1] + mixed[0] + mixed[1] + carry:
            ref[...] = jnp.zeros_like(ref)

    first_b = s % (chunks_per_seq // 2) == 0
    for rows, even, odd, first, state_outs in ((slice(0, T), 0, 1, None, outs),
                                               (slice(T, 2 * T), 1, 0, first_b, None)):
        dense = _alternate(_stage_in_proj(x_in_ref, rows, w, proj[even]),
                           _stage_out_proj(x_res_ref, rows, mixed[even], w, x1_ref))
        _trace_interleaved(_stage_mix(proj[odd], mixed[odd], carry, w, sinks_ref, first, state_outs), dense)


def _const_spec(shape):
    return pl.BlockSpec(shape, lambda *_: (0,) * len(shape))


def _prompt_mixer(x, sinks, norm1, wqkv, wz, wxbc, wdt, wg, convw, convb, dtb, alog, dskip, ssmn,
                  woa, wob, wo):
    nb, seq, _ = x.shape
    assert seq % (2 * CHUNK) == 0
    chunks_per_seq = seq // CHUNK
    n_pairs = nb * chunks_per_seq // 2
    pair = 2 * CHUNK
    consts = (norm1, wqkv, wz, wxbc, wdt, wg, convw, convb, dtb, alog, dskip, ssmn, woa, wob, wo)
    seq_of_mix = lambda s: jnp.clip((2 * s - 1) // chunks_per_seq, 0, nb - 1)
    in_specs = ([pl.BlockSpec(memory_space=pltpu.SMEM),
                 pl.BlockSpec((pair, D_MODEL), lambda s: (jnp.minimum(s, n_pairs - 1), 0)),
                 pl.BlockSpec((pair, D_MODEL), lambda s: (jnp.maximum(s - 1, 0), 0))]
                + [_const_spec(a.shape) for a in consts])
    out_shape = (jax.ShapeDtypeStruct((nb * seq, D_MODEL), F32),
                 jax.ShapeDtypeStruct((nb, WINDOW, KV_WIDTH), F32),
                 jax.ShapeDtypeStruct((nb, WINDOW, KV_WIDTH), F32),
                 jax.ShapeDtypeStruct((nb, CONV_W - 1, CONV_DIM), F32),
                 jax.ShapeDtypeStruct((nb, D_INNER, D_STATE), F32))
    out_specs = (pl.BlockSpec((pair, D_MODEL), lambda s: (jnp.maximum(s - 1, 0), 0)),
                 pl.BlockSpec((1, WINDOW, KV_WIDTH), lambda s: (seq_of_mix(s), 0, 0)),
                 pl.BlockSpec((1, WINDOW, KV_WIDTH), lambda s: (seq_of_mix(s), 0, 0)),
                 pl.BlockSpec((1, CONV_W - 1, CONV_DIM), lambda s: (seq_of_mix(s), 0, 0)),
                 pl.BlockSpec((1, D_INNER, D_STATE), lambda s: (seq_of_mix(s), 0, 0)))
    proj_bufs = [pltpu.VMEM((CHUNK, QKV_WIDTH), F32), pltpu.VMEM((CHUNK, D_INNER), F32),
                 pltpu.VMEM((CHUNK, CONV_DIM), F32), pltpu.VMEM((CHUNK, LANES), F32)]
    mixed_bufs = [pltpu.VMEM((CHUNK, ATTN_WIDTH), BF16), pltpu.VMEM((CHUNK, D_INNER), BF16)]
    carry = [pltpu.VMEM((CHUNK, KV_WIDTH), F32), pltpu.VMEM((CHUNK, KV_WIDTH), F32),
             pltpu.VMEM((SUBLANES, CONV_DIM), F32), pltpu.VMEM((D_STATE, D_INNER), F32)]
    x_rows = x.reshape(nb * seq, D_MODEL)
    return pl.pallas_call(
        functools.partial(_prompt_mixer_kernel, chunks_per_seq=chunks_per_seq),
        grid=(n_pairs + 1,), in_specs=in_specs, out_specs=out_specs,
        out_shape=out_shape, scratch_shapes=proj_bufs * 2 + mixed_bufs * 2 + carry, name="prompt_mixer",
        compiler_params=pltpu.CompilerParams(dimension_semantics=("arbitrary",),
                                             vmem_limit_bytes=VMEM_LIMIT),
    )(sinks, x_rows, x_rows, *consts)


MLP_TILE = 512
FF_BLOCK = 1024


def _mlp_kernel(x_ref, norm2_ref, wup_ref, wdown_ref, fnorm_ref, y_ref):
    x = x_ref[...]
    hm = _rms(x, norm2_ref[...]).astype(BF16)
    acc = x
    for j in range(D_FF // FF_BLOCK):
        h = _mm(hm, wup_ref[:, j * FF_BLOCK:(j + 1) * FF_BLOCK])
        h = jnp.square(jnp.maximum(h, 0.0))
        acc = acc + _mm(h, wdown_ref[j * FF_BLOCK:(j + 1) * FF_BLOCK, :])
    y_ref[...] = _rms(acc, fnorm_ref[...])


def _mlp(x, norm2, wup, wdown, fnorm):
    rows = x.shape[0]
    assert rows % MLP_TILE == 0
    consts = (norm2, wup, wdown, fnorm)
    return pl.pallas_call(
        _mlp_kernel, grid=(rows // MLP_TILE,),
        in_specs=[pl.BlockSpec((MLP_TILE, D_MODEL), lambda i: (i, 0))] + [_const_spec(a.shape) for a in consts],
        out_specs=pl.BlockSpec((MLP_TILE, D_MODEL), lambda i: (i, 0)),
        out_shape=jax.ShapeDtypeStruct((rows, D_MODEL), F32), name="mlp",
        compiler_params=pltpu.CompilerParams(dimension_semantics=("arbitrary",),
                                             vmem_limit_bytes=VMEM_LIMIT),
    )(x, *consts)


ROW_TILE = 256


def _in_proj_kernel(x_ref, norm1_ref, wqkv_ref, wz_ref, wxbc_ref, wdt_ref, wg_ref,
                    qkv_ref, z_ref, xbc_ref, dt_ref, g_ref):
    hb = _rms(x_ref[...], norm1_ref[...]).astype(BF16)
    qkv_ref[...] = _mm(hb, wqkv_ref[...])
    z_ref[...] = _mm(hb, wz_ref[...])
    xbc_ref[...] = _mm(hb, wxbc_ref[...])
    dt_ref[...] = _mm(hb, wdt_ref[...])
    g_ref[...] = _mm(hb, wg_ref[...])


def _in_proj(x, norm1, wqkv, wz, wxbc, wdt, wg):
    rows = x.shape[0]
    assert rows % ROW_TILE == 0
    consts = (norm1, wqkv, wz, wxbc, wdt, wg)
    widths = (QKV_WIDTH, D_INNER, CONV_DIM, LANES, 2 * D_MODEL)
    return pl.pallas_call(
        _in_proj_kernel, grid=(rows // ROW_TILE,),
        in_specs=[pl.BlockSpec((ROW_TILE, D_MODEL), lambda i: (i, 0))] + [_const_spec(a.shape) for a in consts],
        out_specs=tuple(pl.BlockSpec((ROW_TILE, w), lambda i: (i, 0)) for w in widths),
        out_shape=tuple(jax.ShapeDtypeStruct((rows, w), F32) for w in widths), name="sample_in_proj",
        compiler_params=pltpu.CompilerParams(dimension_semantics=("arbitrary",),
                                             vmem_limit_bytes=VMEM_LIMIT),
    )(x, *consts)


def _tile_roll(x, shift):
    rows, width = x.shape
    tiles = x.reshape(rows // SUBLANES, SUBLANES, width)
    return pltpu.roll(tiles, shift % SUBLANES, axis=1).reshape(rows, width)


def _expand_heads(cols, expand):
    terms = []
    for c in cols:
        hi = c.astype(BF16)
        r1 = c - hi.astype(F32)
        mid = r1.astype(BF16)
        lo = (r1 - mid.astype(F32)).astype(BF16)
        terms.extend([hi, mid, lo])
    rows = cols[0].shape[0]
    out = jnp.dot(jnp.concatenate(terms, axis=0), expand, preferred_element_type=F32)
    return [out[(3 * i) * rows:(3 * i + 1) * rows] + out[(3 * i + 1) * rows:(3 * i + 2) * rows]
            + out[(3 * i + 2) * rows:(3 * i + 3) * rows] for i in range(len(cols))]


def _sample_mixer_kernel(sinks_ref, qkv_ref, z_ref, xbc_ref, dtraw_ref, cprev_ref, kc_ref, vc_ref, st_ref,
                         convw_ref, convb_ref, dtb_ref, alog_ref, dskip_ref, ssmn_ref,
                         aout_ref, mout_ref, nk_ref, nv_ref, nconv_ref, nst_ref):
    R = SAMPLE_GROUP * SAMPLE_PAD
    L = 4
    rm = lax.broadcasted_iota(jnp.int32, (R, 1), 0) % SAMPLE_PAD

    xr = xbc_raw = xbc_ref[...]
    with_prev = jnp.where(rm >= SAMPLE_PAD - (CONV_W - 1), cprev_ref[...], xr)
    yc = xr * convw_ref[CONV_W - 1:CONV_W, :]
    for kshift in range(1, CONV_W):
        yc = yc + _tile_roll(with_prev, kshift) * convw_ref[CONV_W - 1 - kshift:CONV_W - kshift, :]
    nconv_ref[...] = _tile_roll(xbc_raw, -1)
    xbc = _silu(yc + convb_ref[...])
    xs = xbc[:, :D_INNER]
    bm = xbc[:, D_INNER:D_INNER + N_GROUPS_M * D_STATE]
    cm = xbc[:, D_INNER + N_GROUPS_M * D_STATE:]

    dt = _softplus(dtraw_ref[...] + dtb_ref[...])
    d_a = dt * (-jnp.exp(alog_ref[...]))
    a_cs = d_a
    suf = jnp.zeros_like(d_a)
    for kshift in range(1, L):
        a_cs = a_cs + jnp.where(rm >= kshift, _tile_roll(d_a, kshift), 0.0)
        suf = suf + jnp.where(rm <= L - 1 - kshift, _tile_roll(d_a, -kshift), 0.0)

    expand = (lax.broadcasted_iota(jnp.int32, (LANES, D_INNER), 1) // HEAD_DIM_M
              == lax.broadcasted_iota(jnp.int32, (LANES, D_INNER), 0)).astype(BF16)
    dt_e, acs_e, suf_e = _expand_heads([dt, a_cs, suf], expand)
    xdt = xs * dt_e
    real = rm < L
    xd_t = jnp.where(real, xdt * jnp.exp(suf_e), 0.0).T.astype(BF16)

    y = xs * dskip_ref[...]
    for kshift in range(L):
        bk = bm if kshift == 0 else _tile_roll(bm, kshift)
        cb = cm * bk
        cb_e = jnp.concatenate(
            [jnp.broadcast_to(jnp.sum(cb[:, g * D_STATE:(g + 1) * D_STATE], axis=-1, keepdims=True),
                              (R, GROUP_WIDTH)) for g in range(N_GROUPS_M)], axis=1)
        if kshift == 0:
            y = y + cb_e * xdt
        else:
            seg = jnp.exp(acs_e - _tile_roll(acs_e, kshift))
            y = y + jnp.where(rm >= kshift, cb_e * seg * _tile_roll(xdt, kshift), 0.0)

    qkv = qkv_ref[...]
    q = qkv[:, :ATTN_WIDTH] * (HEAD_DIM_A ** -0.5)
    q_swapped = jnp.concatenate(
        [pltpu.roll(q[:, s * LANES:(s + 1) * LANES], HEAD_DIM_A, axis=1) for s in range(ATTN_WIDTH // LANES)], axis=1)
    kn = qkv[:, ATTN_WIDTH:ATTN_WIDTH + KV_WIDTH]
    vn = qkv[:, ATTN_WIDTH + KV_WIDTH:]
    lo8 = _lane_lo((SAMPLE_PAD, LANES))
    zero8 = jnp.zeros((SAMPLE_PAD, LANES), F32)
    seq_rows = [slice(i * SAMPLE_PAD, (i + 1) * SAMPLE_PAD) for i in range(SAMPLE_GROUP)]
    HROWS = N_HEADS_A * SAMPLE_PAD

    s_c, s_n = [], []
    for rows in seq_rows:
        pieces = []
        for kv in range(N_KV_A):
            for g in range(Q_PER_KV):
                slab = kv * (Q_PER_KV // 2) + g // 2
                src = q if g % 2 == kv else q_swapped
                piece = src[rows, slab * LANES:(slab + 1) * LANES]
                pieces.append(jnp.where(lo8, piece, zero8) if kv == 0 else jnp.where(lo8, zero8, piece))
        lhs = jnp.concatenate(pieces, axis=0).astype(BF16)
        i = len(s_c)
        s_c.append(_mm_nt(lhs, kc_ref[i]))
        s_n.append(_mm_nt(lhs, kn[rows]))
    s_c = jnp.concatenate(s_c, axis=0)
    s_n = jnp.concatenate(s_n, axis=0)
    n_rows = SAMPLE_GROUP * HROWS
    tok_c = lax.broadcasted_iota(jnp.int32, (n_rows, WINDOW), 0) % SAMPLE_PAD
    s_c = jnp.where(lax.broadcasted_iota(jnp.int32, (n_rows, WINDOW), 1) >= tok_c, s_c, NEG_BIG)
    tok_n = lax.broadcasted_iota(jnp.int32, (n_rows, SAMPLE_PAD), 0) % SAMPLE_PAD
    col_n = lax.broadcasted_iota(jnp.int32, (n_rows, SAMPLE_PAD), 1)
    s_n = jnp.where((col_n <= tok_n) & (col_n < L), s_n, NEG_BIG)
    sk = jnp.concatenate([jnp.full((SAMPLE_PAD, 1), sinks_ref[h], F32) for h in range(N_HEADS_A)] * SAMPLE_GROUP,
                         axis=0)
    m = jnp.maximum(jnp.maximum(jnp.max(s_c, axis=-1, keepdims=True), jnp.max(s_n, axis=-1, keepdims=True)), sk)
    e_c = jnp.exp(s_c - m)
    e_n = jnp.exp(s_n - m)
    inv_den = 1.0 / (jnp.sum(e_c, axis=-1, keepdims=True) + jnp.sum(e_n, axis=-1, keepdims=True) + jnp.exp(sk - m))
    e_c = e_c.astype(BF16)
    e_n = e_n.astype(BF16)
    o = jnp.concatenate([_mm(e_c[i * HROWS:(i + 1) * HROWS], vc_ref[i]) + _mm(e_n[i * HROWS:(i + 1) * HROWS], vn[rows])
                         for i, rows in enumerate(seq_rows)], axis=0) * inv_den
    o_swapped = pltpu.roll(o, HEAD_DIM_A, axis=1)
    a_rows = []
    for i in range(SAMPLE_GROUP):
        blk = lambda arr, kv, g: arr[i * HROWS + (kv * Q_PER_KV + g) * SAMPLE_PAD:
                                     i * HROWS + (kv * Q_PER_KV + g + 1) * SAMPLE_PAD]
        slabs = []
        for kv in range(N_KV_A):
            for j in range(Q_PER_KV // 2):
                first = blk(o if kv == 0 else o_swapped, kv, 2 * j)
                second = blk(o_swapped if kv == 0 else o, kv, 2 * j + 1)
                slabs.append(jnp.where(lo8, first, second))
        a_rows.append(jnp.concatenate(slabs, axis=1))
    aout_ref[...] = jnp.concatenate(a_rows, axis=0)

    r_w = lax.broadcasted_iota(jnp.int32, (WINDOW, LANES), 0) % SUBLANES
    for i, rows in enumerate(seq_rows):
        for cache_ref, new, out_ref in ((kc_ref, kn, nk_ref), (vc_ref, vn, nv_ref)):
            cache = cache_ref[i]
            nxt = jnp.concatenate([cache[SUBLANES:], new[rows]], axis=0)
            out_ref[i] = _tile_roll(jnp.where(r_w >= L, cache, nxt), -L)

    row_r = lax.broadcasted_iota(jnp.int32, (R, D_STATE), 0)
    heads_per_group = N_HEADS_M // N_GROUPS_M
    yoff = []
    for i, rows in enumerate(seq_rows):
        state = st_ref[i]
        state_b = state.astype(BF16)
        a_tot = a_cs[i * SAMPLE_PAD + L - 1:i * SAMPLE_PAD + L, :]
        in_seq = (row_r >= i * SAMPLE_PAD) & (row_r < (i + 1) * SAMPLE_PAD)
        yo = []
        for g in range(N_GROUPS_M):
            gs = slice(g * GROUP_WIDTH, (g + 1) * GROUP_WIDTH)
            yo.append(_mm_nt(cm[rows, g * D_STATE:(g + 1) * D_STATE], state_b[gs]))
            bsel = jnp.where(in_seq, bm[:, g * D_STATE:(g + 1) * D_STATE], 0.0)
            upd = _mm(xd_t[gs, :], bsel)
            for hh in range(heads_per_group):
                h = g * heads_per_group + hh
                hs = slice(h * HEAD_DIM_M, (h + 1) * HEAD_DIM_M)
                decay = jnp.exp(jnp.broadcast_to(a_tot[:, h:h + 1], (HEAD_DIM_M, D_STATE)))
                nst_ref[i, hs, :] = state[hs] * decay + upd[hh * HEAD_DIM_M:(hh + 1) * HEAD_DIM_M]
        yoff.append(jnp.concatenate(yo, axis=1))
    yoff = jnp.concatenate(yoff, axis=0)

    y = (y + yoff * jnp.exp(acs_e)) * _silu(z_ref[...])
    outs = []
    for g in range(N_GROUPS_M):
        yg = y[:, g * GROUP_WIDTH:(g + 1) * GROUP_WIDTH]
        outs.append(yg * lax.rsqrt(jnp.mean(yg * yg, axis=-1, keepdims=True) + EPS))
    mout_ref[...] = jnp.concatenate(outs, axis=1) * ssmn_ref[...]


def _sample_mixer(sinks, qkv, z, xbc, dtraw, cprev, kc, vc, st, convw, convb, dtb, alog, dskip, ssmn):
    nseq = kc.shape[0]
    assert nseq % SAMPLE_GROUP == 0
    R = SAMPLE_GROUP * SAMPLE_PAD
    rows = nseq * SAMPLE_PAD
    consts = (convw, convb, dtb, alog, dskip, ssmn)
    row_spec = lambda w: pl.BlockSpec((R, w), lambda i: (i, 0))
    seq_spec = lambda a, b: pl.BlockSpec((SAMPLE_GROUP, a, b), lambda i: (i, 0, 0))
    in_specs = ([pl.BlockSpec(memory_space=pltpu.SMEM),
                 row_spec(QKV_WIDTH), row_spec(D_INNER), row_spec(CONV_DIM), row_spec(LANES), row_spec(CONV_DIM),
                 seq_spec(WINDOW, KV_WIDTH), seq_spec(WINDOW, KV_WIDTH), seq_spec(D_INNER, D_STATE)]
                + [_const_spec(a.shape) for a in consts])
    out_shape = (jax.ShapeDtypeStruct((rows, ATTN_WIDTH), F32), jax.ShapeDtypeStruct((rows, D_INNER), F32),
                 jax.ShapeDtypeStruct((nseq, WINDOW, KV_WIDTH), F32),
                 jax.ShapeDtypeStruct((nseq, WINDOW, KV_WIDTH), F32),
                 jax.ShapeDtypeStruct((rows, CONV_DIM), F32),
                 jax.ShapeDtypeStruct((nseq, D_INNER, D_STATE), F32))
    out_specs = (row_spec(ATTN_WIDTH), row_spec(D_INNER), seq_spec(WINDOW, KV_WIDTH), seq_spec(WINDOW, KV_WIDTH),
                 row_spec(CONV_DIM), seq_spec(D_INNER, D_STATE))
    return pl.pallas_call(
        _sample_mixer_kernel, grid=(nseq // SAMPLE_GROUP,), in_specs=in_specs, out_specs=out_specs,
        out_shape=out_shape, name="sample_mixer",
        compiler_params=pltpu.CompilerParams(dimension_semantics=("arbitrary",),
                                             vmem_limit_bytes=VMEM_LIMIT),
    )(sinks, qkv, z, xbc, dtraw, cprev, kc, vc, st, *consts)


def _out_proj_kernel(x_ref, a_ref, m_ref, g_ref, woa_ref, wob_ref, wo_ref, x1_ref):
    gates = g_ref[...]
    merged = (_sigmoid(gates[:, :D_MODEL]) * _mm(a_ref[...], woa_ref[...])
              + _sigmoid(gates[:, D_MODEL:]) * _mm(m_ref[...], wob_ref[...]))
    x1_ref[...] = x_ref[...] + _mm(merged, wo_ref[...])


def _out_proj(x, a_out, m_out, gates, woa, wob, wo):
    rows = x.shape[0]
    assert rows % ROW_TILE == 0
    consts = (woa, wob, wo)
    widths = (D_MODEL, ATTN_WIDTH, D_INNER, 2 * D_MODEL)
    return pl.pallas_call(
        _out_proj_kernel, grid=(rows // ROW_TILE,),
        in_specs=([pl.BlockSpec((ROW_TILE, w), lambda i: (i, 0)) for w in widths]
                  + [_const_spec(a.shape) for a in consts]),
        out_specs=pl.BlockSpec((ROW_TILE, D_MODEL), lambda i: (i, 0)),
        out_shape=jax.ShapeDtypeStruct((rows, D_MODEL), F32), name="sample_out_proj",
        compiler_params=pltpu.CompilerParams(dimension_semantics=("arbitrary",),
                                             vmem_limit_bytes=VMEM_LIMIT),
    )(x, a_out, m_out, gates, *consts)


def _layer_params(norm1, w_in, sinks, conv_w, conv_b, dt_bias, a_log, d_skip, ssm_norm, w_oa, w_ob, w_o,
                  norm2, w_up, w_down):
    off = [0]

    def take(width):
        piece = w_in[:, off[0]:off[0] + width]
        off[0] += width
        return piece

    wq, wk, wv = take(ATTN_WIDTH), take(KV_WIDTH), take(KV_WIDTH)
    wz, wxbc, wdt = take(D_INNER), take(CONV_DIM), take(N_HEADS_M)
    wga, wgb = take(D_MODEL), take(D_MODEL)
    pad_lanes = lambda a: jnp.pad(a, ((0, 0), (0, LANES - a.shape[1])))
    return dict(
        sinks=sinks.astype(F32), norm1=norm1[None, :],
        wqkv=jnp.concatenate([wq, wk, wv], axis=1).astype(BF16), wz=wz.astype(BF16),
        wxbc=wxbc.astype(BF16), wdt=pad_lanes(wdt).astype(BF16),
        wg=jnp.concatenate([wga, wgb], axis=1).astype(BF16),
        convw=conv_w, convb=conv_b[None, :], dtb=pad_lanes(dt_bias[None, :]), alog=pad_lanes(a_log[None, :]),
        dskip=jnp.repeat(d_skip, HEAD_DIM_M)[None, :], ssmn=ssm_norm[None, :],
        woa=w_oa.astype(BF16), wob=w_ob.astype(BF16), wo=w_o.astype(BF16),
        norm2=norm2[None, :], wup=w_up.astype(BF16), wdown=w_down.astype(BF16))


def kernel(x_prompt, x_sample, cache_swa_k, cache_swa_v, state_conv, state_ssm, norm1, w_in, sinks, conv_w,
           conv_b, dt_bias, a_log, d_skip, ssm_norm, w_oa, w_ob, w_o, norm2, w_up, w_down, final_norm):
    depth = w_in.shape[0]
    assert depth == 1
    nb, seq, _ = x_prompt.shape
    ns, ls, _ = x_sample.shape
    assert ls == 4
    p = _layer_params(norm1[0], w_in[0], sinks[0], conv_w[0], conv_b[0], dt_bias[0], a_log[0], d_skip[0],
                      ssm_norm[0], w_oa[0], w_ob[0], w_o[0], norm2[0], w_up[0], w_down[0])
    fnorm = final_norm[None, :]
    mixer_consts = (p["convw"], p["convb"], p["dtb"], p["alog"], p["dskip"], p["ssmn"])

    x1p, pk, pv, pc, pst = _prompt_mixer(
        x_prompt, p["sinks"], p["norm1"], p["wqkv"], p["wz"], p["wxbc"], p["wdt"], p["wg"],
        *mixer_consts, p["woa"], p["wob"], p["wo"])
    y_prompt = _mlp(x1p, p["norm2"], p["wup"], p["wdown"], fnorm)
    y_prompt = y_prompt.reshape(nb, seq, D_MODEL)

    xs_pad = jnp.pad(x_sample, ((0, 0), (0, SAMPLE_PAD - ls), (0, 0))).reshape(ns * SAMPLE_PAD, D_MODEL)
    cprev = jnp.pad(state_conv[0], ((0, 0), (SAMPLE_PAD - (CONV_W - 1), 0), (0, 0)))
    cprev = cprev.reshape(ns * SAMPLE_PAD, CONV_DIM)
    qkv, z, xbc, dtraw, gates = _in_proj(xs_pad, p["norm1"], p["wqkv"], p["wz"], p["wxbc"], p["wdt"], p["wg"])
    a_out, m_out, sk, sv, sc, sst = _sample_mixer(
        p["sinks"], qkv, z, xbc, dtraw, cprev,
        cache_swa_k[0].reshape(ns, WINDOW, KV_WIDTH), cache_swa_v[0].reshape(ns, WINDOW, KV_WIDTH),
        state_ssm[0].reshape(ns, D_INNER, D_STATE), *mixer_consts)
    x1s = _out_proj(xs_pad, a_out, m_out, gates, p["woa"], p["wob"], p["wo"])
    y_sample = _mlp(x1s, p["norm2"], p["wup"], p["wdown"], fnorm)
    y_sample = y_sample.reshape(ns, SAMPLE_PAD, D_MODEL)[:, :ls]
    sc = sc.reshape(ns, SAMPLE_PAD, CONV_DIM)[:, :CONV_W - 1]

    kv_shape = (1, -1, WINDOW, N_KV_A, HEAD_DIM_A)
    ssm_shape = (1, -1, N_HEADS_M, HEAD_DIM_M, D_STATE)
    return (y_prompt, y_sample,
            pk.reshape(kv_shape), pv.reshape(kv_shape), pc[None], pst.reshape(ssm_shape),
            sk.reshape(kv_shape), sv.reshape(kv_shape), sc[None], sst.reshape(ssm_shape))
```

```python
import collections
import functools

import jax
import jax.numpy as jnp
from jax import lax
from jax.experimental import pallas as pl
from jax.experimental.pallas import tpu as pltpu

F32 = jnp.float32
BF16 = jnp.bfloat16

D_MODEL = 1024
N_HEADS_A = 8
N_KV_A = 2
Q_PER_KV = N_HEADS_A // N_KV_A
HEAD_DIM_A = 64
WINDOW = 128
ATTN_WIDTH = N_HEADS_A * HEAD_DIM_A
KV_WIDTH = N_KV_A * HEAD_DIM_A
D_INNER = 1024
HEAD_DIM_M = 64
N_HEADS_M = D_INNER // HEAD_DIM_M
N_GROUPS_M = 2
GROUP_WIDTH = D_INNER // N_GROUPS_M
D_STATE = 128
CONV_W = 4
CONV_DIM = D_INNER + 2 * N_GROUPS_M * D_STATE
CHUNK = 128
D_FF = 4 * D_MODEL
EPS = 1e-6

LANES = 128
SUBLANES = 8
QKV_WIDTH = ATTN_WIDTH + 2 * KV_WIDTH
NEG_BIG = -1e30
VMEM_LIMIT = 56 * 1024 * 1024
SAMPLE_PAD = SUBLANES
SAMPLE_GROUP = 16


def _mm(a, b):
    return jnp.dot(a.astype(BF16), b.astype(BF16), preferred_element_type=F32)


def _mm_nt(a, b):
    return lax.dot_general(a.astype(BF16), b.astype(BF16), (((1,), (1,)), ((), ())),
                           preferred_element_type=F32)


def _rms(x, w):
    return x * lax.rsqrt(jnp.mean(x * x, axis=-1, keepdims=True) + EPS) * w


def _sigmoid(x):
    return 0.5 + 0.5 * jnp.tanh(0.5 * x)


def _silu(x):
    h = 0.5 * x
    return h + h * jnp.tanh(h)


def _softplus(x):
    return jnp.maximum(x, 0.0) + jnp.log(1.0 + jnp.exp(-jnp.abs(x)))


def _lane_lo(shape):
    return (lax.broadcasted_iota(jnp.int32, shape, len(shape) - 1) % LANES) < HEAD_DIM_A


def _dup_half(x, lo, first):
    xr = pltpu.roll(x, HEAD_DIM_A, axis=1)
    return jnp.where(lo, x, xr) if first else jnp.where(lo, xr, x)


def _stack_heads(q, kv, lo):
    qa = q[:, kv * 2 * LANES: kv * 2 * LANES + LANES]
    qb = q[:, kv * 2 * LANES + LANES: (kv + 1) * 2 * LANES]
    zero = jnp.zeros_like(qa)
    return jnp.concatenate([jnp.where(lo, qa, zero), jnp.where(lo, zero, qa),
                            jnp.where(lo, qb, zero), jnp.where(lo, zero, qb)], axis=0)


def _unstack_heads(o, rows, lo):
    return jnp.concatenate([jnp.where(lo, o[0:rows], o[rows:2 * rows]),
                            jnp.where(lo, o[2 * rows:3 * rows], o[3 * rows:4 * rows])], axis=1)


ProjBufs = collections.namedtuple("ProjBufs", "qkv z xbc dt")
MixedBufs = collections.namedtuple("MixedBufs", "attn ssd")
Carry = collections.namedtuple("Carry", "kprev vprev conv_tail ht dt_buf")
MixerWeights = collections.namedtuple(
    "MixerWeights", "norm1 win wdt convw convb dtb alog dskip ssmn woa wob wo")

PIECE = 256
QKV_PIECE0 = 0
Z_PIECE0 = QKV_PIECE0 + QKV_WIDTH // PIECE
XBC_PIECE0 = Z_PIECE0 + D_INNER // PIECE
GATE_PIECE0 = XBC_PIECE0 + CONV_DIM // PIECE
N_IN_PIECES = GATE_PIECE0 + 2 * D_MODEL // PIECE
StateOuts = collections.namedtuple("StateOuts", "k v conv ssm")


def _split3_bf16(x):
    hi = x.astype(BF16)
    r1 = x - hi.astype(F32)
    mid = r1.astype(BF16)
    return hi, mid, (r1 - mid.astype(F32)).astype(BF16)


def _piece_cols(p):
    return slice(p * PIECE, (p + 1) * PIECE)


def _in_proj_pieces(hb, w, dsts):
    for dst, first in zip(dsts, (QKV_PIECE0, Z_PIECE0, XBC_PIECE0)):
        for p in range(dst.shape[1] // PIECE):
            dst[:, _piece_cols(p)] = _mm(hb, w.win[first + p])
            yield


def _out_proj_pieces(x_ref, rows, hb, attn_ref, ssd_ref, w, out_ref):
    merged = []
    for half, (src_ref, w_ref) in enumerate(((attn_ref, w.woa), (ssd_ref, w.wob))):
        parts = []
        for p in range(D_MODEL // PIECE):
            gate = _sigmoid(_mm(hb, w.win[GATE_PIECE0 + half * (D_MODEL // PIECE) + p]))
            yield
            parts.append(gate * _mm(src_ref[...], w_ref[p]))
            yield
        merged.append(jnp.concatenate(parts, axis=1))
    merged = (merged[0] + merged[1]).astype(BF16)
    for p in range(D_MODEL // PIECE):
        out_ref[rows, _piece_cols(p)] = x_ref[rows, _piece_cols(p)] + _mm(merged, w.wo[p])
        yield


def _stage_in_proj(x_ref, rows, w, proj, hnorm_ref):
    hb = _rms(x_ref[rows, :], w.norm1[...]).astype(BF16)
    hnorm_ref[...] = hb
    yield
    yield from _in_proj_pieces(hb, w, (proj.qkv, proj.z, proj.xbc))
    proj.dt[...] = _mm(hb, w.wdt[...])
    yield


def _stage_out_proj(x_ref, rows, mixed, w, x1_ref, hnorm_ref):
    hb = hnorm_ref[...]
    yield
    yield from _out_proj_pieces(x_ref, rows, hb, mixed.attn, mixed.ssd, w, x1_ref)


def _alternate(*gens):
    gens = list(gens)
    while gens:
        for g in list(gens):
            try:
                next(g)
                yield
            except StopIteration:
                gens.remove(g)


FILL_PER_PIECE = 1


def _trace_interleaved(primary, filler):
    for _ in primary:
        for _ in range(FILL_PER_PIECE):
            next(filler, None)
    for _ in filler:
        pass


def _stage_mix(proj, mixed, rows, carry, w, sinks_ref, first):
    T = CHUNK
    reset = (lambda a: a) if first is None else (lambda a: jnp.where(first, 0.0, a))

    qkv = proj.qkv[rows, :]
    q = qkv[:, :ATTN_WIDTH] * (HEAD_DIM_A ** -0.5)
    k = qkv[:, ATTN_WIDTH:ATTN_WIDTH + KV_WIDTH]
    v = qkv[:, ATTN_WIDTH + KV_WIDTH:]
    kk = jnp.concatenate([carry.kprev[...], k], axis=0)
    vv = jnp.concatenate([carry.vprev[...], v], axis=0)
    carry.kprev[...] = k
    carry.vprev[...] = v

    lo = _lane_lo((T, LANES))
    lo2 = _lane_lo((2 * T, LANES))
    r = lax.broadcasted_iota(jnp.int32, (T, 2 * T), 0)
    col = lax.broadcasted_iota(jnp.int32, (T, 2 * T), 1)
    valid = (col >= r) & (col <= r + WINDOW)
    if first is not None:
        valid = valid & (col >= jnp.where(first, T, 0))
    a_slabs = []
    for kv in range(N_KV_A):
        kd = _dup_half(kk, lo2, kv == 0).astype(BF16)
        vd = _dup_half(vv, lo2, kv == 0).astype(BF16)
        s = _mm_nt(_stack_heads(q, kv, lo), kd)
        es, inv = [], []
        for g in range(Q_PER_KV):
            sk = sinks_ref[kv * Q_PER_KV + g]
            sg = jnp.where(valid, s[g * T:(g + 1) * T], NEG_BIG)
            m = jnp.maximum(jnp.max(sg, axis=-1, keepdims=True), sk)
            e = jnp.exp(sg - m)
            den = jnp.sum(e, axis=-1, keepdims=True) + jnp.exp(sk - m)
            es.append(e.astype(BF16))
            inv.append(1.0 / den)
        yield
        o = _mm(jnp.concatenate(es, axis=0), vd) * jnp.concatenate(inv, axis=0)
        a_slabs.append(_unstack_heads(o, T, lo))
        yield
    mixed.attn[rows, :] = jnp.concatenate(a_slabs, axis=1).astype(BF16)

    xr = proj.xbc[rows, :]
    prev_tile = jnp.concatenate([reset(carry.conv_tail[...]), xr[:T - SUBLANES]], axis=0)
    carry.conv_tail[...] = xr[T - SUBLANES:]
    rm = lax.broadcasted_iota(jnp.int32, (T, 1), 0) % SUBLANES
    yc = xr * w.convw[CONV_W - 1:CONV_W, :]
    for shift in range(1, CONV_W):
        shifted = _tile_roll(jnp.where(rm >= SUBLANES - shift, prev_tile, xr), shift)
        yc = yc + shifted * w.convw[CONV_W - 1 - shift:CONV_W - shift, :]
        yield
    xbc = _silu(yc + w.convb[...])
    xs = xbc[:, :D_INNER]
    bm = xbc[:, D_INNER:D_INNER + N_GROUPS_M * D_STATE]
    cm = xbc[:, D_INNER + N_GROUPS_M * D_STATE:]

    z = proj.z[rows, :]
    carry.dt_buf[...] = _softplus(proj.dt[rows, :] + w.dtb[...])
    dt = carry.dt_buf[...]
    rr = lax.broadcasted_iota(jnp.int32, (T, T), 0)
    cc = lax.broadcasted_iota(jnp.int32, (T, T), 1)
    tril = cc <= rr
    cs = jnp.dot(jnp.where(tril, 1.0, 0.0).astype(BF16),
                 jnp.concatenate(_split3_bf16(dt * -jnp.exp(w.alog[...])), axis=1), preferred_element_type=F32)
    a_cs = cs[:, :LANES] + cs[:, LANES:2 * LANES] + cs[:, 2 * LANES:]
    a_cs_t = a_cs.T
    yield

    ys = []
    pairs_per_group = N_HEADS_M // 2 // N_GROUPS_M
    for g in range(N_GROUPS_M):
        gcols = slice(g * GROUP_WIDTH, (g + 1) * GROUP_WIDTH)
        cg = cm[:, g * D_STATE:(g + 1) * D_STATE]
        bg = bm[:, g * D_STATE:(g + 1) * D_STATE]
        cb = _mm_nt(cg, bg)
        h_prev = reset(carry.ht[:, gcols])
        y_off = _mm(cg, h_prev)
        xds, decays = [], []
        for jj in range(pairs_per_group):
            j = g * pairs_per_group + jj
            sl = slice(j * LANES, (j + 1) * LANES)
            ws, colbs, dtbs = [], [], []
            for h in (2 * j, 2 * j + 1):
                colb = jnp.broadcast_to(a_cs[:, h:h + 1], (T, T))
                rowb = jnp.broadcast_to(a_cs_t[h:h + 1, :], (T, T))
                seg = jnp.where(tril, jnp.exp(colb - rowb), 0.0)
                ws.append((cb * seg).astype(BF16))
                colbs.append(colb)
                dtbs.append(jnp.broadcast_to(dt[:, h:h + 1], (T, LANES)))
            dt_e = jnp.where(lo, dtbs[0], dtbs[1])
            acs_e = jnp.where(lo, colbs[0], colbs[1])
            xs_j = xs[:, sl]
            xdt = xs_j * dt_e
            zero = jnp.zeros_like(xdt)
            rhs = jnp.concatenate([jnp.where(lo, xdt, zero), jnp.where(lo, zero, xdt)], axis=0)
            y_diag = _mm(jnp.concatenate(ws, axis=1), rhs)
            alast = acs_e[T - 1:T, :]
            xds.append(xdt * jnp.exp(alast - acs_e))
            decays.append(jnp.exp(alast))
            y = y_diag + y_off[:, jj * LANES:(jj + 1) * LANES] * jnp.exp(acs_e) + xs_j * w.dskip[:, sl]
            ys.append(y * _silu(z[:, sl]))
            yield
        carry.ht[:, gcols] = (h_prev * jnp.concatenate(decays, axis=1)
                              + _mm(bg.T, jnp.concatenate(xds, axis=1)))
        yield

    m_slabs = []
    for g in range(N_GROUPS_M):
        grp = ys[g * pairs_per_group:(g + 1) * pairs_per_group]
        ssq = grp[0] * grp[0]
        for y in grp[1:]:
            ssq = ssq + y * y
        scale = lax.rsqrt(jnp.sum(ssq, axis=-1, keepdims=True) * (1.0 / GROUP_WIDTH) + EPS)
        m_slabs.extend([y * scale for y in grp])
    mixed.ssd[rows, :] = (jnp.concatenate(m_slabs, axis=1) * w.ssmn[...]).astype(BF16)


def _chain(*gens):
    for g in gens:
        yield from g


def _prompt_mixer_kernel(sinks_ref, x_in_ref, x_res_ref, *refs, chunks_per_seq, n_pairs):
    refs = list(refs)
    take = lambda n: [refs.pop(0) for _ in range(n)]
    w = MixerWeights(*take(len(MixerWeights._fields)))
    x1_ref, = take(1)
    outs = StateOuts(*take(4))
    proj = (ProjBufs(*take(4)), ProjBufs(*take(4)))
    mixed = (MixedBufs(*take(2)), MixedBufs(*take(2)))
    hnorm = take(2)
    carry = Carry(*take(len(Carry._fields)))
    assert not refs
    T = CHUNK
    s = pl.program_id(0)
    every = slice(None)

    @pl.when(s == 0)
    def _():
        for ref in list(proj[1] + mixed[0] + mixed[1] + carry) + hnorm:
            ref[...] = jnp.zeros_like(ref)

    first = (s - 1) % (chunks_per_seq // 2) == 0

    def step(cur):
        prv = 1 - cur
        dense = _alternate(_stage_out_proj(x_res_ref, every, mixed[cur], w, x1_ref, hnorm[cur]),
                           _stage_in_proj(x_in_ref, every, w, proj[cur], hnorm[cur]))
        mix = _chain(_stage_mix(proj[prv], mixed[prv], slice(0, T), carry, w, sinks_ref, first),
                     _stage_mix(proj[prv], mixed[prv], slice(T, 2 * T), carry, w, sinks_ref, None))
        _trace_interleaved(mix, dense)

    for parity in (0, 1):
        pl.when(s % 2 == parity)(functools.partial(step, parity))

    @pl.when(s <= n_pairs)
    def _():
        outs.k[0] = carry.kprev[...]
        outs.v[0] = carry.vprev[...]
        outs.conv[0] = carry.conv_tail[SUBLANES - (CONV_W - 1):, :]
        outs.ssm[0] = carry.ht[...].T


def _const_spec(shape):
    return pl.BlockSpec(shape, lambda *_: (0,) * len(shape))


def _prompt_mixer(x, sinks, weights):
    nb, seq, _ = x.shape
    assert seq % (2 * CHUNK) == 0
    chunks_per_seq = seq // CHUNK
    n_pairs = nb * chunks_per_seq // 2
    pair = 2 * CHUNK
    consts = tuple(weights)
    seq_of_mix = lambda s: jnp.clip((2 * s - 1) // chunks_per_seq, 0, nb - 1)
    in_specs = ([pl.BlockSpec(memory_space=pltpu.SMEM),
                 pl.BlockSpec((pair, D_MODEL), lambda s: (jnp.minimum(s, n_pairs - 1), 0)),
                 pl.BlockSpec((pair, D_MODEL), lambda s: (jnp.maximum(s - 2, 0), 0))]
                + [_const_spec(a.shape) for a in consts])
    out_shape = (jax.ShapeDtypeStruct((nb * seq, D_MODEL), F32),
                 jax.ShapeDtypeStruct((nb, WINDOW, KV_WIDTH), F32),
                 jax.ShapeDtypeStruct((nb, WINDOW, KV_WIDTH), F32),
                 jax.ShapeDtypeStruct((nb, CONV_W - 1, CONV_DIM), F32),
                 jax.ShapeDtypeStruct((nb, D_INNER, D_STATE), F32))
    out_specs = (pl.BlockSpec((pair, D_MODEL), lambda s: (jnp.maximum(s - 2, 0), 0)),
                 pl.BlockSpec((1, WINDOW, KV_WIDTH), lambda s: (seq_of_mix(s), 0, 0)),
                 pl.BlockSpec((1, WINDOW, KV_WIDTH), lambda s: (seq_of_mix(s), 0, 0)),
                 pl.BlockSpec((1, CONV_W - 1, CONV_DIM), lambda s: (seq_of_mix(s), 0, 0)),
                 pl.BlockSpec((1, D_INNER, D_STATE), lambda s: (seq_of_mix(s), 0, 0)))
    proj_bufs = [pltpu.VMEM((pair, QKV_WIDTH), F32), pltpu.VMEM((pair, D_INNER), F32),
                 pltpu.VMEM((pair, CONV_DIM), F32), pltpu.VMEM((pair, LANES), F32)]
    mixed_bufs = [pltpu.VMEM((pair, ATTN_WIDTH), BF16), pltpu.VMEM((pair, D_INNER), BF16)]
    hnorm_bufs = [pltpu.VMEM((pair, D_MODEL), BF16)] * 2
    carry = [pltpu.VMEM((CHUNK, KV_WIDTH), F32), pltpu.VMEM((CHUNK, KV_WIDTH), F32),
             pltpu.VMEM((SUBLANES, CONV_DIM), F32), pltpu.VMEM((D_STATE, D_INNER), F32),
             pltpu.VMEM((CHUNK, LANES), F32)]
    x_rows = x.reshape(nb * seq, D_MODEL)
    return pl.pallas_call(
        functools.partial(_prompt_mixer_kernel, chunks_per_seq=chunks_per_seq, n_pairs=n_pairs),
        grid=(n_pairs + 2,), in_specs=in_specs, out_specs=out_specs,
        out_shape=out_shape, scratch_shapes=proj_bufs * 2 + mixed_bufs * 2 + hnorm_bufs + carry,
        name="prompt_mixer",
        compiler_params=pltpu.CompilerParams(dimension_semantics=("arbitrary",),
                                             vmem_limit_bytes=VMEM_LIMIT),
    )(sinks, x_rows, x_rows, *consts)


MLP_TILE = 512
FF_BLOCK = 1024


def _mlp_kernel(x_ref, norm2_ref, wup_ref, wdown_ref, fnorm_ref, y_ref):
    x = x_ref[...]
    hm = _rms(x, norm2_ref[...]).astype(BF16)
    acc = x
    for j in range(D_FF // FF_BLOCK):
        h = _mm(hm, wup_ref[:, j * FF_BLOCK:(j + 1) * FF_BLOCK])
        h = jnp.square(jnp.maximum(h, 0.0))
        acc = acc + _mm(h, wdown_ref[j * FF_BLOCK:(j + 1) * FF_BLOCK, :])
    y_ref[...] = _rms(acc, fnorm_ref[...])


def _mlp(x, norm2, wup, wdown, fnorm):
    rows = x.shape[0]
    assert rows % MLP_TILE == 0
    consts = (norm2, wup, wdown, fnorm)
    return pl.pallas_call(
        _mlp_kernel, grid=(rows // MLP_TILE,),
        in_specs=[pl.BlockSpec((MLP_TILE, D_MODEL), lambda i: (i, 0))] + [_const_spec(a.shape) for a in consts],
        out_specs=pl.BlockSpec((MLP_TILE, D_MODEL), lambda i: (i, 0)),
        out_shape=jax.ShapeDtypeStruct((rows, D_MODEL), F32), name="mlp",
        compiler_params=pltpu.CompilerParams(dimension_semantics=("arbitrary",),
                                             vmem_limit_bytes=VMEM_LIMIT),
    )(x, *consts)


ROW_TILE = 256


def _mixer_weights(**refs):
    return MixerWeights(**{f: refs.get(f) for f in MixerWeights._fields})


def _in_proj_kernel(x_ref, norm1_ref, win_ref, wdt_ref, qkv_ref, z_ref, xbc_ref, dt_ref, hnorm_ref):
    hb = _rms(x_ref[...], norm1_ref[...]).astype(BF16)
    hnorm_ref[...] = hb
    for _ in _in_proj_pieces(hb, _mixer_weights(win=win_ref), (qkv_ref, z_ref, xbc_ref)):
        pass
    dt_ref[...] = _mm(hb, wdt_ref[...])


def _in_proj(x, norm1, win, wdt):
    rows = x.shape[0]
    assert rows % ROW_TILE == 0
    consts = (norm1, win, wdt)
    outs = ((QKV_WIDTH, F32), (D_INNER, F32), (CONV_DIM, F32), (LANES, F32), (D_MODEL, BF16))
    return pl.pallas_call(
        _in_proj_kernel, grid=(rows // ROW_TILE,),
        in_specs=[pl.BlockSpec((ROW_TILE, D_MODEL), lambda i: (i, 0))] + [_const_spec(a.shape) for a in consts],
        out_specs=tuple(pl.BlockSpec((ROW_TILE, width), lambda i: (i, 0)) for width, _ in outs),
        out_shape=tuple(jax.ShapeDtypeStruct((rows, width), dtype) for width, dtype in outs),
        name="sample_in_proj",
        compiler_params=pltpu.CompilerParams(dimension_semantics=("arbitrary",),
                                             vmem_limit_bytes=VMEM_LIMIT),
    )(x, *consts)


def _tile_roll(x, shift):
    rows, width = x.shape
    tiles = x.reshape(rows // SUBLANES, SUBLANES, width)
    return pltpu.roll(tiles, shift % SUBLANES, axis=1).reshape(rows, width)


def _expand_heads(cols, expand):
    terms = []
    for c in cols:
        hi = c.astype(BF16)
        r1 = c - hi.astype(F32)
        mid = r1.astype(BF16)
        lo = (r1 - mid.astype(F32)).astype(BF16)
        terms.extend([hi, mid, lo])
    rows = cols[0].shape[0]
    out = jnp.dot(jnp.concatenate(terms, axis=0), expand, preferred_element_type=F32)
    return [out[(3 * i) * rows:(3 * i + 1) * rows] + out[(3 * i + 1) * rows:(3 * i + 2) * rows]
            + out[(3 * i + 2) * rows:(3 * i + 3) * rows] for i in range(len(cols))]


def _sample_mixer_kernel(sinks_ref, qkv_ref, z_ref, xbc_ref, dtraw_ref, cprev_ref, kc_ref, vc_ref, st_ref,
                         convw_ref, convb_ref, dtb_ref, alog_ref, dskip_ref, ssmn_ref,
                         aout_ref, mout_ref, nk_ref, nv_ref, nconv_ref, nst_ref):
    R = SAMPLE_GROUP * SAMPLE_PAD
    L = 4
    rm = lax.broadcasted_iota(jnp.int32, (R, 1), 0) % SAMPLE_PAD

    xr = xbc_raw = xbc_ref[...]
    with_prev = jnp.where(rm >= SAMPLE_PAD - (CONV_W - 1), cprev_ref[...], xr)
    yc = xr * convw_ref[CONV_W - 1:CONV_W, :]
    for kshift in range(1, CONV_W):
        yc = yc + _tile_roll(with_prev, kshift) * convw_ref[CONV_W - 1 - kshift:CONV_W - kshift, :]
    nconv_ref[...] = _tile_roll(xbc_raw, -1)
    xbc = _silu(yc + convb_ref[...])
    xs = xbc[:, :D_INNER]
    bm = xbc[:, D_INNER:D_INNER + N_GROUPS_M * D_STATE]
    cm = xbc[:, D_INNER + N_GROUPS_M * D_STATE:]

    dt = _softplus(dtraw_ref[...] + dtb_ref[...])
    d_a = dt * (-jnp.exp(alog_ref[...]))
    a_cs = d_a
    suf = jnp.zeros_like(d_a)
    for kshift in range(1, L):
        a_cs = a_cs + jnp.where(rm >= kshift, _tile_roll(d_a, kshift), 0.0)
        suf = suf + jnp.where(rm <= L - 1 - kshift, _tile_roll(d_a, -kshift), 0.0)

    expand = (lax.broadcasted_iota(jnp.int32, (LANES, D_INNER), 1) // HEAD_DIM_M
              == lax.broadcasted_iota(jnp.int32, (LANES, D_INNER), 0)).astype(BF16)
    dt_e, acs_e, suf_e = _expand_heads([dt, a_cs, suf], expand)
    xdt = xs * dt_e
    real = rm < L
    xd_t = jnp.where(real, xdt * jnp.exp(suf_e), 0.0).T.astype(BF16)

    y = xs * dskip_ref[...]
    for kshift in range(L):
        bk = bm if kshift == 0 else _tile_roll(bm, kshift)
        cb = cm * bk
        cb_e = jnp.concatenate(
            [jnp.broadcast_to(jnp.sum(cb[:, g * D_STATE:(g + 1) * D_STATE], axis=-1, keepdims=True),
                              (R, GROUP_WIDTH)) for g in range(N_GROUPS_M)], axis=1)
        if kshift == 0:
            y = y + cb_e * xdt
        else:
            seg = jnp.exp(acs_e - _tile_roll(acs_e, kshift))
            y = y + jnp.where(rm >= kshift, cb_e * seg * _tile_roll(xdt, kshift), 0.0)

    qkv = qkv_ref[...]
    q = qkv[:, :ATTN_WIDTH] * (HEAD_DIM_A ** -0.5)
    q_swapped = jnp.concatenate(
        [pltpu.roll(q[:, s * LANES:(s + 1) * LANES], HEAD_DIM_A, axis=1) for s in range(ATTN_WIDTH // LANES)], axis=1)
    kn = qkv[:, ATTN_WIDTH:ATTN_WIDTH + KV_WIDTH]
    vn = qkv[:, ATTN_WIDTH + KV_WIDTH:]
    lo8 = _lane_lo((SAMPLE_PAD, LANES))
    zero8 = jnp.zeros((SAMPLE_PAD, LANES), F32)
    seq_rows = [slice(i * SAMPLE_PAD, (i + 1) * SAMPLE_PAD) for i in range(SAMPLE_GROUP)]
    HROWS = N_HEADS_A * SAMPLE_PAD

    s_c, s_n = [], []
    for rows in seq_rows:
        pieces = []
        for kv in range(N_KV_A):
            for g in range(Q_PER_KV):
                slab = kv * (Q_PER_KV // 2) + g // 2
                src = q if g % 2 == kv else q_swapped
                piece = src[rows, slab * LANES:(slab + 1) * LANES]
                pieces.append(jnp.where(lo8, piece, zero8) if kv == 0 else jnp.where(lo8, zero8, piece))
        lhs = jnp.concatenate(pieces, axis=0).astype(BF16)
        i = len(s_c)
        s_c.append(_mm_nt(lhs, kc_ref[i]))
        s_n.append(_mm_nt(lhs, kn[rows]))
    s_c = jnp.concatenate(s_c, axis=0)
    s_n = jnp.concatenate(s_n, axis=0)
    n_rows = SAMPLE_GROUP * HROWS
    tok_c = lax.broadcasted_iota(jnp.int32, (n_rows, WINDOW), 0) % SAMPLE_PAD
    s_c = jnp.where(lax.broadcasted_iota(jnp.int32, (n_rows, WINDOW), 1) >= tok_c, s_c, NEG_BIG)
    tok_n = lax.broadcasted_iota(jnp.int32, (n_rows, SAMPLE_PAD), 0) % SAMPLE_PAD
    col_n = lax.broadcasted_iota(jnp.int32, (n_rows, SAMPLE_PAD), 1)
    s_n = jnp.where((col_n <= tok_n) & (col_n < L), s_n, NEG_BIG)
    sk = jnp.concatenate([jnp.full((SAMPLE_PAD, 1), sinks_ref[h], F32) for h in range(N_HEADS_A)] * SAMPLE_GROUP,
                         axis=0)
    m = jnp.maximum(jnp.maximum(jnp.max(s_c, axis=-1, keepdims=True), jnp.max(s_n, axis=-1, keepdims=True)), sk)
    e_c = jnp.exp(s_c - m)
    e_n = jnp.exp(s_n - m)
    inv_den = 1.0 / (jnp.sum(e_c, axis=-1, keepdims=True) + jnp.sum(e_n, axis=-1, keepdims=True) + jnp.exp(sk - m))
    e_c = e_c.astype(BF16)
    e_n = e_n.astype(BF16)
    o = jnp.concatenate([_mm(e_c[i * HROWS:(i + 1) * HROWS], vc_ref[i]) + _mm(e_n[i * HROWS:(i + 1) * HROWS], vn[rows])
                         for i, rows in enumerate(seq_rows)], axis=0) * inv_den
    o_swapped = pltpu.roll(o, HEAD_DIM_A, axis=1)
    a_rows = []
    for i in range(SAMPLE_GROUP):
        blk = lambda arr, kv, g: arr[i * HROWS + (kv * Q_PER_KV + g) * SAMPLE_PAD:
                                     i * HROWS + (kv * Q_PER_KV + g + 1) * SAMPLE_PAD]
        slabs = []
        for kv in range(N_KV_A):
            for j in range(Q_PER_KV // 2):
                first = blk(o if kv == 0 else o_swapped, kv, 2 * j)
                second = blk(o_swapped if kv == 0 else o, kv, 2 * j + 1)
                slabs.append(jnp.where(lo8, first, second))
        a_rows.append(jnp.concatenate(slabs, axis=1))
    aout_ref[...] = jnp.concatenate(a_rows, axis=0)

    r_w = lax.broadcasted_iota(jnp.int32, (WINDOW, LANES), 0) % SUBLANES
    for i, rows in enumerate(seq_rows):
        for cache_ref, new, out_ref in ((kc_ref, kn, nk_ref), (vc_ref, vn, nv_ref)):
            cache = cache_ref[i]
            nxt = jnp.concatenate([cache[SUBLANES:], new[rows]], axis=0)
            out_ref[i] = _tile_roll(jnp.where(r_w >= L, cache, nxt), -L)

    row_r = lax.broadcasted_iota(jnp.int32, (R, D_STATE), 0)
    heads_per_group = N_HEADS_M // N_GROUPS_M
    yoff = []
    for i, rows in enumerate(seq_rows):
        state = st_ref[i]
        state_b = state.astype(BF16)
        a_tot = a_cs[i * SAMPLE_PAD + L - 1:i * SAMPLE_PAD + L, :]
        in_seq = (row_r >= i * SAMPLE_PAD) & (row_r < (i + 1) * SAMPLE_PAD)
        yo = []
        for g in range(N_GROUPS_M):
            gs = slice(g * GROUP_WIDTH, (g + 1) * GROUP_WIDTH)
            yo.append(_mm_nt(cm[rows, g * D_STATE:(g + 1) * D_STATE], state_b[gs]))
            bsel = jnp.where(in_seq, bm[:, g * D_STATE:(g + 1) * D_STATE], 0.0)
            upd = _mm(xd_t[gs, :], bsel)
            for hh in range(heads_per_group):
                h = g * heads_per_group + hh
                hs = slice(h * HEAD_DIM_M, (h + 1) * HEAD_DIM_M)
                decay = jnp.exp(jnp.broadcast_to(a_tot[:, h:h + 1], (HEAD_DIM_M, D_STATE)))
                nst_ref[i, hs, :] = state[hs] * decay + upd[hh * HEAD_DIM_M:(hh + 1) * HEAD_DIM_M]
        yoff.append(jnp.concatenate(yo, axis=1))
    yoff = jnp.concatenate(yoff, axis=0)

    y = (y + yoff * jnp.exp(acs_e)) * _silu(z_ref[...])
    outs = []
    for g in range(N_GROUPS_M):
        yg = y[:, g * GROUP_WIDTH:(g + 1) * GROUP_WIDTH]
        outs.append(yg * lax.rsqrt(jnp.mean(yg * yg, axis=-1, keepdims=True) + EPS))
    mout_ref[...] = jnp.concatenate(outs, axis=1) * ssmn_ref[...]


def _sample_mixer(sinks, qkv, z, xbc, dtraw, cprev, kc, vc, st, convw, convb, dtb, alog, dskip, ssmn):
    nseq = kc.shape[0]
    assert nseq % SAMPLE_GROUP == 0
    R = SAMPLE_GROUP * SAMPLE_PAD
    rows = nseq * SAMPLE_PAD
    consts = (convw, convb, dtb, alog, dskip, ssmn)
    row_spec = lambda w: pl.BlockSpec((R, w), lambda i: (i, 0))
    seq_spec = lambda a, b: pl.BlockSpec((SAMPLE_GROUP, a, b), lambda i: (i, 0, 0))
    in_specs = ([pl.BlockSpec(memory_space=pltpu.SMEM),
                 row_spec(QKV_WIDTH), row_spec(D_INNER), row_spec(CONV_DIM), row_spec(LANES), row_spec(CONV_DIM),
                 seq_spec(WINDOW, KV_WIDTH), seq_spec(WINDOW, KV_WIDTH), seq_spec(D_INNER, D_STATE)]
                + [_const_spec(a.shape) for a in consts])
    out_shape = (jax.ShapeDtypeStruct((rows, ATTN_WIDTH), F32), jax.ShapeDtypeStruct((rows, D_INNER), F32),
                 jax.ShapeDtypeStruct((nseq, WINDOW, KV_WIDTH), F32),
                 jax.ShapeDtypeStruct((nseq, WINDOW, KV_WIDTH), F32),
                 jax.ShapeDtypeStruct((rows, CONV_DIM), F32),
                 jax.ShapeDtypeStruct((nseq, D_INNER, D_STATE), F32))
    out_specs = (row_spec(ATTN_WIDTH), row_spec(D_INNER), seq_spec(WINDOW, KV_WIDTH), seq_spec(WINDOW, KV_WIDTH),
                 row_spec(CONV_DIM), seq_spec(D_INNER, D_STATE))
    return pl.pallas_call(
        _sample_mixer_kernel, grid=(nseq // SAMPLE_GROUP,), in_specs=in_specs, out_specs=out_specs,
        out_shape=out_shape, name="sample_mixer",
        compiler_params=pltpu.CompilerParams(dimension_semantics=("arbitrary",),
                                             vmem_limit_bytes=VMEM_LIMIT),
    )(sinks, qkv, z, xbc, dtraw, cprev, kc, vc, st, *consts)


def _out_proj_kernel(x_ref, a_ref, m_ref, hnorm_ref, win_ref, woa_ref, wob_ref, wo_ref, x1_ref):
    w = _mixer_weights(win=win_ref, woa=woa_ref, wob=wob_ref, wo=wo_ref)
    for _ in _out_proj_pieces(x_ref, slice(None), hnorm_ref[...], a_ref, m_ref, w, x1_ref):
        pass


def _out_proj(x, a_out, m_out, hnorm, win, woa, wob, wo):
    rows = x.shape[0]
    assert rows % ROW_TILE == 0
    consts = (win, woa, wob, wo)
    widths = (D_MODEL, ATTN_WIDTH, D_INNER, D_MODEL)
    return pl.pallas_call(
        _out_proj_kernel, grid=(rows // ROW_TILE,),
        in_specs=([pl.BlockSpec((ROW_TILE, w), lambda i: (i, 0)) for w in widths]
                  + [_const_spec(a.shape) for a in consts]),
        out_specs=pl.BlockSpec((ROW_TILE, D_MODEL), lambda i: (i, 0)),
        out_shape=jax.ShapeDtypeStruct((rows, D_MODEL), F32), name="sample_out_proj",
        compiler_params=pltpu.CompilerParams(dimension_semantics=("arbitrary",),
                                             vmem_limit_bytes=VMEM_LIMIT),
    )(x, a_out, m_out, hnorm, *consts)


def _layer_params(norm1, w_in, sinks, conv_w, conv_b, dt_bias, a_log, d_skip, ssm_norm, w_oa, w_ob, w_o,
                  norm2, w_up, w_down):
    dt0 = QKV_WIDTH + D_INNER + CONV_DIM
    wdt = w_in[:, dt0:dt0 + N_HEADS_M]
    w_rest = jnp.concatenate([w_in[:, :dt0], w_in[:, dt0 + N_HEADS_M:]], axis=1)
    pad_lanes = lambda a: jnp.pad(a, ((0, 0), (0, LANES - a.shape[1])))
    pieces = lambda a: a.astype(BF16).reshape(a.shape[0], a.shape[1] // PIECE, PIECE).transpose(1, 0, 2)
    mixer = MixerWeights(
        norm1=norm1[None, :], win=pieces(w_rest), wdt=pad_lanes(wdt).astype(BF16),
        convw=conv_w, convb=conv_b[None, :], dtb=pad_lanes(dt_bias[None, :]), alog=pad_lanes(a_log[None, :]),
        dskip=jnp.repeat(d_skip, HEAD_DIM_M)[None, :], ssmn=ssm_norm[None, :],
        woa=pieces(w_oa), wob=pieces(w_ob), wo=pieces(w_o))
    assert mixer.win.shape[0] == N_IN_PIECES
    return dict(sinks=sinks.astype(F32), mixer=mixer,
                norm2=norm2[None, :], wup=w_up.astype(BF16), wdown=w_down.astype(BF16))


def kernel(x_prompt, x_sample, cache_swa_k, cache_swa_v, state_conv, state_ssm, norm1, w_in, sinks, conv_w,
           conv_b, dt_bias, a_log, d_skip, ssm_norm, w_oa, w_ob, w_o, norm2, w_up, w_down, final_norm):
    depth = w_in.shape[0]
    assert depth == 1
    nb, seq, _ = x_prompt.shape
    ns, ls, _ = x_sample.shape
    assert ls == 4
    p = _layer_params(norm1[0], w_in[0], sinks[0], conv_w[0], conv_b[0], dt_bias[0], a_log[0], d_skip[0],
                      ssm_norm[0], w_oa[0], w_ob[0], w_o[0], norm2[0], w_up[0], w_down[0])
    fnorm = final_norm[None, :]
    mw = p["mixer"]
    mixer_consts = (mw.convw, mw.convb, mw.dtb, mw.alog, mw.dskip, mw.ssmn)

    x1p, pk, pv, pc, pst = _prompt_mixer(x_prompt, p["sinks"], mw)
    y_prompt = _mlp(x1p, p["norm2"], p["wup"], p["wdown"], fnorm)
    y_prompt = y_prompt.reshape(nb, seq, D_MODEL)

    xs_pad = jnp.pad(x_sample, ((0, 0), (0, SAMPLE_PAD - ls), (0, 0))).reshape(ns * SAMPLE_PAD, D_MODEL)
    cprev = jnp.pad(state_conv[0], ((0, 0), (SAMPLE_PAD - (CONV_W - 1), 0), (0, 0)))
    cprev = cprev.reshape(ns * SAMPLE_PAD, CONV_DIM)
    qkv, z, xbc, dtraw, hnorm = _in_proj(xs_pad, mw.norm1, mw.win, mw.wdt)
    a_out, m_out, sk, sv, sc, sst = _sample_mixer(
        p["sinks"], qkv, z, xbc, dtraw, cprev,
        cache_swa_k[0].reshape(ns, WINDOW, KV_WIDTH), cache_swa_v[0].reshape(ns, WINDOW, KV_WIDTH),
        state_ssm[0].reshape(ns, D_INNER, D_STATE), *mixer_consts)
    x1s = _out_proj(xs_pad, a_out, m_out, hnorm, mw.win, mw.woa, mw.wob, mw.wo)
    y_sample = _mlp(x1s, p["norm2"], p["wup"], p["wdown"], fnorm)
    y_sample = y_sample.reshape(ns, SAMPLE_PAD, D_MODEL)[:, :ls]
    sc = sc.reshape(ns, SAMPLE_PAD, CONV_DIM)[:, :CONV_W - 1]

    kv_shape = (1, -1, WINDOW, N_KV_A, HEAD_DIM_A)
    ssm_shape = (1, -1, N_HEADS_M, HEAD_DIM_M, D_STATE)
    return (y_prompt, y_sample,
            pk.reshape(kv_shape), pv.reshape(kv_shape), pc[None], pst.reshape(ssm_shape),
            sk.reshape(kv_shape), sv.reshape(kv_shape), sc[None], sst.reshape(ssm_shape))
```

```python
import collections
import functools

import jax
import jax.numpy as jnp
from jax import lax
from jax.experimental import pallas as pl
from jax.experimental.pallas import tpu as pltpu

F32 = jnp.float32
BF16 = jnp.bfloat16

D_MODEL = 1024
N_HEADS_A = 8
N_KV_A = 2
Q_PER_KV = N_HEADS_A // N_KV_A
HEAD_DIM_A = 64
WINDOW = 128
ATTN_WIDTH = N_HEADS_A * HEAD_DIM_A
KV_WIDTH = N_KV_A * HEAD_DIM_A
D_INNER = 1024
HEAD_DIM_M = 64
N_HEADS_M = D_INNER // HEAD_DIM_M
N_GROUPS_M = 2
GROUP_WIDTH = D_INNER // N_GROUPS_M
D_STATE = 128
CONV_W = 4
CONV_DIM = D_INNER + 2 * N_GROUPS_M * D_STATE
CHUNK = 128
D_FF = 4 * D_MODEL
EPS = 1e-6

LANES = 128
SUBLANES = 8
QKV_WIDTH = ATTN_WIDTH + 2 * KV_WIDTH
NEG_BIG = -1e30
VMEM_LIMIT = 56 * 1024 * 1024
SAMPLE_PAD = SUBLANES
SAMPLE_GROUP = 16


def _mm(a, b):
    return jnp.dot(a.astype(BF16), b.astype(BF16), preferred_element_type=F32)


def _mm_nt(a, b):
    return lax.dot_general(a.astype(BF16), b.astype(BF16), (((1,), (1,)), ((), ())),
                           preferred_element_type=F32)


def _rms(x, w):
    return x * lax.rsqrt(jnp.mean(x * x, axis=-1, keepdims=True) + EPS) * w


def _sigmoid(x):
    return 0.5 + 0.5 * jnp.tanh(0.5 * x)


def _silu(x):
    h = 0.5 * x
    return h + h * jnp.tanh(h)


def _softplus(x):
    return jnp.maximum(x, 0.0) + jnp.log(1.0 + jnp.exp(-jnp.abs(x)))


def _lane_lo(shape):
    return (lax.broadcasted_iota(jnp.int32, shape, len(shape) - 1) % LANES) < HEAD_DIM_A


def _dup_half(x, lo, first):
    xr = pltpu.roll(x, HEAD_DIM_A, axis=1)
    return jnp.where(lo, x, xr) if first else jnp.where(lo, xr, x)


def _stack_heads(q, kv, lo):
    qa = q[:, kv * 2 * LANES: kv * 2 * LANES + LANES]
    qb = q[:, kv * 2 * LANES + LANES: (kv + 1) * 2 * LANES]
    zero = jnp.zeros_like(qa)
    return jnp.concatenate([jnp.where(lo, qa, zero), jnp.where(lo, zero, qa),
                            jnp.where(lo, qb, zero), jnp.where(lo, zero, qb)], axis=0)


def _unstack_heads(o, rows, lo):
    return jnp.concatenate([jnp.where(lo, o[0:rows], o[rows:2 * rows]),
                            jnp.where(lo, o[2 * rows:3 * rows], o[3 * rows:4 * rows])], axis=1)


ProjBufs = collections.namedtuple("ProjBufs", "qkv z xbc dt")
MixedBufs = collections.namedtuple("MixedBufs", "attn ssd")
Carry = collections.namedtuple("Carry", "kprev vprev conv_tail ht dt_buf")
MixerWeights = collections.namedtuple(
    "MixerWeights", "norm1 win wgate wdt convw convb dtb alog dskip ssmn woa wob wo")

PIECE = 256
QKV_PIECE0 = 0
Z_PIECE0 = QKV_PIECE0 + QKV_WIDTH // PIECE
XBC_PIECE0 = Z_PIECE0 + D_INNER // PIECE
N_IN_PIECES = XBC_PIECE0 + CONV_DIM // PIECE
PREP_PIECES = 4


def _prep_pieces_kernel(w_ref, o_ref):
    for j in range(PREP_PIECES):
        o_ref[j] = w_ref[:, j * PIECE:(j + 1) * PIECE].astype(BF16)


def _prep_pieces(w, n_pieces=None, k_block=None):
    k_dim, n_dim = w.shape
    n_pieces = n_dim // PIECE if n_pieces is None else n_pieces
    k_block = k_dim if k_block is None else k_block
    cols = PREP_PIECES * PIECE
    return pl.pallas_call(
        _prep_pieces_kernel, grid=(pl.cdiv(n_pieces, PREP_PIECES), k_dim // k_block),
        in_specs=[pl.BlockSpec((k_block, cols), lambda p, k: (k, p))],
        out_specs=pl.BlockSpec((PREP_PIECES, k_block, PIECE), lambda p, k: (p, k, 0)),
        out_shape=jax.ShapeDtypeStruct((n_pieces, k_dim, PIECE), BF16), name="prep_weight_pieces",
        compiler_params=pltpu.CompilerParams(dimension_semantics=("arbitrary", "arbitrary"),
                                             vmem_limit_bytes=VMEM_LIMIT),
    )(w)


def _cast_kernel(w_ref, o_ref):
    o_ref[...] = w_ref[...].astype(BF16)


def _cast_bf16(w, block):
    return pl.pallas_call(
        _cast_kernel, grid=(w.shape[0] // block[0], w.shape[1] // block[1]),
        in_specs=[pl.BlockSpec(block, lambda i, j: (i, j))], out_specs=pl.BlockSpec(block, lambda i, j: (i, j)),
        out_shape=jax.ShapeDtypeStruct(w.shape, BF16), name="cast_weight",
        compiler_params=pltpu.CompilerParams(dimension_semantics=("arbitrary", "arbitrary"),
                                             vmem_limit_bytes=VMEM_LIMIT),
    )(w)


StateOuts = collections.namedtuple("StateOuts", "k v conv ssm")


def _split3_bf16(x):
    hi = x.astype(BF16)
    r1 = x - hi.astype(F32)
    mid = r1.astype(BF16)
    return hi, mid, (r1 - mid.astype(F32)).astype(BF16)


def _piece_cols(p):
    return slice(p * PIECE, (p + 1) * PIECE)


def _in_proj_pieces(hb, w, dsts):
    for dst, first in zip(dsts, (QKV_PIECE0, Z_PIECE0, XBC_PIECE0)):
        for p in range(dst.shape[1] // PIECE):
            dst[:, _piece_cols(p)] = _mm(hb, w.win[first + p])
            yield


def _out_proj_pieces(x_ref, rows, hb, attn_ref, ssd_ref, w, out_ref):
    merged = []
    for half, (src_ref, w_ref) in enumerate(((attn_ref, w.woa), (ssd_ref, w.wob))):
        parts = []
        for p in range(D_MODEL // PIECE):
            gate = _sigmoid(_mm(hb, w.wgate[half * (D_MODEL // PIECE) + p]))
            yield
            parts.append(gate * _mm(src_ref[...], w_ref[p]))
            yield
        merged.append(jnp.concatenate(parts, axis=1))
    merged = (merged[0] + merged[1]).astype(BF16)
    for p in range(D_MODEL // PIECE):
        out_ref[rows, _piece_cols(p)] = x_ref[rows, _piece_cols(p)] + _mm(merged, w.wo[p])
        yield


def _stage_in_proj(x_ref, rows, w, proj, hnorm_ref):
    hb = _rms(x_ref[rows, :], w.norm1[...]).astype(BF16)
    hnorm_ref[...] = hb
    yield
    yield from _in_proj_pieces(hb, w, (proj.qkv, proj.z, proj.xbc))
    proj.dt[...] = _mm(hb, w.wdt[...])
    yield


def _stage_out_proj(x_ref, rows, mixed, w, x1_ref, hnorm_ref):
    hb = hnorm_ref[...]
    yield
    yield from _out_proj_pieces(x_ref, rows, hb, mixed.attn, mixed.ssd, w, x1_ref)


def _alternate(*gens):
    gens = list(gens)
    while gens:
        for g in list(gens):
            try:
                next(g)
                yield
            except StopIteration:
                gens.remove(g)


FILL_PER_PIECE = 1


def _trace_interleaved(primary, filler):
    for _ in primary:
        for _ in range(FILL_PER_PIECE):
            next(filler, None)
    for _ in filler:
        pass


def _stage_mix(proj, mixed, rows, carry, w, sinks_ref, first):
    T = CHUNK
    reset = (lambda a: a) if first is None else (lambda a: jnp.where(first, 0.0, a))

    qkv = proj.qkv[rows, :]
    q = qkv[:, :ATTN_WIDTH] * (HEAD_DIM_A ** -0.5)
    k = qkv[:, ATTN_WIDTH:ATTN_WIDTH + KV_WIDTH]
    v = qkv[:, ATTN_WIDTH + KV_WIDTH:]
    kk = jnp.concatenate([carry.kprev[...], k], axis=0)
    vv = jnp.concatenate([carry.vprev[...], v], axis=0)
    carry.kprev[...] = k
    carry.vprev[...] = v

    lo = _lane_lo((T, LANES))
    lo2 = _lane_lo((2 * T, LANES))
    r = lax.broadcasted_iota(jnp.int32, (T, 2 * T), 0)
    col = lax.broadcasted_iota(jnp.int32, (T, 2 * T), 1)
    valid = (col >= r) & (col <= r + WINDOW)
    if first is not None:
        valid = valid & (col >= jnp.where(first, T, 0))
    a_slabs = []
    for kv in range(N_KV_A):
        kd = _dup_half(kk, lo2, kv == 0).astype(BF16)
        vd = _dup_half(vv, lo2, kv == 0).astype(BF16)
        s = _mm_nt(_stack_heads(q, kv, lo), kd)
        es, inv = [], []
        for g in range(Q_PER_KV):
            sk = sinks_ref[kv * Q_PER_KV + g]
            sg = jnp.where(valid, s[g * T:(g + 1) * T], NEG_BIG)
            m = jnp.maximum(jnp.max(sg, axis=-1, keepdims=True), sk)
            e = jnp.exp(sg - m)
            den = jnp.sum(e, axis=-1, keepdims=True) + jnp.exp(sk - m)
            es.append(e.astype(BF16))
            inv.append(1.0 / den)
        yield
        o = _mm(jnp.concatenate(es, axis=0), vd) * jnp.concatenate(inv, axis=0)
        a_slabs.append(_unstack_heads(o, T, lo))
        yield
    mixed.attn[rows, :] = jnp.concatenate(a_slabs, axis=1).astype(BF16)

    xr = proj.xbc[rows, :]
    prev_tile = jnp.concatenate([reset(carry.conv_tail[...]), xr[:T - SUBLANES]], axis=0)
    carry.conv_tail[...] = xr[T - SUBLANES:]
    rm = lax.broadcasted_iota(jnp.int32, (T, 1), 0) % SUBLANES
    yc = xr * w.convw[CONV_W - 1:CONV_W, :]
    for shift in range(1, CONV_W):
        shifted = _tile_roll(jnp.where(rm >= SUBLANES - shift, prev_tile, xr), shift)
        yc = yc + shifted * w.convw[CONV_W - 1 - shift:CONV_W - shift, :]
        yield
    xbc = _silu(yc + w.convb[...])
    xs = xbc[:, :D_INNER]
    bm = xbc[:, D_INNER:D_INNER + N_GROUPS_M * D_STATE]
    cm = xbc[:, D_INNER + N_GROUPS_M * D_STATE:]

    z = proj.z[rows, :]
    carry.dt_buf[...] = _softplus(proj.dt[rows, :] + w.dtb[...])
    dt = carry.dt_buf[...]
    rr = lax.broadcasted_iota(jnp.int32, (T, T), 0)
    cc = lax.broadcasted_iota(jnp.int32, (T, T), 1)
    tril = cc <= rr
    cs = jnp.dot(jnp.where(tril, 1.0, 0.0).astype(BF16),
                 jnp.concatenate(_split3_bf16(dt * -jnp.exp(w.alog[...])), axis=1), preferred_element_type=F32)
    a_cs = cs[:, :LANES] + cs[:, LANES:2 * LANES] + cs[:, 2 * LANES:]
    a_cs_t = a_cs.T
    yield

    ys = []
    pairs_per_group = N_HEADS_M // 2 // N_GROUPS_M
    for g in range(N_GROUPS_M):
        gcols = slice(g * GROUP_WIDTH, (g + 1) * GROUP_WIDTH)
        cg = cm[:, g * D_STATE:(g + 1) * D_STATE]
        bg = bm[:, g * D_STATE:(g + 1) * D_STATE]
        cb = _mm_nt(cg, bg)
        h_prev = reset(carry.ht[:, gcols])
        y_off = _mm(cg, h_prev)
        xds, decays = [], []
        for jj in range(pairs_per_group):
            j = g * pairs_per_group + jj
            sl = slice(j * LANES, (j + 1) * LANES)
            ws, colbs, dtbs = [], [], []
            for h in (2 * j, 2 * j + 1):
                colb = jnp.broadcast_to(a_cs[:, h:h + 1], (T, T))
                rowb = jnp.broadcast_to(a_cs_t[h:h + 1, :], (T, T))
                seg = jnp.where(tril, jnp.exp(colb - rowb), 0.0)
                ws.append((cb * seg).astype(BF16))
                colbs.append(colb)
                dtbs.append(jnp.broadcast_to(dt[:, h:h + 1], (T, LANES)))
            dt_e = jnp.where(lo, dtbs[0], dtbs[1])
            acs_e = jnp.where(lo, colbs[0], colbs[1])
            xs_j = xs[:, sl]
            xdt = xs_j * dt_e
            zero = jnp.zeros_like(xdt)
            rhs = jnp.concatenate([jnp.where(lo, xdt, zero), jnp.where(lo, zero, xdt)], axis=0)
            y_diag = _mm(jnp.concatenate(ws, axis=1), rhs)
            alast = acs_e[T - 1:T, :]
            xds.append(xdt * jnp.exp(alast - acs_e))
            decays.append(jnp.exp(alast))
            y = y_diag + y_off[:, jj * LANES:(jj + 1) * LANES] * jnp.exp(acs_e) + xs_j * w.dskip[:, sl]
            ys.append(y * _silu(z[:, sl]))
            yield
        carry.ht[:, gcols] = (h_prev * jnp.concatenate(decays, axis=1)
                              + _mm(bg.T, jnp.concatenate(xds, axis=1)))
        yield

    m_slabs = []
    for g in range(N_GROUPS_M):
        grp = ys[g * pairs_per_group:(g + 1) * pairs_per_group]
        ssq = grp[0] * grp[0]
        for y in grp[1:]:
            ssq = ssq + y * y
        scale = lax.rsqrt(jnp.sum(ssq, axis=-1, keepdims=True) * (1.0 / GROUP_WIDTH) + EPS)
        m_slabs.extend([y * scale for y in grp])
    mixed.ssd[rows, :] = (jnp.concatenate(m_slabs, axis=1) * w.ssmn[...]).astype(BF16)


def _chain(*gens):
    for g in gens:
        yield from g


def _prompt_mixer_kernel(sinks_ref, x_in_ref, x_res_ref, *refs, chunks_per_seq, n_pairs):
    refs = list(refs)
    take = lambda n: [refs.pop(0) for _ in range(n)]
    w = MixerWeights(*take(len(MixerWeights._fields)))
    x1_ref, = take(1)
    outs = StateOuts(*take(4))
    proj = (ProjBufs(*take(4)), ProjBufs(*take(4)))
    mixed = (MixedBufs(*take(2)), MixedBufs(*take(2)))
    hnorm = take(2)
    carry = Carry(*take(len(Carry._fields)))
    assert not refs
    T = CHUNK
    s = pl.program_id(0)
    every = slice(None)

    @pl.when(s == 0)
    def _():
        for ref in list(proj[1] + mixed[0] + mixed[1] + carry) + hnorm:
            ref[...] = jnp.zeros_like(ref)

    first = (s - 1) % (chunks_per_seq // 2) == 0

    def step(cur):
        prv = 1 - cur
        dense = _alternate(_stage_out_proj(x_res_ref, every, mixed[cur], w, x1_ref, hnorm[cur]),
                           _stage_in_proj(x_in_ref, every, w, proj[cur], hnorm[cur]))
        mix = _chain(_stage_mix(proj[prv], mixed[prv], slice(0, T), carry, w, sinks_ref, first),
                     _stage_mix(proj[prv], mixed[prv], slice(T, 2 * T), carry, w, sinks_ref, None))
        _trace_interleaved(mix, dense)

    for parity in (0, 1):
        pl.when(s % 2 == parity)(functools.partial(step, parity))

    @pl.when(s <= n_pairs)
    def _():
        outs.k[0] = carry.kprev[...]
        outs.v[0] = carry.vprev[...]
        outs.conv[0] = carry.conv_tail[SUBLANES - (CONV_W - 1):, :]
        outs.ssm[0] = carry.ht[...].T


def _const_spec(shape):
    return pl.BlockSpec(shape, lambda *_: (0,) * len(shape))


def _prompt_mixer(x, sinks, weights):
    nb, seq, _ = x.shape
    assert seq % (2 * CHUNK) == 0
    chunks_per_seq = seq // CHUNK
    n_pairs = nb * chunks_per_seq // 2
    pair = 2 * CHUNK
    consts = tuple(weights)
    seq_of_mix = lambda s: jnp.clip((2 * s - 1) // chunks_per_seq, 0, nb - 1)
    in_specs = ([pl.BlockSpec(memory_space=pltpu.SMEM),
                 pl.BlockSpec((pair, D_MODEL), lambda s: (jnp.minimum(s, n_pairs - 1), 0)),
                 pl.BlockSpec((pair, D_MODEL), lambda s: (jnp.maximum(s - 2, 0), 0))]
                + [_const_spec(a.shape) for a in consts])
    out_shape = (jax.ShapeDtypeStruct((nb * seq, D_MODEL), F32),
                 jax.ShapeDtypeStruct((nb, WINDOW, KV_WIDTH), F32),
                 jax.ShapeDtypeStruct((nb, WINDOW, KV_WIDTH), F32),
                 jax.ShapeDtypeStruct((nb, CONV_W - 1, CONV_DIM), F32),
                 jax.ShapeDtypeStruct((nb, D_INNER, D_STATE), F32))
    out_specs = (pl.BlockSpec((pair, D_MODEL), lambda s: (jnp.maximum(s - 2, 0), 0)),
                 pl.BlockSpec((1, WINDOW, KV_WIDTH), lambda s: (seq_of_mix(s), 0, 0)),
                 pl.BlockSpec((1, WINDOW, KV_WIDTH), lambda s: (seq_of_mix(s), 0, 0)),
                 pl.BlockSpec((1, CONV_W - 1, CONV_DIM), lambda s: (seq_of_mix(s), 0, 0)),
                 pl.BlockSpec((1, D_INNER, D_STATE), lambda s: (seq_of_mix(s), 0, 0)))
    proj_bufs = [pltpu.VMEM((pair, QKV_WIDTH), F32), pltpu.VMEM((pair, D_INNER), F32),
                 pltpu.VMEM((pair, CONV_DIM), F32), pltpu.VMEM((pair, LANES), F32)]
    mixed_bufs = [pltpu.VMEM((pair, ATTN_WIDTH), BF16), pltpu.VMEM((pair, D_INNER), BF16)]
    hnorm_bufs = [pltpu.VMEM((pair, D_MODEL), BF16)] * 2
    carry = [pltpu.VMEM((CHUNK, KV_WIDTH), F32), pltpu.VMEM((CHUNK, KV_WIDTH), F32),
             pltpu.VMEM((SUBLANES, CONV_DIM), F32), pltpu.VMEM((D_STATE, D_INNER), F32),
             pltpu.VMEM((CHUNK, LANES), F32)]
    x_rows = x.reshape(nb * seq, D_MODEL)
    return pl.pallas_call(
        functools.partial(_prompt_mixer_kernel, chunks_per_seq=chunks_per_seq, n_pairs=n_pairs),
        grid=(n_pairs + 2,), in_specs=in_specs, out_specs=out_specs,
        out_shape=out_shape, scratch_shapes=proj_bufs * 2 + mixed_bufs * 2 + hnorm_bufs + carry,
        name="prompt_mixer",
        compiler_params=pltpu.CompilerParams(dimension_semantics=("arbitrary",),
                                             vmem_limit_bytes=VMEM_LIMIT),
    )(sinks, x_rows, x_rows, *consts)


MLP_TILE = 512
FF_BLOCK = 1024


def _mlp_kernel(x_ref, norm2_ref, wup_ref, wdown_ref, fnorm_ref, y_ref):
    x = x_ref[...]
    hm = _rms(x, norm2_ref[...]).astype(BF16)
    acc = x
    for j in range(D_FF // FF_BLOCK):
        h = _mm(hm, wup_ref[:, j * FF_BLOCK:(j + 1) * FF_BLOCK])
        h = jnp.square(jnp.maximum(h, 0.0))
        acc = acc + _mm(h, wdown_ref[j * FF_BLOCK:(j + 1) * FF_BLOCK, :])
    y_ref[...] = _rms(acc, fnorm_ref[...])


def _mlp(x, norm2, wup, wdown, fnorm):
    rows = x.shape[0]
    assert rows % MLP_TILE == 0
    consts = (norm2, wup, wdown, fnorm)
    return pl.pallas_call(
        _mlp_kernel, grid=(rows // MLP_TILE,),
        in_specs=[pl.BlockSpec((MLP_TILE, D_MODEL), lambda i: (i, 0))] + [_const_spec(a.shape) for a in consts],
        out_specs=pl.BlockSpec((MLP_TILE, D_MODEL), lambda i: (i, 0)),
        out_shape=jax.ShapeDtypeStruct((rows, D_MODEL), F32), name="mlp",
        compiler_params=pltpu.CompilerParams(dimension_semantics=("arbitrary",),
                                             vmem_limit_bytes=VMEM_LIMIT),
    )(x, *consts)


ROW_TILE = 256


def _mixer_weights(**refs):
    return MixerWeights(**{f: refs.get(f) for f in MixerWeights._fields})


def _in_proj_kernel(x_ref, norm1_ref, win_ref, wdt_ref, qkv_ref, z_ref, xbc_ref, dt_ref, hnorm_ref):
    hb = _rms(x_ref[...], norm1_ref[...]).astype(BF16)
    hnorm_ref[...] = hb
    for _ in _in_proj_pieces(hb, _mixer_weights(win=win_ref), (qkv_ref, z_ref, xbc_ref)):
        pass
    dt_ref[...] = _mm(hb, wdt_ref[...])


def _in_proj(x, norm1, win, wdt):
    rows = x.shape[0]
    assert rows % ROW_TILE == 0
    consts = (norm1, win, wdt)
    outs = ((QKV_WIDTH, F32), (D_INNER, F32), (CONV_DIM, F32), (LANES, F32), (D_MODEL, BF16))
    return pl.pallas_call(
        _in_proj_kernel, grid=(rows // ROW_TILE,),
        in_specs=[pl.BlockSpec((ROW_TILE, D_MODEL), lambda i: (i, 0))] + [_const_spec(a.shape) for a in consts],
        out_specs=tuple(pl.BlockSpec((ROW_TILE, width), lambda i: (i, 0)) for width, _ in outs),
        out_shape=tuple(jax.ShapeDtypeStruct((rows, width), dtype) for width, dtype in outs),
        name="sample_in_proj",
        compiler_params=pltpu.CompilerParams(dimension_semantics=("arbitrary",),
                                             vmem_limit_bytes=VMEM_LIMIT),
    )(x, *consts)


def _tile_roll(x, shift):
    rows, width = x.shape
    tiles = x.reshape(rows // SUBLANES, SUBLANES, width)
    return pltpu.roll(tiles, shift % SUBLANES, axis=1).reshape(rows, width)


def _spread_rows(x):
    tiles = []
    for t in range(x.shape[0] // SUBLANES):
        two_seqs = x[t * SUBLANES:(t + 1) * SUBLANES]
        tiles += [two_seqs, pltpu.roll(two_seqs, SUBLANES // 2, axis=0)]
    return jnp.concatenate(tiles, axis=0)


def _gather_rows(y):
    low = lax.broadcasted_iota(jnp.int32, (SUBLANES, 1), 0) < SUBLANES // 2
    tiles = []
    for t in range(y.shape[0] // (2 * SUBLANES)):
        a = y[2 * t * SUBLANES:(2 * t + 1) * SUBLANES]
        b = y[(2 * t + 1) * SUBLANES:(2 * t + 2) * SUBLANES]
        tiles.append(jnp.where(low, a, pltpu.roll(b, SUBLANES // 2, axis=0)))
    return jnp.concatenate(tiles, axis=0)


def _expand_heads(cols, expand):
    terms = []
    for c in cols:
        hi = c.astype(BF16)
        r1 = c - hi.astype(F32)
        mid = r1.astype(BF16)
        lo = (r1 - mid.astype(F32)).astype(BF16)
        terms.extend([hi, mid, lo])
    rows = cols[0].shape[0]
    out = jnp.dot(jnp.concatenate(terms, axis=0), expand, preferred_element_type=F32)
    return [out[(3 * i) * rows:(3 * i + 1) * rows] + out[(3 * i + 1) * rows:(3 * i + 2) * rows]
            + out[(3 * i + 2) * rows:(3 * i + 3) * rows] for i in range(len(cols))]


def _sample_mixer_kernel(sinks_ref, qkv_ref, z_ref, xbc_ref, dtraw_ref, cprev_ref, kc_ref, vc_ref, st_ref,
                         convw_ref, convb_ref, dtb_ref, alog_ref, dskip_ref, ssmn_ref,
                         aout_ref, mout_ref, nk_ref, nv_ref, nconv_ref, nst_ref):
    R = SAMPLE_GROUP * SAMPLE_PAD
    L = 4
    rm = lax.broadcasted_iota(jnp.int32, (R, 1), 0) % SAMPLE_PAD

    xr = xbc_raw = _spread_rows(xbc_ref[...])
    with_prev = jnp.where(rm >= SAMPLE_PAD - (CONV_W - 1), cprev_ref[...], xr)
    yc = xr * convw_ref[CONV_W - 1:CONV_W, :]
    for kshift in range(1, CONV_W):
        yc = yc + _tile_roll(with_prev, kshift) * convw_ref[CONV_W - 1 - kshift:CONV_W - kshift, :]
    nconv_ref[...] = _tile_roll(xbc_raw, -1)
    xbc = _silu(yc + convb_ref[...])
    xs = xbc[:, :D_INNER]
    bm = xbc[:, D_INNER:D_INNER + N_GROUPS_M * D_STATE]
    cm = xbc[:, D_INNER + N_GROUPS_M * D_STATE:]

    dt = _softplus(_spread_rows(dtraw_ref[...]) + dtb_ref[...])
    d_a = dt * (-jnp.exp(alog_ref[...]))
    a_cs = d_a
    suf = jnp.zeros_like(d_a)
    for kshift in range(1, L):
        a_cs = a_cs + jnp.where(rm >= kshift, _tile_roll(d_a, kshift), 0.0)
        suf = suf + jnp.where(rm <= L - 1 - kshift, _tile_roll(d_a, -kshift), 0.0)

    expand = (lax.broadcasted_iota(jnp.int32, (LANES, D_INNER), 1) // HEAD_DIM_M
              == lax.broadcasted_iota(jnp.int32, (LANES, D_INNER), 0)).astype(BF16)
    dt_e, acs_e, suf_e = _expand_heads([dt, a_cs, suf], expand)
    xdt = xs * dt_e
    real = rm < L
    xd_t = jnp.where(real, xdt * jnp.exp(suf_e), 0.0).T.astype(BF16)

    y = xs * dskip_ref[...]
    for kshift in range(L):
        bk = bm if kshift == 0 else _tile_roll(bm, kshift)
        cb = cm * bk
        cb_e = jnp.concatenate(
            [jnp.broadcast_to(jnp.sum(cb[:, g * D_STATE:(g + 1) * D_STATE], axis=-1, keepdims=True),
                              (R, GROUP_WIDTH)) for g in range(N_GROUPS_M)], axis=1)
        if kshift == 0:
            y = y + cb_e * xdt
        else:
            seg = jnp.exp(acs_e - _tile_roll(acs_e, kshift))
            y = y + jnp.where(rm >= kshift, cb_e * seg * _tile_roll(xdt, kshift), 0.0)

    qkv = _spread_rows(qkv_ref[...])
    q = qkv[:, :ATTN_WIDTH] * (HEAD_DIM_A ** -0.5)
    q_swapped = jnp.concatenate(
        [pltpu.roll(q[:, s * LANES:(s + 1) * LANES], HEAD_DIM_A, axis=1) for s in range(ATTN_WIDTH // LANES)], axis=1)
    kn = qkv[:, ATTN_WIDTH:ATTN_WIDTH + KV_WIDTH]
    vn = qkv[:, ATTN_WIDTH + KV_WIDTH:]
    lo8 = _lane_lo((SAMPLE_PAD, LANES))
    zero8 = jnp.zeros((SAMPLE_PAD, LANES), F32)
    seq_rows = [slice(i * SAMPLE_PAD, (i + 1) * SAMPLE_PAD) for i in range(SAMPLE_GROUP)]
    HROWS = N_HEADS_A * SAMPLE_PAD

    s_c, s_n = [], []
    for rows in seq_rows:
        pieces = []
        for kv in range(N_KV_A):
            for g in range(Q_PER_KV):
                slab = kv * (Q_PER_KV // 2) + g // 2
                src = q if g % 2 == kv else q_swapped
                piece = src[rows, slab * LANES:(slab + 1) * LANES]
                pieces.append(jnp.where(lo8, piece, zero8) if kv == 0 else jnp.where(lo8, zero8, piece))
        lhs = jnp.concatenate(pieces, axis=0).astype(BF16)
        i = len(s_c)
        s_c.append(_mm_nt(lhs, kc_ref[i]))
        s_n.append(_mm_nt(lhs, kn[rows]))
    s_c = jnp.concatenate(s_c, axis=0)
    s_n = jnp.concatenate(s_n, axis=0)
    n_rows = SAMPLE_GROUP * HROWS
    tok_c = lax.broadcasted_iota(jnp.int32, (n_rows, WINDOW), 0) % SAMPLE_PAD
    s_c = jnp.where(lax.broadcasted_iota(jnp.int32, (n_rows, WINDOW), 1) >= tok_c, s_c, NEG_BIG)
    tok_n = lax.broadcasted_iota(jnp.int32, (n_rows, SAMPLE_PAD), 0) % SAMPLE_PAD
    col_n = lax.broadcasted_iota(jnp.int32, (n_rows, SAMPLE_PAD), 1)
    s_n = jnp.where((col_n <= tok_n) & (col_n < L), s_n, NEG_BIG)
    sk = jnp.concatenate([jnp.full((SAMPLE_PAD, 1), sinks_ref[h], F32) for h in range(N_HEADS_A)] * SAMPLE_GROUP,
                         axis=0)
    m = jnp.maximum(jnp.maximum(jnp.max(s_c, axis=-1, keepdims=True), jnp.max(s_n, axis=-1, keepdims=True)), sk)
    e_c = jnp.exp(s_c - m)
    e_n = jnp.exp(s_n - m)
    inv_den = 1.0 / (jnp.sum(e_c, axis=-1, keepdims=True) + jnp.sum(e_n, axis=-1, keepdims=True) + jnp.exp(sk - m))
    e_c = e_c.astype(BF16)
    e_n = e_n.astype(BF16)
    o = jnp.concatenate([_mm(e_c[i * HROWS:(i + 1) * HROWS], vc_ref[i]) + _mm(e_n[i * HROWS:(i + 1) * HROWS], vn[rows])
                         for i, rows in enumerate(seq_rows)], axis=0) * inv_den
    o_swapped = pltpu.roll(o, HEAD_DIM_A, axis=1)
    a_rows = []
    for i in range(SAMPLE_GROUP):
        blk = lambda arr, kv, g: arr[i * HROWS + (kv * Q_PER_KV + g) * SAMPLE_PAD:
                                     i * HROWS + (kv * Q_PER_KV + g + 1) * SAMPLE_PAD]
        slabs = []
        for kv in range(N_KV_A):
            for j in range(Q_PER_KV // 2):
                first = blk(o if kv == 0 else o_swapped, kv, 2 * j)
                second = blk(o_swapped if kv == 0 else o, kv, 2 * j + 1)
                slabs.append(jnp.where(lo8, first, second))
        a_rows.append(jnp.concatenate(slabs, axis=1))
    aout_ref[...] = _gather_rows(jnp.concatenate(a_rows, axis=0))

    r_w = lax.broadcasted_iota(jnp.int32, (WINDOW, LANES), 0) % SUBLANES
    for i, rows in enumerate(seq_rows):
        for cache_ref, new, out_ref in ((kc_ref, kn, nk_ref), (vc_ref, vn, nv_ref)):
            cache = cache_ref[i]
            nxt = jnp.concatenate([cache[SUBLANES:], new[rows]], axis=0)
            out_ref[i] = _tile_roll(jnp.where(r_w >= L, cache, nxt), -L)

    row_r = lax.broadcasted_iota(jnp.int32, (R, D_STATE), 0)
    heads_per_group = N_HEADS_M // N_GROUPS_M
    yoff = []
    for i, rows in enumerate(seq_rows):
        state = st_ref[i]
        state_b = state.astype(BF16)
        a_tot = a_cs[i * SAMPLE_PAD + L - 1:i * SAMPLE_PAD + L, :]
        in_seq = (row_r >= i * SAMPLE_PAD) & (row_r < (i + 1) * SAMPLE_PAD)
        yo = []
        for g in range(N_GROUPS_M):
            gs = slice(g * GROUP_WIDTH, (g + 1) * GROUP_WIDTH)
            yo.append(_mm_nt(cm[rows, g * D_STATE:(g + 1) * D_STATE], state_b[gs]))
            bsel = jnp.where(in_seq, bm[:, g * D_STATE:(g + 1) * D_STATE], 0.0)
            upd = _mm(xd_t[gs, :], bsel)
            for hh in range(heads_per_group):
                h = g * heads_per_group + hh
                hs = slice(h * HEAD_DIM_M, (h + 1) * HEAD_DIM_M)
                decay = jnp.exp(jnp.broadcast_to(a_tot[:, h:h + 1], (HEAD_DIM_M, D_STATE)))
                nst_ref[i, hs, :] = state[hs] * decay + upd[hh * HEAD_DIM_M:(hh + 1) * HEAD_DIM_M]
        yoff.append(jnp.concatenate(yo, axis=1))
    yoff = jnp.concatenate(yoff, axis=0)

    y = (y + yoff * jnp.exp(acs_e)) * _silu(_spread_rows(z_ref[...]))
    outs = []
    for g in range(N_GROUPS_M):
        yg = y[:, g * GROUP_WIDTH:(g + 1) * GROUP_WIDTH]
        outs.append(yg * lax.rsqrt(jnp.mean(yg * yg, axis=-1, keepdims=True) + EPS))
    mout_ref[...] = _gather_rows(jnp.concatenate(outs, axis=1) * ssmn_ref[...])


def _sample_mixer(sinks, qkv, z, xbc, dtraw, cprev, kc, vc, st, convw, convb, dtb, alog, dskip, ssmn):
    nseq = kc.shape[0]
    assert nseq % SAMPLE_GROUP == 0
    R = SAMPLE_GROUP * SAMPLE_PAD
    T4 = SAMPLE_GROUP * 4
    rows = nseq * SAMPLE_PAD
    consts = (convw, convb, dtb, alog, dskip, ssmn)
    row_spec = lambda w: pl.BlockSpec((R, w), lambda i: (i, 0))
    tok_spec = lambda w: pl.BlockSpec((T4, w), lambda i: (i, 0))
    seq_spec = lambda a, b: pl.BlockSpec((SAMPLE_GROUP, a, b), lambda i: (i, 0, 0))
    in_specs = ([pl.BlockSpec(memory_space=pltpu.SMEM),
                 tok_spec(QKV_WIDTH), tok_spec(D_INNER), tok_spec(CONV_DIM), tok_spec(LANES), row_spec(CONV_DIM),
                 seq_spec(WINDOW, KV_WIDTH), seq_spec(WINDOW, KV_WIDTH), seq_spec(D_INNER, D_STATE)]
                + [_const_spec(a.shape) for a in consts])
    out_shape = (jax.ShapeDtypeStruct((nseq * 4, ATTN_WIDTH), F32), jax.ShapeDtypeStruct((nseq * 4, D_INNER), F32),
                 jax.ShapeDtypeStruct((nseq, WINDOW, KV_WIDTH), F32),
                 jax.ShapeDtypeStruct((nseq, WINDOW, KV_WIDTH), F32),
                 jax.ShapeDtypeStruct((rows, CONV_DIM), F32),
                 jax.ShapeDtypeStruct((nseq, D_INNER, D_STATE), F32))
    out_specs = (tok_spec(ATTN_WIDTH), tok_spec(D_INNER), seq_spec(WINDOW, KV_WIDTH), seq_spec(WINDOW, KV_WIDTH),
                 row_spec(CONV_DIM), seq_spec(D_INNER, D_STATE))
    return pl.pallas_call(
        _sample_mixer_kernel, grid=(nseq // SAMPLE_GROUP,), in_specs=in_specs, out_specs=out_specs,
        out_shape=out_shape, name="sample_mixer",
        compiler_params=pltpu.CompilerParams(dimension_semantics=("arbitrary",),
                                             vmem_limit_bytes=VMEM_LIMIT),
    )(sinks, qkv, z, xbc, dtraw, cprev, kc, vc, st, *consts)


def _out_proj_kernel(x_ref, a_ref, m_ref, hnorm_ref, wgate_ref, woa_ref, wob_ref, wo_ref, x1_ref):
    w = _mixer_weights(wgate=wgate_ref, woa=woa_ref, wob=wob_ref, wo=wo_ref)
    for _ in _out_proj_pieces(x_ref, slice(None), hnorm_ref[...], a_ref, m_ref, w, x1_ref):
        pass


def _out_proj(x, a_out, m_out, hnorm, wgate, woa, wob, wo):
    rows = x.shape[0]
    assert rows % ROW_TILE == 0
    consts = (wgate, woa, wob, wo)
    widths = (D_MODEL, ATTN_WIDTH, D_INNER, D_MODEL)
    return pl.pallas_call(
        _out_proj_kernel, grid=(rows // ROW_TILE,),
        in_specs=([pl.BlockSpec((ROW_TILE, w), lambda i: (i, 0)) for w in widths]
                  + [_const_spec(a.shape) for a in consts]),
        out_specs=pl.BlockSpec((ROW_TILE, D_MODEL), lambda i: (i, 0)),
        out_shape=jax.ShapeDtypeStruct((rows, D_MODEL), F32), name="sample_out_proj",
        compiler_params=pltpu.CompilerParams(dimension_semantics=("arbitrary",),
                                             vmem_limit_bytes=VMEM_LIMIT),
    )(x, a_out, m_out, hnorm, *consts)


def _layer_params(norm1, w_in, sinks, conv_w, conv_b, dt_bias, a_log, d_skip, ssm_norm, w_oa, w_ob, w_o,
                  norm2, w_up, w_down):
    dt0 = QKV_WIDTH + D_INNER + CONV_DIM
    assert dt0 == N_IN_PIECES * PIECE
    wdt = w_in[:, dt0:dt0 + N_HEADS_M]
    pad_lanes = lambda a: jnp.pad(a, ((0, 0), (0, LANES - a.shape[1])))
    mixer = MixerWeights(
        norm1=norm1[None, :], win=_prep_pieces(w_in, N_IN_PIECES), wgate=_prep_pieces(w_in[:, dt0 + N_HEADS_M:]),
        wdt=pad_lanes(wdt).astype(BF16),
        convw=conv_w, convb=conv_b[None, :], dtb=pad_lanes(dt_bias[None, :]), alog=pad_lanes(a_log[None, :]),
        dskip=jnp.repeat(d_skip, HEAD_DIM_M)[None, :], ssmn=ssm_norm[None, :],
        woa=_prep_pieces(w_oa), wob=_prep_pieces(w_ob), wo=_prep_pieces(w_o))
    wide = (D_MODEL, D_MODEL)
    return dict(sinks=sinks.astype(F32), mixer=mixer,
                norm2=norm2[None, :], wup=_cast_bf16(w_up, wide), wdown=_cast_bf16(w_down, wide))


def kernel(x_prompt, x_sample, cache_swa_k, cache_swa_v, state_conv, state_ssm, norm1, w_in, sinks, conv_w,
           conv_b, dt_bias, a_log, d_skip, ssm_norm, w_oa, w_ob, w_o, norm2, w_up, w_down, final_norm):
    depth = w_in.shape[0]
    assert depth == 1
    nb, seq, _ = x_prompt.shape
    ns, ls, _ = x_sample.shape
    assert ls == 4
    p = _layer_params(norm1[0], w_in[0], sinks[0], conv_w[0], conv_b[0], dt_bias[0], a_log[0], d_skip[0],
                      ssm_norm[0], w_oa[0], w_ob[0], w_o[0], norm2[0], w_up[0], w_down[0])
    fnorm = final_norm[None, :]
    mw = p["mixer"]
    mixer_consts = (mw.convw, mw.convb, mw.dtb, mw.alog, mw.dskip, mw.ssmn)

    x1p, pk, pv, pc, pst = _prompt_mixer(x_prompt, p["sinks"], mw)
    y_prompt = _mlp(x1p, p["norm2"], p["wup"], p["wdown"], fnorm)
    y_prompt = y_prompt.reshape(nb, seq, D_MODEL)

    xs_rows = x_sample.reshape(ns * ls, D_MODEL)
    cprev = jnp.pad(state_conv[0], ((0, 0), (SAMPLE_PAD - (CONV_W - 1), 0), (0, 0)))
    cprev = cprev.reshape(ns * SAMPLE_PAD, CONV_DIM)
    qkv, z, xbc, dtraw, hnorm = _in_proj(xs_rows, mw.norm1, mw.win, mw.wdt)
    a_out, m_out, sk, sv, sc, sst = _sample_mixer(
        p["sinks"], qkv, z, xbc, dtraw, cprev,
        cache_swa_k[0].reshape(ns, WINDOW, KV_WIDTH), cache_swa_v[0].reshape(ns, WINDOW, KV_WIDTH),
        state_ssm[0].reshape(ns, D_INNER, D_STATE), *mixer_consts)
    x1s = _out_proj(xs_rows, a_out, m_out, hnorm, mw.wgate, mw.woa, mw.wob, mw.wo)
    y_sample = _mlp(x1s, p["norm2"], p["wup"], p["wdown"], fnorm).reshape(ns, ls, D_MODEL)
    sc = sc.reshape(ns, SAMPLE_PAD, CONV_DIM)[:, :CONV_W - 1]

    kv_shape = (1, -1, WINDOW, N_KV_A, HEAD_DIM_A)
    ssm_shape = (1, -1, N_HEADS_M, HEAD_DIM_M, D_STATE)
    return (y_prompt, y_sample,
            pk.reshape(kv_shape), pv.reshape(kv_shape), pc[None], pst.reshape(ssm_shape),
            sk.reshape(kv_shape), sv.reshape(kv_shape), sc[None], sst.reshape(ssm_shape))
```

```python
import collections
import functools

import jax
import jax.numpy as jnp
from jax import lax
from jax.experimental import pallas as pl
from jax.experimental.pallas import tpu as pltpu

F32 = jnp.float32
BF16 = jnp.bfloat16

D_MODEL = 1024
N_HEADS_A = 8
N_KV_A = 2
Q_PER_KV = N_HEADS_A // N_KV_A
HEAD_DIM_A = 64
WINDOW = 128
ATTN_WIDTH = N_HEADS_A * HEAD_DIM_A
KV_WIDTH = N_KV_A * HEAD_DIM_A
D_INNER = 1024
HEAD_DIM_M = 64
N_HEADS_M = D_INNER // HEAD_DIM_M
N_GROUPS_M = 2
GROUP_WIDTH = D_INNER // N_GROUPS_M
D_STATE = 128
CONV_W = 4
CONV_DIM = D_INNER + 2 * N_GROUPS_M * D_STATE
CHUNK = 128
D_FF = 4 * D_MODEL
EPS = 1e-6

LANES = 128
SUBLANES = 8
QKV_WIDTH = ATTN_WIDTH + 2 * KV_WIDTH
NEG_BIG = -1e30
VMEM_LIMIT = 56 * 1024 * 1024
SAMPLE_PAD = SUBLANES
SAMPLE_GROUP = 16


def _mm(a, b):
    return jnp.dot(a.astype(BF16), b.astype(BF16), preferred_element_type=F32)


def _mm_nt(a, b):
    return lax.dot_general(a.astype(BF16), b.astype(BF16), (((1,), (1,)), ((), ())),
                           preferred_element_type=F32)


def _rms(x, w):
    return x * lax.rsqrt(jnp.mean(x * x, axis=-1, keepdims=True) + EPS) * w


def _sigmoid(x):
    return 0.5 + 0.5 * jnp.tanh(0.5 * x)


def _silu(x):
    h = 0.5 * x
    return h + h * jnp.tanh(h)


def _softplus(x):
    return jnp.maximum(x, 0.0) + jnp.log(1.0 + jnp.exp(-jnp.abs(x)))


def _lane_lo(shape):
    return (lax.broadcasted_iota(jnp.int32, shape, len(shape) - 1) % LANES) < HEAD_DIM_A


def _dup_half(x, lo, first):
    xr = pltpu.roll(x, HEAD_DIM_A, axis=1)
    return jnp.where(lo, x, xr) if first else jnp.where(lo, xr, x)


def _stack_heads(q, kv, lo):
    qa = q[:, kv * 2 * LANES: kv * 2 * LANES + LANES]
    qb = q[:, kv * 2 * LANES + LANES: (kv + 1) * 2 * LANES]
    zero = jnp.zeros_like(qa)
    return jnp.concatenate([jnp.where(lo, qa, zero), jnp.where(lo, zero, qa),
                            jnp.where(lo, qb, zero), jnp.where(lo, zero, qb)], axis=0)


def _unstack_heads(o, rows, lo):
    return jnp.concatenate([jnp.where(lo, o[0:rows], o[rows:2 * rows]),
                            jnp.where(lo, o[2 * rows:3 * rows], o[3 * rows:4 * rows])], axis=1)


ProjBufs = collections.namedtuple("ProjBufs", "qkv z xbc dt")
MixedBufs = collections.namedtuple("MixedBufs", "attn ssd")
Carry = collections.namedtuple("Carry", "kprev vprev conv_tail ht dt_buf")
MixerWeights = collections.namedtuple(
    "MixerWeights", "norm1 win wgate wdt convw convb dtb alog dskip ssmn woa wob wo")

PIECE = 256
QKV_PIECE0 = 0
Z_PIECE0 = QKV_PIECE0 + QKV_WIDTH // PIECE
XBC_PIECE0 = Z_PIECE0 + D_INNER // PIECE
N_IN_PIECES = XBC_PIECE0 + CONV_DIM // PIECE
PREP_PIECES = 4


def _prep_pieces_kernel(w_ref, o_ref):
    for j in range(PREP_PIECES):
        o_ref[j] = w_ref[:, j * PIECE:(j + 1) * PIECE].astype(BF16)


def _prep_pieces(w, n_pieces=None, k_block=None):
    k_dim, n_dim = w.shape
    n_pieces = n_dim // PIECE if n_pieces is None else n_pieces
    k_block = k_dim if k_block is None else k_block
    cols = PREP_PIECES * PIECE
    return pl.pallas_call(
        _prep_pieces_kernel, grid=(pl.cdiv(n_pieces, PREP_PIECES), k_dim // k_block),
        in_specs=[pl.BlockSpec((k_block, cols), lambda p, k: (k, p))],
        out_specs=pl.BlockSpec((PREP_PIECES, k_block, PIECE), lambda p, k: (p, k, 0)),
        out_shape=jax.ShapeDtypeStruct((n_pieces, k_dim, PIECE), BF16), name="prep_weight_pieces",
        compiler_params=pltpu.CompilerParams(dimension_semantics=("arbitrary", "arbitrary"),
                                             vmem_limit_bytes=VMEM_LIMIT),
    )(w)


def _prep_pieces_t_kernel(wt_ref, o_ref):
    for j in range(PREP_PIECES):
        o_ref[j] = wt_ref[j * PIECE:(j + 1) * PIECE, :].T.astype(BF16)


def _prep_pieces_t(wt, n_pieces=None):
    n_dim, k_dim = wt.shape
    n_pieces = n_dim // PIECE if n_pieces is None else n_pieces
    return pl.pallas_call(
        _prep_pieces_t_kernel, grid=(pl.cdiv(n_pieces, PREP_PIECES),),
        in_specs=[pl.BlockSpec((PREP_PIECES * PIECE, k_dim), lambda p: (p, 0))],
        out_specs=pl.BlockSpec((PREP_PIECES, k_dim, PIECE), lambda p: (p, 0, 0)),
        out_shape=jax.ShapeDtypeStruct((n_pieces, k_dim, PIECE), BF16), name="prep_weight_pieces_t",
        compiler_params=pltpu.CompilerParams(dimension_semantics=("arbitrary",),
                                             vmem_limit_bytes=VMEM_LIMIT),
    )(wt)


def _prep_pieces_t_shifted_kernel(a_ref, b_ref, o_ref, *, shift):
    rows = jnp.concatenate([a_ref[shift:, :], b_ref[:shift, :]], axis=0)
    o_ref[0] = rows.T.astype(BF16)


def _prep_pieces_t_shifted(wt, row0, n_pieces):
    k_dim = wt.shape[1]
    block0, shift = divmod(row0, PIECE)
    assert shift % SUBLANES == 0 and shift > 0
    return pl.pallas_call(
        functools.partial(_prep_pieces_t_shifted_kernel, shift=shift), grid=(n_pieces,),
        in_specs=[pl.BlockSpec((PIECE, k_dim), lambda p: (block0 + p, 0)),
                  pl.BlockSpec((PIECE, k_dim), lambda p: (block0 + p + 1, 0))],
        out_specs=pl.BlockSpec((1, k_dim, PIECE), lambda p: (p, 0, 0)),
        out_shape=jax.ShapeDtypeStruct((n_pieces, k_dim, PIECE), BF16), name="prep_weight_pieces_t_shifted",
        compiler_params=pltpu.CompilerParams(dimension_semantics=("arbitrary",),
                                             vmem_limit_bytes=VMEM_LIMIT),
    )(wt, wt)


def _prep_dt_kernel(wt_ref, o_ref):
    rows = wt_ref[...]
    keep = lax.broadcasted_iota(jnp.int32, rows.shape, 0) < N_HEADS_M
    o_ref[...] = jnp.where(keep, rows, 0.0).T.astype(BF16)


def _prep_dt(wt, row0):
    k_dim = wt.shape[1]
    assert row0 % LANES == 0
    return pl.pallas_call(
        _prep_dt_kernel, grid=(1,),
        in_specs=[pl.BlockSpec((LANES, k_dim), lambda i: (row0 // LANES, 0))],
        out_specs=pl.BlockSpec((k_dim, LANES), lambda i: (0, 0)),
        out_shape=jax.ShapeDtypeStruct((k_dim, LANES), BF16), name="prep_weight_dt",
        compiler_params=pltpu.CompilerParams(dimension_semantics=("arbitrary",), vmem_limit_bytes=VMEM_LIMIT),
    )(wt)


def _cast_kernel(w_ref, o_ref):
    o_ref[...] = w_ref[...].astype(BF16)


def _cast_bf16(w, block):
    return pl.pallas_call(
        _cast_kernel, grid=(w.shape[0] // block[0], w.shape[1] // block[1]),
        in_specs=[pl.BlockSpec(block, lambda i, j: (i, j))], out_specs=pl.BlockSpec(block, lambda i, j: (i, j)),
        out_shape=jax.ShapeDtypeStruct(w.shape, BF16), name="cast_weight",
        compiler_params=pltpu.CompilerParams(dimension_semantics=("arbitrary", "arbitrary"),
                                             vmem_limit_bytes=VMEM_LIMIT),
    )(w)


StateOuts = collections.namedtuple("StateOuts", "k v conv ssm")


def _split3_bf16(x):
    hi = x.astype(BF16)
    r1 = x - hi.astype(F32)
    mid = r1.astype(BF16)
    return hi, mid, (r1 - mid.astype(F32)).astype(BF16)


def _piece_cols(p):
    return slice(p * PIECE, (p + 1) * PIECE)


def _in_proj_pieces(hb, w, dsts):
    for dst, first in zip(dsts, (QKV_PIECE0, Z_PIECE0, XBC_PIECE0)):
        for p in range(dst.shape[1] // PIECE):
            dst[:, _piece_cols(p)] = _mm(hb, w.win[first + p])
            yield


def _out_proj_pieces(x_ref, rows, hb, attn_ref, ssd_ref, w, out_ref):
    merged = []
    for half, (src_ref, w_ref) in enumerate(((attn_ref, w.woa), (ssd_ref, w.wob))):
        parts = []
        for p in range(D_MODEL // PIECE):
            gate = _sigmoid(_mm(hb, w.wgate[half * (D_MODEL // PIECE) + p]))
            yield
            parts.append(gate * _mm(src_ref[...], w_ref[p]))
            yield
        merged.append(jnp.concatenate(parts, axis=1))
    merged = (merged[0] + merged[1]).astype(BF16)
    for p in range(D_MODEL // PIECE):
        out_ref[rows, _piece_cols(p)] = x_ref[rows, _piece_cols(p)] + _mm(merged, w.wo[p])
        yield


def _stage_in_proj(x_ref, rows, w, proj, hnorm_ref):
    hb = _rms(x_ref[rows, :], w.norm1[...]).astype(BF16)
    hnorm_ref[...] = hb
    yield
    yield from _in_proj_pieces(hb, w, (proj.qkv, proj.z, proj.xbc))
    proj.dt[...] = _mm(hb, w.wdt[...])
    yield


def _stage_out_proj(x_ref, rows, mixed, w, x1_ref, hnorm_ref):
    hb = hnorm_ref[...]
    yield
    yield from _out_proj_pieces(x_ref, rows, hb, mixed.attn, mixed.ssd, w, x1_ref)


def _alternate(*gens):
    gens = list(gens)
    while gens:
        for g in list(gens):
            try:
                next(g)
                yield
            except StopIteration:
                gens.remove(g)


FILL_PER_PIECE = 1


def _trace_interleaved(primary, filler):
    for _ in primary:
        for _ in range(FILL_PER_PIECE):
            next(filler, None)
    for _ in filler:
        pass


def _stage_mix(proj, mixed, rows, carry, w, sinks_ref, first):
    T = CHUNK
    reset = (lambda a: a) if first is None else (lambda a: jnp.where(first, 0.0, a))

    qkv = proj.qkv[rows, :]
    q = qkv[:, :ATTN_WIDTH] * (HEAD_DIM_A ** -0.5)
    k = qkv[:, ATTN_WIDTH:ATTN_WIDTH + KV_WIDTH]
    v = qkv[:, ATTN_WIDTH + KV_WIDTH:]
    kk = jnp.concatenate([carry.kprev[...], k], axis=0)
    vv = jnp.concatenate([carry.vprev[...], v], axis=0)
    carry.kprev[...] = k
    carry.vprev[...] = v

    lo = _lane_lo((T, LANES))
    lo2 = _lane_lo((2 * T, LANES))
    r = lax.broadcasted_iota(jnp.int32, (T, 2 * T), 0)
    col = lax.broadcasted_iota(jnp.int32, (T, 2 * T), 1)
    valid = (col >= r) & (col <= r + WINDOW)
    if first is not None:
        valid = valid & (col >= jnp.where(first, T, 0))
    a_slabs = []
    for kv in range(N_KV_A):
        kd = _dup_half(kk, lo2, kv == 0).astype(BF16)
        vd = _dup_half(vv, lo2, kv == 0).astype(BF16)
        s = _mm_nt(_stack_heads(q, kv, lo), kd)
        es, inv = [], []
        for g in range(Q_PER_KV):
            sk = sinks_ref[kv * Q_PER_KV + g]
            sg = jnp.where(valid, s[g * T:(g + 1) * T], NEG_BIG)
            m = jnp.maximum(jnp.max(sg, axis=-1, keepdims=True), sk)
            e = jnp.exp(sg - m)
            den = jnp.sum(e, axis=-1, keepdims=True) + jnp.exp(sk - m)
            es.append(e.astype(BF16))
            inv.append(1.0 / den)
        yield
        o = _mm(jnp.concatenate(es, axis=0), vd) * jnp.concatenate(inv, axis=0)
        a_slabs.append(_unstack_heads(o, T, lo))
        yield
    mixed.attn[rows, :] = jnp.concatenate(a_slabs, axis=1).astype(BF16)

    xr = proj.xbc[rows, :]
    prev_tile = jnp.concatenate([reset(carry.conv_tail[...]), xr[:T - SUBLANES]], axis=0)
    carry.conv_tail[...] = xr[T - SUBLANES:]
    rm = lax.broadcasted_iota(jnp.int32, (T, 1), 0) % SUBLANES
    yc = xr * w.convw[CONV_W - 1:CONV_W, :]
    for shift in range(1, CONV_W):
        shifted = _tile_roll(jnp.where(rm >= SUBLANES - shift, prev_tile, xr), shift)
        yc = yc + shifted * w.convw[CONV_W - 1 - shift:CONV_W - shift, :]
        yield
    xbc = _silu(yc + w.convb[...])
    xs = xbc[:, :D_INNER]
    bm = xbc[:, D_INNER:D_INNER + N_GROUPS_M * D_STATE]
    cm = xbc[:, D_INNER + N_GROUPS_M * D_STATE:]

    z = proj.z[rows, :]
    carry.dt_buf[...] = _softplus(proj.dt[rows, :] + w.dtb[...])
    dt = carry.dt_buf[...]
    rr = lax.broadcasted_iota(jnp.int32, (T, T), 0)
    cc = lax.broadcasted_iota(jnp.int32, (T, T), 1)
    tril = cc <= rr
    cs = jnp.dot(jnp.where(tril, 1.0, 0.0).astype(BF16),
                 jnp.concatenate(_split3_bf16(dt * -jnp.exp(w.alog[...])), axis=1), preferred_element_type=F32)
    a_cs = cs[:, :LANES] + cs[:, LANES:2 * LANES] + cs[:, 2 * LANES:]
    a_cs_t = a_cs.T
    yield

    ys = []
    pairs_per_group = N_HEADS_M // 2 // N_GROUPS_M
    for g in range(N_GROUPS_M):
        gcols = slice(g * GROUP_WIDTH, (g + 1) * GROUP_WIDTH)
        cg = cm[:, g * D_STATE:(g + 1) * D_STATE]
        bg = bm[:, g * D_STATE:(g + 1) * D_STATE]
        cb = _mm_nt(cg, bg)
        h_prev = reset(carry.ht[:, gcols])
        y_off = _mm(cg, h_prev)
        xds, decays = [], []
        for jj in range(pairs_per_group):
            j = g * pairs_per_group + jj
            sl = slice(j * LANES, (j + 1) * LANES)
            ws, colbs, dtbs = [], [], []
            for h in (2 * j, 2 * j + 1):
                colb = jnp.broadcast_to(a_cs[:, h:h + 1], (T, T))
                rowb = jnp.broadcast_to(a_cs_t[h:h + 1, :], (T, T))
                seg = jnp.where(tril, jnp.exp(colb - rowb), 0.0)
                ws.append((cb * seg).astype(BF16))
                colbs.append(colb)
                dtbs.append(jnp.broadcast_to(dt[:, h:h + 1], (T, LANES)))
            dt_e = jnp.where(lo, dtbs[0], dtbs[1])
            acs_e = jnp.where(lo, colbs[0], colbs[1])
            xs_j = xs[:, sl]
            xdt = xs_j * dt_e
            zero = jnp.zeros_like(xdt)
            rhs = jnp.concatenate([jnp.where(lo, xdt, zero), jnp.where(lo, zero, xdt)], axis=0)
            y_diag = _mm(jnp.concatenate(ws, axis=1), rhs)
            alast = acs_e[T - 1:T, :]
            xds.append(xdt * jnp.exp(alast - acs_e))
            decays.append(jnp.exp(alast))
            y = y_diag + y_off[:, jj * LANES:(jj + 1) * LANES] * jnp.exp(acs_e) + xs_j * w.dskip[:, sl]
            ys.append(y * _silu(z[:, sl]))
            yield
        carry.ht[:, gcols] = (h_prev * jnp.concatenate(decays, axis=1)
                              + _mm(bg.T, jnp.concatenate(xds, axis=1)))
        yield

    m_slabs = []
    for g in range(N_GROUPS_M):
        grp = ys[g * pairs_per_group:(g + 1) * pairs_per_group]
        ssq = grp[0] * grp[0]
        for y in grp[1:]:
            ssq = ssq + y * y
        scale = lax.rsqrt(jnp.sum(ssq, axis=-1, keepdims=True) * (1.0 / GROUP_WIDTH) + EPS)
        m_slabs.extend([y * scale for y in grp])
    mixed.ssd[rows, :] = (jnp.concatenate(m_slabs, axis=1) * w.ssmn[...]).astype(BF16)


def _chain(*gens):
    for g in gens:
        yield from g


def _prompt_mixer_kernel(sinks_ref, x_in_ref, x_res_ref, *refs, chunks_per_seq, n_pairs):
    refs = list(refs)
    take = lambda n: [refs.pop(0) for _ in range(n)]
    w = MixerWeights(*take(len(MixerWeights._fields)))
    x1_ref, = take(1)
    outs = StateOuts(*take(4))
    proj = (ProjBufs(*take(4)), ProjBufs(*take(4)))
    mixed = (MixedBufs(*take(2)), MixedBufs(*take(2)))
    hnorm = take(2)
    carry = Carry(*take(len(Carry._fields)))
    assert not refs
    T = CHUNK
    s = pl.program_id(0)
    every = slice(None)

    @pl.when(s == 0)
    def _():
        for ref in list(proj[1] + mixed[0] + mixed[1] + carry) + hnorm:
            ref[...] = jnp.zeros_like(ref)

    first = (s - 1) % (chunks_per_seq // 2) == 0

    def step(cur):
        prv = 1 - cur
        dense = _alternate(_stage_out_proj(x_res_ref, every, mixed[cur], w, x1_ref, hnorm[cur]),
                           _stage_in_proj(x_in_ref, every, w, proj[cur], hnorm[cur]))
        mix = _chain(_stage_mix(proj[prv], mixed[prv], slice(0, T), carry, w, sinks_ref, first),
                     _stage_mix(proj[prv], mixed[prv], slice(T, 2 * T), carry, w, sinks_ref, None))
        _trace_interleaved(mix, dense)

    for parity in (0, 1):
        pl.when(s % 2 == parity)(functools.partial(step, parity))

    @pl.when(s <= n_pairs)
    def _():
        outs.k[0] = carry.kprev[...]
        outs.v[0] = carry.vprev[...]
        outs.conv[0] = carry.conv_tail[SUBLANES - (CONV_W - 1):, :]
        outs.ssm[0] = carry.ht[...].T


def _const_spec(shape):
    return pl.BlockSpec(shape, lambda *_: (0,) * len(shape))


def _prompt_mixer(x, sinks, weights):
    nb, seq, _ = x.shape
    assert seq % (2 * CHUNK) == 0
    chunks_per_seq = seq // CHUNK
    n_pairs = nb * chunks_per_seq // 2
    pair = 2 * CHUNK
    consts = tuple(weights)
    seq_of_mix = lambda s: jnp.clip((2 * s - 1) // chunks_per_seq, 0, nb - 1)
    in_specs = ([pl.BlockSpec(memory_space=pltpu.SMEM),
                 pl.BlockSpec((pair, D_MODEL), lambda s: (jnp.minimum(s, n_pairs - 1), 0)),
                 pl.BlockSpec((pair, D_MODEL), lambda s: (jnp.maximum(s - 2, 0), 0))]
                + [_const_spec(a.shape) for a in consts])
    out_shape = (jax.ShapeDtypeStruct((nb * seq, D_MODEL), F32),
                 jax.ShapeDtypeStruct((nb, WINDOW, KV_WIDTH), F32),
                 jax.ShapeDtypeStruct((nb, WINDOW, KV_WIDTH), F32),
                 jax.ShapeDtypeStruct((nb, CONV_W - 1, CONV_DIM), F32),
                 jax.ShapeDtypeStruct((nb, D_INNER, D_STATE), F32))
    out_specs = (pl.BlockSpec((pair, D_MODEL), lambda s: (jnp.maximum(s - 2, 0), 0)),
                 pl.BlockSpec((1, WINDOW, KV_WIDTH), lambda s: (seq_of_mix(s), 0, 0)),
                 pl.BlockSpec((1, WINDOW, KV_WIDTH), lambda s: (seq_of_mix(s), 0, 0)),
                 pl.BlockSpec((1, CONV_W - 1, CONV_DIM), lambda s: (seq_of_mix(s), 0, 0)),
                 pl.BlockSpec((1, D_INNER, D_STATE), lambda s: (seq_of_mix(s), 0, 0)))
    proj_bufs = [pltpu.VMEM((pair, QKV_WIDTH), F32), pltpu.VMEM((pair, D_INNER), F32),
                 pltpu.VMEM((pair, CONV_DIM), F32), pltpu.VMEM((pair, LANES), F32)]
    mixed_bufs = [pltpu.VMEM((pair, ATTN_WIDTH), BF16), pltpu.VMEM((pair, D_INNER), BF16)]
    hnorm_bufs = [pltpu.VMEM((pair, D_MODEL), BF16)] * 2
    carry = [pltpu.VMEM((CHUNK, KV_WIDTH), F32), pltpu.VMEM((CHUNK, KV_WIDTH), F32),
             pltpu.VMEM((SUBLANES, CONV_DIM), F32), pltpu.VMEM((D_STATE, D_INNER), F32),
             pltpu.VMEM((CHUNK, LANES), F32)]
    x_rows = x.reshape(nb * seq, D_MODEL)
    return pl.pallas_call(
        functools.partial(_prompt_mixer_kernel, chunks_per_seq=chunks_per_seq, n_pairs=n_pairs),
        grid=(n_pairs + 2,), in_specs=in_specs, out_specs=out_specs,
        out_shape=out_shape, scratch_shapes=proj_bufs * 2 + mixed_bufs * 2 + hnorm_bufs + carry,
        name="prompt_mixer",
        compiler_params=pltpu.CompilerParams(dimension_semantics=("arbitrary",),
                                             vmem_limit_bytes=VMEM_LIMIT),
    )(sinks, x_rows, x_rows, *consts)


MLP_TILE = 512
FF_BLOCK = 1024


def _mlp_kernel(x_ref, norm2_ref, wup_ref, wdown_ref, fnorm_ref, y_ref):
    x = x_ref[...]
    hm = _rms(x, norm2_ref[...]).astype(BF16)
    acc = x
    for j in range(D_FF // FF_BLOCK):
        h = _mm(hm, wup_ref[:, j * FF_BLOCK:(j + 1) * FF_BLOCK])
        h = jnp.square(jnp.maximum(h, 0.0))
        acc = acc + _mm(h, wdown_ref[j * FF_BLOCK:(j + 1) * FF_BLOCK, :])
    y_ref[...] = _rms(acc, fnorm_ref[...])


def _mlp(x, norm2, wup, wdown, fnorm):
    rows = x.shape[0]
    assert rows % MLP_TILE == 0
    consts = (norm2, wup, wdown, fnorm)
    return pl.pallas_call(
        _mlp_kernel, grid=(rows // MLP_TILE,),
        in_specs=[pl.BlockSpec((MLP_TILE, D_MODEL), lambda i: (i, 0))] + [_const_spec(a.shape) for a in consts],
        out_specs=pl.BlockSpec((MLP_TILE, D_MODEL), lambda i: (i, 0)),
        out_shape=jax.ShapeDtypeStruct((rows, D_MODEL), F32), name="mlp",
        compiler_params=pltpu.CompilerParams(dimension_semantics=("arbitrary",),
                                             vmem_limit_bytes=VMEM_LIMIT),
    )(x, *consts)


ROW_TILE = 256


def _mixer_weights(**refs):
    return MixerWeights(**{f: refs.get(f) for f in MixerWeights._fields})


def _in_proj_kernel(x_ref, norm1_ref, win_ref, wdt_ref, qkv_ref, z_ref, xbc_ref, dt_ref, hnorm_ref):
    hb = _rms(x_ref[...], norm1_ref[...]).astype(BF16)
    hnorm_ref[...] = hb
    for _ in _in_proj_pieces(hb, _mixer_weights(win=win_ref), (qkv_ref, z_ref, xbc_ref)):
        pass
    dt_ref[...] = _mm(hb, wdt_ref[...])


def _in_proj(x, norm1, win, wdt):
    rows = x.shape[0]
    assert rows % ROW_TILE == 0
    consts = (norm1, win, wdt)
    outs = ((QKV_WIDTH, F32), (D_INNER, F32), (CONV_DIM, F32), (LANES, F32), (D_MODEL, BF16))
    return pl.pallas_call(
        _in_proj_kernel, grid=(rows // ROW_TILE,),
        in_specs=[pl.BlockSpec((ROW_TILE, D_MODEL), lambda i: (i, 0))] + [_const_spec(a.shape) for a in consts],
        out_specs=tuple(pl.BlockSpec((ROW_TILE, width), lambda i: (i, 0)) for width, _ in outs),
        out_shape=tuple(jax.ShapeDtypeStruct((rows, width), dtype) for width, dtype in outs),
        name="sample_in_proj",
        compiler_params=pltpu.CompilerParams(dimension_semantics=("arbitrary",),
                                             vmem_limit_bytes=VMEM_LIMIT),
    )(x, *consts)


def _tile_roll(x, shift):
    rows, width = x.shape
    tiles = x.reshape(rows // SUBLANES, SUBLANES, width)
    return pltpu.roll(tiles, shift % SUBLANES, axis=1).reshape(rows, width)


def _spread_rows(x):
    tiles = []
    for t in range(x.shape[0] // SUBLANES):
        two_seqs = x[t * SUBLANES:(t + 1) * SUBLANES]
        tiles += [two_seqs, pltpu.roll(two_seqs, SUBLANES // 2, axis=0)]
    return jnp.concatenate(tiles, axis=0)


def _gather_rows(y):
    low = lax.broadcasted_iota(jnp.int32, (SUBLANES, 1), 0) < SUBLANES // 2
    tiles = []
    for t in range(y.shape[0] // (2 * SUBLANES)):
        a = y[2 * t * SUBLANES:(2 * t + 1) * SUBLANES]
        b = y[(2 * t + 1) * SUBLANES:(2 * t + 2) * SUBLANES]
        tiles.append(jnp.where(low, a, pltpu.roll(b, SUBLANES // 2, axis=0)))
    return jnp.concatenate(tiles, axis=0)


def _expand_heads(cols, expand):
    terms = []
    for c in cols:
        hi = c.astype(BF16)
        r1 = c - hi.astype(F32)
        mid = r1.astype(BF16)
        lo = (r1 - mid.astype(F32)).astype(BF16)
        terms.extend([hi, mid, lo])
    rows = cols[0].shape[0]
    out = jnp.dot(jnp.concatenate(terms, axis=0), expand, preferred_element_type=F32)
    return [out[(3 * i) * rows:(3 * i + 1) * rows] + out[(3 * i + 1) * rows:(3 * i + 2) * rows]
            + out[(3 * i + 2) * rows:(3 * i + 3) * rows] for i in range(len(cols))]


def _sample_mixer_kernel(sinks_ref, qkv_ref, z_ref, xbc_ref, dtraw_ref, cprev_ref, kc_ref, vc_ref, st_ref,
                         convw_ref, convb_ref, dtb_ref, alog_ref, dskip_ref, ssmn_ref,
                         aout_ref, mout_ref, nk_ref, nv_ref, nconv_ref, nst_ref):
    R = SAMPLE_GROUP * SAMPLE_PAD
    L = 4
    rm = lax.broadcasted_iota(jnp.int32, (R, 1), 0) % SAMPLE_PAD

    xr = xbc_raw = _spread_rows(xbc_ref[...])
    with_prev = jnp.where(rm >= SAMPLE_PAD - (CONV_W - 1), cprev_ref[...], xr)
    yc = xr * convw_ref[CONV_W - 1:CONV_W, :]
    for kshift in range(1, CONV_W):
        yc = yc + _tile_roll(with_prev, kshift) * convw_ref[CONV_W - 1 - kshift:CONV_W - kshift, :]
    nconv_ref[...] = _tile_roll(xbc_raw, -1)
    xbc = _silu(yc + convb_ref[...])
    xs = xbc[:, :D_INNER]
    bm = xbc[:, D_INNER:D_INNER + N_GROUPS_M * D_STATE]
    cm = xbc[:, D_INNER + N_GROUPS_M * D_STATE:]

    dt = _softplus(_spread_rows(dtraw_ref[...]) + dtb_ref[...])
    d_a = dt * (-jnp.exp(alog_ref[...]))
    a_cs = d_a
    suf = jnp.zeros_like(d_a)
    for kshift in range(1, L):
        a_cs = a_cs + jnp.where(rm >= kshift, _tile_roll(d_a, kshift), 0.0)
        suf = suf + jnp.where(rm <= L - 1 - kshift, _tile_roll(d_a, -kshift), 0.0)

    expand = (lax.broadcasted_iota(jnp.int32, (LANES, D_INNER), 1) // HEAD_DIM_M
              == lax.broadcasted_iota(jnp.int32, (LANES, D_INNER), 0)).astype(BF16)
    dt_e, acs_e, suf_e = _expand_heads([dt, a_cs, suf], expand)
    xdt = xs * dt_e
    real = rm < L
    xd_t = jnp.where(real, xdt * jnp.exp(suf_e), 0.0).T.astype(BF16)

    y = xs * dskip_ref[...]
    for kshift in range(L):
        bk = bm if kshift == 0 else _tile_roll(bm, kshift)
        cb = cm * bk
        cb_e = jnp.concatenate(
            [jnp.broadcast_to(jnp.sum(cb[:, g * D_STATE:(g + 1) * D_STATE], axis=-1, keepdims=True),
                              (R, GROUP_WIDTH)) for g in range(N_GROUPS_M)], axis=1)
        if kshift == 0:
            y = y + cb_e * xdt
        else:
            seg = jnp.exp(acs_e - _tile_roll(acs_e, kshift))
            y = y + jnp.where(rm >= kshift, cb_e * seg * _tile_roll(xdt, kshift), 0.0)

    qkv = _spread_rows(qkv_ref[...])
    q = qkv[:, :ATTN_WIDTH] * (HEAD_DIM_A ** -0.5)
    q_swapped = jnp.concatenate(
        [pltpu.roll(q[:, s * LANES:(s + 1) * LANES], HEAD_DIM_A, axis=1) for s in range(ATTN_WIDTH // LANES)], axis=1)
    kn = qkv[:, ATTN_WIDTH:ATTN_WIDTH + KV_WIDTH]
    vn = qkv[:, ATTN_WIDTH + KV_WIDTH:]
    lo8 = _lane_lo((SAMPLE_PAD, LANES))
    zero8 = jnp.zeros((SAMPLE_PAD, LANES), F32)
    seq_rows = [slice(i * SAMPLE_PAD, (i + 1) * SAMPLE_PAD) for i in range(SAMPLE_GROUP)]
    HROWS = N_HEADS_A * SAMPLE_PAD

    s_c, s_n = [], []
    for rows in seq_rows:
        pieces = []
        for kv in range(N_KV_A):
            for g in range(Q_PER_KV):
                slab = kv * (Q_PER_KV // 2) + g // 2
                src = q if g % 2 == kv else q_swapped
                piece = src[rows, slab * LANES:(slab + 1) * LANES]
                pieces.append(jnp.where(lo8, piece, zero8) if kv == 0 else jnp.where(lo8, zero8, piece))
        lhs = jnp.concatenate(pieces, axis=0).astype(BF16)
        i = len(s_c)
        s_c.append(_mm(lhs, kc_ref[i]))
        s_n.append(_mm_nt(lhs, kn[rows]))
    s_c = jnp.concatenate(s_c, axis=0)
    s_n = jnp.concatenate(s_n, axis=0)
    n_rows = SAMPLE_GROUP * HROWS
    tok_c = lax.broadcasted_iota(jnp.int32, (n_rows, WINDOW), 0) % SAMPLE_PAD
    s_c = jnp.where(lax.broadcasted_iota(jnp.int32, (n_rows, WINDOW), 1) >= tok_c, s_c, NEG_BIG)
    tok_n = lax.broadcasted_iota(jnp.int32, (n_rows, SAMPLE_PAD), 0) % SAMPLE_PAD
    col_n = lax.broadcasted_iota(jnp.int32, (n_rows, SAMPLE_PAD), 1)
    s_n = jnp.where((col_n <= tok_n) & (col_n < L), s_n, NEG_BIG)
    sk = jnp.concatenate([jnp.full((SAMPLE_PAD, 1), sinks_ref[h], F32) for h in range(N_HEADS_A)] * SAMPLE_GROUP,
                         axis=0)
    m = jnp.maximum(jnp.maximum(jnp.max(s_c, axis=-1, keepdims=True), jnp.max(s_n, axis=-1, keepdims=True)), sk)
    e_c = jnp.exp(s_c - m)
    e_n = jnp.exp(s_n - m)
    inv_den = 1.0 / (jnp.sum(e_c, axis=-1, keepdims=True) + jnp.sum(e_n, axis=-1, keepdims=True) + jnp.exp(sk - m))
    e_c = e_c.astype(BF16)
    e_n = e_n.astype(BF16)
    o = jnp.concatenate([_mm_nt(e_c[i * HROWS:(i + 1) * HROWS], vc_ref[i]) + _mm(e_n[i * HROWS:(i + 1) * HROWS], vn[rows])
                         for i, rows in enumerate(seq_rows)], axis=0) * inv_den
    o_swapped = pltpu.roll(o, HEAD_DIM_A, axis=1)
    a_rows = []
    for i in range(SAMPLE_GROUP):
        blk = lambda arr, kv, g: arr[i * HROWS + (kv * Q_PER_KV + g) * SAMPLE_PAD:
                                     i * HROWS + (kv * Q_PER_KV + g + 1) * SAMPLE_PAD]
        slabs = []
        for kv in range(N_KV_A):
            for j in range(Q_PER_KV // 2):
                first = blk(o if kv == 0 else o_swapped, kv, 2 * j)
                second = blk(o_swapped if kv == 0 else o, kv, 2 * j + 1)
                slabs.append(jnp.where(lo8, first, second))
        a_rows.append(jnp.concatenate(slabs, axis=1))
    aout_ref[...] = _gather_rows(jnp.concatenate(a_rows, axis=0))

    key = lax.broadcasted_iota(jnp.int32, (KV_WIDTH, WINDOW), 1)
    for cache_ref, new, out_ref in ((kc_ref, kn, nk_ref), (vc_ref, vn, nv_ref)):
        new_t = new.T
        for i in range(SAMPLE_GROUP):
            kept = pltpu.roll(cache_ref[i], WINDOW - L, axis=1)
            fresh = pltpu.roll(new_t, (WINDOW - L - i * SAMPLE_PAD) % WINDOW, axis=1)
            out_ref[i] = jnp.where(key >= WINDOW - L, fresh, kept)

    row_r = lax.broadcasted_iota(jnp.int32, (R, D_STATE), 0)
    heads_per_group = N_HEADS_M // N_GROUPS_M
    yoff = []
    for i, rows in enumerate(seq_rows):
        state = st_ref[i]
        state_b = state.astype(BF16)
        a_tot = a_cs[i * SAMPLE_PAD + L - 1:i * SAMPLE_PAD + L, :]
        in_seq = (row_r >= i * SAMPLE_PAD) & (row_r < (i + 1) * SAMPLE_PAD)
        yo = []
        for g in range(N_GROUPS_M):
            gs = slice(g * GROUP_WIDTH, (g + 1) * GROUP_WIDTH)
            yo.append(_mm_nt(cm[rows, g * D_STATE:(g + 1) * D_STATE], state_b[gs]))
            bsel = jnp.where(in_seq, bm[:, g * D_STATE:(g + 1) * D_STATE], 0.0)
            upd = _mm(xd_t[gs, :], bsel)
            for hh in range(heads_per_group):
                h = g * heads_per_group + hh
                hs = slice(h * HEAD_DIM_M, (h + 1) * HEAD_DIM_M)
                decay = jnp.exp(jnp.broadcast_to(a_tot[:, h:h + 1], (HEAD_DIM_M, D_STATE)))
                nst_ref[i, hs, :] = state[hs] * decay + upd[hh * HEAD_DIM_M:(hh + 1) * HEAD_DIM_M]
        yoff.append(jnp.concatenate(yo, axis=1))
    yoff = jnp.concatenate(yoff, axis=0)

    y = (y + yoff * jnp.exp(acs_e)) * _silu(_spread_rows(z_ref[...]))
    outs = []
    for g in range(N_GROUPS_M):
        yg = y[:, g * GROUP_WIDTH:(g + 1) * GROUP_WIDTH]
        outs.append(yg * lax.rsqrt(jnp.mean(yg * yg, axis=-1, keepdims=True) + EPS))
    mout_ref[...] = _gather_rows(jnp.concatenate(outs, axis=1) * ssmn_ref[...])


def _sample_mixer(sinks, qkv, z, xbc, dtraw, cprev, kc, vc, st, convw, convb, dtb, alog, dskip, ssmn):
    nseq = kc.shape[0]
    assert nseq % SAMPLE_GROUP == 0
    R = SAMPLE_GROUP * SAMPLE_PAD
    T4 = SAMPLE_GROUP * 4
    rows = nseq * SAMPLE_PAD
    consts = (convw, convb, dtb, alog, dskip, ssmn)
    row_spec = lambda w: pl.BlockSpec((R, w), lambda i: (i, 0))
    tok_spec = lambda w: pl.BlockSpec((T4, w), lambda i: (i, 0))
    seq_spec = lambda a, b: pl.BlockSpec((SAMPLE_GROUP, a, b), lambda i: (i, 0, 0))
    in_specs = ([pl.BlockSpec(memory_space=pltpu.SMEM),
                 tok_spec(QKV_WIDTH), tok_spec(D_INNER), tok_spec(CONV_DIM), tok_spec(LANES), row_spec(CONV_DIM),
                 seq_spec(WINDOW, KV_WIDTH), seq_spec(WINDOW, KV_WIDTH), seq_spec(D_INNER, D_STATE)]
                + [_const_spec(a.shape) for a in consts])
    out_shape = (jax.ShapeDtypeStruct((nseq * 4, ATTN_WIDTH), F32), jax.ShapeDtypeStruct((nseq * 4, D_INNER), F32),
                 jax.ShapeDtypeStruct((nseq, WINDOW, KV_WIDTH), F32),
                 jax.ShapeDtypeStruct((nseq, WINDOW, KV_WIDTH), F32),
                 jax.ShapeDtypeStruct((rows, CONV_DIM), F32),
                 jax.ShapeDtypeStruct((nseq, D_INNER, D_STATE), F32))
    out_specs = (tok_spec(ATTN_WIDTH), tok_spec(D_INNER), seq_spec(WINDOW, KV_WIDTH), seq_spec(WINDOW, KV_WIDTH),
                 row_spec(CONV_DIM), seq_spec(D_INNER, D_STATE))
    return pl.pallas_call(
        _sample_mixer_kernel, grid=(nseq // SAMPLE_GROUP,), in_specs=in_specs, out_specs=out_specs,
        out_shape=out_shape, name="sample_mixer",
        compiler_params=pltpu.CompilerParams(dimension_semantics=("arbitrary",),
                                             vmem_limit_bytes=VMEM_LIMIT),
    )(sinks, qkv, z, xbc, dtraw, cprev, kc, vc, st, *consts)


def _out_proj_kernel(x_ref, a_ref, m_ref, hnorm_ref, wgate_ref, woa_ref, wob_ref, wo_ref, x1_ref):
    w = _mixer_weights(wgate=wgate_ref, woa=woa_ref, wob=wob_ref, wo=wo_ref)
    for _ in _out_proj_pieces(x_ref, slice(None), hnorm_ref[...], a_ref, m_ref, w, x1_ref):
        pass


def _out_proj(x, a_out, m_out, hnorm, wgate, woa, wob, wo):
    rows = x.shape[0]
    assert rows % ROW_TILE == 0
    consts = (wgate, woa, wob, wo)
    widths = (D_MODEL, ATTN_WIDTH, D_INNER, D_MODEL)
    return pl.pallas_call(
        _out_proj_kernel, grid=(rows // ROW_TILE,),
        in_specs=([pl.BlockSpec((ROW_TILE, w), lambda i: (i, 0)) for w in widths]
                  + [_const_spec(a.shape) for a in consts]),
        out_specs=pl.BlockSpec((ROW_TILE, D_MODEL), lambda i: (i, 0)),
        out_shape=jax.ShapeDtypeStruct((rows, D_MODEL), F32), name="sample_out_proj",
        compiler_params=pltpu.CompilerParams(dimension_semantics=("arbitrary",),
                                             vmem_limit_bytes=VMEM_LIMIT),
    )(x, a_out, m_out, hnorm, *consts)


def _layer_params(norm1, w_in, sinks, conv_w, conv_b, dt_bias, a_log, d_skip, ssm_norm, w_oa, w_ob, w_o,
                  norm2, w_up, w_down):
    dt0 = QKV_WIDTH + D_INNER + CONV_DIM
    assert dt0 == N_IN_PIECES * PIECE
    pad_lanes = lambda a: jnp.pad(a, ((0, 0), (0, LANES - a.shape[1])))
    w_in_t = w_in.T
    mixer = MixerWeights(
        norm1=norm1[None, :], win=_prep_pieces_t(w_in_t, N_IN_PIECES),
        wgate=_prep_pieces_t_shifted(w_in_t, dt0 + N_HEADS_M, 2 * D_MODEL // PIECE),
        wdt=_prep_dt(w_in_t, dt0),
        convw=conv_w, convb=conv_b[None, :], dtb=pad_lanes(dt_bias[None, :]), alog=pad_lanes(a_log[None, :]),
        dskip=jnp.repeat(d_skip, HEAD_DIM_M)[None, :], ssmn=ssm_norm[None, :],
        woa=_prep_pieces(w_oa), wob=_prep_pieces(w_ob), wo=_prep_pieces(w_o))
    wide = (D_MODEL, D_MODEL)
    return dict(sinks=sinks.astype(F32), mixer=mixer,
                norm2=norm2[None, :], wup=_cast_bf16(w_up, wide), wdown=_cast_bf16(w_down, wide))


def kernel(x_prompt, x_sample, cache_swa_k, cache_swa_v, state_conv, state_ssm, norm1, w_in, sinks, conv_w,
           conv_b, dt_bias, a_log, d_skip, ssm_norm, w_oa, w_ob, w_o, norm2, w_up, w_down, final_norm):
    depth = w_in.shape[0]
    assert depth == 1
    nb, seq, _ = x_prompt.shape
    ns, ls, _ = x_sample.shape
    assert ls == 4
    p = _layer_params(norm1[0], w_in[0], sinks[0], conv_w[0], conv_b[0], dt_bias[0], a_log[0], d_skip[0],
                      ssm_norm[0], w_oa[0], w_ob[0], w_o[0], norm2[0], w_up[0], w_down[0])
    fnorm = final_norm[None, :]
    mw = p["mixer"]
    mixer_consts = (mw.convw, mw.convb, mw.dtb, mw.alog, mw.dskip, mw.ssmn)

    x1p, pk, pv, pc, pst = _prompt_mixer(x_prompt, p["sinks"], mw)
    y_prompt = _mlp(x1p, p["norm2"], p["wup"], p["wdown"], fnorm)
    y_prompt = y_prompt.reshape(nb, seq, D_MODEL)

    xs_rows = x_sample.reshape(ns * ls, D_MODEL)
    dim_major = lambda a: jnp.swapaxes(a, 1, 2)
    cprev = jnp.pad(state_conv[0], ((0, 0), (SAMPLE_PAD - (CONV_W - 1), 0), (0, 0)))
    cprev = cprev.reshape(ns * SAMPLE_PAD, CONV_DIM)
    qkv, z, xbc, dtraw, hnorm = _in_proj(xs_rows, mw.norm1, mw.win, mw.wdt)
    a_out, m_out, sk, sv, sc, sst = _sample_mixer(
        p["sinks"], qkv, z, xbc, dtraw, cprev,
        dim_major(cache_swa_k[0].reshape(ns, WINDOW, KV_WIDTH)), dim_major(cache_swa_v[0].reshape(ns, WINDOW, KV_WIDTH)),
        state_ssm[0].reshape(ns, D_INNER, D_STATE), *mixer_consts)
    sk, sv = dim_major(sk), dim_major(sv)
    x1s = _out_proj(xs_rows, a_out, m_out, hnorm, mw.wgate, mw.woa, mw.wob, mw.wo)
    y_sample = _mlp(x1s, p["norm2"], p["wup"], p["wdown"], fnorm).reshape(ns, ls, D_MODEL)
    sc = sc.reshape(ns, SAMPLE_PAD, CONV_DIM)[:, :CONV_W - 1]

    kv_shape = (1, -1, WINDOW, N_KV_A, HEAD_DIM_A)
    ssm_shape = (1, -1, N_HEADS_M, HEAD_DIM_M, D_STATE)
    return (y_prompt, y_sample,
            pk.reshape(kv_shape), pv.reshape(kv_shape), pc[None], pst.reshape(ssm_shape),
            sk.reshape(kv_shape), sv.reshape(kv_shape), sc[None], sst.reshape(ssm_shape))
```

```python
import collections
import functools

import jax
import jax.numpy as jnp
from jax import lax
from jax.experimental import pallas as pl
from jax.experimental.pallas import tpu as pltpu

F32 = jnp.float32
BF16 = jnp.bfloat16

D_MODEL = 1024
N_HEADS_A = 8
N_KV_A = 2
Q_PER_KV = N_HEADS_A // N_KV_A
HEAD_DIM_A = 64
WINDOW = 128
ATTN_WIDTH = N_HEADS_A * HEAD_DIM_A
KV_WIDTH = N_KV_A * HEAD_DIM_A
D_INNER = 1024
HEAD_DIM_M = 64
N_HEADS_M = D_INNER // HEAD_DIM_M
N_GROUPS_M = 2
GROUP_WIDTH = D_INNER // N_GROUPS_M
D_STATE = 128
CONV_W = 4
CONV_DIM = D_INNER + 2 * N_GROUPS_M * D_STATE
CHUNK = 128
D_FF = 4 * D_MODEL
EPS = 1e-6

LANES = 128
SUBLANES = 8
QKV_WIDTH = ATTN_WIDTH + 2 * KV_WIDTH
NEG_BIG = -1e30
VMEM_LIMIT = 56 * 1024 * 1024
SAMPLE_PAD = SUBLANES
SAMPLE_GROUP = 16


def _mm(a, b):
    return jnp.dot(a.astype(BF16), b.astype(BF16), preferred_element_type=F32)


def _mm_nt(a, b):
    return lax.dot_general(a.astype(BF16), b.astype(BF16), (((1,), (1,)), ((), ())),
                           preferred_element_type=F32)


def _rms(x, w):
    return x * lax.rsqrt(jnp.mean(x * x, axis=-1, keepdims=True) + EPS) * w


def _sigmoid(x):
    return 0.5 + 0.5 * jnp.tanh(0.5 * x)


def _silu(x):
    h = 0.5 * x
    return h + h * jnp.tanh(h)


def _softplus(x):
    return jnp.maximum(x, 0.0) + jnp.log(1.0 + jnp.exp(-jnp.abs(x)))


def _lane_lo(shape):
    return (lax.broadcasted_iota(jnp.int32, shape, len(shape) - 1) % LANES) < HEAD_DIM_A


def _dup_half(x, lo, first):
    xr = pltpu.roll(x, HEAD_DIM_A, axis=1)
    return jnp.where(lo, x, xr) if first else jnp.where(lo, xr, x)


def _stack_heads(q, kv, lo):
    qa = q[:, kv * 2 * LANES: kv * 2 * LANES + LANES]
    qb = q[:, kv * 2 * LANES + LANES: (kv + 1) * 2 * LANES]
    zero = jnp.zeros_like(qa)
    return jnp.concatenate([jnp.where(lo, qa, zero), jnp.where(lo, zero, qa),
                            jnp.where(lo, qb, zero), jnp.where(lo, zero, qb)], axis=0)


def _unstack_heads(o, rows, lo):
    return jnp.concatenate([jnp.where(lo, o[0:rows], o[rows:2 * rows]),
                            jnp.where(lo, o[2 * rows:3 * rows], o[3 * rows:4 * rows])], axis=1)


ProjBufs = collections.namedtuple("ProjBufs", "qkv z xbc dt")
MixedBufs = collections.namedtuple("MixedBufs", "attn ssd")
Carry = collections.namedtuple("Carry", "kprev vprev conv_tail ht dt_buf")
MixerWeights = collections.namedtuple(
    "MixerWeights", "norm1 win wgate wdt convw convb dtb alog dskip ssmn woa wob wo")

PIECE = 256
QKV_PIECE0 = 0
Z_PIECE0 = QKV_PIECE0 + QKV_WIDTH // PIECE
XBC_PIECE0 = Z_PIECE0 + D_INNER // PIECE
N_IN_PIECES = XBC_PIECE0 + CONV_DIM // PIECE
PREP_PIECES = 4


def _prep_pieces_kernel(w_ref, o_ref):
    for j in range(PREP_PIECES):
        o_ref[j] = w_ref[:, j * PIECE:(j + 1) * PIECE].astype(BF16)


def _prep_pieces(w, n_pieces=None, k_block=None):
    k_dim, n_dim = w.shape
    n_pieces = n_dim // PIECE if n_pieces is None else n_pieces
    k_block = k_dim if k_block is None else k_block
    cols = PREP_PIECES * PIECE
    return pl.pallas_call(
        _prep_pieces_kernel, grid=(pl.cdiv(n_pieces, PREP_PIECES), k_dim // k_block),
        in_specs=[pl.BlockSpec((k_block, cols), lambda p, k: (k, p))],
        out_specs=pl.BlockSpec((PREP_PIECES, k_block, PIECE), lambda p, k: (p, k, 0)),
        out_shape=jax.ShapeDtypeStruct((n_pieces, k_dim, PIECE), BF16), name="prep_weight_pieces",
        compiler_params=pltpu.CompilerParams(dimension_semantics=("arbitrary", "arbitrary"),
                                             vmem_limit_bytes=VMEM_LIMIT),
    )(w)


def _prep_pieces_t_kernel(wt_ref, o_ref):
    for j in range(PREP_PIECES):
        o_ref[j] = wt_ref[j * PIECE:(j + 1) * PIECE, :].T.astype(BF16)


def _prep_pieces_t(wt, n_pieces=None):
    n_dim, k_dim = wt.shape
    n_pieces = n_dim // PIECE if n_pieces is None else n_pieces
    return pl.pallas_call(
        _prep_pieces_t_kernel, grid=(pl.cdiv(n_pieces, PREP_PIECES),),
        in_specs=[pl.BlockSpec((PREP_PIECES * PIECE, k_dim), lambda p: (p, 0))],
        out_specs=pl.BlockSpec((PREP_PIECES, k_dim, PIECE), lambda p: (p, 0, 0)),
        out_shape=jax.ShapeDtypeStruct((n_pieces, k_dim, PIECE), BF16), name="prep_weight_pieces_t",
        compiler_params=pltpu.CompilerParams(dimension_semantics=("arbitrary",),
                                             vmem_limit_bytes=VMEM_LIMIT),
    )(wt)


def _prep_pieces_t_shifted_kernel(a_ref, b_ref, o_ref, *, shift):
    rows = jnp.concatenate([a_ref[shift:, :], b_ref[:shift, :]], axis=0)
    o_ref[0] = rows.T.astype(BF16)


def _prep_pieces_t_shifted(wt, row0, n_pieces):
    k_dim = wt.shape[1]
    block0, shift = divmod(row0, PIECE)
    assert shift % SUBLANES == 0 and shift > 0
    return pl.pallas_call(
        functools.partial(_prep_pieces_t_shifted_kernel, shift=shift), grid=(n_pieces,),
        in_specs=[pl.BlockSpec((PIECE, k_dim), lambda p: (block0 + p, 0)),
                  pl.BlockSpec((PIECE, k_dim), lambda p: (block0 + p + 1, 0))],
        out_specs=pl.BlockSpec((1, k_dim, PIECE), lambda p: (p, 0, 0)),
        out_shape=jax.ShapeDtypeStruct((n_pieces, k_dim, PIECE), BF16), name="prep_weight_pieces_t_shifted",
        compiler_params=pltpu.CompilerParams(dimension_semantics=("arbitrary",),
                                             vmem_limit_bytes=VMEM_LIMIT),
    )(wt, wt)


def _prep_dt_kernel(wt_ref, o_ref):
    rows = wt_ref[...]
    keep = lax.broadcasted_iota(jnp.int32, rows.shape, 0) < N_HEADS_M
    o_ref[...] = jnp.where(keep, rows, 0.0).T.astype(BF16)


def _prep_dt(wt, row0):
    k_dim = wt.shape[1]
    assert row0 % LANES == 0
    return pl.pallas_call(
        _prep_dt_kernel, grid=(1,),
        in_specs=[pl.BlockSpec((LANES, k_dim), lambda i: (row0 // LANES, 0))],
        out_specs=pl.BlockSpec((k_dim, LANES), lambda i: (0, 0)),
        out_shape=jax.ShapeDtypeStruct((k_dim, LANES), BF16), name="prep_weight_dt",
        compiler_params=pltpu.CompilerParams(dimension_semantics=("arbitrary",), vmem_limit_bytes=VMEM_LIMIT),
    )(wt)


def _cast_kernel(w_ref, o_ref):
    o_ref[...] = w_ref[...].astype(BF16)


def _cast_bf16(w, block):
    return pl.pallas_call(
        _cast_kernel, grid=(w.shape[0] // block[0], w.shape[1] // block[1]),
        in_specs=[pl.BlockSpec(block, lambda i, j: (i, j))], out_specs=pl.BlockSpec(block, lambda i, j: (i, j)),
        out_shape=jax.ShapeDtypeStruct(w.shape, BF16), name="cast_weight",
        compiler_params=pltpu.CompilerParams(dimension_semantics=("arbitrary", "arbitrary"),
                                             vmem_limit_bytes=VMEM_LIMIT),
    )(w)


StateOuts = collections.namedtuple("StateOuts", "k v conv ssm")


def _split3_bf16(x):
    hi = x.astype(BF16)
    r1 = x - hi.astype(F32)
    mid = r1.astype(BF16)
    return hi, mid, (r1 - mid.astype(F32)).astype(BF16)


def _piece_cols(p):
    return slice(p * PIECE, (p + 1) * PIECE)


def _in_proj_pieces(hb, w, dsts):
    for dst, first in zip(dsts, (QKV_PIECE0, Z_PIECE0, XBC_PIECE0)):
        for p in range(dst.shape[1] // PIECE):
            dst[:, _piece_cols(p)] = _mm(hb, w.win[first + p])
            yield


def _out_proj_pieces(x_ref, rows, hb, attn_ref, ssd_ref, w, out_ref):
    merged = []
    for half, (src_ref, w_ref) in enumerate(((attn_ref, w.woa), (ssd_ref, w.wob))):
        parts = []
        for p in range(D_MODEL // PIECE):
            gate = _sigmoid(_mm(hb, w.wgate[half * (D_MODEL // PIECE) + p]))
            yield
            parts.append(gate * _mm(src_ref[...], w_ref[p]))
            yield
        merged.append(jnp.concatenate(parts, axis=1))
    merged = (merged[0] + merged[1]).astype(BF16)
    for p in range(D_MODEL // PIECE):
        out_ref[rows, _piece_cols(p)] = x_ref[rows, _piece_cols(p)] + _mm(merged, w.wo[p])
        yield


def _stage_in_proj(x_ref, rows, w, proj, hnorm_ref):
    hb = _rms(x_ref[rows, :], w.norm1[...]).astype(BF16)
    hnorm_ref[...] = hb
    yield
    yield from _in_proj_pieces(hb, w, (proj.qkv, proj.z, proj.xbc))
    proj.dt[...] = _mm(hb, w.wdt[...])
    yield


def _stage_out_proj(x_ref, rows, mixed, w, x1_ref, hnorm_ref):
    hb = hnorm_ref[...]
    yield
    yield from _out_proj_pieces(x_ref, rows, hb, mixed.attn, mixed.ssd, w, x1_ref)


def _alternate(*gens):
    gens = list(gens)
    while gens:
        for g in list(gens):
            try:
                next(g)
                yield
            except StopIteration:
                gens.remove(g)


FILL_PER_PIECE = 1


def _trace_interleaved(primary, filler):
    for _ in primary:
        for _ in range(FILL_PER_PIECE):
            next(filler, None)
    for _ in filler:
        pass


def _stage_mix(proj, mixed, rows, carry, w, sinks_ref, first):
    T = CHUNK
    reset = (lambda a: a) if first is None else (lambda a: jnp.where(first, 0.0, a))
    lo = _lane_lo((T, LANES))
    lo2 = _lane_lo((2 * T, LANES))
    rm = lax.broadcasted_iota(jnp.int32, (T, 1), 0) % SUBLANES
    rr = lax.broadcasted_iota(jnp.int32, (T, T), 0)
    cc = lax.broadcasted_iota(jnp.int32, (T, T), 1)
    tril = cc <= rr
    pairs_per_group = N_HEADS_M // 2 // N_GROUPS_M
    n_pairs = N_HEADS_M // 2
    gcols = lambda g: slice(g * GROUP_WIDTH, (g + 1) * GROUP_WIDTH)

    def conv_silu(cols):
        xr = proj.xbc[rows, cols]
        prev_tile = jnp.concatenate([reset(carry.conv_tail[:, cols]), xr[:T - SUBLANES]], axis=0)
        carry.conv_tail[:, cols] = xr[T - SUBLANES:]
        yc = xr * w.convw[CONV_W - 1:CONV_W, cols]
        for shift in range(1, CONV_W):
            shifted = _tile_roll(jnp.where(rm >= SUBLANES - shift, prev_tile, xr), shift)
            yc = yc + shifted * w.convw[CONV_W - 1 - shift:CONV_W - shift, cols]
        return _silu(yc + w.convb[:, cols])

    def softmax(s, kv):
        r = lax.broadcasted_iota(jnp.int32, (T, 2 * T), 0)
        col = lax.broadcasted_iota(jnp.int32, (T, 2 * T), 1)
        valid = (col >= r) & (col <= r + WINDOW)
        if first is not None:
            valid = valid & (col >= jnp.where(first, T, 0))
        es, inv = [], []
        for g in range(Q_PER_KV):
            sk = sinks_ref[kv * Q_PER_KV + g]
            sg = jnp.where(valid, s[g * T:(g + 1) * T], NEG_BIG)
            m = jnp.maximum(jnp.max(sg, axis=-1, keepdims=True), sk)
            e = jnp.exp(sg - m)
            es.append(e.astype(BF16))
            inv.append(1.0 / (jnp.sum(e, axis=-1, keepdims=True) + jnp.exp(sk - m)))
        return jnp.concatenate(es, axis=0), jnp.concatenate(inv, axis=0)

    qkv = proj.qkv[rows, :]
    q = qkv[:, :ATTN_WIDTH] * (HEAD_DIM_A ** -0.5)
    k = qkv[:, ATTN_WIDTH:ATTN_WIDTH + KV_WIDTH]
    v = qkv[:, ATTN_WIDTH + KV_WIDTH:]
    kk = jnp.concatenate([carry.kprev[...], k], axis=0)
    vv = jnp.concatenate([carry.vprev[...], v], axis=0)
    carry.kprev[...] = k
    carry.vprev[...] = v
    carry.dt_buf[...] = _softplus(proj.dt[rows, :] + w.dtb[...])
    da_split = jnp.concatenate(_split3_bf16(carry.dt_buf[...] * -jnp.exp(w.alog[...])), axis=1)
    vds = [_dup_half(vv, lo2, kv == 0).astype(BF16) for kv in range(N_KV_A)]
    scores = [_mm_nt(_stack_heads(q, kv, lo), _dup_half(kk, lo2, kv == 0).astype(BF16))
              for kv in range(N_KV_A)]
    yield

    bc = conv_silu(slice(D_INNER, CONV_DIM))
    bgs = [bc[:, g * D_STATE:(g + 1) * D_STATE].astype(BF16) for g in range(N_GROUPS_M)]
    cgs = [bc[:, (N_GROUPS_M + g) * D_STATE:(N_GROUPS_M + g + 1) * D_STATE].astype(BF16)
           for g in range(N_GROUPS_M)]
    bgts = [bc[:, g * D_STATE:(g + 1) * D_STATE].T.astype(BF16) for g in range(N_GROUPS_M)]
    yield

    es0, inv0 = softmax(scores[0], 0)
    cs = jnp.dot(jnp.where(tril, 1.0, 0.0).astype(BF16), da_split, preferred_element_type=F32)
    yield

    es1, inv1 = softmax(scores[1], 1)
    cbs = [_mm_nt(cgs[g], bgs[g]) for g in range(N_GROUPS_M)]
    h_prevs = [reset(carry.ht[:, gcols(g)]) for g in range(N_GROUPS_M)]
    y_offs = [_mm(cgs[g], h_prevs[g]) for g in range(N_GROUPS_M)]
    o0 = _mm(es0, vds[0])
    yield

    a_cs = cs[:, :LANES] + cs[:, LANES:2 * LANES] + cs[:, 2 * LANES:]
    a_cs_t = a_cs.T
    xs_groups = [conv_silu(gcols(0))]
    o1 = _mm(es1, vds[1])
    yield

    xs_groups.append(conv_silu(gcols(1)))
    mixed.attn[rows, 0:2 * LANES] = _unstack_heads(o0 * inv0, T, lo).astype(BF16)
    yield

    def prep(j):
        g = j // pairs_per_group
        dt = carry.dt_buf[...]
        ws, colbs, dtbs = [], [], []
        for h in (2 * j, 2 * j + 1):
            colb = jnp.broadcast_to(a_cs[:, h:h + 1], (T, T))
            rowb = jnp.broadcast_to(a_cs_t[h:h + 1, :], (T, T))
            seg = jnp.where(tril, jnp.exp(colb - rowb), 0.0)
            ws.append((cbs[g] * seg).astype(BF16))
            colbs.append(colb)
            dtbs.append(jnp.broadcast_to(dt[:, h:h + 1], (T, LANES)))
        dt_e = jnp.where(lo, dtbs[0], dtbs[1])
        acs_e = jnp.where(lo, colbs[0], colbs[1])
        xs_j = xs_groups[g][:, (j % pairs_per_group) * LANES:(j % pairs_per_group + 1) * LANES]
        xdt = xs_j * dt_e
        zero = jnp.zeros_like(xdt)
        rhs = jnp.concatenate([jnp.where(lo, xdt, zero), jnp.where(lo, zero, xdt)], axis=0).astype(BF16)
        alast = acs_e[T - 1:T, :]
        return dict(lhs=jnp.concatenate(ws, axis=1), rhs=rhs, xs=xs_j, e_acs=jnp.exp(acs_e),
                    xd=(xdt * jnp.exp(alast - acs_e)).astype(BF16), decay=jnp.exp(alast))

    def finish(j, p, y_diag):
        g, jj = divmod(j, pairs_per_group)
        sl = slice(j * LANES, (j + 1) * LANES)
        y = y_diag + y_offs[g][:, jj * LANES:(jj + 1) * LANES] * p["e_acs"] + p["xs"] * w.dskip[:, sl]
        return y * _silu(proj.z[rows, sl])

    mixed.attn[rows, 2 * LANES:4 * LANES] = _unstack_heads(o1 * inv1, T, lo).astype(BF16)
    preps = {0: prep(0)}
    yield

    ys, y_diags = [], {}
    for j in range(n_pairs):
        if j + 1 < n_pairs:
            preps[j + 1] = prep(j + 1)
        y_diags[j] = _mm(preps[j]["lhs"], preps[j]["rhs"])
        if j >= 1:
            ys.append(finish(j - 1, preps[j - 1], y_diags.pop(j - 1)))
        if j % pairs_per_group == pairs_per_group - 1:
            g = j // pairs_per_group
            grp = [preps[i] for i in range(g * pairs_per_group, (g + 1) * pairs_per_group)]
            carry.ht[:, gcols(g)] = (h_prevs[g] * jnp.concatenate([p["decay"] for p in grp], axis=1)
                                     + _mm(bgts[g], jnp.concatenate([p["xd"] for p in grp], axis=1)))
        yield
    ys.append(finish(n_pairs - 1, preps[n_pairs - 1], y_diags.pop(n_pairs - 1)))
    yield

    m_slabs = []
    for g in range(N_GROUPS_M):
        grp = ys[g * pairs_per_group:(g + 1) * pairs_per_group]
        ssq = grp[0] * grp[0]
        for y in grp[1:]:
            ssq = ssq + y * y
        scale = lax.rsqrt(jnp.sum(ssq, axis=-1, keepdims=True) * (1.0 / GROUP_WIDTH) + EPS)
        m_slabs.extend([y * scale for y in grp])
    mixed.ssd[rows, :] = (jnp.concatenate(m_slabs, axis=1) * w.ssmn[...]).astype(BF16)


def _chain(*gens):
    for g in gens:
        yield from g


def _prompt_mixer_kernel(sinks_ref, x_in_ref, x_res_ref, *refs, chunks_per_seq):
    refs = list(refs)
    take = lambda n: [refs.pop(0) for _ in range(n)]
    w = MixerWeights(*take(len(MixerWeights._fields)))
    x1_ref, = take(1)
    outs = StateOuts(*take(4))
    proj = (ProjBufs(*take(4)), ProjBufs(*take(4)))
    mixed = (MixedBufs(*take(2)), MixedBufs(*take(2)))
    hnorm = take(2)
    carry = Carry(*take(len(Carry._fields)))
    assert not refs
    T = CHUNK
    s = pl.program_id(0)
    every = slice(None)

    @pl.when(s == 0)
    def _():
        for ref in list(proj[1] + mixed[0] + mixed[1] + carry) + hnorm:
            ref[...] = jnp.zeros_like(ref)

    first = (s - 1) % (chunks_per_seq // 2) == 0

    def step(cur):
        prv = 1 - cur
        dense = _alternate(_stage_out_proj(x_res_ref, every, mixed[cur], w, x1_ref, hnorm[cur]),
                           _stage_in_proj(x_in_ref, every, w, proj[cur], hnorm[cur]))
        mix = _chain(_stage_mix(proj[prv], mixed[prv], slice(0, T), carry, w, sinks_ref, first),
                     _stage_mix(proj[prv], mixed[prv], slice(T, 2 * T), carry, w, sinks_ref, None))
        _trace_interleaved(mix, dense)

    for parity in (0, 1):
        pl.when(s % 2 == parity)(functools.partial(step, parity))

    @pl.when((s >= 1) & (s % (chunks_per_seq // 2) == 0))
    def _():
        outs.k[0] = carry.kprev[...]
        outs.v[0] = carry.vprev[...]
        outs.conv[0] = carry.conv_tail[SUBLANES - (CONV_W - 1):, :]
        outs.ssm[0] = carry.ht[...].T


def _const_spec(shape):
    return pl.BlockSpec(shape, lambda *_: (0,) * len(shape), pipeline_mode=pl.Buffered(1))


def _prompt_mixer(x, sinks, weights):
    nb, seq, _ = x.shape
    assert seq % (2 * CHUNK) == 0
    chunks_per_seq = seq // CHUNK
    n_pairs = nb * chunks_per_seq // 2
    pair = 2 * CHUNK
    consts = tuple(weights)
    seq_of_mix = lambda s: jnp.clip((2 * s - 1) // chunks_per_seq, 0, nb - 1)
    in_specs = ([pl.BlockSpec(memory_space=pltpu.SMEM),
                 pl.BlockSpec((pair, D_MODEL), lambda s: (jnp.minimum(s, n_pairs - 1), 0)),
                 pl.BlockSpec((pair, D_MODEL), lambda s: (jnp.maximum(s - 2, 0), 0))]
                + [_const_spec(a.shape) for a in consts])
    out_shape = (jax.ShapeDtypeStruct((nb * seq, D_MODEL), F32),
                 jax.ShapeDtypeStruct((nb, WINDOW, KV_WIDTH), F32),
                 jax.ShapeDtypeStruct((nb, WINDOW, KV_WIDTH), F32),
                 jax.ShapeDtypeStruct((nb, CONV_W - 1, CONV_DIM), F32),
                 jax.ShapeDtypeStruct((nb, D_INNER, D_STATE), F32))
    out_specs = (pl.BlockSpec((pair, D_MODEL), lambda s: (jnp.maximum(s - 2, 0), 0)),
                 pl.BlockSpec((1, WINDOW, KV_WIDTH), lambda s: (seq_of_mix(s), 0, 0)),
                 pl.BlockSpec((1, WINDOW, KV_WIDTH), lambda s: (seq_of_mix(s), 0, 0)),
                 pl.BlockSpec((1, CONV_W - 1, CONV_DIM), lambda s: (seq_of_mix(s), 0, 0)),
                 pl.BlockSpec((1, D_INNER, D_STATE), lambda s: (seq_of_mix(s), 0, 0)))
    proj_bufs = [pltpu.VMEM((pair, QKV_WIDTH), F32), pltpu.VMEM((pair, D_INNER), F32),
                 pltpu.VMEM((pair, CONV_DIM), F32), pltpu.VMEM((pair, LANES), F32)]
    mixed_bufs = [pltpu.VMEM((pair, ATTN_WIDTH), BF16), pltpu.VMEM((pair, D_INNER), BF16)]
    hnorm_bufs = [pltpu.VMEM((pair, D_MODEL), BF16)] * 2
    carry = [pltpu.VMEM((CHUNK, KV_WIDTH), F32), pltpu.VMEM((CHUNK, KV_WIDTH), F32),
             pltpu.VMEM((SUBLANES, CONV_DIM), F32), pltpu.VMEM((D_STATE, D_INNER), F32),
             pltpu.VMEM((CHUNK, LANES), F32)]
    x_rows = x.reshape(nb * seq, D_MODEL)
    return pl.pallas_call(
        functools.partial(_prompt_mixer_kernel, chunks_per_seq=chunks_per_seq),
        grid=(n_pairs + 2,), in_specs=in_specs, out_specs=out_specs,
        out_shape=out_shape, scratch_shapes=proj_bufs * 2 + mixed_bufs * 2 + hnorm_bufs + carry,
        name="prompt_mixer",
        compiler_params=pltpu.CompilerParams(dimension_semantics=("arbitrary",),
                                             vmem_limit_bytes=VMEM_LIMIT),
    )(sinks, x_rows, x_rows, *consts)


MLP_TILE = 1024
FF_BLOCK = 1024


def _mlp_kernel(x_ref, norm2_ref, wup_ref, wdown_ref, fnorm_ref, y_ref):
    x = x_ref[...]
    hm = _rms(x, norm2_ref[...]).astype(BF16)
    acc = x
    for j in range(D_FF // FF_BLOCK):
        h = _mm(hm, wup_ref[:, j * FF_BLOCK:(j + 1) * FF_BLOCK])
        h = jnp.square(jnp.maximum(h, 0.0))
        acc = acc + _mm(h, wdown_ref[j * FF_BLOCK:(j + 1) * FF_BLOCK, :])
    y_ref[...] = _rms(acc, fnorm_ref[...])


def _mlp(x, norm2, wup, wdown, fnorm):
    rows = x.shape[0]
    tile = min(MLP_TILE, rows)
    assert rows % tile == 0 and tile % SUBLANES == 0
    consts = (norm2, wup, wdown, fnorm)
    return pl.pallas_call(
        _mlp_kernel, grid=(rows // tile,),
        in_specs=[pl.BlockSpec((tile, D_MODEL), lambda i: (i, 0))] + [_const_spec(a.shape) for a in consts],
        out_specs=pl.BlockSpec((tile, D_MODEL), lambda i: (i, 0)),
        out_shape=jax.ShapeDtypeStruct((rows, D_MODEL), F32), name="mlp",
        compiler_params=pltpu.CompilerParams(dimension_semantics=("arbitrary",),
                                             vmem_limit_bytes=VMEM_LIMIT),
    )(x, *consts)


ROW_TILE = 256


def _mixer_weights(**refs):
    return MixerWeights(**{f: refs.get(f) for f in MixerWeights._fields})


def _in_proj_kernel(x_ref, norm1_ref, win_ref, wdt_ref, qkv_ref, z_ref, xbc_ref, dt_ref, hnorm_ref):
    hb = _rms(x_ref[...], norm1_ref[...]).astype(BF16)
    hnorm_ref[...] = hb
    for _ in _in_proj_pieces(hb, _mixer_weights(win=win_ref), (qkv_ref, z_ref, xbc_ref)):
        pass
    dt_ref[...] = _mm(hb, wdt_ref[...])


def _in_proj(x, norm1, win, wdt):
    rows = x.shape[0]
    assert rows % ROW_TILE == 0
    consts = (norm1, win, wdt)
    outs = ((QKV_WIDTH, F32), (D_INNER, F32), (CONV_DIM, F32), (LANES, F32), (D_MODEL, BF16))
    return pl.pallas_call(
        _in_proj_kernel, grid=(rows // ROW_TILE,),
        in_specs=[pl.BlockSpec((ROW_TILE, D_MODEL), lambda i: (i, 0))] + [_const_spec(a.shape) for a in consts],
        out_specs=tuple(pl.BlockSpec((ROW_TILE, width), lambda i: (i, 0)) for width, _ in outs),
        out_shape=tuple(jax.ShapeDtypeStruct((rows, width), dtype) for width, dtype in outs),
        name="sample_in_proj",
        compiler_params=pltpu.CompilerParams(dimension_semantics=("arbitrary",),
                                             vmem_limit_bytes=VMEM_LIMIT),
    )(x, *consts)


def _tile_roll(x, shift):
    rows, width = x.shape
    tiles = x.reshape(rows // SUBLANES, SUBLANES, width)
    return pltpu.roll(tiles, shift % SUBLANES, axis=1).reshape(rows, width)


def _spread_rows(x):
    tiles = []
    for t in range(x.shape[0] // SUBLANES):
        two_seqs = x[t * SUBLANES:(t + 1) * SUBLANES]
        tiles += [two_seqs, pltpu.roll(two_seqs, SUBLANES // 2, axis=0)]
    return jnp.concatenate(tiles, axis=0)


def _gather_rows(y):
    low = lax.broadcasted_iota(jnp.int32, (SUBLANES, 1), 0) < SUBLANES // 2
    tiles = []
    for t in range(y.shape[0] // (2 * SUBLANES)):
        a = y[2 * t * SUBLANES:(2 * t + 1) * SUBLANES]
        b = y[(2 * t + 1) * SUBLANES:(2 * t + 2) * SUBLANES]
        tiles.append(jnp.where(low, a, pltpu.roll(b, SUBLANES // 2, axis=0)))
    return jnp.concatenate(tiles, axis=0)


def _expand_heads(cols, expand):
    terms = []
    for c in cols:
        hi = c.astype(BF16)
        r1 = c - hi.astype(F32)
        mid = r1.astype(BF16)
        lo = (r1 - mid.astype(F32)).astype(BF16)
        terms.extend([hi, mid, lo])
    rows = cols[0].shape[0]
    out = jnp.dot(jnp.concatenate(terms, axis=0), expand, preferred_element_type=F32)
    return [out[(3 * i) * rows:(3 * i + 1) * rows] + out[(3 * i + 1) * rows:(3 * i + 2) * rows]
            + out[(3 * i + 2) * rows:(3 * i + 3) * rows] for i in range(len(cols))]


def _sample_mixer_kernel(sinks_ref, qkv_ref, z_ref, xbc_ref, dtraw_ref, cprev_ref, kc_ref, vc_ref, st_ref,
                         convw_ref, convb_ref, dtb_ref, alog_ref, dskip_ref, ssmn_ref,
                         aout_ref, mout_ref, nk_ref, nv_ref, nconv_ref, nst_ref):
    R = SAMPLE_GROUP * SAMPLE_PAD
    L = 4
    rm = lax.broadcasted_iota(jnp.int32, (R, 1), 0) % SAMPLE_PAD

    xr = xbc_raw = _spread_rows(xbc_ref[...])
    with_prev = jnp.where(rm >= SAMPLE_PAD - (CONV_W - 1), cprev_ref[...], xr)
    yc = xr * convw_ref[CONV_W - 1:CONV_W, :]
    for kshift in range(1, CONV_W):
        yc = yc + _tile_roll(with_prev, kshift) * convw_ref[CONV_W - 1 - kshift:CONV_W - kshift, :]
    nconv_ref[...] = _tile_roll(xbc_raw, -1)
    xbc = _silu(yc + convb_ref[...])
    xs = xbc[:, :D_INNER]
    bm = xbc[:, D_INNER:D_INNER + N_GROUPS_M * D_STATE]
    cm = xbc[:, D_INNER + N_GROUPS_M * D_STATE:]

    dt = _softplus(_spread_rows(dtraw_ref[...]) + dtb_ref[...])
    d_a = dt * (-jnp.exp(alog_ref[...]))
    a_cs = d_a
    suf = jnp.zeros_like(d_a)
    for kshift in range(1, L):
        a_cs = a_cs + jnp.where(rm >= kshift, _tile_roll(d_a, kshift), 0.0)
        suf = suf + jnp.where(rm <= L - 1 - kshift, _tile_roll(d_a, -kshift), 0.0)

    expand = (lax.broadcasted_iota(jnp.int32, (LANES, D_INNER), 1) // HEAD_DIM_M
              == lax.broadcasted_iota(jnp.int32, (LANES, D_INNER), 0)).astype(BF16)
    dt_e, acs_e, suf_e = _expand_heads([dt, a_cs, suf], expand)
    xdt = xs * dt_e
    real = rm < L
    xd_t = jnp.where(real, xdt * jnp.exp(suf_e), 0.0).T.astype(BF16)

    y = xs * dskip_ref[...]
    for kshift in range(L):
        bk = bm if kshift == 0 else _tile_roll(bm, kshift)
        cb = cm * bk
        cb_e = jnp.concatenate(
            [jnp.broadcast_to(jnp.sum(cb[:, g * D_STATE:(g + 1) * D_STATE], axis=-1, keepdims=True),
                              (R, GROUP_WIDTH)) for g in range(N_GROUPS_M)], axis=1)
        if kshift == 0:
            y = y + cb_e * xdt
        else:
            seg = jnp.exp(acs_e - _tile_roll(acs_e, kshift))
            y = y + jnp.where(rm >= kshift, cb_e * seg * _tile_roll(xdt, kshift), 0.0)

    qkv = _spread_rows(qkv_ref[...])
    q = qkv[:, :ATTN_WIDTH] * (HEAD_DIM_A ** -0.5)
    q_swapped = jnp.concatenate(
        [pltpu.roll(q[:, s * LANES:(s + 1) * LANES], HEAD_DIM_A, axis=1) for s in range(ATTN_WIDTH // LANES)], axis=1)
    kn = qkv[:, ATTN_WIDTH:ATTN_WIDTH + KV_WIDTH]
    vn = qkv[:, ATTN_WIDTH + KV_WIDTH:]
    lo8 = _lane_lo((SAMPLE_PAD, LANES))
    zero8 = jnp.zeros((SAMPLE_PAD, LANES), F32)
    seq_rows = [slice(i * SAMPLE_PAD, (i + 1) * SAMPLE_PAD) for i in range(SAMPLE_GROUP)]
    HROWS = N_HEADS_A * SAMPLE_PAD

    s_c, s_n = [], []
    for rows in seq_rows:
        pieces = []
        for kv in range(N_KV_A):
            for g in range(Q_PER_KV):
                slab = kv * (Q_PER_KV // 2) + g // 2
                src = q if g % 2 == kv else q_swapped
                piece = src[rows, slab * LANES:(slab + 1) * LANES]
                pieces.append(jnp.where(lo8, piece, zero8) if kv == 0 else jnp.where(lo8, zero8, piece))
        lhs = jnp.concatenate(pieces, axis=0).astype(BF16)
        i = len(s_c)
        s_c.append(_mm(lhs, kc_ref[i]))
        s_n.append(_mm_nt(lhs, kn[rows]))
    s_c = jnp.concatenate(s_c, axis=0)
    s_n = jnp.concatenate(s_n, axis=0)
    n_rows = SAMPLE_GROUP * HROWS
    tok_c = lax.broadcasted_iota(jnp.int32, (n_rows, WINDOW), 0) % SAMPLE_PAD
    s_c = jnp.where(lax.broadcasted_iota(jnp.int32, (n_rows, WINDOW), 1) >= tok_c, s_c, NEG_BIG)
    tok_n = lax.broadcasted_iota(jnp.int32, (n_rows, SAMPLE_PAD), 0) % SAMPLE_PAD
    col_n = lax.broadcasted_iota(jnp.int32, (n_rows, SAMPLE_PAD), 1)
    s_n = jnp.where((col_n <= tok_n) & (col_n < L), s_n, NEG_BIG)
    sk = jnp.concatenate([jnp.full((SAMPLE_PAD, 1), sinks_ref[h], F32) for h in range(N_HEADS_A)] * SAMPLE_GROUP,
                         axis=0)
    m = jnp.maximum(jnp.maximum(jnp.max(s_c, axis=-1, keepdims=True), jnp.max(s_n, axis=-1, keepdims=True)), sk)
    e_c = jnp.exp(s_c - m)
    e_n = jnp.exp(s_n - m)
    inv_den = 1.0 / (jnp.sum(e_c, axis=-1, keepdims=True) + jnp.sum(e_n, axis=-1, keepdims=True) + jnp.exp(sk - m))
    e_c = e_c.astype(BF16)
    e_n = e_n.astype(BF16)
    o = jnp.concatenate([_mm_nt(e_c[i * HROWS:(i + 1) * HROWS], vc_ref[i]) + _mm(e_n[i * HROWS:(i + 1) * HROWS], vn[rows])
                         for i, rows in enumerate(seq_rows)], axis=0) * inv_den
    o_swapped = pltpu.roll(o, HEAD_DIM_A, axis=1)
    a_rows = []
    for i in range(SAMPLE_GROUP):
        blk = lambda arr, kv, g: arr[i * HROWS + (kv * Q_PER_KV + g) * SAMPLE_PAD:
                                     i * HROWS + (kv * Q_PER_KV + g + 1) * SAMPLE_PAD]
        slabs = []
        for kv in range(N_KV_A):
            for j in range(Q_PER_KV // 2):
                first = blk(o if kv == 0 else o_swapped, kv, 2 * j)
                second = blk(o_swapped if kv == 0 else o, kv, 2 * j + 1)
                slabs.append(jnp.where(lo8, first, second))
        a_rows.append(jnp.concatenate(slabs, axis=1))
    aout_ref[...] = _gather_rows(jnp.concatenate(a_rows, axis=0))

    key = lax.broadcasted_iota(jnp.int32, (KV_WIDTH, WINDOW), 1)
    for cache_ref, new, out_ref in ((kc_ref, kn, nk_ref), (vc_ref, vn, nv_ref)):
        new_t = new.T
        for i in range(SAMPLE_GROUP):
            kept = pltpu.roll(cache_ref[i], WINDOW - L, axis=1)
            fresh = pltpu.roll(new_t, (WINDOW - L - i * SAMPLE_PAD) % WINDOW, axis=1)
            out_ref[i] = jnp.where(key >= WINDOW - L, fresh, kept)

    row_r = lax.broadcasted_iota(jnp.int32, (R, D_STATE), 0)
    heads_per_group = N_HEADS_M // N_GROUPS_M
    yoff = []
    for i, rows in enumerate(seq_rows):
        state = st_ref[i]
        state_b = state.astype(BF16)
        a_tot = a_cs[i * SAMPLE_PAD + L - 1:i * SAMPLE_PAD + L, :]
        in_seq = (row_r >= i * SAMPLE_PAD) & (row_r < (i + 1) * SAMPLE_PAD)
        yo = []
        for g in range(N_GROUPS_M):
            gs = slice(g * GROUP_WIDTH, (g + 1) * GROUP_WIDTH)
            yo.append(_mm_nt(cm[rows, g * D_STATE:(g + 1) * D_STATE], state_b[gs]))
            bsel = jnp.where(in_seq, bm[:, g * D_STATE:(g + 1) * D_STATE], 0.0)
            upd = _mm(xd_t[gs, :], bsel)
            for hh in range(heads_per_group):
                h = g * heads_per_group + hh
                hs = slice(h * HEAD_DIM_M, (h + 1) * HEAD_DIM_M)
                decay = jnp.exp(jnp.broadcast_to(a_tot[:, h:h + 1], (HEAD_DIM_M, D_STATE)))
                nst_ref[i, hs, :] = state[hs] * decay + upd[hh * HEAD_DIM_M:(hh + 1) * HEAD_DIM_M]
        yoff.append(jnp.concatenate(yo, axis=1))
    yoff = jnp.concatenate(yoff, axis=0)

    y = (y + yoff * jnp.exp(acs_e)) * _silu(_spread_rows(z_ref[...]))
    outs = []
    for g in range(N_GROUPS_M):
        yg = y[:, g * GROUP_WIDTH:(g + 1) * GROUP_WIDTH]
        outs.append(yg * lax.rsqrt(jnp.mean(yg * yg, axis=-1, keepdims=True) + EPS))
    mout_ref[...] = _gather_rows(jnp.concatenate(outs, axis=1) * ssmn_ref[...])


def _sample_mixer(sinks, qkv, z, xbc, dtraw, cprev, kc, vc, st, convw, convb, dtb, alog, dskip, ssmn):
    nseq = kc.shape[0]
    assert nseq % SAMPLE_GROUP == 0
    R = SAMPLE_GROUP * SAMPLE_PAD
    T4 = SAMPLE_GROUP * 4
    rows = nseq * SAMPLE_PAD
    consts = (convw, convb, dtb, alog, dskip, ssmn)
    row_spec = lambda w: pl.BlockSpec((R, w), lambda i: (i, 0))
    tok_spec = lambda w: pl.BlockSpec((T4, w), lambda i: (i, 0))
    seq_spec = lambda a, b: pl.BlockSpec((SAMPLE_GROUP, a, b), lambda i: (i, 0, 0))
    in_specs = ([pl.BlockSpec(memory_space=pltpu.SMEM),
                 tok_spec(QKV_WIDTH), tok_spec(D_INNER), tok_spec(CONV_DIM), tok_spec(LANES), row_spec(CONV_DIM),
                 seq_spec(WINDOW, KV_WIDTH), seq_spec(WINDOW, KV_WIDTH), seq_spec(D_INNER, D_STATE)]
                + [_const_spec(a.shape) for a in consts])
    out_shape = (jax.ShapeDtypeStruct((nseq * 4, ATTN_WIDTH), F32), jax.ShapeDtypeStruct((nseq * 4, D_INNER), F32),
                 jax.ShapeDtypeStruct((nseq, WINDOW, KV_WIDTH), F32),
                 jax.ShapeDtypeStruct((nseq, WINDOW, KV_WIDTH), F32),
                 jax.ShapeDtypeStruct((rows, CONV_DIM), F32),
                 jax.ShapeDtypeStruct((nseq, D_INNER, D_STATE), F32))
    out_specs = (tok_spec(ATTN_WIDTH), tok_spec(D_INNER), seq_spec(WINDOW, KV_WIDTH), seq_spec(WINDOW, KV_WIDTH),
                 row_spec(CONV_DIM), seq_spec(D_INNER, D_STATE))
    return pl.pallas_call(
        _sample_mixer_kernel, grid=(nseq // SAMPLE_GROUP,), in_specs=in_specs, out_specs=out_specs,
        out_shape=out_shape, name="sample_mixer",
        compiler_params=pltpu.CompilerParams(dimension_semantics=("arbitrary",),
                                             vmem_limit_bytes=VMEM_LIMIT),
    )(sinks, qkv, z, xbc, dtraw, cprev, kc, vc, st, *consts)


def _out_proj_kernel(x_ref, a_ref, m_ref, hnorm_ref, wgate_ref, woa_ref, wob_ref, wo_ref, x1_ref):
    w = _mixer_weights(wgate=wgate_ref, woa=woa_ref, wob=wob_ref, wo=wo_ref)
    for _ in _out_proj_pieces(x_ref, slice(None), hnorm_ref[...], a_ref, m_ref, w, x1_ref):
        pass


def _out_proj(x, a_out, m_out, hnorm, wgate, woa, wob, wo):
    rows = x.shape[0]
    assert rows % ROW_TILE == 0
    consts = (wgate, woa, wob, wo)
    widths = (D_MODEL, ATTN_WIDTH, D_INNER, D_MODEL)
    return pl.pallas_call(
        _out_proj_kernel, grid=(rows // ROW_TILE,),
        in_specs=([pl.BlockSpec((ROW_TILE, w), lambda i: (i, 0)) for w in widths]
                  + [_const_spec(a.shape) for a in consts]),
        out_specs=pl.BlockSpec((ROW_TILE, D_MODEL), lambda i: (i, 0)),
        out_shape=jax.ShapeDtypeStruct((rows, D_MODEL), F32), name="sample_out_proj",
        compiler_params=pltpu.CompilerParams(dimension_semantics=("arbitrary",),
                                             vmem_limit_bytes=VMEM_LIMIT),
    )(x, a_out, m_out, hnorm, *consts)


def _layer_params(norm1, w_in, sinks, conv_w, conv_b, dt_bias, a_log, d_skip, ssm_norm, w_oa, w_ob, w_o,
                  norm2, w_up, w_down):
    dt0 = QKV_WIDTH + D_INNER + CONV_DIM
    assert dt0 == N_IN_PIECES * PIECE
    pad_lanes = lambda a: jnp.pad(a, ((0, 0), (0, LANES - a.shape[1])))
    w_in_t = w_in.T
    mixer = MixerWeights(
        norm1=norm1[None, :], win=_prep_pieces_t(w_in_t, N_IN_PIECES),
        wgate=_prep_pieces_t_shifted(w_in_t, dt0 + N_HEADS_M, 2 * D_MODEL // PIECE),
        wdt=_prep_dt(w_in_t, dt0),
        convw=conv_w, convb=conv_b[None, :], dtb=pad_lanes(dt_bias[None, :]), alog=pad_lanes(a_log[None, :]),
        dskip=jnp.repeat(d_skip, HEAD_DIM_M)[None, :], ssmn=ssm_norm[None, :],
        woa=_prep_pieces(w_oa), wob=_prep_pieces(w_ob), wo=_prep_pieces(w_o))
    wide = (D_MODEL, D_MODEL)
    return dict(sinks=sinks.astype(F32), mixer=mixer,
                norm2=norm2[None, :], wup=_cast_bf16(w_up, wide), wdown=_cast_bf16(w_down, wide))


def kernel(x_prompt, x_sample, cache_swa_k, cache_swa_v, state_conv, state_ssm, norm1, w_in, sinks, conv_w,
           conv_b, dt_bias, a_log, d_skip, ssm_norm, w_oa, w_ob, w_o, norm2, w_up, w_down, final_norm):
    depth = w_in.shape[0]
    assert depth == 1
    nb, seq, _ = x_prompt.shape
    ns, ls, _ = x_sample.shape
    assert ls == 4
    p = _layer_params(norm1[0], w_in[0], sinks[0], conv_w[0], conv_b[0], dt_bias[0], a_log[0], d_skip[0],
                      ssm_norm[0], w_oa[0], w_ob[0], w_o[0], norm2[0], w_up[0], w_down[0])
    fnorm = final_norm[None, :]
    mw = p["mixer"]
    mixer_consts = (mw.convw, mw.convb, mw.dtb, mw.alog, mw.dskip, mw.ssmn)

    x1p, pk, pv, pc, pst = _prompt_mixer(x_prompt, p["sinks"], mw)
    y_prompt = _mlp(x1p, p["norm2"], p["wup"], p["wdown"], fnorm)
    y_prompt = y_prompt.reshape(nb, seq, D_MODEL)

    xs_rows = x_sample.reshape(ns * ls, D_MODEL)
    dim_major = lambda a: jnp.swapaxes(a, 1, 2)
    cprev = jnp.pad(state_conv[0], ((0, 0), (SAMPLE_PAD - (CONV_W - 1), 0), (0, 0)))
    cprev = cprev.reshape(ns * SAMPLE_PAD, CONV_DIM)
    qkv, z, xbc, dtraw, hnorm = _in_proj(xs_rows, mw.norm1, mw.win, mw.wdt)
    a_out, m_out, sk, sv, sc, sst = _sample_mixer(
        p["sinks"], qkv, z, xbc, dtraw, cprev,
        dim_major(cache_swa_k[0].reshape(ns, WINDOW, KV_WIDTH)), dim_major(cache_swa_v[0].reshape(ns, WINDOW, KV_WIDTH)),
        state_ssm[0].reshape(ns, D_INNER, D_STATE), *mixer_consts)
    sk, sv = dim_major(sk), dim_major(sv)
    x1s = _out_proj(xs_rows, a_out, m_out, hnorm, mw.wgate, mw.woa, mw.wob, mw.wo)
    y_sample = _mlp(x1s, p["norm2"], p["wup"], p["wdown"], fnorm).reshape(ns, ls, D_MODEL)
    sc = sc.reshape(ns, SAMPLE_PAD, CONV_DIM)[:, :CONV_W - 1]

    kv_shape = (1, -1, WINDOW, N_KV_A, HEAD_DIM_A)
    ssm_shape = (1, -1, N_HEADS_M, HEAD_DIM_M, D_STATE)
    return (y_prompt, y_sample,
            pk.reshape(kv_shape), pv.reshape(kv_shape), pc[None], pst.reshape(ssm_shape),
            sk.reshape(kv_shape), sv.reshape(kv_shape), sc[None], sst.reshape(ssm_shape))
```

```python
import collections
import functools

import jax
import jax.numpy as jnp
from jax import lax
from jax.experimental import pallas as pl
from jax.experimental.pallas import tpu as pltpu

F32 = jnp.float32
BF16 = jnp.bfloat16

D_MODEL = 1024
N_HEADS_A = 8
N_KV_A = 2
Q_PER_KV = N_HEADS_A // N_KV_A
HEAD_DIM_A = 64
WINDOW = 128
ATTN_WIDTH = N_HEADS_A * HEAD_DIM_A
KV_WIDTH = N_KV_A * HEAD_DIM_A
D_INNER = 1024
HEAD_DIM_M = 64
N_HEADS_M = D_INNER // HEAD_DIM_M
N_GROUPS_M = 2
GROUP_WIDTH = D_INNER // N_GROUPS_M
D_STATE = 128
CONV_W = 4
CONV_DIM = D_INNER + 2 * N_GROUPS_M * D_STATE
CHUNK = 128
D_FF = 4 * D_MODEL
EPS = 1e-6

LANES = 128
SUBLANES = 8
QKV_WIDTH = ATTN_WIDTH + 2 * KV_WIDTH
NEG_BIG = -1e30
VMEM_LIMIT = 56 * 1024 * 1024
SAMPLE_PAD = SUBLANES
SAMPLE_GROUP = 16


def _mm(a, b):
    return jnp.dot(a.astype(BF16), b.astype(BF16), preferred_element_type=F32)


def _mm_nt(a, b):
    return lax.dot_general(a.astype(BF16), b.astype(BF16), (((1,), (1,)), ((), ())),
                           preferred_element_type=F32)


def _rms(x, w):
    return x * lax.rsqrt(jnp.mean(x * x, axis=-1, keepdims=True) + EPS) * w


def _sigmoid(x):
    return 0.5 + 0.5 * jnp.tanh(0.5 * x)


def _silu(x):
    h = 0.5 * x
    return h + h * jnp.tanh(h)


def _softplus(x):
    return jnp.maximum(x, 0.0) + jnp.log(1.0 + jnp.exp(-jnp.abs(x)))


def _lane_lo(shape):
    return (lax.broadcasted_iota(jnp.int32, shape, len(shape) - 1) % LANES) < HEAD_DIM_A


def _dup_half(x, lo, first):
    xr = pltpu.roll(x, HEAD_DIM_A, axis=1)
    return jnp.where(lo, x, xr) if first else jnp.where(lo, xr, x)


def _stack_heads(q, kv, lo):
    qa = q[:, kv * 2 * LANES: kv * 2 * LANES + LANES]
    qb = q[:, kv * 2 * LANES + LANES: (kv + 1) * 2 * LANES]
    zero = jnp.zeros_like(qa)
    return jnp.concatenate([jnp.where(lo, qa, zero), jnp.where(lo, zero, qa),
                            jnp.where(lo, qb, zero), jnp.where(lo, zero, qb)], axis=0)


def _unstack_heads(o, rows, lo):
    return jnp.concatenate([jnp.where(lo, o[0:rows], o[rows:2 * rows]),
                            jnp.where(lo, o[2 * rows:3 * rows], o[3 * rows:4 * rows])], axis=1)


ProjBufs = collections.namedtuple("ProjBufs", "qkv z xbc dt")
MixedBufs = collections.namedtuple("MixedBufs", "attn ssd")
Carry = collections.namedtuple("Carry", "kprev vprev conv_tail ht dt_buf")
MixerWeights = collections.namedtuple(
    "MixerWeights", "norm1 win wgate wdt convw convb dtb alog dskip ssmn woa wob wo")

PIECE = 256
QKV_PIECE0 = 0
Z_PIECE0 = QKV_PIECE0 + QKV_WIDTH // PIECE
XBC_PIECE0 = Z_PIECE0 + D_INNER // PIECE
N_IN_PIECES = XBC_PIECE0 + CONV_DIM // PIECE
PREP_PIECES = 2


def _prep_pieces_kernel(w_ref, o_ref):
    for j in range(PREP_PIECES):
        o_ref[j] = w_ref[:, j * PIECE:(j + 1) * PIECE].astype(BF16)


def _prep_pieces(w, n_pieces=None, k_block=None):
    k_dim, n_dim = w.shape
    n_pieces = n_dim // PIECE if n_pieces is None else n_pieces
    k_block = k_dim if k_block is None else k_block
    cols = PREP_PIECES * PIECE
    return pl.pallas_call(
        _prep_pieces_kernel, grid=(pl.cdiv(n_pieces, PREP_PIECES), k_dim // k_block),
        in_specs=[pl.BlockSpec((k_block, cols), lambda p, k: (k, p))],
        out_specs=pl.BlockSpec((PREP_PIECES, k_block, PIECE), lambda p, k: (p, k, 0)),
        out_shape=jax.ShapeDtypeStruct((n_pieces, k_dim, PIECE), BF16), name="prep_weight_pieces",
        compiler_params=pltpu.CompilerParams(dimension_semantics=("arbitrary", "arbitrary"),
                                             vmem_limit_bytes=VMEM_LIMIT),
    )(w)


def _prep_pieces_t_kernel(wt_ref, o_ref):
    for j in range(PREP_PIECES):
        o_ref[j] = wt_ref[j * PIECE:(j + 1) * PIECE, :].T.astype(BF16)


def _prep_pieces_t(wt, n_pieces=None):
    n_dim, k_dim = wt.shape
    n_pieces = n_dim // PIECE if n_pieces is None else n_pieces
    return pl.pallas_call(
        _prep_pieces_t_kernel, grid=(pl.cdiv(n_pieces, PREP_PIECES),),
        in_specs=[pl.BlockSpec((PREP_PIECES * PIECE, k_dim), lambda p: (p, 0))],
        out_specs=pl.BlockSpec((PREP_PIECES, k_dim, PIECE), lambda p: (p, 0, 0)),
        out_shape=jax.ShapeDtypeStruct((n_pieces, k_dim, PIECE), BF16), name="prep_weight_pieces_t",
        compiler_params=pltpu.CompilerParams(dimension_semantics=("arbitrary",),
                                             vmem_limit_bytes=VMEM_LIMIT),
    )(wt)


def _prep_pieces_t_shifted_kernel(a_ref, b_ref, o_ref, *, shift):
    rows = jnp.concatenate([a_ref[shift:, :], b_ref[:shift, :]], axis=0)
    o_ref[0] = rows.T.astype(BF16)


def _prep_pieces_t_shifted(wt, row0, n_pieces):
    k_dim = wt.shape[1]
    block0, shift = divmod(row0, PIECE)
    assert shift % SUBLANES == 0 and shift > 0
    return pl.pallas_call(
        functools.partial(_prep_pieces_t_shifted_kernel, shift=shift), grid=(n_pieces,),
        in_specs=[pl.BlockSpec((PIECE, k_dim), lambda p: (block0 + p, 0)),
                  pl.BlockSpec((PIECE, k_dim), lambda p: (block0 + p + 1, 0))],
        out_specs=pl.BlockSpec((1, k_dim, PIECE), lambda p: (p, 0, 0)),
        out_shape=jax.ShapeDtypeStruct((n_pieces, k_dim, PIECE), BF16), name="prep_weight_pieces_t_shifted",
        compiler_params=pltpu.CompilerParams(dimension_semantics=("arbitrary",),
                                             vmem_limit_bytes=VMEM_LIMIT),
    )(wt, wt)


def _prep_dt_kernel(wt_ref, o_ref):
    rows = wt_ref[...]
    keep = lax.broadcasted_iota(jnp.int32, rows.shape, 0) < N_HEADS_M
    o_ref[...] = jnp.where(keep, rows, 0.0).T.astype(BF16)


def _prep_dt(wt, row0):
    k_dim = wt.shape[1]
    assert row0 % LANES == 0
    return pl.pallas_call(
        _prep_dt_kernel, grid=(1,),
        in_specs=[pl.BlockSpec((LANES, k_dim), lambda i: (row0 // LANES, 0))],
        out_specs=pl.BlockSpec((k_dim, LANES), lambda i: (0, 0)),
        out_shape=jax.ShapeDtypeStruct((k_dim, LANES), BF16), name="prep_weight_dt",
        compiler_params=pltpu.CompilerParams(dimension_semantics=("arbitrary",), vmem_limit_bytes=VMEM_LIMIT),
    )(wt)


def _cast_kernel(w_ref, o_ref):
    o_ref[...] = w_ref[...].astype(BF16)


def _cast_bf16(w, block):
    return pl.pallas_call(
        _cast_kernel, grid=(w.shape[0] // block[0], w.shape[1] // block[1]),
        in_specs=[pl.BlockSpec(block, lambda i, j: (i, j))], out_specs=pl.BlockSpec(block, lambda i, j: (i, j)),
        out_shape=jax.ShapeDtypeStruct(w.shape, BF16), name="cast_weight",
        compiler_params=pltpu.CompilerParams(dimension_semantics=("arbitrary", "arbitrary"),
                                             vmem_limit_bytes=VMEM_LIMIT),
    )(w)


StateOuts = collections.namedtuple("StateOuts", "k v conv ssm")


def _split3_bf16(x):
    hi = x.astype(BF16)
    r1 = x - hi.astype(F32)
    mid = r1.astype(BF16)
    return hi, mid, (r1 - mid.astype(F32)).astype(BF16)


def _piece_cols(p):
    return slice(p * PIECE, (p + 1) * PIECE)


def _in_proj_pieces(hb, w, dsts):
    for dst, first in zip(dsts, (QKV_PIECE0, Z_PIECE0, XBC_PIECE0)):
        for p in range(dst.shape[1] // PIECE):
            dst[:, _piece_cols(p)] = _mm(hb, w.win[first + p])
            yield


def _out_proj_pieces(x_ref, rows, hb, attn_ref, ssd_ref, w, out_ref):
    merged = []
    for half, (src_ref, w_ref) in enumerate(((attn_ref, w.woa), (ssd_ref, w.wob))):
        parts = []
        for p in range(D_MODEL // PIECE):
            gate = _sigmoid(_mm(hb, w.wgate[half * (D_MODEL // PIECE) + p]))
            yield
            parts.append(gate * _mm(src_ref[...], w_ref[p]))
            yield
        merged.append(jnp.concatenate(parts, axis=1))
    merged = (merged[0] + merged[1]).astype(BF16)
    for p in range(D_MODEL // PIECE):
        out_ref[rows, _piece_cols(p)] = x_ref[rows, _piece_cols(p)] + _mm(merged, w.wo[p])
        yield


def _stage_in_proj(x_ref, rows, w, proj, hnorm_ref):
    hb = _rms(x_ref[rows, :], w.norm1[...]).astype(BF16)
    hnorm_ref[...] = hb
    yield
    yield from _in_proj_pieces(hb, w, (proj.qkv, proj.z, proj.xbc))
    proj.dt[...] = _mm(hb, w.wdt[...])
    yield


def _stage_out_proj(x_ref, rows, mixed, w, x1_ref, hnorm_ref):
    hb = hnorm_ref[...]
    yield
    yield from _out_proj_pieces(x_ref, rows, hb, mixed.attn, mixed.ssd, w, x1_ref)


def _alternate(*gens):
    gens = list(gens)
    while gens:
        for g in list(gens):
            try:
                next(g)
                yield
            except StopIteration:
                gens.remove(g)


def _trace_interleaved(primary, filler):
    for _ in primary:
        next(filler, None)
    for _ in filler:
        pass


def _stage_mix(proj, mixed, rows, carry, w, sinks_ref, first):
    T = CHUNK
    reset = (lambda a: a) if first is None else (lambda a: jnp.where(first, 0.0, a))
    lo = _lane_lo((T, LANES))
    lo2 = _lane_lo((2 * T, LANES))
    rm = lax.broadcasted_iota(jnp.int32, (T, 1), 0) % SUBLANES
    rr = lax.broadcasted_iota(jnp.int32, (T, T), 0)
    cc = lax.broadcasted_iota(jnp.int32, (T, T), 1)
    tril = cc <= rr
    pairs_per_group = N_HEADS_M // 2 // N_GROUPS_M
    n_pairs = N_HEADS_M // 2
    gcols = lambda g: slice(g * GROUP_WIDTH, (g + 1) * GROUP_WIDTH)

    def conv_silu(cols):
        xr = proj.xbc[rows, cols]
        prev_tile = jnp.concatenate([reset(carry.conv_tail[:, cols]), xr[:T - SUBLANES]], axis=0)
        carry.conv_tail[:, cols] = xr[T - SUBLANES:]
        yc = xr * w.convw[CONV_W - 1:CONV_W, cols]
        for shift in range(1, CONV_W):
            shifted = _tile_roll(jnp.where(rm >= SUBLANES - shift, prev_tile, xr), shift)
            yc = yc + shifted * w.convw[CONV_W - 1 - shift:CONV_W - shift, cols]
        return _silu(yc + w.convb[:, cols])

    def softmax(s, kv):
        r = lax.broadcasted_iota(jnp.int32, (T, 2 * T), 0)
        col = lax.broadcasted_iota(jnp.int32, (T, 2 * T), 1)
        valid = (col >= r) & (col <= r + WINDOW)
        if first is not None:
            valid = valid & (col >= jnp.where(first, T, 0))
        es, inv = [], []
        for g in range(Q_PER_KV):
            sk = sinks_ref[kv * Q_PER_KV + g]
            sg = jnp.where(valid, s[g * T:(g + 1) * T], NEG_BIG)
            m = jnp.maximum(jnp.max(sg, axis=-1, keepdims=True), sk)
            e = jnp.exp(sg - m)
            es.append(e.astype(BF16))
            inv.append(1.0 / (jnp.sum(e, axis=-1, keepdims=True) + jnp.exp(sk - m)))
        return jnp.concatenate(es, axis=0), jnp.concatenate(inv, axis=0)

    qkv = proj.qkv[rows, :]
    q = qkv[:, :ATTN_WIDTH] * (HEAD_DIM_A ** -0.5)
    k = qkv[:, ATTN_WIDTH:ATTN_WIDTH + KV_WIDTH]
    v = qkv[:, ATTN_WIDTH + KV_WIDTH:]
    kk = jnp.concatenate([carry.kprev[...], k], axis=0)
    vv = jnp.concatenate([carry.vprev[...], v], axis=0)
    carry.kprev[...] = k
    carry.vprev[...] = v
    carry.dt_buf[...] = _softplus(proj.dt[rows, :] + w.dtb[...])
    da_split = jnp.concatenate(_split3_bf16(carry.dt_buf[...] * -jnp.exp(w.alog[...])), axis=1)
    vds = [_dup_half(vv, lo2, kv == 0).astype(BF16) for kv in range(N_KV_A)]
    scores = [_mm_nt(_stack_heads(q, kv, lo), _dup_half(kk, lo2, kv == 0).astype(BF16))
              for kv in range(N_KV_A)]
    yield

    bc = conv_silu(slice(D_INNER, CONV_DIM))
    bgs = [bc[:, g * D_STATE:(g + 1) * D_STATE].astype(BF16) for g in range(N_GROUPS_M)]
    cgs = [bc[:, (N_GROUPS_M + g) * D_STATE:(N_GROUPS_M + g + 1) * D_STATE].astype(BF16)
           for g in range(N_GROUPS_M)]
    bgts = [bc[:, g * D_STATE:(g + 1) * D_STATE].T.astype(BF16) for g in range(N_GROUPS_M)]
    yield

    es0, inv0 = softmax(scores[0], 0)
    yield

    es1, inv1 = softmax(scores[1], 1)
    cbs = [_mm_nt(cgs[g], bgs[g]) for g in range(N_GROUPS_M)]
    h_prevs = [reset(carry.ht[:, gcols(g)]) for g in range(N_GROUPS_M)]
    y_offs = [_mm(cgs[g], h_prevs[g]) for g in range(N_GROUPS_M)]
    o0 = _mm(es0, vds[0])
    cs = jnp.dot(jnp.where(tril, 1.0, 0.0).astype(BF16), da_split, preferred_element_type=F32)
    yield

    a_cs = cs[:, :LANES] + cs[:, LANES:2 * LANES] + cs[:, 2 * LANES:]
    a_cs_t = a_cs.T
    xs_groups = [conv_silu(gcols(0))]
    o1 = _mm(es1, vds[1])
    yield

    xs_groups.append(conv_silu(gcols(1)))
    mixed.attn[rows, 0:2 * LANES] = _unstack_heads(o0 * inv0, T, lo).astype(BF16)
    yield

    def prep(j):
        g = j // pairs_per_group
        dt = carry.dt_buf[...]
        ws, colbs, dtbs = [], [], []
        for h in (2 * j, 2 * j + 1):
            colb = jnp.broadcast_to(a_cs[:, h:h + 1], (T, T))
            rowb = jnp.broadcast_to(a_cs_t[h:h + 1, :], (T, T))
            seg = jnp.where(tril, jnp.exp(colb - rowb), 0.0)
            ws.append((cbs[g] * seg).astype(BF16))
            colbs.append(colb)
            dtbs.append(jnp.broadcast_to(dt[:, h:h + 1], (T, LANES)))
        dt_e = jnp.where(lo, dtbs[0], dtbs[1])
        acs_e = jnp.where(lo, colbs[0], colbs[1])
        xs_j = xs_groups[g][:, (j % pairs_per_group) * LANES:(j % pairs_per_group + 1) * LANES]
        xdt = xs_j * dt_e
        zero = jnp.zeros_like(xdt)
        rhs = jnp.concatenate([jnp.where(lo, xdt, zero), jnp.where(lo, zero, xdt)], axis=0).astype(BF16)
        alast = acs_e[T - 1:T, :]
        return dict(lhs=jnp.concatenate(ws, axis=1), rhs=rhs, xs=xs_j, e_acs=jnp.exp(acs_e),
                    xd=(xdt * jnp.exp(alast - acs_e)).astype(BF16), decay=jnp.exp(alast))

    def finish(j, p, y_diag):
        g, jj = divmod(j, pairs_per_group)
        sl = slice(j * LANES, (j + 1) * LANES)
        y = y_diag + y_offs[g][:, jj * LANES:(jj + 1) * LANES] * p["e_acs"] + p["xs"] * w.dskip[:, sl]
        return y * _silu(proj.z[rows, sl])

    mixed.attn[rows, 2 * LANES:4 * LANES] = _unstack_heads(o1 * inv1, T, lo).astype(BF16)
    preps = {0: prep(0)}
    yield

    ys, y_diags = [], {}
    for j in range(n_pairs):
        if j + 1 < n_pairs:
            preps[j + 1] = prep(j + 1)
        y_diags[j] = _mm(preps[j]["lhs"], preps[j]["rhs"])
        if j >= 1:
            ys.append(finish(j - 1, preps[j - 1], y_diags.pop(j - 1)))
        if j % pairs_per_group == pairs_per_group - 1:
            g = j // pairs_per_group
            grp = [preps[i] for i in range(g * pairs_per_group, (g + 1) * pairs_per_group)]
            carry.ht[:, gcols(g)] = (h_prevs[g] * jnp.concatenate([p["decay"] for p in grp], axis=1)
                                     + _mm(bgts[g], jnp.concatenate([p["xd"] for p in grp], axis=1)))
        yield
    ys.append(finish(n_pairs - 1, preps[n_pairs - 1], y_diags.pop(n_pairs - 1)))
    yield

    m_slabs = []
    for g in range(N_GROUPS_M):
        grp = ys[g * pairs_per_group:(g + 1) * pairs_per_group]
        ssq = grp[0] * grp[0]
        for y in grp[1:]:
            ssq = ssq + y * y
        scale = lax.rsqrt(jnp.sum(ssq, axis=-1, keepdims=True) * (1.0 / GROUP_WIDTH) + EPS)
        m_slabs.extend([y * scale for y in grp])
    mixed.ssd[rows, :] = (jnp.concatenate(m_slabs, axis=1) * w.ssmn[...]).astype(BF16)


def _chain(*gens):
    for g in gens:
        yield from g


def _prompt_mixer_kernel(sinks_ref, x_in_ref, x_res_ref, *refs, chunks_per_seq):
    refs = list(refs)
    take = lambda n: [refs.pop(0) for _ in range(n)]
    w = MixerWeights(*take(len(MixerWeights._fields)))
    x1_ref, = take(1)
    outs = StateOuts(*take(4))
    proj = (ProjBufs(*take(4)), ProjBufs(*take(4)))
    mixed = (MixedBufs(*take(2)), MixedBufs(*take(2)))
    hnorm = take(2)
    carry = Carry(*take(len(Carry._fields)))
    assert not refs
    T = CHUNK
    s = pl.program_id(0)
    every = slice(None)

    @pl.when(s == 0)
    def _():
        for ref in list(proj[1] + mixed[0] + mixed[1] + carry) + hnorm:
            ref[...] = jnp.zeros_like(ref)

    first = (s - 1) % (chunks_per_seq // 2) == 0

    def step(cur):
        prv = 1 - cur
        dense = _alternate(_stage_out_proj(x_res_ref, every, mixed[cur], w, x1_ref, hnorm[cur]),
                           _stage_in_proj(x_in_ref, every, w, proj[cur], hnorm[cur]))
        mix = _chain(_stage_mix(proj[prv], mixed[prv], slice(0, T), carry, w, sinks_ref, first),
                     _stage_mix(proj[prv], mixed[prv], slice(T, 2 * T), carry, w, sinks_ref, None))
        _trace_interleaved(mix, dense)

    for parity in (0, 1):
        pl.when(s % 2 == parity)(functools.partial(step, parity))

    @pl.when((s >= 1) & (s % (chunks_per_seq // 2) == 0))
    def _():
        outs.k[0] = carry.kprev[...]
        outs.v[0] = carry.vprev[...]
        outs.conv[0] = carry.conv_tail[SUBLANES - (CONV_W - 1):, :]
        outs.ssm[0] = carry.ht[...].T


def _const_spec(shape):
    return pl.BlockSpec(shape, lambda *_: (0,) * len(shape), pipeline_mode=pl.Buffered(1))


def _prompt_mixer(x, sinks, weights):
    nb, seq, _ = x.shape
    assert seq % (2 * CHUNK) == 0
    chunks_per_seq = seq // CHUNK
    n_pairs = nb * chunks_per_seq // 2
    pair = 2 * CHUNK
    consts = tuple(weights)
    seq_of_mix = lambda s: jnp.clip((2 * s - 1) // chunks_per_seq, 0, nb - 1)
    in_specs = ([pl.BlockSpec(memory_space=pltpu.SMEM),
                 pl.BlockSpec((pair, D_MODEL), lambda s: (jnp.minimum(s, n_pairs - 1), 0)),
                 pl.BlockSpec((pair, D_MODEL), lambda s: (jnp.maximum(s - 2, 0), 0))]
                + [_const_spec(a.shape) for a in consts])
    out_shape = (jax.ShapeDtypeStruct((nb * seq, D_MODEL), F32),
                 jax.ShapeDtypeStruct((nb, WINDOW, KV_WIDTH), F32),
                 jax.ShapeDtypeStruct((nb, WINDOW, KV_WIDTH), F32),
                 jax.ShapeDtypeStruct((nb, CONV_W - 1, CONV_DIM), F32),
                 jax.ShapeDtypeStruct((nb, D_INNER, D_STATE), F32))
    out_specs = (pl.BlockSpec((pair, D_MODEL), lambda s: (jnp.maximum(s - 2, 0), 0)),
                 pl.BlockSpec((1, WINDOW, KV_WIDTH), lambda s: (seq_of_mix(s), 0, 0)),
                 pl.BlockSpec((1, WINDOW, KV_WIDTH), lambda s: (seq_of_mix(s), 0, 0)),
                 pl.BlockSpec((1, CONV_W - 1, CONV_DIM), lambda s: (seq_of_mix(s), 0, 0)),
                 pl.BlockSpec((1, D_INNER, D_STATE), lambda s: (seq_of_mix(s), 0, 0)))
    proj_bufs = [pltpu.VMEM((pair, QKV_WIDTH), F32), pltpu.VMEM((pair, D_INNER), F32),
                 pltpu.VMEM((pair, CONV_DIM), F32), pltpu.VMEM((pair, LANES), F32)]
    mixed_bufs = [pltpu.VMEM((pair, ATTN_WIDTH), BF16), pltpu.VMEM((pair, D_INNER), BF16)]
    hnorm_bufs = [pltpu.VMEM((pair, D_MODEL), BF16)] * 2
    carry = [pltpu.VMEM((CHUNK, KV_WIDTH), F32), pltpu.VMEM((CHUNK, KV_WIDTH), F32),
             pltpu.VMEM((SUBLANES, CONV_DIM), F32), pltpu.VMEM((D_STATE, D_INNER), F32),
             pltpu.VMEM((CHUNK, LANES), F32)]
    x_rows = x.reshape(nb * seq, D_MODEL)
    return pl.pallas_call(
        functools.partial(_prompt_mixer_kernel, chunks_per_seq=chunks_per_seq),
        grid=(n_pairs + 2,), in_specs=in_specs, out_specs=out_specs,
        out_shape=out_shape, scratch_shapes=proj_bufs * 2 + mixed_bufs * 2 + hnorm_bufs + carry,
        name="prompt_mixer",
        compiler_params=pltpu.CompilerParams(dimension_semantics=("arbitrary",),
                                             vmem_limit_bytes=VMEM_LIMIT),
    )(sinks, x_rows, x_rows, *consts)


MLP_TILE = 1024
FF_BLOCK = 1024


def _mlp_kernel(x_ref, norm2_ref, wup_ref, wdown_ref, fnorm_ref, y_ref):
    x = x_ref[...]
    hm = _rms(x, norm2_ref[...]).astype(BF16)
    acc = x
    for j in range(D_FF // FF_BLOCK):
        h = _mm(hm, wup_ref[:, j * FF_BLOCK:(j + 1) * FF_BLOCK])
        h = jnp.square(jnp.maximum(h, 0.0))
        acc = acc + _mm(h, wdown_ref[j * FF_BLOCK:(j + 1) * FF_BLOCK, :])
    y_ref[...] = _rms(acc, fnorm_ref[...])


def _mlp(x, norm2, wup, wdown, fnorm):
    rows = x.shape[0]
    tile = min(MLP_TILE, rows)
    assert rows % tile == 0 and tile % SUBLANES == 0
    consts = (norm2, wup, wdown, fnorm)
    return pl.pallas_call(
        _mlp_kernel, grid=(rows // tile,),
        in_specs=[pl.BlockSpec((tile, D_MODEL), lambda i: (i, 0))] + [_const_spec(a.shape) for a in consts],
        out_specs=pl.BlockSpec((tile, D_MODEL), lambda i: (i, 0)),
        out_shape=jax.ShapeDtypeStruct((rows, D_MODEL), F32), name="mlp",
        compiler_params=pltpu.CompilerParams(dimension_semantics=("arbitrary",),
                                             vmem_limit_bytes=VMEM_LIMIT),
    )(x, *consts)


def _in_proj_kernel(x_ref, norm1_ref, win_ref, wdt_ref, u_ref, dt_ref, hnorm_ref, hb_s):
    @pl.when(pl.program_id(0) == 0)
    def _():
        hb = _rms(x_ref[...], norm1_ref[...]).astype(BF16)
        hb_s[...] = hb
        hnorm_ref[...] = hb
        dt_ref[...] = _mm(hb, wdt_ref[...])

    u_ref[...] = _mm(hb_s[...], win_ref[0])


def _in_proj(x, norm1, win, wdt):
    rows = x.shape[0]
    n_pieces, k_dim, _ = win.shape
    return pl.pallas_call(
        _in_proj_kernel, grid=(n_pieces,),
        in_specs=[_const_spec(x.shape), _const_spec(norm1.shape),
                  pl.BlockSpec((1, k_dim, PIECE), lambda p: (p, 0, 0)), _const_spec(wdt.shape)],
        out_specs=(pl.BlockSpec((rows, PIECE), lambda p: (0, p)),
                   pl.BlockSpec((rows, LANES), lambda p: (0, 0)), pl.BlockSpec((rows, D_MODEL), lambda p: (0, 0))),
        out_shape=(jax.ShapeDtypeStruct((rows, n_pieces * PIECE), F32), jax.ShapeDtypeStruct((rows, LANES), F32),
                   jax.ShapeDtypeStruct((rows, D_MODEL), BF16)),
        scratch_shapes=[pltpu.VMEM((rows, D_MODEL), BF16)], name="sample_in_proj",
        compiler_params=pltpu.CompilerParams(dimension_semantics=("arbitrary",),
                                             vmem_limit_bytes=VMEM_LIMIT),
    )(x, norm1, win, wdt)


def _tile_roll(x, shift):
    rows, width = x.shape
    tiles = x.reshape(rows // SUBLANES, SUBLANES, width)
    return pltpu.roll(tiles, shift % SUBLANES, axis=1).reshape(rows, width)


def _spread_rows(x):
    tiles = []
    for t in range(x.shape[0] // SUBLANES):
        two_seqs = x[t * SUBLANES:(t + 1) * SUBLANES]
        tiles += [two_seqs, pltpu.roll(two_seqs, SUBLANES // 2, axis=0)]
    return jnp.concatenate(tiles, axis=0)


def _gather_rows(y):
    low = lax.broadcasted_iota(jnp.int32, (SUBLANES, 1), 0) < SUBLANES // 2
    tiles = []
    for t in range(y.shape[0] // (2 * SUBLANES)):
        a = y[2 * t * SUBLANES:(2 * t + 1) * SUBLANES]
        b = y[(2 * t + 1) * SUBLANES:(2 * t + 2) * SUBLANES]
        tiles.append(jnp.where(low, a, pltpu.roll(b, SUBLANES // 2, axis=0)))
    return jnp.concatenate(tiles, axis=0)


def _expand_heads(cols, expand):
    terms = []
    for c in cols:
        hi = c.astype(BF16)
        r1 = c - hi.astype(F32)
        mid = r1.astype(BF16)
        lo = (r1 - mid.astype(F32)).astype(BF16)
        terms.extend([hi, mid, lo])
    rows = cols[0].shape[0]
    out = jnp.dot(jnp.concatenate(terms, axis=0), expand, preferred_element_type=F32)
    return [out[(3 * i) * rows:(3 * i + 1) * rows] + out[(3 * i + 1) * rows:(3 * i + 2) * rows]
            + out[(3 * i + 2) * rows:(3 * i + 3) * rows] for i in range(len(cols))]


def _sample_mixer_kernel(sinks_ref, u_ref, dtraw_ref, cprev_ref, kc_ref, vc_ref, st_ref,
                         convw_ref, convb_ref, dtb_ref, alog_ref, dskip_ref, ssmn_ref,
                         aout_ref, mout_ref, nk_ref, nv_ref, nconv_ref, nst_ref):
    R = SAMPLE_GROUP * SAMPLE_PAD
    L = 4
    qkv_ref = u_ref.at[:, QKV_PIECE0 * PIECE:Z_PIECE0 * PIECE]
    z_ref = u_ref.at[:, Z_PIECE0 * PIECE:XBC_PIECE0 * PIECE]
    xbc_ref = u_ref.at[:, XBC_PIECE0 * PIECE:N_IN_PIECES * PIECE]
    rm = lax.broadcasted_iota(jnp.int32, (R, 1), 0) % SAMPLE_PAD

    xr = xbc_raw = _spread_rows(xbc_ref[...])
    with_prev = jnp.where(rm >= SAMPLE_PAD - (CONV_W - 1), cprev_ref[...], xr)
    yc = xr * convw_ref[CONV_W - 1:CONV_W, :]
    for kshift in range(1, CONV_W):
        yc = yc + _tile_roll(with_prev, kshift) * convw_ref[CONV_W - 1 - kshift:CONV_W - kshift, :]
    nconv_ref[...] = _tile_roll(xbc_raw, -1)
    xbc = _silu(yc + convb_ref[...])
    xs = xbc[:, :D_INNER]
    bm = xbc[:, D_INNER:D_INNER + N_GROUPS_M * D_STATE]
    cm = xbc[:, D_INNER + N_GROUPS_M * D_STATE:]

    dt = _softplus(_spread_rows(dtraw_ref[...]) + dtb_ref[...])
    d_a = dt * (-jnp.exp(alog_ref[...]))
    a_cs = d_a
    suf = jnp.zeros_like(d_a)
    for kshift in range(1, L):
        a_cs = a_cs + jnp.where(rm >= kshift, _tile_roll(d_a, kshift), 0.0)
        suf = suf + jnp.where(rm <= L - 1 - kshift, _tile_roll(d_a, -kshift), 0.0)

    expand = (lax.broadcasted_iota(jnp.int32, (LANES, D_INNER), 1) // HEAD_DIM_M
              == lax.broadcasted_iota(jnp.int32, (LANES, D_INNER), 0)).astype(BF16)
    dt_e, acs_e, suf_e = _expand_heads([dt, a_cs, suf], expand)
    xdt = xs * dt_e
    real = rm < L
    xd_t = jnp.where(real, xdt * jnp.exp(suf_e), 0.0).T.astype(BF16)

    y = xs * dskip_ref[...]
    for kshift in range(L):
        bk = bm if kshift == 0 else _tile_roll(bm, kshift)
        cb = cm * bk
        cb_e = jnp.concatenate(
            [jnp.broadcast_to(jnp.sum(cb[:, g * D_STATE:(g + 1) * D_STATE], axis=-1, keepdims=True),
                              (R, GROUP_WIDTH)) for g in range(N_GROUPS_M)], axis=1)
        if kshift == 0:
            y = y + cb_e * xdt
        else:
            seg = jnp.exp(acs_e - _tile_roll(acs_e, kshift))
            y = y + jnp.where(rm >= kshift, cb_e * seg * _tile_roll(xdt, kshift), 0.0)

    qkv = _spread_rows(qkv_ref[...])
    q = qkv[:, :ATTN_WIDTH] * (HEAD_DIM_A ** -0.5)
    q_swapped = jnp.concatenate(
        [pltpu.roll(q[:, s * LANES:(s + 1) * LANES], HEAD_DIM_A, axis=1) for s in range(ATTN_WIDTH // LANES)], axis=1)
    kn = qkv[:, ATTN_WIDTH:ATTN_WIDTH + KV_WIDTH]
    vn = qkv[:, ATTN_WIDTH + KV_WIDTH:]
    lo8 = _lane_lo((SAMPLE_PAD, LANES))
    zero8 = jnp.zeros((SAMPLE_PAD, LANES), F32)
    seq_rows = [slice(i * SAMPLE_PAD, (i + 1) * SAMPLE_PAD) for i in range(SAMPLE_GROUP)]
    HROWS = N_HEADS_A * SAMPLE_PAD

    s_c, s_n = [], []
    for rows in seq_rows:
        pieces = []
        for kv in range(N_KV_A):
            for g in range(Q_PER_KV):
                slab = kv * (Q_PER_KV // 2) + g // 2
                src = q if g % 2 == kv else q_swapped
                piece = src[rows, slab * LANES:(slab + 1) * LANES]
                pieces.append(jnp.where(lo8, piece, zero8) if kv == 0 else jnp.where(lo8, zero8, piece))
        lhs = jnp.concatenate(pieces, axis=0).astype(BF16)
        i = len(s_c)
        s_c.append(_mm(lhs, kc_ref[i]))
        s_n.append(_mm_nt(lhs, kn[rows]))
    s_c = jnp.concatenate(s_c, axis=0)
    s_n = jnp.concatenate(s_n, axis=0)
    n_rows = SAMPLE_GROUP * HROWS
    tok_c = lax.broadcasted_iota(jnp.int32, (n_rows, WINDOW), 0) % SAMPLE_PAD
    s_c = jnp.where(lax.broadcasted_iota(jnp.int32, (n_rows, WINDOW), 1) >= tok_c, s_c, NEG_BIG)
    tok_n = lax.broadcasted_iota(jnp.int32, (n_rows, SAMPLE_PAD), 0) % SAMPLE_PAD
    col_n = lax.broadcasted_iota(jnp.int32, (n_rows, SAMPLE_PAD), 1)
    s_n = jnp.where((col_n <= tok_n) & (col_n < L), s_n, NEG_BIG)
    sk = jnp.concatenate([jnp.full((SAMPLE_PAD, 1), sinks_ref[h], F32) for h in range(N_HEADS_A)] * SAMPLE_GROUP,
                         axis=0)
    m = jnp.maximum(jnp.maximum(jnp.max(s_c, axis=-1, keepdims=True), jnp.max(s_n, axis=-1, keepdims=True)), sk)
    e_c = jnp.exp(s_c - m)
    e_n = jnp.exp(s_n - m)
    inv_den = 1.0 / (jnp.sum(e_c, axis=-1, keepdims=True) + jnp.sum(e_n, axis=-1, keepdims=True) + jnp.exp(sk - m))
    e_c = e_c.astype(BF16)
    e_n = e_n.astype(BF16)
    o = jnp.concatenate([_mm_nt(e_c[i * HROWS:(i + 1) * HROWS], vc_ref[i]) + _mm(e_n[i * HROWS:(i + 1) * HROWS], vn[rows])
                         for i, rows in enumerate(seq_rows)], axis=0) * inv_den
    o_swapped = pltpu.roll(o, HEAD_DIM_A, axis=1)
    a_rows = []
    for i in range(SAMPLE_GROUP):
        blk = lambda arr, kv, g: arr[i * HROWS + (kv * Q_PER_KV + g) * SAMPLE_PAD:
                                     i * HROWS + (kv * Q_PER_KV + g + 1) * SAMPLE_PAD]
        slabs = []
        for kv in range(N_KV_A):
            for j in range(Q_PER_KV // 2):
                first = blk(o if kv == 0 else o_swapped, kv, 2 * j)
                second = blk(o_swapped if kv == 0 else o, kv, 2 * j + 1)
                slabs.append(jnp.where(lo8, first, second))
        a_rows.append(jnp.concatenate(slabs, axis=1))
    aout_ref[...] = _gather_rows(jnp.concatenate(a_rows, axis=0))

    key = lax.broadcasted_iota(jnp.int32, (KV_WIDTH, WINDOW), 1)
    for cache_ref, new, out_ref in ((kc_ref, kn, nk_ref), (vc_ref, vn, nv_ref)):
        new_t = new.T
        for i in range(SAMPLE_GROUP):
            kept = pltpu.roll(cache_ref[i], WINDOW - L, axis=1)
            fresh = pltpu.roll(new_t, (WINDOW - L - i * SAMPLE_PAD) % WINDOW, axis=1)
            out_ref[i] = jnp.where(key >= WINDOW - L, fresh, kept)

    row_r = lax.broadcasted_iota(jnp.int32, (R, D_STATE), 0)
    heads_per_group = N_HEADS_M // N_GROUPS_M
    yoff = []
    for i, rows in enumerate(seq_rows):
        state = st_ref[i]
        state_b = state.astype(BF16)
        a_tot = a_cs[i * SAMPLE_PAD + L - 1:i * SAMPLE_PAD + L, :]
        in_seq = (row_r >= i * SAMPLE_PAD) & (row_r < (i + 1) * SAMPLE_PAD)
        yo = []
        for g in range(N_GROUPS_M):
            gs = slice(g * GROUP_WIDTH, (g + 1) * GROUP_WIDTH)
            yo.append(_mm_nt(cm[rows, g * D_STATE:(g + 1) * D_STATE], state_b[gs]))
            bsel = jnp.where(in_seq, bm[:, g * D_STATE:(g + 1) * D_STATE], 0.0)
            upd = _mm(xd_t[gs, :], bsel)
            for hh in range(heads_per_group):
                h = g * heads_per_group + hh
                hs = slice(h * HEAD_DIM_M, (h + 1) * HEAD_DIM_M)
                decay = jnp.exp(jnp.broadcast_to(a_tot[:, h:h + 1], (HEAD_DIM_M, D_STATE)))
                nst_ref[i, hs, :] = state[hs] * decay + upd[hh * HEAD_DIM_M:(hh + 1) * HEAD_DIM_M]
        yoff.append(jnp.concatenate(yo, axis=1))
    yoff = jnp.concatenate(yoff, axis=0)

    y = (y + yoff * jnp.exp(acs_e)) * _silu(_spread_rows(z_ref[...]))
    outs = []
    for g in range(N_GROUPS_M):
        yg = y[:, g * GROUP_WIDTH:(g + 1) * GROUP_WIDTH]
        outs.append(yg * lax.rsqrt(jnp.mean(yg * yg, axis=-1, keepdims=True) + EPS))
    mout_ref[...] = _gather_rows(jnp.concatenate(outs, axis=1) * ssmn_ref[...])


def _sample_mixer(sinks, u, dtraw, cprev, kc, vc, st, convw, convb, dtb, alog, dskip, ssmn):
    nseq = kc.shape[0]
    assert nseq % SAMPLE_GROUP == 0
    R = SAMPLE_GROUP * SAMPLE_PAD
    T4 = SAMPLE_GROUP * 4
    rows = nseq * SAMPLE_PAD
    consts = (convw, convb, dtb, alog, dskip, ssmn)
    row_spec = lambda w: pl.BlockSpec((R, w), lambda i: (i, 0))
    tok_spec = lambda w: pl.BlockSpec((T4, w), lambda i: (i, 0))
    seq_spec = lambda a, b: pl.BlockSpec((SAMPLE_GROUP, a, b), lambda i: (i, 0, 0))
    in_specs = ([pl.BlockSpec(memory_space=pltpu.SMEM),
                 tok_spec(N_IN_PIECES * PIECE), tok_spec(LANES), row_spec(CONV_DIM),
                 seq_spec(WINDOW, KV_WIDTH), seq_spec(WINDOW, KV_WIDTH), seq_spec(D_INNER, D_STATE)]
                + [_const_spec(a.shape) for a in consts])
    out_shape = (jax.ShapeDtypeStruct((nseq * 4, ATTN_WIDTH), F32), jax.ShapeDtypeStruct((nseq * 4, D_INNER), F32),
                 jax.ShapeDtypeStruct((nseq, WINDOW, KV_WIDTH), F32),
                 jax.ShapeDtypeStruct((nseq, WINDOW, KV_WIDTH), F32),
                 jax.ShapeDtypeStruct((rows, CONV_DIM), F32),
                 jax.ShapeDtypeStruct((nseq, D_INNER, D_STATE), F32))
    out_specs = (tok_spec(ATTN_WIDTH), tok_spec(D_INNER), seq_spec(WINDOW, KV_WIDTH), seq_spec(WINDOW, KV_WIDTH),
                 row_spec(CONV_DIM), seq_spec(D_INNER, D_STATE))
    return pl.pallas_call(
        _sample_mixer_kernel, grid=(nseq // SAMPLE_GROUP,), in_specs=in_specs, out_specs=out_specs,
        out_shape=out_shape, name="sample_mixer",
        compiler_params=pltpu.CompilerParams(dimension_semantics=("arbitrary",),
                                             vmem_limit_bytes=VMEM_LIMIT),
    )(sinks, u, dtraw, cprev, kc, vc, st, *consts)


N_OUT_PIECES = D_MODEL // PIECE
N_FF_BLOCKS = D_FF // FF_BLOCK


def _sample_post_kernel(x_ref, a_ref, m_ref, hnorm_ref, wga_ref, wgb_ref, woa_ref, wob_ref, wo_ref,
                        norm2_ref, wup_ref, wdown_ref, fnorm_ref, y_ref, merged_s, x1_s, hm_s, acc_s):
    s = pl.program_id(0)

    @pl.when(s < N_OUT_PIECES)
    def _():
        hb = hnorm_ref[...]
        piece = (_sigmoid(_mm(hb, wga_ref[0])) * _mm(a_ref[...], woa_ref[0])
                 + _sigmoid(_mm(hb, wgb_ref[0])) * _mm(m_ref[...], wob_ref[0]))
        for p in range(N_OUT_PIECES):
            @pl.when(s == p)
            def _():
                merged_s[:, _piece_cols(p)] = piece.astype(BF16)

    @pl.when((s >= N_OUT_PIECES) & (s < 2 * N_OUT_PIECES))
    def _():
        piece = _mm(merged_s[...], wo_ref[0])
        for p in range(N_OUT_PIECES):
            @pl.when(s == N_OUT_PIECES + p)
            def _():
                x1_s[:, _piece_cols(p)] = x_ref[:, _piece_cols(p)] + piece

    @pl.when(s == 2 * N_OUT_PIECES)
    def _():
        x1 = x1_s[...]
        hm_s[...] = _rms(x1, norm2_ref[...]).astype(BF16)
        acc_s[...] = x1

    @pl.when(s >= 2 * N_OUT_PIECES)
    def _():
        h = jnp.square(jnp.maximum(_mm(hm_s[...], wup_ref[...]), 0.0))
        acc_s[...] += _mm(h, wdown_ref[...])

    @pl.when(s == 2 * N_OUT_PIECES + N_FF_BLOCKS - 1)
    def _():
        y_ref[...] = _rms(acc_s[...], fnorm_ref[...])


def _sample_post(x, a_out, m_out, hnorm, wgate, woa, wob, wo, norm2, wup, wdown, fnorm):
    rows = x.shape[0]
    first = lambda s: jnp.minimum(s, N_OUT_PIECES - 1)
    second = lambda s: jnp.clip(s - N_OUT_PIECES, 0, N_OUT_PIECES - 1)
    ff = lambda s: jnp.clip(s - 2 * N_OUT_PIECES, 0, N_FF_BLOCKS - 1)
    piece_spec = lambda w, index: pl.BlockSpec((1,) + w.shape[1:], lambda s: (index(s), 0, 0))
    in_specs = [_const_spec(x.shape), _const_spec(a_out.shape), _const_spec(m_out.shape), _const_spec(hnorm.shape),
                piece_spec(wgate, first), piece_spec(wgate, lambda s: N_OUT_PIECES + first(s)),
                piece_spec(woa, first), piece_spec(wob, first), piece_spec(wo, second),
                _const_spec(norm2.shape),
                pl.BlockSpec((D_MODEL, FF_BLOCK), lambda s: (0, ff(s))),
                pl.BlockSpec((FF_BLOCK, D_MODEL), lambda s: (ff(s), 0)),
                _const_spec(fnorm.shape)]
    return pl.pallas_call(
        _sample_post_kernel, grid=(2 * N_OUT_PIECES + N_FF_BLOCKS,), in_specs=in_specs,
        out_specs=pl.BlockSpec((rows, D_MODEL), lambda s: (0, 0)),
        out_shape=jax.ShapeDtypeStruct((rows, D_MODEL), F32),
        scratch_shapes=[pltpu.VMEM((rows, D_MODEL), BF16), pltpu.VMEM((rows, D_MODEL), F32),
                        pltpu.VMEM((rows, D_MODEL), BF16), pltpu.VMEM((rows, D_MODEL), F32)],
        name="sample_post",
        compiler_params=pltpu.CompilerParams(dimension_semantics=("arbitrary",),
                                             vmem_limit_bytes=VMEM_LIMIT),
    )(x, a_out, m_out, hnorm, wgate, wgate, woa, wob, wo, norm2, wup, wdown, fnorm)


def _layer_params(norm1, w_in, sinks, conv_w, conv_b, dt_bias, a_log, d_skip, ssm_norm, w_oa, w_ob, w_o,
                  norm2, w_up, w_down):
    dt0 = QKV_WIDTH + D_INNER + CONV_DIM
    assert dt0 == N_IN_PIECES * PIECE
    pad_lanes = lambda a: jnp.pad(a, ((0, 0), (0, LANES - a.shape[1])))
    w_in_t = w_in.T
    mixer = MixerWeights(
        norm1=norm1[None, :], win=_prep_pieces_t(w_in_t, N_IN_PIECES),
        wgate=_prep_pieces_t_shifted(w_in_t, dt0 + N_HEADS_M, 2 * D_MODEL // PIECE),
        wdt=_prep_dt(w_in_t, dt0),
        convw=conv_w, convb=conv_b[None, :], dtb=pad_lanes(dt_bias[None, :]), alog=pad_lanes(a_log[None, :]),
        dskip=jnp.repeat(d_skip, HEAD_DIM_M)[None, :], ssmn=ssm_norm[None, :],
        woa=_prep_pieces(w_oa), wob=_prep_pieces(w_ob), wo=_prep_pieces(w_o))
    wide = (D_MODEL, D_MODEL // 2)
    return dict(sinks=sinks.astype(F32), mixer=mixer,
                norm2=norm2[None, :], wup=_cast_bf16(w_up, wide), wdown=_cast_bf16(w_down, wide))


def kernel(x_prompt, x_sample, cache_swa_k, cache_swa_v, state_conv, state_ssm, norm1, w_in, sinks, conv_w,
           conv_b, dt_bias, a_log, d_skip, ssm_norm, w_oa, w_ob, w_o, norm2, w_up, w_down, final_norm):
    depth = w_in.shape[0]
    assert depth == 1
    nb, seq, _ = x_prompt.shape
    ns, ls, _ = x_sample.shape
    assert ls == 4
    p = _layer_params(norm1[0], w_in[0], sinks[0], conv_w[0], conv_b[0], dt_bias[0], a_log[0], d_skip[0],
                      ssm_norm[0], w_oa[0], w_ob[0], w_o[0], norm2[0], w_up[0], w_down[0])
    fnorm = final_norm[None, :]
    mw = p["mixer"]
    mixer_consts = (mw.convw, mw.convb, mw.dtb, mw.alog, mw.dskip, mw.ssmn)

    x1p, pk, pv, pc, pst = _prompt_mixer(x_prompt, p["sinks"], mw)
    y_prompt = _mlp(x1p, p["norm2"], p["wup"], p["wdown"], fnorm)
    y_prompt = y_prompt.reshape(nb, seq, D_MODEL)

    xs_rows = x_sample.reshape(ns * ls, D_MODEL)
    dim_major = lambda a: jnp.swapaxes(a, 1, 2)
    cprev = jnp.pad(state_conv[0], ((0, 0), (SAMPLE_PAD - (CONV_W - 1), 0), (0, 0)))
    cprev = cprev.reshape(ns * SAMPLE_PAD, CONV_DIM)
    u, dtraw, hnorm = _in_proj(xs_rows, mw.norm1, mw.win, mw.wdt)
    a_out, m_out, sk, sv, sc, sst = _sample_mixer(
        p["sinks"], u, dtraw, cprev,
        dim_major(cache_swa_k[0].reshape(ns, WINDOW, KV_WIDTH)), dim_major(cache_swa_v[0].reshape(ns, WINDOW, KV_WIDTH)),
        state_ssm[0].reshape(ns, D_INNER, D_STATE), *mixer_consts)
    sk, sv = dim_major(sk), dim_major(sv)
    y_sample = _sample_post(xs_rows, a_out, m_out, hnorm, mw.wgate, mw.woa, mw.wob, mw.wo,
                            p["norm2"], p["wup"], p["wdown"], fnorm).reshape(ns, ls, D_MODEL)
    sc = sc.reshape(ns, SAMPLE_PAD, CONV_DIM)[:, :CONV_W - 1]

    kv_shape = (1, -1, WINDOW, N_KV_A, HEAD_DIM_A)
    ssm_shape = (1, -1, N_HEADS_M, HEAD_DIM_M, D_STATE)
    return (y_prompt, y_sample,
            pk.reshape(kv_shape), pv.reshape(kv_shape), pc[None], pst.reshape(ssm_shape),
            sk.reshape(kv_shape), sv.reshape(kv_shape), sc[None], sst.reshape(ssm_shape))
```

```python
import collections
import functools

import jax
import jax.numpy as jnp
from jax import lax
from jax.experimental import pallas as pl
from jax.experimental.pallas import tpu as pltpu

F32 = jnp.float32
BF16 = jnp.bfloat16

D_MODEL = 1024
N_HEADS_A = 8
N_KV_A = 2
Q_PER_KV = N_HEADS_A // N_KV_A
HEAD_DIM_A = 64
WINDOW = 128
ATTN_WIDTH = N_HEADS_A * HEAD_DIM_A
KV_WIDTH = N_KV_A * HEAD_DIM_A
D_INNER = 1024
HEAD_DIM_M = 64
N_HEADS_M = D_INNER // HEAD_DIM_M
N_GROUPS_M = 2
GROUP_WIDTH = D_INNER // N_GROUPS_M
D_STATE = 128
CONV_W = 4
CONV_DIM = D_INNER + 2 * N_GROUPS_M * D_STATE
CHUNK = 128
D_FF = 4 * D_MODEL
EPS = 1e-6

LANES = 128
SUBLANES = 8
QKV_WIDTH = ATTN_WIDTH + 2 * KV_WIDTH
NEG_BIG = -1e30
VMEM_LIMIT = 56 * 1024 * 1024
SAMPLE_PAD = SUBLANES
SAMPLE_GROUP = 16


def _mm(a, b):
    return jnp.dot(a.astype(BF16), b.astype(BF16), preferred_element_type=F32)


def _mm_nt(a, b):
    return lax.dot_general(a.astype(BF16), b.astype(BF16), (((1,), (1,)), ((), ())),
                           preferred_element_type=F32)


def _rms(x, w):
    return x * lax.rsqrt(jnp.mean(x * x, axis=-1, keepdims=True) + EPS) * w


def _sigmoid(x):
    return 0.5 + 0.5 * jnp.tanh(0.5 * x)


def _silu(x):
    h = 0.5 * x
    return h + h * jnp.tanh(h)


def _softplus(x):
    return jnp.maximum(x, 0.0) + jnp.log(1.0 + jnp.exp(-jnp.abs(x)))


def _lane_lo(shape):
    return (lax.broadcasted_iota(jnp.int32, shape, len(shape) - 1) % LANES) < HEAD_DIM_A


def _dup_half(x, lo, first):
    xr = pltpu.roll(x, HEAD_DIM_A, axis=1)
    return jnp.where(lo, x, xr) if first else jnp.where(lo, xr, x)


def _stack_heads(q, kv, lo):
    qa = q[:, kv * 2 * LANES: kv * 2 * LANES + LANES]
    qb = q[:, kv * 2 * LANES + LANES: (kv + 1) * 2 * LANES]
    zero = jnp.zeros_like(qa)
    return jnp.concatenate([jnp.where(lo, qa, zero), jnp.where(lo, zero, qa),
                            jnp.where(lo, qb, zero), jnp.where(lo, zero, qb)], axis=0)


def _unstack_heads(o, rows, lo):
    return jnp.concatenate([jnp.where(lo, o[0:rows], o[rows:2 * rows]),
                            jnp.where(lo, o[2 * rows:3 * rows], o[3 * rows:4 * rows])], axis=1)


ProjBufs = collections.namedtuple("ProjBufs", "qkv z xbc dt")
MixedBufs = collections.namedtuple("MixedBufs", "attn ssd")
Carry = collections.namedtuple("Carry", "kprev vprev conv_tail ht dt_buf")
MixerWeights = collections.namedtuple(
    "MixerWeights", "norm1 win wgate wdt convw convb dtb alog dskip ssmn woa wob wo")

PIECE = 256
QKV_PIECE0 = 0
Z_PIECE0 = QKV_PIECE0 + QKV_WIDTH // PIECE
XBC_PIECE0 = Z_PIECE0 + D_INNER // PIECE
N_IN_PIECES = XBC_PIECE0 + CONV_DIM // PIECE
PREP_PIECES = 4


def _prep_pieces_kernel(w_ref, o_ref):
    for j in range(PREP_PIECES):
        o_ref[j] = w_ref[:, j * PIECE:(j + 1) * PIECE].astype(BF16)


def _prep_pieces(w, n_pieces=None, k_block=None):
    k_dim, n_dim = w.shape
    n_pieces = n_dim // PIECE if n_pieces is None else n_pieces
    k_block = k_dim if k_block is None else k_block
    cols = PREP_PIECES * PIECE
    return pl.pallas_call(
        _prep_pieces_kernel, grid=(pl.cdiv(n_pieces, PREP_PIECES), k_dim // k_block),
        in_specs=[pl.BlockSpec((k_block, cols), lambda p, k: (k, p))],
        out_specs=pl.BlockSpec((PREP_PIECES, k_block, PIECE), lambda p, k: (p, k, 0)),
        out_shape=jax.ShapeDtypeStruct((n_pieces, k_dim, PIECE), BF16), name="prep_weight_pieces",
        compiler_params=pltpu.CompilerParams(dimension_semantics=("arbitrary", "arbitrary"),
                                             vmem_limit_bytes=VMEM_LIMIT),
    )(w)


def _prep_pieces_t_kernel(wt_ref, o_ref):
    for j in range(PREP_PIECES):
        o_ref[j] = wt_ref[j * PIECE:(j + 1) * PIECE, :].T.astype(BF16)


def _prep_pieces_t(wt, n_pieces=None):
    n_dim, k_dim = wt.shape
    n_pieces = n_dim // PIECE if n_pieces is None else n_pieces
    return pl.pallas_call(
        _prep_pieces_t_kernel, grid=(pl.cdiv(n_pieces, PREP_PIECES),),
        in_specs=[pl.BlockSpec((PREP_PIECES * PIECE, k_dim), lambda p: (p, 0))],
        out_specs=pl.BlockSpec((PREP_PIECES, k_dim, PIECE), lambda p: (p, 0, 0)),
        out_shape=jax.ShapeDtypeStruct((n_pieces, k_dim, PIECE), BF16), name="prep_weight_pieces_t",
        compiler_params=pltpu.CompilerParams(dimension_semantics=("arbitrary",),
                                             vmem_limit_bytes=VMEM_LIMIT),
    )(wt)


def _prep_pieces_t_shifted_kernel(a_ref, b_ref, o_ref, *, shift):
    rows = jnp.concatenate([a_ref[shift:, :], b_ref[:shift, :]], axis=0)
    o_ref[0] = rows.T.astype(BF16)


def _prep_pieces_t_shifted(wt, row0, n_pieces):
    k_dim = wt.shape[1]
    block0, shift = divmod(row0, PIECE)
    assert shift % SUBLANES == 0 and shift > 0
    return pl.pallas_call(
        functools.partial(_prep_pieces_t_shifted_kernel, shift=shift), grid=(n_pieces,),
        in_specs=[pl.BlockSpec((PIECE, k_dim), lambda p: (block0 + p, 0)),
                  pl.BlockSpec((PIECE, k_dim), lambda p: (block0 + p + 1, 0))],
        out_specs=pl.BlockSpec((1, k_dim, PIECE), lambda p: (p, 0, 0)),
        out_shape=jax.ShapeDtypeStruct((n_pieces, k_dim, PIECE), BF16), name="prep_weight_pieces_t_shifted",
        compiler_params=pltpu.CompilerParams(dimension_semantics=("arbitrary",),
                                             vmem_limit_bytes=VMEM_LIMIT),
    )(wt, wt)


def _prep_dt_kernel(wt_ref, o_ref):
    rows = wt_ref[...]
    keep = lax.broadcasted_iota(jnp.int32, rows.shape, 0) < N_HEADS_M
    o_ref[...] = jnp.where(keep, rows, 0.0).T.astype(BF16)


def _prep_dt(wt, row0):
    k_dim = wt.shape[1]
    assert row0 % LANES == 0
    return pl.pallas_call(
        _prep_dt_kernel, grid=(1,),
        in_specs=[pl.BlockSpec((LANES, k_dim), lambda i: (row0 // LANES, 0))],
        out_specs=pl.BlockSpec((k_dim, LANES), lambda i: (0, 0)),
        out_shape=jax.ShapeDtypeStruct((k_dim, LANES), BF16), name="prep_weight_dt",
        compiler_params=pltpu.CompilerParams(dimension_semantics=("arbitrary",), vmem_limit_bytes=VMEM_LIMIT),
    )(wt)


def _cast_kernel(w_ref, o_ref):
    o_ref[...] = w_ref[...].astype(BF16)


def _cast_bf16(w, block):
    return pl.pallas_call(
        _cast_kernel, grid=(w.shape[0] // block[0], w.shape[1] // block[1]),
        in_specs=[pl.BlockSpec(block, lambda i, j: (i, j))], out_specs=pl.BlockSpec(block, lambda i, j: (i, j)),
        out_shape=jax.ShapeDtypeStruct(w.shape, BF16), name="cast_weight",
        compiler_params=pltpu.CompilerParams(dimension_semantics=("arbitrary", "arbitrary"),
                                             vmem_limit_bytes=VMEM_LIMIT),
    )(w)


StateOuts = collections.namedtuple("StateOuts", "k v conv ssm")


def _split3_bf16(x):
    hi = x.astype(BF16)
    r1 = x - hi.astype(F32)
    mid = r1.astype(BF16)
    return hi, mid, (r1 - mid.astype(F32)).astype(BF16)


def _piece_cols(p):
    return slice(p * PIECE, (p + 1) * PIECE)


def _in_proj_pieces(hb, w, dsts):
    for dst, first in zip(dsts, (QKV_PIECE0, Z_PIECE0, XBC_PIECE0)):
        for p in range(dst.shape[1] // PIECE):
            dst[:, _piece_cols(p)] = _mm(hb, w.win[first + p])
            yield


def _out_proj_pieces(x_ref, rows, hb, attn_ref, ssd_ref, w, out_ref):
    merged = []
    for half, (src_ref, w_ref) in enumerate(((attn_ref, w.woa), (ssd_ref, w.wob))):
        parts = []
        for p in range(D_MODEL // PIECE):
            gate = _sigmoid(_mm(hb, w.wgate[half * (D_MODEL // PIECE) + p]))
            yield
            parts.append(gate * _mm(src_ref[...], w_ref[p]))
            yield
        merged.append(jnp.concatenate(parts, axis=1))
    merged = (merged[0] + merged[1]).astype(BF16)
    for p in range(D_MODEL // PIECE):
        out_ref[rows, _piece_cols(p)] = x_ref[rows, _piece_cols(p)] + _mm(merged, w.wo[p])
        yield


def _stage_in_proj(x_ref, rows, w, proj, hnorm_ref):
    hb = _rms(x_ref[rows, :], w.norm1[...]).astype(BF16)
    hnorm_ref[...] = hb
    yield
    yield from _in_proj_pieces(hb, w, (proj.qkv, proj.z, proj.xbc))
    proj.dt[...] = _mm(hb, w.wdt[...])
    yield


def _stage_out_proj(x_ref, rows, mixed, w, x1_ref, hnorm_ref):
    hb = hnorm_ref[...]
    yield
    yield from _out_proj_pieces(x_ref, rows, hb, mixed.attn, mixed.ssd, w, x1_ref)


def _alternate(*gens):
    gens = list(gens)
    while gens:
        for g in list(gens):
            try:
                next(g)
                yield
            except StopIteration:
                gens.remove(g)


def _trace_interleaved(primary, filler):
    for _ in primary:
        next(filler, None)
    for _ in filler:
        pass


def _stage_mix(proj, mixed, rows, carry, w, sinks_ref, first):
    T = CHUNK
    reset = (lambda a: a) if first is None else (lambda a: jnp.where(first, 0.0, a))
    lo = _lane_lo((T, LANES))
    lo2 = _lane_lo((2 * T, LANES))
    rm = lax.broadcasted_iota(jnp.int32, (T, 1), 0) % SUBLANES
    rr = lax.broadcasted_iota(jnp.int32, (T, T), 0)
    cc = lax.broadcasted_iota(jnp.int32, (T, T), 1)
    tril = cc <= rr
    pairs_per_group = N_HEADS_M // 2 // N_GROUPS_M
    n_pairs = N_HEADS_M // 2
    gcols = lambda g: slice(g * GROUP_WIDTH, (g + 1) * GROUP_WIDTH)

    def conv_silu(cols):
        xr = proj.xbc[rows, cols]
        prev_tile = jnp.concatenate([reset(carry.conv_tail[:, cols]), xr[:T - SUBLANES]], axis=0)
        carry.conv_tail[:, cols] = xr[T - SUBLANES:]
        yc = xr * w.convw[CONV_W - 1:CONV_W, cols]
        for shift in range(1, CONV_W):
            shifted = _tile_roll(jnp.where(rm >= SUBLANES - shift, prev_tile, xr), shift)
            yc = yc + shifted * w.convw[CONV_W - 1 - shift:CONV_W - shift, cols]
        return _silu(yc + w.convb[:, cols])

    def softmax(s, kv):
        r = lax.broadcasted_iota(jnp.int32, (T, 2 * T), 0)
        col = lax.broadcasted_iota(jnp.int32, (T, 2 * T), 1)
        valid = (col >= r) & (col <= r + WINDOW)
        if first is not None:
            valid = valid & (col >= jnp.where(first, T, 0))
        es, inv = [], []
        for g in range(Q_PER_KV):
            sk = sinks_ref[kv * Q_PER_KV + g]
            sg = jnp.where(valid, s[g * T:(g + 1) * T], NEG_BIG)
            m = jnp.maximum(jnp.max(sg, axis=-1, keepdims=True), sk)
            e = jnp.exp(sg - m)
            es.append(e.astype(BF16))
            inv.append(1.0 / (jnp.sum(e, axis=-1, keepdims=True) + jnp.exp(sk - m)))
        return jnp.concatenate(es, axis=0), jnp.concatenate(inv, axis=0)

    qkv = proj.qkv[rows, :]
    q = qkv[:, :ATTN_WIDTH] * (HEAD_DIM_A ** -0.5)
    k = qkv[:, ATTN_WIDTH:ATTN_WIDTH + KV_WIDTH]
    v = qkv[:, ATTN_WIDTH + KV_WIDTH:]
    kk = jnp.concatenate([carry.kprev[...], k], axis=0)
    vv = jnp.concatenate([carry.vprev[...], v], axis=0)
    carry.kprev[...] = k
    carry.vprev[...] = v
    carry.dt_buf[...] = _softplus(proj.dt[rows, :] + w.dtb[...])
    da_split = jnp.concatenate(_split3_bf16(carry.dt_buf[...] * -jnp.exp(w.alog[...])), axis=1)
    vds = [_dup_half(vv, lo2, kv == 0).astype(BF16) for kv in range(N_KV_A)]
    scores = [_mm_nt(_stack_heads(q, kv, lo), _dup_half(kk, lo2, kv == 0).astype(BF16))
              for kv in range(N_KV_A)]
    yield

    bc = conv_silu(slice(D_INNER, CONV_DIM))
    bgs = [bc[:, g * D_STATE:(g + 1) * D_STATE].astype(BF16) for g in range(N_GROUPS_M)]
    cgs = [bc[:, (N_GROUPS_M + g) * D_STATE:(N_GROUPS_M + g + 1) * D_STATE].astype(BF16)
           for g in range(N_GROUPS_M)]
    bgts = [bc[:, g * D_STATE:(g + 1) * D_STATE].T.astype(BF16) for g in range(N_GROUPS_M)]
    yield

    es0, inv0 = softmax(scores[0], 0)
    yield

    es1, inv1 = softmax(scores[1], 1)
    cbs = [_mm_nt(cgs[g], bgs[g]) for g in range(N_GROUPS_M)]
    h_prevs = [reset(carry.ht[:, gcols(g)]) for g in range(N_GROUPS_M)]
    y_offs = [_mm(cgs[g], h_prevs[g]) for g in range(N_GROUPS_M)]
    o0 = _mm(es0, vds[0])
    cs = jnp.dot(jnp.where(tril, 1.0, 0.0).astype(BF16), da_split, preferred_element_type=F32)
    yield

    a_cs = cs[:, :LANES] + cs[:, LANES:2 * LANES] + cs[:, 2 * LANES:]
    a_cs_t = a_cs.T
    xs_groups = [conv_silu(gcols(0))]
    o1 = _mm(es1, vds[1])
    yield

    xs_groups.append(conv_silu(gcols(1)))
    mixed.attn[rows, 0:2 * LANES] = _unstack_heads(o0 * inv0, T, lo).astype(BF16)
    yield

    def prep(j):
        g = j // pairs_per_group
        dt = carry.dt_buf[...]
        ws, colbs, dtbs = [], [], []
        for h in (2 * j, 2 * j + 1):
            colb = jnp.broadcast_to(a_cs[:, h:h + 1], (T, T))
            rowb = jnp.broadcast_to(a_cs_t[h:h + 1, :], (T, T))
            seg = jnp.where(tril, jnp.exp(colb - rowb), 0.0)
            ws.append((cbs[g] * seg).astype(BF16))
            colbs.append(colb)
            dtbs.append(jnp.broadcast_to(dt[:, h:h + 1], (T, LANES)))
        dt_e = jnp.where(lo, dtbs[0], dtbs[1])
        acs_e = jnp.where(lo, colbs[0], colbs[1])
        xs_j = xs_groups[g][:, (j % pairs_per_group) * LANES:(j % pairs_per_group + 1) * LANES]
        xdt = xs_j * dt_e
        zero = jnp.zeros_like(xdt)
        rhs = jnp.concatenate([jnp.where(lo, xdt, zero), jnp.where(lo, zero, xdt)], axis=0).astype(BF16)
        alast = acs_e[T - 1:T, :]
        return dict(lhs=jnp.concatenate(ws, axis=1), rhs=rhs, xs=xs_j, e_acs=jnp.exp(acs_e),
                    xd=(xdt * jnp.exp(alast - acs_e)).astype(BF16), decay=jnp.exp(alast))

    def finish(j, p, y_diag):
        g, jj = divmod(j, pairs_per_group)
        sl = slice(j * LANES, (j + 1) * LANES)
        y = y_diag + y_offs[g][:, jj * LANES:(jj + 1) * LANES] * p["e_acs"] + p["xs"] * w.dskip[:, sl]
        return y * _silu(proj.z[rows, sl])

    mixed.attn[rows, 2 * LANES:4 * LANES] = _unstack_heads(o1 * inv1, T, lo).astype(BF16)
    preps = {0: prep(0)}
    yield

    ys, y_diags = [], {}
    for j in range(n_pairs):
        if j + 1 < n_pairs:
            preps[j + 1] = prep(j + 1)
        y_diags[j] = _mm(preps[j]["lhs"], preps[j]["rhs"])
        if j >= 1:
            ys.append(finish(j - 1, preps[j - 1], y_diags.pop(j - 1)))
        if j % pairs_per_group == pairs_per_group - 1:
            g = j // pairs_per_group
            grp = [preps[i] for i in range(g * pairs_per_group, (g + 1) * pairs_per_group)]
            carry.ht[:, gcols(g)] = (h_prevs[g] * jnp.concatenate([p["decay"] for p in grp], axis=1)
                                     + _mm(bgts[g], jnp.concatenate([p["xd"] for p in grp], axis=1)))
        yield
    ys.append(finish(n_pairs - 1, preps[n_pairs - 1], y_diags.pop(n_pairs - 1)))
    yield

    m_slabs = []
    for g in range(N_GROUPS_M):
        grp = ys[g * pairs_per_group:(g + 1) * pairs_per_group]
        ssq = grp[0] * grp[0]
        for y in grp[1:]:
            ssq = ssq + y * y
        scale = lax.rsqrt(jnp.sum(ssq, axis=-1, keepdims=True) * (1.0 / GROUP_WIDTH) + EPS)
        m_slabs.extend([y * scale for y in grp])
    mixed.ssd[rows, :] = (jnp.concatenate(m_slabs, axis=1) * w.ssmn[...]).astype(BF16)


def _chain(*gens):
    for g in gens:
        yield from g


def _prompt_mixer_kernel(sinks_ref, x_in_ref, x_res_ref, *refs, chunks_per_seq):
    refs = list(refs)
    take = lambda n: [refs.pop(0) for _ in range(n)]
    w = MixerWeights(*take(len(MixerWeights._fields)))
    x1_ref, = take(1)
    outs = StateOuts(*take(4))
    proj = (ProjBufs(*take(4)), ProjBufs(*take(4)))
    mixed = (MixedBufs(*take(2)), MixedBufs(*take(2)))
    hnorm = take(2)
    carry = Carry(*take(len(Carry._fields)))
    assert not refs
    T = CHUNK
    s = pl.program_id(0)
    every = slice(None)

    @pl.when(s == 0)
    def _():
        for ref in list(proj[1] + mixed[0] + mixed[1] + carry) + hnorm:
            ref[...] = jnp.zeros_like(ref)

    first = (s - 1) % (chunks_per_seq // 2) == 0

    def step(cur):
        prv = 1 - cur
        dense = _alternate(_stage_out_proj(x_res_ref, every, mixed[cur], w, x1_ref, hnorm[cur]),
                           _stage_in_proj(x_in_ref, every, w, proj[cur], hnorm[cur]))
        mix = _chain(_stage_mix(proj[prv], mixed[prv], slice(0, T), carry, w, sinks_ref, first),
                     _stage_mix(proj[prv], mixed[prv], slice(T, 2 * T), carry, w, sinks_ref, None))
        _trace_interleaved(mix, dense)

    for parity in (0, 1):
        pl.when(s % 2 == parity)(functools.partial(step, parity))

    @pl.when((s >= 1) & (s % (chunks_per_seq // 2) == 0))
    def _():
        outs.k[0] = carry.kprev[...]
        outs.v[0] = carry.vprev[...]
        outs.conv[0] = carry.conv_tail[SUBLANES - (CONV_W - 1):, :]
        outs.ssm[0] = carry.ht[...].T


def _const_spec(shape):
    return pl.BlockSpec(shape, lambda *_: (0,) * len(shape), pipeline_mode=pl.Buffered(1))


def _prompt_mixer(x, sinks, weights):
    nb, seq, _ = x.shape
    assert seq % (2 * CHUNK) == 0
    chunks_per_seq = seq // CHUNK
    n_pairs = nb * chunks_per_seq // 2
    pair = 2 * CHUNK
    consts = tuple(weights)
    seq_of_mix = lambda s: jnp.clip((2 * s - 1) // chunks_per_seq, 0, nb - 1)
    in_specs = ([pl.BlockSpec(memory_space=pltpu.SMEM),
                 pl.BlockSpec((pair, D_MODEL), lambda s: (jnp.minimum(s, n_pairs - 1), 0)),
                 pl.BlockSpec((pair, D_MODEL), lambda s: (jnp.maximum(s - 2, 0), 0))]
                + [_const_spec(a.shape) for a in consts])
    out_shape = (jax.ShapeDtypeStruct((nb * seq, D_MODEL), F32),
                 jax.ShapeDtypeStruct((nb, WINDOW, KV_WIDTH), F32),
                 jax.ShapeDtypeStruct((nb, WINDOW, KV_WIDTH), F32),
                 jax.ShapeDtypeStruct((nb, CONV_W - 1, CONV_DIM), F32),
                 jax.ShapeDtypeStruct((nb, D_INNER, D_STATE), F32))
    out_specs = (pl.BlockSpec((pair, D_MODEL), lambda s: (jnp.maximum(s - 2, 0), 0)),
                 pl.BlockSpec((1, WINDOW, KV_WIDTH), lambda s: (seq_of_mix(s), 0, 0)),
                 pl.BlockSpec((1, WINDOW, KV_WIDTH), lambda s: (seq_of_mix(s), 0, 0)),
                 pl.BlockSpec((1, CONV_W - 1, CONV_DIM), lambda s: (seq_of_mix(s), 0, 0)),
                 pl.BlockSpec((1, D_INNER, D_STATE), lambda s: (seq_of_mix(s), 0, 0)))
    proj_bufs = [pltpu.VMEM((pair, QKV_WIDTH), F32), pltpu.VMEM((pair, D_INNER), F32),
                 pltpu.VMEM((pair, CONV_DIM), F32), pltpu.VMEM((pair, LANES), F32)]
    mixed_bufs = [pltpu.VMEM((pair, ATTN_WIDTH), BF16), pltpu.VMEM((pair, D_INNER), BF16)]
    hnorm_bufs = [pltpu.VMEM((pair, D_MODEL), BF16)] * 2
    carry = [pltpu.VMEM((CHUNK, KV_WIDTH), F32), pltpu.VMEM((CHUNK, KV_WIDTH), F32),
             pltpu.VMEM((SUBLANES, CONV_DIM), F32), pltpu.VMEM((D_STATE, D_INNER), F32),
             pltpu.VMEM((CHUNK, LANES), F32)]
    x_rows = x.reshape(nb * seq, D_MODEL)
    return pl.pallas_call(
        functools.partial(_prompt_mixer_kernel, chunks_per_seq=chunks_per_seq),
        grid=(n_pairs + 2,), in_specs=in_specs, out_specs=out_specs,
        out_shape=out_shape, scratch_shapes=proj_bufs * 2 + mixed_bufs * 2 + hnorm_bufs + carry,
        name="prompt_mixer",
        compiler_params=pltpu.CompilerParams(dimension_semantics=("arbitrary",),
                                             vmem_limit_bytes=VMEM_LIMIT),
    )(sinks, x_rows, x_rows, *consts)


MLP_TILE = 1024
FF_BLOCK = 1024


def _mlp_kernel(x_ref, norm2_ref, wup_ref, wdown_ref, fnorm_ref, y_ref):
    x = x_ref[...]
    hm = _rms(x, norm2_ref[...]).astype(BF16)
    acc = x
    for j in range(D_FF // FF_BLOCK):
        h = _mm(hm, wup_ref[:, j * FF_BLOCK:(j + 1) * FF_BLOCK])
        h = jnp.square(jnp.maximum(h, 0.0))
        acc = acc + _mm(h, wdown_ref[j * FF_BLOCK:(j + 1) * FF_BLOCK, :])
    y_ref[...] = _rms(acc, fnorm_ref[...])


def _mlp(x, norm2, wup, wdown, fnorm):
    rows = x.shape[0]
    tile = min(MLP_TILE, rows)
    assert rows % tile == 0 and tile % SUBLANES == 0
    consts = (norm2, wup, wdown, fnorm)
    return pl.pallas_call(
        _mlp_kernel, grid=(rows // tile,),
        in_specs=[pl.BlockSpec((tile, D_MODEL), lambda i: (i, 0))] + [_const_spec(a.shape) for a in consts],
        out_specs=pl.BlockSpec((tile, D_MODEL), lambda i: (i, 0)),
        out_shape=jax.ShapeDtypeStruct((rows, D_MODEL), F32), name="mlp",
        compiler_params=pltpu.CompilerParams(dimension_semantics=("arbitrary",),
                                             vmem_limit_bytes=VMEM_LIMIT),
    )(x, *consts)


IN_PROJ_ROWS = 256


def _in_proj_kernel(x_ref, norm1_ref, win_ref, wdt_ref, u_ref, dt_ref, hnorm_ref):
    hb = _rms(x_ref[...], norm1_ref[...]).astype(BF16)
    hnorm_ref[...] = hb
    for p in range(win_ref.shape[0]):
        u_ref[:, _piece_cols(p)] = _mm(hb, win_ref[p])
    dt_ref[...] = _mm(hb, wdt_ref[...])


def _in_proj(x, norm1, win, wdt):
    rows = x.shape[0]
    assert rows % IN_PROJ_ROWS == 0
    outs = ((win.shape[0] * PIECE, F32), (LANES, F32), (D_MODEL, BF16))
    row_spec = lambda width: pl.BlockSpec((IN_PROJ_ROWS, width), lambda i: (i, 0))
    return pl.pallas_call(
        _in_proj_kernel, grid=(rows // IN_PROJ_ROWS,),
        in_specs=[row_spec(D_MODEL), _const_spec(norm1.shape), _const_spec(win.shape), _const_spec(wdt.shape)],
        out_specs=tuple(row_spec(width) for width, _ in outs),
        out_shape=tuple(jax.ShapeDtypeStruct((rows, width), dtype) for width, dtype in outs),
        name="sample_in_proj",
        compiler_params=pltpu.CompilerParams(dimension_semantics=("arbitrary",),
                                             vmem_limit_bytes=VMEM_LIMIT),
    )(x, norm1, win, wdt)


def _tile_roll(x, shift):
    rows, width = x.shape
    tiles = x.reshape(rows // SUBLANES, SUBLANES, width)
    return pltpu.roll(tiles, shift % SUBLANES, axis=1).reshape(rows, width)


def _spread_rows(x):
    tiles = []
    for t in range(x.shape[0] // SUBLANES):
        two_seqs = x[t * SUBLANES:(t + 1) * SUBLANES]
        tiles += [two_seqs, pltpu.roll(two_seqs, SUBLANES // 2, axis=0)]
    return jnp.concatenate(tiles, axis=0)


def _gather_rows(y):
    low = lax.broadcasted_iota(jnp.int32, (SUBLANES, 1), 0) < SUBLANES // 2
    tiles = []
    for t in range(y.shape[0] // (2 * SUBLANES)):
        a = y[2 * t * SUBLANES:(2 * t + 1) * SUBLANES]
        b = y[(2 * t + 1) * SUBLANES:(2 * t + 2) * SUBLANES]
        tiles.append(jnp.where(low, a, pltpu.roll(b, SUBLANES // 2, axis=0)))
    return jnp.concatenate(tiles, axis=0)


def _expand_heads(cols, expand):
    terms = []
    for c in cols:
        hi = c.astype(BF16)
        r1 = c - hi.astype(F32)
        mid = r1.astype(BF16)
        lo = (r1 - mid.astype(F32)).astype(BF16)
        terms.extend([hi, mid, lo])
    rows = cols[0].shape[0]
    out = jnp.dot(jnp.concatenate(terms, axis=0), expand, preferred_element_type=F32)
    return [out[(3 * i) * rows:(3 * i + 1) * rows] + out[(3 * i + 1) * rows:(3 * i + 2) * rows]
            + out[(3 * i + 2) * rows:(3 * i + 3) * rows] for i in range(len(cols))]


def _sample_mixer_kernel(sinks_ref, u_ref, dtraw_ref, cprev_ref, kc_ref, vc_ref, st_ref,
                         convw_ref, convb_ref, dtb_ref, alog_ref, dskip_ref, ssmn_ref,
                         aout_ref, mout_ref, nk_ref, nv_ref, nconv_ref, nst_ref):
    R = SAMPLE_GROUP * SAMPLE_PAD
    L = 4
    qkv_ref = u_ref.at[:, QKV_PIECE0 * PIECE:Z_PIECE0 * PIECE]
    z_ref = u_ref.at[:, Z_PIECE0 * PIECE:XBC_PIECE0 * PIECE]
    xbc_ref = u_ref.at[:, XBC_PIECE0 * PIECE:N_IN_PIECES * PIECE]
    rm = lax.broadcasted_iota(jnp.int32, (R, 1), 0) % SAMPLE_PAD

    xr = xbc_raw = _spread_rows(xbc_ref[...])
    with_prev = jnp.where(rm >= SAMPLE_PAD - (CONV_W - 1), cprev_ref[...], xr)
    yc = xr * convw_ref[CONV_W - 1:CONV_W, :]
    for kshift in range(1, CONV_W):
        yc = yc + _tile_roll(with_prev, kshift) * convw_ref[CONV_W - 1 - kshift:CONV_W - kshift, :]
    nconv_ref[...] = _tile_roll(xbc_raw, -1)
    xbc = _silu(yc + convb_ref[...])
    xs = xbc[:, :D_INNER]
    bm = xbc[:, D_INNER:D_INNER + N_GROUPS_M * D_STATE]
    cm = xbc[:, D_INNER + N_GROUPS_M * D_STATE:]

    dt = _softplus(_spread_rows(dtraw_ref[...]) + dtb_ref[...])
    d_a = dt * (-jnp.exp(alog_ref[...]))
    a_cs = d_a
    suf = jnp.zeros_like(d_a)
    for kshift in range(1, L):
        a_cs = a_cs + jnp.where(rm >= kshift, _tile_roll(d_a, kshift), 0.0)
        suf = suf + jnp.where(rm <= L - 1 - kshift, _tile_roll(d_a, -kshift), 0.0)

    expand = (lax.broadcasted_iota(jnp.int32, (LANES, D_INNER), 1) // HEAD_DIM_M
              == lax.broadcasted_iota(jnp.int32, (LANES, D_INNER), 0)).astype(BF16)
    dt_e, acs_e, suf_e = _expand_heads([dt, a_cs, suf], expand)
    xdt = xs * dt_e
    real = rm < L
    xd_t = jnp.where(real, xdt * jnp.exp(suf_e), 0.0).T.astype(BF16)

    y = xs * dskip_ref[...]
    for kshift in range(L):
        bk = bm if kshift == 0 else _tile_roll(bm, kshift)
        cb = cm * bk
        cb_e = jnp.concatenate(
            [jnp.broadcast_to(jnp.sum(cb[:, g * D_STATE:(g + 1) * D_STATE], axis=-1, keepdims=True),
                              (R, GROUP_WIDTH)) for g in range(N_GROUPS_M)], axis=1)
        if kshift == 0:
            y = y + cb_e * xdt
        else:
            seg = jnp.exp(acs_e - _tile_roll(acs_e, kshift))
            y = y + jnp.where(rm >= kshift, cb_e * seg * _tile_roll(xdt, kshift), 0.0)

    qkv = _spread_rows(qkv_ref[...])
    q = qkv[:, :ATTN_WIDTH] * (HEAD_DIM_A ** -0.5)
    q_swapped = jnp.concatenate(
        [pltpu.roll(q[:, s * LANES:(s + 1) * LANES], HEAD_DIM_A, axis=1) for s in range(ATTN_WIDTH // LANES)], axis=1)
    kn = qkv[:, ATTN_WIDTH:ATTN_WIDTH + KV_WIDTH]
    vn = qkv[:, ATTN_WIDTH + KV_WIDTH:]
    lo8 = _lane_lo((SAMPLE_PAD, LANES))
    zero8 = jnp.zeros((SAMPLE_PAD, LANES), F32)
    seq_rows = [slice(i * SAMPLE_PAD, (i + 1) * SAMPLE_PAD) for i in range(SAMPLE_GROUP)]
    HROWS = N_HEADS_A * SAMPLE_PAD

    s_c, s_n = [], []
    for rows in seq_rows:
        pieces = []
        for kv in range(N_KV_A):
            for g in range(Q_PER_KV):
                slab = kv * (Q_PER_KV // 2) + g // 2
                src = q if g % 2 == kv else q_swapped
                piece = src[rows, slab * LANES:(slab + 1) * LANES]
                pieces.append(jnp.where(lo8, piece, zero8) if kv == 0 else jnp.where(lo8, zero8, piece))
        lhs = jnp.concatenate(pieces, axis=0).astype(BF16)
        i = len(s_c)
        s_c.append(_mm(lhs, kc_ref[i]))
        s_n.append(_mm_nt(lhs, kn[rows]))
    s_c = jnp.concatenate(s_c, axis=0)
    s_n = jnp.concatenate(s_n, axis=0)
    n_rows = SAMPLE_GROUP * HROWS
    tok_c = lax.broadcasted_iota(jnp.int32, (n_rows, WINDOW), 0) % SAMPLE_PAD
    s_c = jnp.where(lax.broadcasted_iota(jnp.int32, (n_rows, WINDOW), 1) >= tok_c, s_c, NEG_BIG)
    tok_n = lax.broadcasted_iota(jnp.int32, (n_rows, SAMPLE_PAD), 0) % SAMPLE_PAD
    col_n = lax.broadcasted_iota(jnp.int32, (n_rows, SAMPLE_PAD), 1)
    s_n = jnp.where((col_n <= tok_n) & (col_n < L), s_n, NEG_BIG)
    sk = jnp.concatenate([jnp.full((SAMPLE_PAD, 1), sinks_ref[h], F32) for h in range(N_HEADS_A)] * SAMPLE_GROUP,
                         axis=0)
    m = jnp.maximum(jnp.maximum(jnp.max(s_c, axis=-1, keepdims=True), jnp.max(s_n, axis=-1, keepdims=True)), sk)
    e_c = jnp.exp(s_c - m)
    e_n = jnp.exp(s_n - m)
    inv_den = 1.0 / (jnp.sum(e_c, axis=-1, keepdims=True) + jnp.sum(e_n, axis=-1, keepdims=True) + jnp.exp(sk - m))
    e_c = e_c.astype(BF16)
    e_n = e_n.astype(BF16)
    o = jnp.concatenate([_mm_nt(e_c[i * HROWS:(i + 1) * HROWS], vc_ref[i]) + _mm(e_n[i * HROWS:(i + 1) * HROWS], vn[rows])
                         for i, rows in enumerate(seq_rows)], axis=0) * inv_den
    o_swapped = pltpu.roll(o, HEAD_DIM_A, axis=1)
    a_rows = []
    for i in range(SAMPLE_GROUP):
        blk = lambda arr, kv, g: arr[i * HROWS + (kv * Q_PER_KV + g) * SAMPLE_PAD:
                                     i * HROWS + (kv * Q_PER_KV + g + 1) * SAMPLE_PAD]
        slabs = []
        for kv in range(N_KV_A):
            for j in range(Q_PER_KV // 2):
                first = blk(o if kv == 0 else o_swapped, kv, 2 * j)
                second = blk(o_swapped if kv == 0 else o, kv, 2 * j + 1)
                slabs.append(jnp.where(lo8, first, second))
        a_rows.append(jnp.concatenate(slabs, axis=1))
    aout_ref[...] = _gather_rows(jnp.concatenate(a_rows, axis=0))

    key = lax.broadcasted_iota(jnp.int32, (KV_WIDTH, WINDOW), 1)
    for cache_ref, new, out_ref in ((kc_ref, kn, nk_ref), (vc_ref, vn, nv_ref)):
        new_t = new.T
        for i in range(SAMPLE_GROUP):
            kept = pltpu.roll(cache_ref[i], WINDOW - L, axis=1)
            fresh = pltpu.roll(new_t, (WINDOW - L - i * SAMPLE_PAD) % WINDOW, axis=1)
            out_ref[i] = jnp.where(key >= WINDOW - L, fresh, kept)

    row_r = lax.broadcasted_iota(jnp.int32, (R, D_STATE), 0)
    heads_per_group = N_HEADS_M // N_GROUPS_M
    yoff = []
    for i, rows in enumerate(seq_rows):
        state = st_ref[i]
        state_b = state.astype(BF16)
        a_tot = a_cs[i * SAMPLE_PAD + L - 1:i * SAMPLE_PAD + L, :]
        in_seq = (row_r >= i * SAMPLE_PAD) & (row_r < (i + 1) * SAMPLE_PAD)
        yo = []
        for g in range(N_GROUPS_M):
            gs = slice(g * GROUP_WIDTH, (g + 1) * GROUP_WIDTH)
            yo.append(_mm_nt(cm[rows, g * D_STATE:(g + 1) * D_STATE], state_b[gs]))
            bsel = jnp.where(in_seq, bm[:, g * D_STATE:(g + 1) * D_STATE], 0.0)
            upd = _mm(xd_t[gs, :], bsel)
            for hh in range(heads_per_group):
                h = g * heads_per_group + hh
                hs = slice(h * HEAD_DIM_M, (h + 1) * HEAD_DIM_M)
                decay = jnp.exp(jnp.broadcast_to(a_tot[:, h:h + 1], (HEAD_DIM_M, D_STATE)))
                nst_ref[i, hs, :] = state[hs] * decay + upd[hh * HEAD_DIM_M:(hh + 1) * HEAD_DIM_M]
        yoff.append(jnp.concatenate(yo, axis=1))
    yoff = jnp.concatenate(yoff, axis=0)

    y = (y + yoff * jnp.exp(acs_e)) * _silu(_spread_rows(z_ref[...]))
    outs = []
    for g in range(N_GROUPS_M):
        yg = y[:, g * GROUP_WIDTH:(g + 1) * GROUP_WIDTH]
        outs.append(yg * lax.rsqrt(jnp.mean(yg * yg, axis=-1, keepdims=True) + EPS))
    mout_ref[...] = _gather_rows(jnp.concatenate(outs, axis=1) * ssmn_ref[...])


def _sample_mixer(sinks, u, dtraw, cprev, kc, vc, st, convw, convb, dtb, alog, dskip, ssmn):
    nseq = kc.shape[0]
    assert nseq % SAMPLE_GROUP == 0
    R = SAMPLE_GROUP * SAMPLE_PAD
    T4 = SAMPLE_GROUP * 4
    rows = nseq * SAMPLE_PAD
    consts = (convw, convb, dtb, alog, dskip, ssmn)
    row_spec = lambda w: pl.BlockSpec((R, w), lambda i: (i, 0))
    tok_spec = lambda w: pl.BlockSpec((T4, w), lambda i: (i, 0))
    seq_spec = lambda a, b: pl.BlockSpec((SAMPLE_GROUP, a, b), lambda i: (i, 0, 0))
    in_specs = ([pl.BlockSpec(memory_space=pltpu.SMEM),
                 tok_spec(N_IN_PIECES * PIECE), tok_spec(LANES), row_spec(CONV_DIM),
                 seq_spec(WINDOW, KV_WIDTH), seq_spec(WINDOW, KV_WIDTH), seq_spec(D_INNER, D_STATE)]
                + [_const_spec(a.shape) for a in consts])
    out_shape = (jax.ShapeDtypeStruct((nseq * 4, ATTN_WIDTH), F32), jax.ShapeDtypeStruct((nseq * 4, D_INNER), F32),
                 jax.ShapeDtypeStruct((nseq, WINDOW, KV_WIDTH), F32),
                 jax.ShapeDtypeStruct((nseq, WINDOW, KV_WIDTH), F32),
                 jax.ShapeDtypeStruct((rows, CONV_DIM), F32),
                 jax.ShapeDtypeStruct((nseq, D_INNER, D_STATE), F32))
    out_specs = (tok_spec(ATTN_WIDTH), tok_spec(D_INNER), seq_spec(WINDOW, KV_WIDTH), seq_spec(WINDOW, KV_WIDTH),
                 row_spec(CONV_DIM), seq_spec(D_INNER, D_STATE))
    return pl.pallas_call(
        _sample_mixer_kernel, grid=(nseq // SAMPLE_GROUP,), in_specs=in_specs, out_specs=out_specs,
        out_shape=out_shape, name="sample_mixer",
        compiler_params=pltpu.CompilerParams(dimension_semantics=("arbitrary",),
                                             vmem_limit_bytes=VMEM_LIMIT),
    )(sinks, u, dtraw, cprev, kc, vc, st, *consts)


N_OUT_PIECES = D_MODEL // PIECE
N_FF_BLOCKS = D_FF // FF_BLOCK


def _sample_post_kernel(x_ref, a_ref, m_ref, hnorm_ref, wga_ref, wgb_ref, woa_ref, wob_ref, wo_ref,
                        norm2_ref, wup_ref, wdown_ref, fnorm_ref, y_ref, merged_s, x1_s, hm_s, acc_s):
    s = pl.program_id(0)

    @pl.when(s < N_OUT_PIECES)
    def _():
        hb = hnorm_ref[...]
        piece = (_sigmoid(_mm(hb, wga_ref[0])) * _mm(a_ref[...], woa_ref[0])
                 + _sigmoid(_mm(hb, wgb_ref[0])) * _mm(m_ref[...], wob_ref[0]))
        for p in range(N_OUT_PIECES):
            @pl.when(s == p)
            def _():
                merged_s[:, _piece_cols(p)] = piece.astype(BF16)

    @pl.when((s >= N_OUT_PIECES) & (s < 2 * N_OUT_PIECES))
    def _():
        piece = _mm(merged_s[...], wo_ref[0])
        for p in range(N_OUT_PIECES):
            @pl.when(s == N_OUT_PIECES + p)
            def _():
                x1_s[:, _piece_cols(p)] = x_ref[:, _piece_cols(p)] + piece

    @pl.when(s == 2 * N_OUT_PIECES)
    def _():
        x1 = x1_s[...]
        hm_s[...] = _rms(x1, norm2_ref[...]).astype(BF16)
        acc_s[...] = x1

    @pl.when(s >= 2 * N_OUT_PIECES)
    def _():
        h = jnp.square(jnp.maximum(_mm(hm_s[...], wup_ref[...]), 0.0))
        acc_s[...] += _mm(h, wdown_ref[...])

    @pl.when(s == 2 * N_OUT_PIECES + N_FF_BLOCKS - 1)
    def _():
        y_ref[...] = _rms(acc_s[...], fnorm_ref[...])


def _sample_post(x, a_out, m_out, hnorm, wgate, woa, wob, wo, norm2, wup, wdown, fnorm):
    rows = x.shape[0]
    first = lambda s: jnp.minimum(s, N_OUT_PIECES - 1)
    second = lambda s: jnp.clip(s - N_OUT_PIECES, 0, N_OUT_PIECES - 1)
    ff = lambda s: jnp.clip(s - 2 * N_OUT_PIECES, 0, N_FF_BLOCKS - 1)
    piece_spec = lambda w, index: pl.BlockSpec((1,) + w.shape[1:], lambda s: (index(s), 0, 0))
    in_specs = [_const_spec(x.shape), _const_spec(a_out.shape), _const_spec(m_out.shape), _const_spec(hnorm.shape),
                piece_spec(wgate, first), piece_spec(wgate, lambda s: N_OUT_PIECES + first(s)),
                piece_spec(woa, first), piece_spec(wob, first), piece_spec(wo, second),
                _const_spec(norm2.shape),
                pl.BlockSpec((D_MODEL, FF_BLOCK), lambda s: (0, ff(s))),
                pl.BlockSpec((FF_BLOCK, D_MODEL), lambda s: (ff(s), 0)),
                _const_spec(fnorm.shape)]
    return pl.pallas_call(
        _sample_post_kernel, grid=(2 * N_OUT_PIECES + N_FF_BLOCKS,), in_specs=in_specs,
        out_specs=pl.BlockSpec((rows, D_MODEL), lambda s: (0, 0)),
        out_shape=jax.ShapeDtypeStruct((rows, D_MODEL), F32),
        scratch_shapes=[pltpu.VMEM((rows, D_MODEL), BF16), pltpu.VMEM((rows, D_MODEL), F32),
                        pltpu.VMEM((rows, D_MODEL), BF16), pltpu.VMEM((rows, D_MODEL), F32)],
        name="sample_post",
        compiler_params=pltpu.CompilerParams(dimension_semantics=("arbitrary",),
                                             vmem_limit_bytes=VMEM_LIMIT),
    )(x, a_out, m_out, hnorm, wgate, wgate, woa, wob, wo, norm2, wup, wdown, fnorm)


def _layer_params(norm1, w_in, sinks, conv_w, conv_b, dt_bias, a_log, d_skip, ssm_norm, w_oa, w_ob, w_o,
                  norm2, w_up, w_down):
    dt0 = QKV_WIDTH + D_INNER + CONV_DIM
    assert dt0 == N_IN_PIECES * PIECE
    pad_lanes = lambda a: jnp.pad(a, ((0, 0), (0, LANES - a.shape[1])))
    w_in_t = w_in.T
    mixer = MixerWeights(
        norm1=norm1[None, :], win=_prep_pieces_t(w_in_t, N_IN_PIECES),
        wgate=_prep_pieces_t_shifted(w_in_t, dt0 + N_HEADS_M, 2 * D_MODEL // PIECE),
        wdt=_prep_dt(w_in_t, dt0),
        convw=conv_w, convb=conv_b[None, :], dtb=pad_lanes(dt_bias[None, :]), alog=pad_lanes(a_log[None, :]),
        dskip=jnp.repeat(d_skip, HEAD_DIM_M)[None, :], ssmn=ssm_norm[None, :],
        woa=_prep_pieces(w_oa), wob=_prep_pieces(w_ob), wo=_prep_pieces(w_o))
    wide = (D_MODEL, D_MODEL)
    return dict(sinks=sinks.astype(F32), mixer=mixer,
                norm2=norm2[None, :], wup=_cast_bf16(w_up, wide), wdown=_cast_bf16(w_down, wide))


def kernel(x_prompt, x_sample, cache_swa_k, cache_swa_v, state_conv, state_ssm, norm1, w_in, sinks, conv_w,
           conv_b, dt_bias, a_log, d_skip, ssm_norm, w_oa, w_ob, w_o, norm2, w_up, w_down, final_norm):
    depth = w_in.shape[0]
    assert depth == 1
    nb, seq, _ = x_prompt.shape
    ns, ls, _ = x_sample.shape
    assert ls == 4
    p = _layer_params(norm1[0], w_in[0], sinks[0], conv_w[0], conv_b[0], dt_bias[0], a_log[0], d_skip[0],
                      ssm_norm[0], w_oa[0], w_ob[0], w_o[0], norm2[0], w_up[0], w_down[0])
    fnorm = final_norm[None, :]
    mw = p["mixer"]
    mixer_consts = (mw.convw, mw.convb, mw.dtb, mw.alog, mw.dskip, mw.ssmn)

    x1p, pk, pv, pc, pst = _prompt_mixer(x_prompt, p["sinks"], mw)
    y_prompt = _mlp(x1p, p["norm2"], p["wup"], p["wdown"], fnorm)
    y_prompt = y_prompt.reshape(nb, seq, D_MODEL)

    xs_rows = x_sample.reshape(ns * ls, D_MODEL)
    dim_major = lambda a: jnp.swapaxes(a, 1, 2)
    cprev = jnp.pad(state_conv[0], ((0, 0), (SAMPLE_PAD - (CONV_W - 1), 0), (0, 0)))
    cprev = cprev.reshape(ns * SAMPLE_PAD, CONV_DIM)
    u, dtraw, hnorm = _in_proj(xs_rows, mw.norm1, mw.win, mw.wdt)
    a_out, m_out, sk, sv, sc, sst = _sample_mixer(
        p["sinks"], u, dtraw, cprev,
        dim_major(cache_swa_k[0].reshape(ns, WINDOW, KV_WIDTH)), dim_major(cache_swa_v[0].reshape(ns, WINDOW, KV_WIDTH)),
        state_ssm[0].reshape(ns, D_INNER, D_STATE), *mixer_consts)
    sk, sv = dim_major(sk), dim_major(sv)
    y_sample = _sample_post(xs_rows, a_out, m_out, hnorm, mw.wgate, mw.woa, mw.wob, mw.wo,
                            p["norm2"], p["wup"], p["wdown"], fnorm).reshape(ns, ls, D_MODEL)
    sc = sc.reshape(ns, SAMPLE_PAD, CONV_DIM)[:, :CONV_W - 1]

    kv_shape = (1, -1, WINDOW, N_KV_A, HEAD_DIM_A)
    ssm_shape = (1, -1, N_HEADS_M, HEAD_DIM_M, D_STATE)
    return (y_prompt, y_sample,
            pk.reshape(kv_shape), pv.reshape(kv_shape), pc[None], pst.reshape(ssm_shape),
            sk.reshape(kv_shape), sv.reshape(kv_shape), sc[None], sst.reshape(ssm_shape))
```

```python
import collections
import functools

import jax
import jax.numpy as jnp
from jax import lax
from jax.experimental import pallas as pl
from jax.experimental.pallas import tpu as pltpu

F32 = jnp.float32
BF16 = jnp.bfloat16

D_MODEL = 1024
N_HEADS_A = 8
N_KV_A = 2
Q_PER_KV = N_HEADS_A // N_KV_A
HEAD_DIM_A = 64
WINDOW = 128
ATTN_WIDTH = N_HEADS_A * HEAD_DIM_A
KV_WIDTH = N_KV_A * HEAD_DIM_A
D_INNER = 1024
HEAD_DIM_M = 64
N_HEADS_M = D_INNER // HEAD_DIM_M
N_GROUPS_M = 2
GROUP_WIDTH = D_INNER // N_GROUPS_M
D_STATE = 128
CONV_W = 4
CONV_DIM = D_INNER + 2 * N_GROUPS_M * D_STATE
CHUNK = 128
D_FF = 4 * D_MODEL
EPS = 1e-6

LANES = 128
SUBLANES = 8
QKV_WIDTH = ATTN_WIDTH + 2 * KV_WIDTH
NEG_BIG = -1e30
VMEM_LIMIT = 56 * 1024 * 1024
SAMPLE_PAD = SUBLANES
SAMPLE_GROUP = 16


def _mm(a, b):
    return jnp.dot(a.astype(BF16), b.astype(BF16), preferred_element_type=F32)


def _mm_nt(a, b):
    return lax.dot_general(a.astype(BF16), b.astype(BF16), (((1,), (1,)), ((), ())),
                           preferred_element_type=F32)


def _rms(x, w):
    return x * lax.rsqrt(jnp.mean(x * x, axis=-1, keepdims=True) + EPS) * w


def _sigmoid(x):
    return 0.5 + 0.5 * jnp.tanh(0.5 * x)


def _silu(x):
    h = 0.5 * x
    return h + h * jnp.tanh(h)


def _softplus(x):
    return jnp.maximum(x, 0.0) + jnp.log(1.0 + jnp.exp(-jnp.abs(x)))


def _lane_lo(shape):
    return (lax.broadcasted_iota(jnp.int32, shape, len(shape) - 1) % LANES) < HEAD_DIM_A


def _dup_half(x, lo, first):
    xr = pltpu.roll(x, HEAD_DIM_A, axis=1)
    return jnp.where(lo, x, xr) if first else jnp.where(lo, xr, x)


def _stack_heads(q, kv, lo):
    qa = q[:, kv * 2 * LANES: kv * 2 * LANES + LANES]
    qb = q[:, kv * 2 * LANES + LANES: (kv + 1) * 2 * LANES]
    zero = jnp.zeros_like(qa)
    return jnp.concatenate([jnp.where(lo, qa, zero), jnp.where(lo, zero, qa),
                            jnp.where(lo, qb, zero), jnp.where(lo, zero, qb)], axis=0)


def _unstack_heads(o, rows, lo):
    return jnp.concatenate([jnp.where(lo, o[0:rows], o[rows:2 * rows]),
                            jnp.where(lo, o[2 * rows:3 * rows], o[3 * rows:4 * rows])], axis=1)


ProjBufs = collections.namedtuple("ProjBufs", "qkv z xbc dt")
MixedBufs = collections.namedtuple("MixedBufs", "attn ssd")
Carry = collections.namedtuple("Carry", "kprev vprev conv_tail ht dt_buf")
MixerWeights = collections.namedtuple(
    "MixerWeights", "norm1 win wgate wdt convw convb dtb alog dskip ssmn woa wob wo")

PIECE = 256
QKV_PIECE0 = 0
Z_PIECE0 = QKV_PIECE0 + QKV_WIDTH // PIECE
XBC_PIECE0 = Z_PIECE0 + D_INNER // PIECE
N_IN_PIECES = XBC_PIECE0 + CONV_DIM // PIECE
PREP_PIECES = 4


def _prep_pieces_kernel(w_ref, o_ref):
    for j in range(PREP_PIECES):
        o_ref[j] = w_ref[:, j * PIECE:(j + 1) * PIECE].astype(BF16)


def _prep_pieces(w, n_pieces=None, k_block=None):
    k_dim, n_dim = w.shape
    n_pieces = n_dim // PIECE if n_pieces is None else n_pieces
    k_block = k_dim if k_block is None else k_block
    cols = PREP_PIECES * PIECE
    return pl.pallas_call(
        _prep_pieces_kernel, grid=(pl.cdiv(n_pieces, PREP_PIECES), k_dim // k_block),
        in_specs=[pl.BlockSpec((k_block, cols), lambda p, k: (k, p))],
        out_specs=pl.BlockSpec((PREP_PIECES, k_block, PIECE), lambda p, k: (p, k, 0)),
        out_shape=jax.ShapeDtypeStruct((n_pieces, k_dim, PIECE), BF16), name="prep_weight_pieces",
        compiler_params=pltpu.CompilerParams(dimension_semantics=("arbitrary", "arbitrary"),
                                             vmem_limit_bytes=VMEM_LIMIT),
    )(w)


def _prep_pieces_t_kernel(wt_ref, o_ref):
    for j in range(PREP_PIECES):
        o_ref[j] = wt_ref[j * PIECE:(j + 1) * PIECE, :].T.astype(BF16)


def _prep_pieces_t(wt, n_pieces=None):
    n_dim, k_dim = wt.shape
    n_pieces = n_dim // PIECE if n_pieces is None else n_pieces
    return pl.pallas_call(
        _prep_pieces_t_kernel, grid=(pl.cdiv(n_pieces, PREP_PIECES),),
        in_specs=[pl.BlockSpec((PREP_PIECES * PIECE, k_dim), lambda p: (p, 0))],
        out_specs=pl.BlockSpec((PREP_PIECES, k_dim, PIECE), lambda p: (p, 0, 0)),
        out_shape=jax.ShapeDtypeStruct((n_pieces, k_dim, PIECE), BF16), name="prep_weight_pieces_t",
        compiler_params=pltpu.CompilerParams(dimension_semantics=("arbitrary",),
                                             vmem_limit_bytes=VMEM_LIMIT),
    )(wt)


def _prep_pieces_t_shifted_kernel(a_ref, b_ref, o_ref, *, shift):
    rows = jnp.concatenate([a_ref[shift:, :], b_ref[:shift, :]], axis=0)
    o_ref[0] = rows.T.astype(BF16)


def _prep_pieces_t_shifted(wt, row0, n_pieces):
    k_dim = wt.shape[1]
    block0, shift = divmod(row0, PIECE)
    assert shift % SUBLANES == 0 and shift > 0
    return pl.pallas_call(
        functools.partial(_prep_pieces_t_shifted_kernel, shift=shift), grid=(n_pieces,),
        in_specs=[pl.BlockSpec((PIECE, k_dim), lambda p: (block0 + p, 0)),
                  pl.BlockSpec((PIECE, k_dim), lambda p: (block0 + p + 1, 0))],
        out_specs=pl.BlockSpec((1, k_dim, PIECE), lambda p: (p, 0, 0)),
        out_shape=jax.ShapeDtypeStruct((n_pieces, k_dim, PIECE), BF16), name="prep_weight_pieces_t_shifted",
        compiler_params=pltpu.CompilerParams(dimension_semantics=("arbitrary",),
                                             vmem_limit_bytes=VMEM_LIMIT),
    )(wt, wt)


def _prep_dt_kernel(wt_ref, o_ref):
    rows = wt_ref[...]
    keep = lax.broadcasted_iota(jnp.int32, rows.shape, 0) < N_HEADS_M
    o_ref[...] = jnp.where(keep, rows, 0.0).T.astype(BF16)


def _prep_dt(wt, row0):
    k_dim = wt.shape[1]
    assert row0 % LANES == 0
    return pl.pallas_call(
        _prep_dt_kernel, grid=(1,),
        in_specs=[pl.BlockSpec((LANES, k_dim), lambda i: (row0 // LANES, 0))],
        out_specs=pl.BlockSpec((k_dim, LANES), lambda i: (0, 0)),
        out_shape=jax.ShapeDtypeStruct((k_dim, LANES), BF16), name="prep_weight_dt",
        compiler_params=pltpu.CompilerParams(dimension_semantics=("arbitrary",), vmem_limit_bytes=VMEM_LIMIT),
    )(wt)


def _cast_kernel(w_ref, o_ref):
    o_ref[...] = w_ref[...].astype(BF16)


def _cast_bf16(w, block):
    return pl.pallas_call(
        _cast_kernel, grid=(w.shape[0] // block[0], w.shape[1] // block[1]),
        in_specs=[pl.BlockSpec(block, lambda i, j: (i, j))], out_specs=pl.BlockSpec(block, lambda i, j: (i, j)),
        out_shape=jax.ShapeDtypeStruct(w.shape, BF16), name="cast_weight",
        compiler_params=pltpu.CompilerParams(dimension_semantics=("arbitrary", "arbitrary"),
                                             vmem_limit_bytes=VMEM_LIMIT),
    )(w)


StateOuts = collections.namedtuple("StateOuts", "k v conv ssm")


def _split3_bf16(x):
    hi = x.astype(BF16)
    r1 = x - hi.astype(F32)
    mid = r1.astype(BF16)
    return hi, mid, (r1 - mid.astype(F32)).astype(BF16)


def _piece_cols(p):
    return slice(p * PIECE, (p + 1) * PIECE)


def _in_proj_pieces(hb, w, dsts):
    for dst, first in zip(dsts, (QKV_PIECE0, Z_PIECE0, XBC_PIECE0)):
        for p in range(dst.shape[1] // PIECE):
            dst[:, _piece_cols(p)] = _mm(hb, w.win[first + p])
            yield


def _out_proj_pieces(x_ref, rows, hb, attn_ref, ssd_ref, w, out_ref):
    merged = []
    for half, (src_ref, w_ref) in enumerate(((attn_ref, w.woa), (ssd_ref, w.wob))):
        parts = []
        for p in range(D_MODEL // PIECE):
            gate = _sigmoid(_mm(hb, w.wgate[half * (D_MODEL // PIECE) + p]))
            yield
            parts.append(gate * _mm(src_ref[...], w_ref[p]))
            yield
        merged.append(jnp.concatenate(parts, axis=1))
    merged = (merged[0] + merged[1]).astype(BF16)
    for p in range(D_MODEL // PIECE):
        out_ref[rows, _piece_cols(p)] = x_ref[rows, _piece_cols(p)] + _mm(merged, w.wo[p])
        yield


def _stage_in_proj(x_ref, rows, w, proj, hnorm_ref):
    hb = _rms(x_ref[rows, :], w.norm1[...]).astype(BF16)
    hnorm_ref[...] = hb
    yield
    yield from _in_proj_pieces(hb, w, (proj.qkv, proj.z, proj.xbc))
    proj.dt[...] = _mm(hb, w.wdt[...])
    yield


def _stage_out_proj(x_ref, rows, mixed, w, x1_ref, hnorm_ref):
    hb = hnorm_ref[...]
    yield
    yield from _out_proj_pieces(x_ref, rows, hb, mixed.attn, mixed.ssd, w, x1_ref)


def _alternate(*gens):
    gens = list(gens)
    while gens:
        for g in list(gens):
            try:
                next(g)
                yield
            except StopIteration:
                gens.remove(g)


def _trace_interleaved(primary, filler):
    for _ in primary:
        next(filler, None)
    for _ in filler:
        pass


def _stage_mix(proj, mixed, rows, carry, w, sinks_ref, first):
    T = CHUNK
    reset = (lambda a: a) if first is None else (lambda a: jnp.where(first, 0.0, a))
    lo = _lane_lo((T, LANES))
    lo2 = _lane_lo((2 * T, LANES))
    rm = lax.broadcasted_iota(jnp.int32, (T, 1), 0) % SUBLANES
    rr = lax.broadcasted_iota(jnp.int32, (T, T), 0)
    cc = lax.broadcasted_iota(jnp.int32, (T, T), 1)
    tril = cc <= rr
    pairs_per_group = N_HEADS_M // 2 // N_GROUPS_M
    n_pairs = N_HEADS_M // 2
    gcols = lambda g: slice(g * GROUP_WIDTH, (g + 1) * GROUP_WIDTH)

    def conv_silu(cols):
        xr = proj.xbc[rows, cols]
        prev_tile = jnp.concatenate([reset(carry.conv_tail[:, cols]), xr[:T - SUBLANES]], axis=0)
        carry.conv_tail[:, cols] = xr[T - SUBLANES:]
        yc = xr * w.convw[CONV_W - 1:CONV_W, cols]
        for shift in range(1, CONV_W):
            shifted = _tile_roll(jnp.where(rm >= SUBLANES - shift, prev_tile, xr), shift)
            yc = yc + shifted * w.convw[CONV_W - 1 - shift:CONV_W - shift, cols]
        return _silu(yc + w.convb[:, cols])

    def softmax(s, kv):
        r = lax.broadcasted_iota(jnp.int32, (T, 2 * T), 0)
        col = lax.broadcasted_iota(jnp.int32, (T, 2 * T), 1)
        valid = (col >= r) & (col <= r + WINDOW)
        if first is not None:
            valid = valid & (col >= jnp.where(first, T, 0))
        es, inv = [], []
        for g in range(Q_PER_KV):
            sk = sinks_ref[kv * Q_PER_KV + g]
            sg = jnp.where(valid, s[g * T:(g + 1) * T], NEG_BIG)
            m = jnp.maximum(jnp.max(sg, axis=-1, keepdims=True), sk)
            e = jnp.exp(sg - m)
            es.append(e.astype(BF16))
            inv.append(1.0 / (jnp.sum(e, axis=-1, keepdims=True) + jnp.exp(sk - m)))
        return jnp.concatenate(es, axis=0), jnp.concatenate(inv, axis=0)

    qkv = proj.qkv[rows, :]
    q = qkv[:, :ATTN_WIDTH] * (HEAD_DIM_A ** -0.5)
    k = qkv[:, ATTN_WIDTH:ATTN_WIDTH + KV_WIDTH]
    v = qkv[:, ATTN_WIDTH + KV_WIDTH:]
    kk = jnp.concatenate([carry.kprev[...], k], axis=0)
    vv = jnp.concatenate([carry.vprev[...], v], axis=0)
    carry.kprev[...] = k
    carry.vprev[...] = v
    carry.dt_buf[...] = _softplus(proj.dt[rows, :] + w.dtb[...])
    da_split = jnp.concatenate(_split3_bf16(carry.dt_buf[...] * -jnp.exp(w.alog[...])), axis=1)
    vds = [_dup_half(vv, lo2, kv == 0).astype(BF16) for kv in range(N_KV_A)]
    scores = [_mm_nt(_stack_heads(q, kv, lo), _dup_half(kk, lo2, kv == 0).astype(BF16))
              for kv in range(N_KV_A)]
    yield

    bc = conv_silu(slice(D_INNER, CONV_DIM))
    bgs = [bc[:, g * D_STATE:(g + 1) * D_STATE].astype(BF16) for g in range(N_GROUPS_M)]
    cgs = [bc[:, (N_GROUPS_M + g) * D_STATE:(N_GROUPS_M + g + 1) * D_STATE].astype(BF16)
           for g in range(N_GROUPS_M)]
    bgts = [bc[:, g * D_STATE:(g + 1) * D_STATE].T.astype(BF16) for g in range(N_GROUPS_M)]
    yield

    es0, inv0 = softmax(scores[0], 0)
    yield

    es1, inv1 = softmax(scores[1], 1)
    cbs = [_mm_nt(cgs[g], bgs[g]) for g in range(N_GROUPS_M)]
    h_prevs = [reset(carry.ht[:, gcols(g)]) for g in range(N_GROUPS_M)]
    y_offs = [_mm(cgs[g], h_prevs[g]) for g in range(N_GROUPS_M)]
    o0 = _mm(es0, vds[0])
    cs = jnp.dot(jnp.where(tril, 1.0, 0.0).astype(BF16), da_split, preferred_element_type=F32)
    yield

    a_cs = cs[:, :LANES] + cs[:, LANES:2 * LANES] + cs[:, 2 * LANES:]
    a_cs_t = a_cs.T
    xs_groups = [conv_silu(gcols(0))]
    o1 = _mm(es1, vds[1])
    yield

    xs_groups.append(conv_silu(gcols(1)))
    mixed.attn[rows, 0:2 * LANES] = _unstack_heads(o0 * inv0, T, lo).astype(BF16)
    yield

    def prep(j):
        g = j // pairs_per_group
        dt = carry.dt_buf[...]
        ws, colbs, dtbs = [], [], []
        for h in (2 * j, 2 * j + 1):
            colb = jnp.broadcast_to(a_cs[:, h:h + 1], (T, T))
            rowb = jnp.broadcast_to(a_cs_t[h:h + 1, :], (T, T))
            seg = jnp.where(tril, jnp.exp(colb - rowb), 0.0)
            ws.append((cbs[g] * seg).astype(BF16))
            colbs.append(colb)
            dtbs.append(jnp.broadcast_to(dt[:, h:h + 1], (T, LANES)))
        dt_e = jnp.where(lo, dtbs[0], dtbs[1])
        acs_e = jnp.where(lo, colbs[0], colbs[1])
        xs_j = xs_groups[g][:, (j % pairs_per_group) * LANES:(j % pairs_per_group + 1) * LANES]
        xdt = xs_j * dt_e
        zero = jnp.zeros_like(xdt)
        rhs = jnp.concatenate([jnp.where(lo, xdt, zero), jnp.where(lo, zero, xdt)], axis=0).astype(BF16)
        alast = acs_e[T - 1:T, :]
        return dict(lhs=jnp.concatenate(ws, axis=1), rhs=rhs, xs=xs_j, e_acs=jnp.exp(acs_e),
                    xd=(xdt * jnp.exp(alast - acs_e)).astype(BF16), decay=jnp.exp(alast))

    def finish(j, p, y_diag):
        g, jj = divmod(j, pairs_per_group)
        sl = slice(j * LANES, (j + 1) * LANES)
        y = y_diag + y_offs[g][:, jj * LANES:(jj + 1) * LANES] * p["e_acs"] + p["xs"] * w.dskip[:, sl]
        return y * _silu(proj.z[rows, sl])

    mixed.attn[rows, 2 * LANES:4 * LANES] = _unstack_heads(o1 * inv1, T, lo).astype(BF16)
    preps = {0: prep(0)}
    yield

    ys, y_diags = [], {}
    for i in range(n_pairs + 2):
        if i + 1 < n_pairs:
            preps[i + 1] = prep(i + 1)
        j = i - 1
        if 0 <= j < n_pairs:
            y_diags[j] = _mm(preps[j]["lhs"], preps[j]["rhs"])
            if j % pairs_per_group == pairs_per_group - 1:
                g = j // pairs_per_group
                grp = [preps[n] for n in range(g * pairs_per_group, (g + 1) * pairs_per_group)]
                carry.ht[:, gcols(g)] = (h_prevs[g] * jnp.concatenate([p["decay"] for p in grp], axis=1)
                                         + _mm(bgts[g], jnp.concatenate([p["xd"] for p in grp], axis=1)))
        if 0 <= i - 2 < n_pairs:
            ys.append(finish(i - 2, preps[i - 2], y_diags.pop(i - 2)))
        yield

    m_slabs = []
    for g in range(N_GROUPS_M):
        grp = ys[g * pairs_per_group:(g + 1) * pairs_per_group]
        ssq = grp[0] * grp[0]
        for y in grp[1:]:
            ssq = ssq + y * y
        scale = lax.rsqrt(jnp.sum(ssq, axis=-1, keepdims=True) * (1.0 / GROUP_WIDTH) + EPS)
        m_slabs.extend([y * scale for y in grp])
    mixed.ssd[rows, :] = (jnp.concatenate(m_slabs, axis=1) * w.ssmn[...]).astype(BF16)


def _chain(*gens):
    for g in gens:
        yield from g


def _prompt_mixer_kernel(sinks_ref, x_in_ref, x_res_ref, *refs, chunks_per_seq):
    refs = list(refs)
    take = lambda n: [refs.pop(0) for _ in range(n)]
    w = MixerWeights(*take(len(MixerWeights._fields)))
    x1_ref, = take(1)
    outs = StateOuts(*take(4))
    proj = (ProjBufs(*take(4)), ProjBufs(*take(4)))
    mixed = (MixedBufs(*take(2)), MixedBufs(*take(2)))
    hnorm = take(2)
    carry = Carry(*take(len(Carry._fields)))
    assert not refs
    T = CHUNK
    s = pl.program_id(0)
    every = slice(None)

    @pl.when(s == 0)
    def _():
        for ref in list(proj[1] + mixed[0] + mixed[1] + carry) + hnorm:
            ref[...] = jnp.zeros_like(ref)

    first = (s - 1) % (chunks_per_seq // 2) == 0

    def step(cur):
        prv = 1 - cur
        dense = _alternate(_stage_out_proj(x_res_ref, every, mixed[cur], w, x1_ref, hnorm[cur]),
                           _stage_in_proj(x_in_ref, every, w, proj[cur], hnorm[cur]))
        mix = _chain(_stage_mix(proj[prv], mixed[prv], slice(0, T), carry, w, sinks_ref, first),
                     _stage_mix(proj[prv], mixed[prv], slice(T, 2 * T), carry, w, sinks_ref, None))
        _trace_interleaved(mix, dense)

    for parity in (0, 1):
        pl.when(s % 2 == parity)(functools.partial(step, parity))

    @pl.when((s >= 1) & (s % (chunks_per_seq // 2) == 0))
    def _():
        outs.k[0] = carry.kprev[...]
        outs.v[0] = carry.vprev[...]
        outs.conv[0] = carry.conv_tail[SUBLANES - (CONV_W - 1):, :]
        outs.ssm[0] = carry.ht[...].T


def _const_spec(shape):
    return pl.BlockSpec(shape, lambda *_: (0,) * len(shape), pipeline_mode=pl.Buffered(1))


def _prompt_mixer(x, sinks, weights):
    nb, seq, _ = x.shape
    assert seq % (2 * CHUNK) == 0
    chunks_per_seq = seq // CHUNK
    n_pairs = nb * chunks_per_seq // 2
    pair = 2 * CHUNK
    consts = tuple(weights)
    seq_of_mix = lambda s: jnp.clip((2 * s - 1) // chunks_per_seq, 0, nb - 1)
    in_specs = ([pl.BlockSpec(memory_space=pltpu.SMEM),
                 pl.BlockSpec((pair, D_MODEL), lambda s: (jnp.minimum(s, n_pairs - 1), 0)),
                 pl.BlockSpec((pair, D_MODEL), lambda s: (jnp.maximum(s - 2, 0), 0))]
                + [_const_spec(a.shape) for a in consts])
    out_shape = (jax.ShapeDtypeStruct((nb * seq, D_MODEL), F32),
                 jax.ShapeDtypeStruct((nb, WINDOW, KV_WIDTH), F32),
                 jax.ShapeDtypeStruct((nb, WINDOW, KV_WIDTH), F32),
                 jax.ShapeDtypeStruct((nb, CONV_W - 1, CONV_DIM), F32),
                 jax.ShapeDtypeStruct((nb, D_INNER, D_STATE), F32))
    out_specs = (pl.BlockSpec((pair, D_MODEL), lambda s: (jnp.maximum(s - 2, 0), 0)),
                 pl.BlockSpec((1, WINDOW, KV_WIDTH), lambda s: (seq_of_mix(s), 0, 0)),
                 pl.BlockSpec((1, WINDOW, KV_WIDTH), lambda s: (seq_of_mix(s), 0, 0)),
                 pl.BlockSpec((1, CONV_W - 1, CONV_DIM), lambda s: (seq_of_mix(s), 0, 0)),
                 pl.BlockSpec((1, D_INNER, D_STATE), lambda s: (seq_of_mix(s), 0, 0)))
    proj_bufs = [pltpu.VMEM((pair, QKV_WIDTH), F32), pltpu.VMEM((pair, D_INNER), F32),
                 pltpu.VMEM((pair, CONV_DIM), F32), pltpu.VMEM((pair, LANES), F32)]
    mixed_bufs = [pltpu.VMEM((pair, ATTN_WIDTH), BF16), pltpu.VMEM((pair, D_INNER), BF16)]
    hnorm_bufs = [pltpu.VMEM((pair, D_MODEL), BF16)] * 2
    carry = [pltpu.VMEM((CHUNK, KV_WIDTH), F32), pltpu.VMEM((CHUNK, KV_WIDTH), F32),
             pltpu.VMEM((SUBLANES, CONV_DIM), F32), pltpu.VMEM((D_STATE, D_INNER), F32),
             pltpu.VMEM((CHUNK, LANES), F32)]
    x_rows = x.reshape(nb * seq, D_MODEL)
    return pl.pallas_call(
        functools.partial(_prompt_mixer_kernel, chunks_per_seq=chunks_per_seq),
        grid=(n_pairs + 2,), in_specs=in_specs, out_specs=out_specs,
        out_shape=out_shape, scratch_shapes=proj_bufs * 2 + mixed_bufs * 2 + hnorm_bufs + carry,
        name="prompt_mixer",
        compiler_params=pltpu.CompilerParams(dimension_semantics=("arbitrary",),
                                             vmem_limit_bytes=VMEM_LIMIT),
    )(sinks, x_rows, x_rows, *consts)


MLP_TILE = 1024
FF_BLOCK = 1024


def _mlp_kernel(x_ref, norm2_ref, wup_ref, wdown_ref, fnorm_ref, y_ref):
    x = x_ref[...]
    hm = _rms(x, norm2_ref[...]).astype(BF16)
    acc = x
    for j in range(D_FF // FF_BLOCK):
        h = _mm(hm, wup_ref[:, j * FF_BLOCK:(j + 1) * FF_BLOCK])
        h = jnp.square(jnp.maximum(h, 0.0))
        acc = acc + _mm(h, wdown_ref[j * FF_BLOCK:(j + 1) * FF_BLOCK, :])
    y_ref[...] = _rms(acc, fnorm_ref[...])


def _mlp(x, norm2, wup, wdown, fnorm):
    rows = x.shape[0]
    tile = min(MLP_TILE, rows)
    assert rows % tile == 0 and tile % SUBLANES == 0
    consts = (norm2, wup, wdown, fnorm)
    return pl.pallas_call(
        _mlp_kernel, grid=(rows // tile,),
        in_specs=[pl.BlockSpec((tile, D_MODEL), lambda i: (i, 0))] + [_const_spec(a.shape) for a in consts],
        out_specs=pl.BlockSpec((tile, D_MODEL), lambda i: (i, 0)),
        out_shape=jax.ShapeDtypeStruct((rows, D_MODEL), F32), name="mlp",
        compiler_params=pltpu.CompilerParams(dimension_semantics=("arbitrary",),
                                             vmem_limit_bytes=VMEM_LIMIT),
    )(x, *consts)


IN_PROJ_ROWS = 256


def _in_proj_kernel(x_ref, norm1_ref, win_ref, wdt_ref, u_ref, dt_ref, hnorm_ref):
    hb = _rms(x_ref[...], norm1_ref[...]).astype(BF16)
    hnorm_ref[...] = hb
    for p in range(win_ref.shape[0]):
        u_ref[:, _piece_cols(p)] = _mm(hb, win_ref[p])
    dt_ref[...] = _mm(hb, wdt_ref[...])


def _in_proj(x, norm1, win, wdt):
    rows = x.shape[0]
    assert rows % IN_PROJ_ROWS == 0
    outs = ((win.shape[0] * PIECE, F32), (LANES, F32), (D_MODEL, BF16))
    row_spec = lambda width: pl.BlockSpec((IN_PROJ_ROWS, width), lambda i: (i, 0))
    return pl.pallas_call(
        _in_proj_kernel, grid=(rows // IN_PROJ_ROWS,),
        in_specs=[row_spec(D_MODEL), _const_spec(norm1.shape), _const_spec(win.shape), _const_spec(wdt.shape)],
        out_specs=tuple(row_spec(width) for width, _ in outs),
        out_shape=tuple(jax.ShapeDtypeStruct((rows, width), dtype) for width, dtype in outs),
        name="sample_in_proj",
        compiler_params=pltpu.CompilerParams(dimension_semantics=("arbitrary",),
                                             vmem_limit_bytes=VMEM_LIMIT),
    )(x, norm1, win, wdt)


def _tile_roll(x, shift):
    rows, width = x.shape
    tiles = x.reshape(rows // SUBLANES, SUBLANES, width)
    return pltpu.roll(tiles, shift % SUBLANES, axis=1).reshape(rows, width)


def _spread_rows(x):
    tiles = []
    for t in range(x.shape[0] // SUBLANES):
        two_seqs = x[t * SUBLANES:(t + 1) * SUBLANES]
        tiles += [two_seqs, pltpu.roll(two_seqs, SUBLANES // 2, axis=0)]
    return jnp.concatenate(tiles, axis=0)


def _gather_rows(y):
    low = lax.broadcasted_iota(jnp.int32, (SUBLANES, 1), 0) < SUBLANES // 2
    tiles = []
    for t in range(y.shape[0] // (2 * SUBLANES)):
        a = y[2 * t * SUBLANES:(2 * t + 1) * SUBLANES]
        b = y[(2 * t + 1) * SUBLANES:(2 * t + 2) * SUBLANES]
        tiles.append(jnp.where(low, a, pltpu.roll(b, SUBLANES // 2, axis=0)))
    return jnp.concatenate(tiles, axis=0)


def _expand_heads(cols, expand):
    terms = []
    for c in cols:
        hi = c.astype(BF16)
        r1 = c - hi.astype(F32)
        mid = r1.astype(BF16)
        lo = (r1 - mid.astype(F32)).astype(BF16)
        terms.extend([hi, mid, lo])
    rows = cols[0].shape[0]
    out = jnp.dot(jnp.concatenate(terms, axis=0), expand, preferred_element_type=F32)
    return [out[(3 * i) * rows:(3 * i + 1) * rows] + out[(3 * i + 1) * rows:(3 * i + 2) * rows]
            + out[(3 * i + 2) * rows:(3 * i + 3) * rows] for i in range(len(cols))]


def _sample_mixer_kernel(sinks_ref, u_ref, dtraw_ref, cprev_ref, kc_ref, vc_ref, st_ref,
                         convw_ref, convb_ref, dtb_ref, alog_ref, dskip_ref, ssmn_ref,
                         aout_ref, mout_ref, nk_ref, nv_ref, nconv_ref, nst_ref):
    R = SAMPLE_GROUP * SAMPLE_PAD
    L = 4
    qkv_ref = u_ref.at[:, QKV_PIECE0 * PIECE:Z_PIECE0 * PIECE]
    z_ref = u_ref.at[:, Z_PIECE0 * PIECE:XBC_PIECE0 * PIECE]
    xbc_ref = u_ref.at[:, XBC_PIECE0 * PIECE:N_IN_PIECES * PIECE]
    rm = lax.broadcasted_iota(jnp.int32, (R, 1), 0) % SAMPLE_PAD

    xr = xbc_raw = _spread_rows(xbc_ref[...])
    with_prev = jnp.where(rm >= SAMPLE_PAD - (CONV_W - 1), cprev_ref[...], xr)
    yc = xr * convw_ref[CONV_W - 1:CONV_W, :]
    for kshift in range(1, CONV_W):
        yc = yc + _tile_roll(with_prev, kshift) * convw_ref[CONV_W - 1 - kshift:CONV_W - kshift, :]
    nconv_ref[...] = _tile_roll(xbc_raw, -1)
    xbc = _silu(yc + convb_ref[...])
    xs = xbc[:, :D_INNER]
    bm = xbc[:, D_INNER:D_INNER + N_GROUPS_M * D_STATE]
    cm = xbc[:, D_INNER + N_GROUPS_M * D_STATE:]

    dt = _softplus(_spread_rows(dtraw_ref[...]) + dtb_ref[...])
    d_a = dt * (-jnp.exp(alog_ref[...]))
    a_cs = d_a
    suf = jnp.zeros_like(d_a)
    for kshift in range(1, L):
        a_cs = a_cs + jnp.where(rm >= kshift, _tile_roll(d_a, kshift), 0.0)
        suf = suf + jnp.where(rm <= L - 1 - kshift, _tile_roll(d_a, -kshift), 0.0)

    expand = (lax.broadcasted_iota(jnp.int32, (LANES, D_INNER), 1) // HEAD_DIM_M
              == lax.broadcasted_iota(jnp.int32, (LANES, D_INNER), 0)).astype(BF16)
    dt_e, acs_e, suf_e = _expand_heads([dt, a_cs, suf], expand)
    xdt = xs * dt_e
    real = rm < L
    xd_t = jnp.where(real, xdt * jnp.exp(suf_e), 0.0).T.astype(BF16)

    y = xs * dskip_ref[...]
    for kshift in range(L):
        bk = bm if kshift == 0 else _tile_roll(bm, kshift)
        cb = cm * bk
        cb_e = jnp.concatenate(
            [jnp.broadcast_to(jnp.sum(cb[:, g * D_STATE:(g + 1) * D_STATE], axis=-1, keepdims=True),
                              (R, GROUP_WIDTH)) for g in range(N_GROUPS_M)], axis=1)
        if kshift == 0:
            y = y + cb_e * xdt
        else:
            seg = jnp.exp(acs_e - _tile_roll(acs_e, kshift))
            y = y + jnp.where(rm >= kshift, cb_e * seg * _tile_roll(xdt, kshift), 0.0)

    qkv = _spread_rows(qkv_ref[...])
    q = qkv[:, :ATTN_WIDTH] * (HEAD_DIM_A ** -0.5)
    q_swapped = jnp.concatenate(
        [pltpu.roll(q[:, s * LANES:(s + 1) * LANES], HEAD_DIM_A, axis=1) for s in range(ATTN_WIDTH // LANES)], axis=1)
    kn = qkv[:, ATTN_WIDTH:ATTN_WIDTH + KV_WIDTH]
    vn = qkv[:, ATTN_WIDTH + KV_WIDTH:]
    lo8 = _lane_lo((SAMPLE_PAD, LANES))
    zero8 = jnp.zeros((SAMPLE_PAD, LANES), F32)
    seq_rows = [slice(i * SAMPLE_PAD, (i + 1) * SAMPLE_PAD) for i in range(SAMPLE_GROUP)]
    HROWS = N_HEADS_A * SAMPLE_PAD

    s_c, s_n = [], []
    for rows in seq_rows:
        pieces = []
        for kv in range(N_KV_A):
            for g in range(Q_PER_KV):
                slab = kv * (Q_PER_KV // 2) + g // 2
                src = q if g % 2 == kv else q_swapped
                piece = src[rows, slab * LANES:(slab + 1) * LANES]
                pieces.append(jnp.where(lo8, piece, zero8) if kv == 0 else jnp.where(lo8, zero8, piece))
        lhs = jnp.concatenate(pieces, axis=0).astype(BF16)
        i = len(s_c)
        s_c.append(_mm(lhs, kc_ref[i]))
        s_n.append(_mm_nt(lhs, kn[rows]))
    s_c = jnp.concatenate(s_c, axis=0)
    s_n = jnp.concatenate(s_n, axis=0)
    n_rows = SAMPLE_GROUP * HROWS
    tok_c = lax.broadcasted_iota(jnp.int32, (n_rows, WINDOW), 0) % SAMPLE_PAD
    s_c = jnp.where(lax.broadcasted_iota(jnp.int32, (n_rows, WINDOW), 1) >= tok_c, s_c, NEG_BIG)
    tok_n = lax.broadcasted_iota(jnp.int32, (n_rows, SAMPLE_PAD), 0) % SAMPLE_PAD
    col_n = lax.broadcasted_iota(jnp.int32, (n_rows, SAMPLE_PAD), 1)
    s_n = jnp.where((col_n <= tok_n) & (col_n < L), s_n, NEG_BIG)
    sk = jnp.concatenate([jnp.full((SAMPLE_PAD, 1), sinks_ref[h], F32) for h in range(N_HEADS_A)] * SAMPLE_GROUP,
                         axis=0)
    m = jnp.maximum(jnp.maximum(jnp.max(s_c, axis=-1, keepdims=True), jnp.max(s_n, axis=-1, keepdims=True)), sk)
    e_c = jnp.exp(s_c - m)
    e_n = jnp.exp(s_n - m)
    inv_den = 1.0 / (jnp.sum(e_c, axis=-1, keepdims=True) + jnp.sum(e_n, axis=-1, keepdims=True) + jnp.exp(sk - m))
    e_c = e_c.astype(BF16)
    e_n = e_n.astype(BF16)
    o = jnp.concatenate([_mm_nt(e_c[i * HROWS:(i + 1) * HROWS], vc_ref[i]) + _mm(e_n[i * HROWS:(i + 1) * HROWS], vn[rows])
                         for i, rows in enumerate(seq_rows)], axis=0) * inv_den
    o_swapped = pltpu.roll(o, HEAD_DIM_A, axis=1)
    a_rows = []
    for i in range(SAMPLE_GROUP):
        blk = lambda arr, kv, g: arr[i * HROWS + (kv * Q_PER_KV + g) * SAMPLE_PAD:
                                     i * HROWS + (kv * Q_PER_KV + g + 1) * SAMPLE_PAD]
        slabs = []
        for kv in range(N_KV_A):
            for j in range(Q_PER_KV // 2):
                first = blk(o if kv == 0 else o_swapped, kv, 2 * j)
                second = blk(o_swapped if kv == 0 else o, kv, 2 * j + 1)
                slabs.append(jnp.where(lo8, first, second))
        a_rows.append(jnp.concatenate(slabs, axis=1))
    aout_ref[...] = _gather_rows(jnp.concatenate(a_rows, axis=0))

    key = lax.broadcasted_iota(jnp.int32, (KV_WIDTH, WINDOW), 1)
    for cache_ref, new, out_ref in ((kc_ref, kn, nk_ref), (vc_ref, vn, nv_ref)):
        new_t = new.T
        for i in range(SAMPLE_GROUP):
            kept = pltpu.roll(cache_ref[i], WINDOW - L, axis=1)
            fresh = pltpu.roll(new_t, (WINDOW - L - i * SAMPLE_PAD) % WINDOW, axis=1)
            out_ref[i] = jnp.where(key >= WINDOW - L, fresh, kept)

    row_r = lax.broadcasted_iota(jnp.int32, (R, D_STATE), 0)
    heads_per_group = N_HEADS_M // N_GROUPS_M
    yoff = []
    for i0 in range(0, SAMPLE_GROUP, 2):
        pair = (i0, i0 + 1)
        states = [st_ref[i] for i in pair]
        yos = [[], []]
        for g in range(N_GROUPS_M):
            gs = slice(g * GROUP_WIDTH, (g + 1) * GROUP_WIDTH)
            bg = bm[:, g * D_STATE:(g + 1) * D_STATE]
            bsel = [jnp.where((row_r >= i * SAMPLE_PAD) & (row_r < (i + 1) * SAMPLE_PAD), bg, 0.0) for i in pair]
            upd = _mm(xd_t[gs, :], jnp.concatenate(bsel, axis=1))
            for n, i in enumerate(pair):
                yos[n].append(_mm_nt(cm[seq_rows[i], g * D_STATE:(g + 1) * D_STATE],
                                     states[n][gs].astype(BF16)))
                a_tot = a_cs[i * SAMPLE_PAD + L - 1:i * SAMPLE_PAD + L, :]
                for hh in range(heads_per_group):
                    h = g * heads_per_group + hh
                    hs = slice(h * HEAD_DIM_M, (h + 1) * HEAD_DIM_M)
                    decay = jnp.exp(jnp.broadcast_to(a_tot[:, h:h + 1], (HEAD_DIM_M, D_STATE)))
                    nst_ref[i, hs, :] = (states[n][hs] * decay
                                         + upd[hh * HEAD_DIM_M:(hh + 1) * HEAD_DIM_M, n * D_STATE:(n + 1) * D_STATE])
        yoff.extend(jnp.concatenate(yo, axis=1) for yo in yos)
    yoff = jnp.concatenate(yoff, axis=0)

    y = (y + yoff * jnp.exp(acs_e)) * _silu(_spread_rows(z_ref[...]))
    outs = []
    for g in range(N_GROUPS_M):
        yg = y[:, g * GROUP_WIDTH:(g + 1) * GROUP_WIDTH]
        outs.append(yg * lax.rsqrt(jnp.mean(yg * yg, axis=-1, keepdims=True) + EPS))
    mout_ref[...] = _gather_rows(jnp.concatenate(outs, axis=1) * ssmn_ref[...])


def _sample_mixer(sinks, u, dtraw, cprev, kc, vc, st, convw, convb, dtb, alog, dskip, ssmn):
    nseq = kc.shape[0]
    assert nseq % SAMPLE_GROUP == 0
    R = SAMPLE_GROUP * SAMPLE_PAD
    T4 = SAMPLE_GROUP * 4
    rows = nseq * SAMPLE_PAD
    consts = (convw, convb, dtb, alog, dskip, ssmn)
    row_spec = lambda w: pl.BlockSpec((R, w), lambda i: (i, 0))
    tok_spec = lambda w: pl.BlockSpec((T4, w), lambda i: (i, 0))
    seq_spec = lambda a, b: pl.BlockSpec((SAMPLE_GROUP, a, b), lambda i: (i, 0, 0))
    in_specs = ([pl.BlockSpec(memory_space=pltpu.SMEM),
                 tok_spec(N_IN_PIECES * PIECE), tok_spec(LANES), row_spec(CONV_DIM),
                 seq_spec(WINDOW, KV_WIDTH), seq_spec(WINDOW, KV_WIDTH), seq_spec(D_INNER, D_STATE)]
                + [_const_spec(a.shape) for a in consts])
    out_shape = (jax.ShapeDtypeStruct((nseq * 4, ATTN_WIDTH), F32), jax.ShapeDtypeStruct((nseq * 4, D_INNER), F32),
                 jax.ShapeDtypeStruct((nseq, WINDOW, KV_WIDTH), F32),
                 jax.ShapeDtypeStruct((nseq, WINDOW, KV_WIDTH), F32),
                 jax.ShapeDtypeStruct((rows, CONV_DIM), F32),
                 jax.ShapeDtypeStruct((nseq, D_INNER, D_STATE), F32))
    out_specs = (tok_spec(ATTN_WIDTH), tok_spec(D_INNER), seq_spec(WINDOW, KV_WIDTH), seq_spec(WINDOW, KV_WIDTH),
                 row_spec(CONV_DIM), seq_spec(D_INNER, D_STATE))
    return pl.pallas_call(
        _sample_mixer_kernel, grid=(nseq // SAMPLE_GROUP,), in_specs=in_specs, out_specs=out_specs,
        out_shape=out_shape, name="sample_mixer",
        compiler_params=pltpu.CompilerParams(dimension_semantics=("arbitrary",),
                                             vmem_limit_bytes=VMEM_LIMIT),
    )(sinks, u, dtraw, cprev, kc, vc, st, *consts)


N_OUT_PIECES = D_MODEL // PIECE
N_FF_BLOCKS = D_FF // FF_BLOCK


def _sample_post_kernel(x_ref, a_ref, m_ref, hnorm_ref, wga_ref, wgb_ref, woa_ref, wob_ref, wo_ref,
                        norm2_ref, wup_ref, wdown_ref, fnorm_ref, y_ref, merged_s, x1_s, hm_s, acc_s):
    s = pl.program_id(0)

    @pl.when(s < N_OUT_PIECES)
    def _():
        hb = hnorm_ref[...]
        piece = (_sigmoid(_mm(hb, wga_ref[0])) * _mm(a_ref[...], woa_ref[0])
                 + _sigmoid(_mm(hb, wgb_ref[0])) * _mm(m_ref[...], wob_ref[0]))
        for p in range(N_OUT_PIECES):
            @pl.when(s == p)
            def _():
                merged_s[:, _piece_cols(p)] = piece.astype(BF16)

    @pl.when((s >= N_OUT_PIECES) & (s < 2 * N_OUT_PIECES))
    def _():
        piece = _mm(merged_s[...], wo_ref[0])
        for p in range(N_OUT_PIECES):
            @pl.when(s == N_OUT_PIECES + p)
            def _():
                x1_s[:, _piece_cols(p)] = x_ref[:, _piece_cols(p)] + piece

    @pl.when(s == 2 * N_OUT_PIECES)
    def _():
        x1 = x1_s[...]
        hm_s[...] = _rms(x1, norm2_ref[...]).astype(BF16)
        acc_s[...] = x1

    @pl.when(s >= 2 * N_OUT_PIECES)
    def _():
        h = jnp.square(jnp.maximum(_mm(hm_s[...], wup_ref[...]), 0.0))
        acc_s[...] += _mm(h, wdown_ref[...])

    @pl.when(s == 2 * N_OUT_PIECES + N_FF_BLOCKS - 1)
    def _():
        y_ref[...] = _rms(acc_s[...], fnorm_ref[...])


def _sample_post(x, a_out, m_out, hnorm, wgate, woa, wob, wo, norm2, wup, wdown, fnorm):
    rows = x.shape[0]
    first = lambda s: jnp.minimum(s, N_OUT_PIECES - 1)
    second = lambda s: jnp.clip(s - N_OUT_PIECES, 0, N_OUT_PIECES - 1)
    ff = lambda s: jnp.clip(s - 2 * N_OUT_PIECES, 0, N_FF_BLOCKS - 1)
    piece_spec = lambda w, index: pl.BlockSpec((1,) + w.shape[1:], lambda s: (index(s), 0, 0))
    in_specs = [_const_spec(x.shape), _const_spec(a_out.shape), _const_spec(m_out.shape), _const_spec(hnorm.shape),
                piece_spec(wgate, first), piece_spec(wgate, lambda s: N_OUT_PIECES + first(s)),
                piece_spec(woa, first), piece_spec(wob, first), piece_spec(wo, second),
                _const_spec(norm2.shape),
                pl.BlockSpec((D_MODEL, FF_BLOCK), lambda s: (0, ff(s))),
                pl.BlockSpec((FF_BLOCK, D_MODEL), lambda s: (ff(s), 0)),
                _const_spec(fnorm.shape)]
    return pl.pallas_call(
        _sample_post_kernel, grid=(2 * N_OUT_PIECES + N_FF_BLOCKS,), in_specs=in_specs,
        out_specs=pl.BlockSpec((rows, D_MODEL), lambda s: (0, 0)),
        out_shape=jax.ShapeDtypeStruct((rows, D_MODEL), F32),
        scratch_shapes=[pltpu.VMEM((rows, D_MODEL), BF16), pltpu.VMEM((rows, D_MODEL), F32),
                        pltpu.VMEM((rows, D_MODEL), BF16), pltpu.VMEM((rows, D_MODEL), F32)],
        name="sample_post",
        compiler_params=pltpu.CompilerParams(dimension_semantics=("arbitrary",),
                                             vmem_limit_bytes=VMEM_LIMIT),
    )(x, a_out, m_out, hnorm, wgate, wgate, woa, wob, wo, norm2, wup, wdown, fnorm)


def _layer_params(norm1, w_in, sinks, conv_w, conv_b, dt_bias, a_log, d_skip, ssm_norm, w_oa, w_ob, w_o,
                  norm2, w_up, w_down):
    dt0 = QKV_WIDTH + D_INNER + CONV_DIM
    assert dt0 == N_IN_PIECES * PIECE
    pad_lanes = lambda a: jnp.pad(a, ((0, 0), (0, LANES - a.shape[1])))
    w_in_t = w_in.T
    mixer = MixerWeights(
        norm1=norm1[None, :], win=_prep_pieces_t(w_in_t, N_IN_PIECES),
        wgate=_prep_pieces_t_shifted(w_in_t, dt0 + N_HEADS_M, 2 * D_MODEL // PIECE),
        wdt=_prep_dt(w_in_t, dt0),
        convw=conv_w, convb=conv_b[None, :], dtb=pad_lanes(dt_bias[None, :]), alog=pad_lanes(a_log[None, :]),
        dskip=jnp.repeat(d_skip, HEAD_DIM_M)[None, :], ssmn=ssm_norm[None, :],
        woa=_prep_pieces(w_oa), wob=_prep_pieces(w_ob), wo=_prep_pieces(w_o))
    wide = (D_MODEL, D_MODEL)
    return dict(sinks=sinks.astype(F32), mixer=mixer,
                norm2=norm2[None, :], wup=_cast_bf16(w_up, wide), wdown=_cast_bf16(w_down, wide))


def kernel(x_prompt, x_sample, cache_swa_k, cache_swa_v, state_conv, state_ssm, norm1, w_in, sinks, conv_w,
           conv_b, dt_bias, a_log, d_skip, ssm_norm, w_oa, w_ob, w_o, norm2, w_up, w_down, final_norm):
    depth = w_in.shape[0]
    assert depth == 1
    nb, seq, _ = x_prompt.shape
    ns, ls, _ = x_sample.shape
    assert ls == 4
    p = _layer_params(norm1[0], w_in[0], sinks[0], conv_w[0], conv_b[0], dt_bias[0], a_log[0], d_skip[0],
                      ssm_norm[0], w_oa[0], w_ob[0], w_o[0], norm2[0], w_up[0], w_down[0])
    fnorm = final_norm[None, :]
    mw = p["mixer"]
    mixer_consts = (mw.convw, mw.convb, mw.dtb, mw.alog, mw.dskip, mw.ssmn)

    x1p, pk, pv, pc, pst = _prompt_mixer(x_prompt, p["sinks"], mw)
    y_prompt = _mlp(x1p, p["norm2"], p["wup"], p["wdown"], fnorm)
    y_prompt = y_prompt.reshape(nb, seq, D_MODEL)

    xs_rows = x_sample.reshape(ns * ls, D_MODEL)
    dim_major = lambda a: jnp.swapaxes(a, 1, 2)
    cprev = jnp.pad(state_conv[0], ((0, 0), (SAMPLE_PAD - (CONV_W - 1), 0), (0, 0)))
    cprev = cprev.reshape(ns * SAMPLE_PAD, CONV_DIM)
    u, dtraw, hnorm = _in_proj(xs_rows, mw.norm1, mw.win, mw.wdt)
    a_out, m_out, sk, sv, sc, sst = _sample_mixer(
        p["sinks"], u, dtraw, cprev,
        dim_major(cache_swa_k[0].reshape(ns, WINDOW, KV_WIDTH)), dim_major(cache_swa_v[0].reshape(ns, WINDOW, KV_WIDTH)),
        state_ssm[0].reshape(ns, D_INNER, D_STATE), *mixer_consts)
    sk, sv = dim_major(sk), dim_major(sv)
    y_sample = _sample_post(xs_rows, a_out, m_out, hnorm, mw.wgate, mw.woa, mw.wob, mw.wo,
                            p["norm2"], p["wup"], p["wdown"], fnorm).reshape(ns, ls, D_MODEL)
    sc = sc.reshape(ns, SAMPLE_PAD, CONV_DIM)[:, :CONV_W - 1]

    kv_shape = (1, -1, WINDOW, N_KV_A, HEAD_DIM_A)
    ssm_shape = (1, -1, N_HEADS_M, HEAD_DIM_M, D_STATE)
    return (y_prompt, y_sample,
            pk.reshape(kv_shape), pv.reshape(kv_shape), pc[None], pst.reshape(ssm_shape),
            sk.reshape(kv_shape), sv.reshape(kv_shape), sc[None], sst.reshape(ssm_shape))
```

```python
import collections
import functools

import jax
import jax.numpy as jnp
from jax import lax
from jax.experimental import pallas as pl
from jax.experimental.pallas import tpu as pltpu

F32 = jnp.float32
BF16 = jnp.bfloat16

D_MODEL = 1024
N_HEADS_A = 8
N_KV_A = 2
Q_PER_KV = N_HEADS_A // N_KV_A
HEAD_DIM_A = 64
WINDOW = 128
ATTN_WIDTH = N_HEADS_A * HEAD_DIM_A
KV_WIDTH = N_KV_A * HEAD_DIM_A
D_INNER = 1024
HEAD_DIM_M = 64
N_HEADS_M = D_INNER // HEAD_DIM_M
N_GROUPS_M = 2
GROUP_WIDTH = D_INNER // N_GROUPS_M
D_STATE = 128
CONV_W = 4
CONV_DIM = D_INNER + 2 * N_GROUPS_M * D_STATE
CHUNK = 128
D_FF = 4 * D_MODEL
EPS = 1e-6

LANES = 128
SUBLANES = 8
QKV_WIDTH = ATTN_WIDTH + 2 * KV_WIDTH
NEG_BIG = -1e30
VMEM_LIMIT = 56 * 1024 * 1024
SAMPLE_PAD = SUBLANES
SAMPLE_GROUP = 16


def _mm(a, b):
    return jnp.dot(a.astype(BF16), b.astype(BF16), preferred_element_type=F32)


def _mm_nt(a, b):
    return lax.dot_general(a.astype(BF16), b.astype(BF16), (((1,), (1,)), ((), ())),
                           preferred_element_type=F32)


def _rms(x, w):
    return x * lax.rsqrt(jnp.mean(x * x, axis=-1, keepdims=True) + EPS) * w


def _sigmoid(x):
    return 0.5 + 0.5 * jnp.tanh(0.5 * x)


def _silu(x):
    h = 0.5 * x
    return h + h * jnp.tanh(h)


def _softplus(x):
    return jnp.maximum(x, 0.0) + jnp.log(1.0 + jnp.exp(-jnp.abs(x)))


def _lane_lo(shape):
    return (lax.broadcasted_iota(jnp.int32, shape, len(shape) - 1) % LANES) < HEAD_DIM_A


def _dup_half(x, lo, first):
    xr = pltpu.roll(x, HEAD_DIM_A, axis=1)
    return jnp.where(lo, x, xr) if first else jnp.where(lo, xr, x)


def _stack_heads(q, kv, lo):
    qa = q[:, kv * 2 * LANES: kv * 2 * LANES + LANES]
    qb = q[:, kv * 2 * LANES + LANES: (kv + 1) * 2 * LANES]
    zero = jnp.zeros_like(qa)
    return jnp.concatenate([jnp.where(lo, qa, zero), jnp.where(lo, zero, qa),
                            jnp.where(lo, qb, zero), jnp.where(lo, zero, qb)], axis=0)


def _unstack_heads(o, rows, lo):
    return jnp.concatenate([jnp.where(lo, o[0:rows], o[rows:2 * rows]),
                            jnp.where(lo, o[2 * rows:3 * rows], o[3 * rows:4 * rows])], axis=1)


ProjBufs = collections.namedtuple("ProjBufs", "qkv z xbc dt")
MixedBufs = collections.namedtuple("MixedBufs", "attn ssd")
Carry = collections.namedtuple("Carry", "kprev vprev conv_tail ht dt_buf")
MixerWeights = collections.namedtuple(
    "MixerWeights", "norm1 win wgate wdt convw convb dtb alog dskip ssmn woa wob wo")

PIECE = 256
QKV_PIECE0 = 0
Z_PIECE0 = QKV_PIECE0 + QKV_WIDTH // PIECE
XBC_PIECE0 = Z_PIECE0 + D_INNER // PIECE
N_IN_PIECES = XBC_PIECE0 + CONV_DIM // PIECE
PREP_PIECES = 4


def _prep_pieces_kernel(w_ref, o_ref):
    for j in range(PREP_PIECES):
        o_ref[j] = w_ref[:, j * PIECE:(j + 1) * PIECE].astype(BF16)


def _prep_pieces(w, n_pieces=None, k_block=None):
    k_dim, n_dim = w.shape
    n_pieces = n_dim // PIECE if n_pieces is None else n_pieces
    k_block = k_dim if k_block is None else k_block
    cols = PREP_PIECES * PIECE
    return pl.pallas_call(
        _prep_pieces_kernel, grid=(pl.cdiv(n_pieces, PREP_PIECES), k_dim // k_block),
        in_specs=[pl.BlockSpec((k_block, cols), lambda p, k: (k, p))],
        out_specs=pl.BlockSpec((PREP_PIECES, k_block, PIECE), lambda p, k: (p, k, 0)),
        out_shape=jax.ShapeDtypeStruct((n_pieces, k_dim, PIECE), BF16), name="prep_weight_pieces",
        compiler_params=pltpu.CompilerParams(dimension_semantics=("arbitrary", "arbitrary"),
                                             vmem_limit_bytes=VMEM_LIMIT),
    )(w)


def _prep_pieces_t_kernel(wt_ref, o_ref):
    for j in range(PREP_PIECES):
        o_ref[j] = wt_ref[j * PIECE:(j + 1) * PIECE, :].T.astype(BF16)


def _prep_pieces_t(wt, n_pieces=None):
    n_dim, k_dim = wt.shape
    n_pieces = n_dim // PIECE if n_pieces is None else n_pieces
    return pl.pallas_call(
        _prep_pieces_t_kernel, grid=(pl.cdiv(n_pieces, PREP_PIECES),),
        in_specs=[pl.BlockSpec((PREP_PIECES * PIECE, k_dim), lambda p: (p, 0))],
        out_specs=pl.BlockSpec((PREP_PIECES, k_dim, PIECE), lambda p: (p, 0, 0)),
        out_shape=jax.ShapeDtypeStruct((n_pieces, k_dim, PIECE), BF16), name="prep_weight_pieces_t",
        compiler_params=pltpu.CompilerParams(dimension_semantics=("arbitrary",),
                                             vmem_limit_bytes=VMEM_LIMIT),
    )(wt)


def _prep_pieces_t_shifted_kernel(a_ref, b_ref, o_ref, *, shift):
    rows = jnp.concatenate([a_ref[shift:, :], b_ref[:shift, :]], axis=0)
    o_ref[0] = rows.T.astype(BF16)


def _prep_pieces_t_shifted(wt, row0, n_pieces):
    k_dim = wt.shape[1]
    block0, shift = divmod(row0, PIECE)
    assert shift % SUBLANES == 0 and shift > 0
    return pl.pallas_call(
        functools.partial(_prep_pieces_t_shifted_kernel, shift=shift), grid=(n_pieces,),
        in_specs=[pl.BlockSpec((PIECE, k_dim), lambda p: (block0 + p, 0)),
                  pl.BlockSpec((PIECE, k_dim), lambda p: (block0 + p + 1, 0))],
        out_specs=pl.BlockSpec((1, k_dim, PIECE), lambda p: (p, 0, 0)),
        out_shape=jax.ShapeDtypeStruct((n_pieces, k_dim, PIECE), BF16), name="prep_weight_pieces_t_shifted",
        compiler_params=pltpu.CompilerParams(dimension_semantics=("arbitrary",),
                                             vmem_limit_bytes=VMEM_LIMIT),
    )(wt, wt)


def _prep_dt_kernel(wt_ref, o_ref):
    rows = wt_ref[...]
    keep = lax.broadcasted_iota(jnp.int32, rows.shape, 0) < N_HEADS_M
    o_ref[...] = jnp.where(keep, rows, 0.0).T.astype(BF16)


def _prep_dt(wt, row0):
    k_dim = wt.shape[1]
    assert row0 % LANES == 0
    return pl.pallas_call(
        _prep_dt_kernel, grid=(1,),
        in_specs=[pl.BlockSpec((LANES, k_dim), lambda i: (row0 // LANES, 0))],
        out_specs=pl.BlockSpec((k_dim, LANES), lambda i: (0, 0)),
        out_shape=jax.ShapeDtypeStruct((k_dim, LANES), BF16), name="prep_weight_dt",
        compiler_params=pltpu.CompilerParams(dimension_semantics=("arbitrary",), vmem_limit_bytes=VMEM_LIMIT),
    )(wt)


StateOuts = collections.namedtuple("StateOuts", "k v conv ssm")


def _split3_bf16(x):
    hi = x.astype(BF16)
    r1 = x - hi.astype(F32)
    mid = r1.astype(BF16)
    return hi, mid, (r1 - mid.astype(F32)).astype(BF16)


def _piece_cols(p):
    return slice(p * PIECE, (p + 1) * PIECE)


def _in_proj_pieces(hb, w, dsts):
    for dst, first in zip(dsts, (QKV_PIECE0, Z_PIECE0, XBC_PIECE0)):
        for p in range(dst.shape[1] // PIECE):
            dst[:, _piece_cols(p)] = _mm(hb, w.win[first + p])
            yield


def _out_proj_pieces(x_ref, rows, hb, attn_ref, ssd_ref, w, out_ref):
    merged = []
    for half, (src_ref, w_ref) in enumerate(((attn_ref, w.woa), (ssd_ref, w.wob))):
        parts = []
        for p in range(D_MODEL // PIECE):
            gate = _sigmoid(_mm(hb, w.wgate[half * (D_MODEL // PIECE) + p]))
            yield
            parts.append(gate * _mm(src_ref[...], w_ref[p]))
            yield
        merged.append(jnp.concatenate(parts, axis=1))
    merged = (merged[0] + merged[1]).astype(BF16)
    for p in range(D_MODEL // PIECE):
        out_ref[rows, _piece_cols(p)] = x_ref[rows, _piece_cols(p)] + _mm(merged, w.wo[p])
        yield


def _stage_in_proj(x_ref, rows, w, proj, hnorm_ref):
    hb = _rms(x_ref[rows, :], w.norm1[...]).astype(BF16)
    hnorm_ref[...] = hb
    yield
    yield from _in_proj_pieces(hb, w, (proj.qkv, proj.z, proj.xbc))
    proj.dt[...] = _mm(hb, w.wdt[...])
    yield


def _stage_out_proj(x_ref, rows, mixed, w, x1_ref, hnorm_ref):
    hb = hnorm_ref[...]
    yield
    yield from _out_proj_pieces(x_ref, rows, hb, mixed.attn, mixed.ssd, w, x1_ref)


def _alternate(*gens):
    gens = list(gens)
    while gens:
        for g in list(gens):
            try:
                next(g)
                yield
            except StopIteration:
                gens.remove(g)


def _trace_interleaved(primary, filler):
    for _ in primary:
        next(filler, None)
    for _ in filler:
        pass


def _stage_mix(proj, mixed, rows, carry, w, sinks_ref, first):
    T = CHUNK
    reset = (lambda a: a) if first is None else (lambda a: jnp.where(first, 0.0, a))
    lo = _lane_lo((T, LANES))
    lo2 = _lane_lo((2 * T, LANES))
    rm = lax.broadcasted_iota(jnp.int32, (T, 1), 0) % SUBLANES
    rr = lax.broadcasted_iota(jnp.int32, (T, T), 0)
    cc = lax.broadcasted_iota(jnp.int32, (T, T), 1)
    tril = cc <= rr
    pairs_per_group = N_HEADS_M // 2 // N_GROUPS_M
    n_pairs = N_HEADS_M // 2
    gcols = lambda g: slice(g * GROUP_WIDTH, (g + 1) * GROUP_WIDTH)

    def conv_silu(cols):
        xr = proj.xbc[rows, cols]
        prev_tile = jnp.concatenate([reset(carry.conv_tail[:, cols]), xr[:T - SUBLANES]], axis=0)
        carry.conv_tail[:, cols] = xr[T - SUBLANES:]
        yc = xr * w.convw[CONV_W - 1:CONV_W, cols]
        for shift in range(1, CONV_W):
            shifted = _tile_roll(jnp.where(rm >= SUBLANES - shift, prev_tile, xr), shift)
            yc = yc + shifted * w.convw[CONV_W - 1 - shift:CONV_W - shift, cols]
        return _silu(yc + w.convb[:, cols])

    def softmax(s, kv):
        r = lax.broadcasted_iota(jnp.int32, (T, 2 * T), 0)
        col = lax.broadcasted_iota(jnp.int32, (T, 2 * T), 1)
        valid = (col >= r) & (col <= r + WINDOW)
        if first is not None:
            valid = valid & (col >= jnp.where(first, T, 0))
        es, inv = [], []
        for g in range(Q_PER_KV):
            sk = sinks_ref[kv * Q_PER_KV + g]
            sg = jnp.where(valid, s[g * T:(g + 1) * T], NEG_BIG)
            m = jnp.maximum(jnp.max(sg, axis=-1, keepdims=True), sk)
            e = jnp.exp(sg - m)
            es.append(e.astype(BF16))
            inv.append(1.0 / (jnp.sum(e, axis=-1, keepdims=True) + jnp.exp(sk - m)))
        return jnp.concatenate(es, axis=0), jnp.concatenate(inv, axis=0)

    qkv = proj.qkv[rows, :]
    q = qkv[:, :ATTN_WIDTH] * (HEAD_DIM_A ** -0.5)
    k = qkv[:, ATTN_WIDTH:ATTN_WIDTH + KV_WIDTH]
    v = qkv[:, ATTN_WIDTH + KV_WIDTH:]
    kk = jnp.concatenate([carry.kprev[...], k], axis=0)
    vv = jnp.concatenate([carry.vprev[...], v], axis=0)
    carry.kprev[...] = k
    carry.vprev[...] = v
    carry.dt_buf[...] = _softplus(proj.dt[rows, :] + w.dtb[...])
    da_split = jnp.concatenate(_split3_bf16(carry.dt_buf[...] * -jnp.exp(w.alog[...])), axis=1)
    vds = [_dup_half(vv, lo2, kv == 0).astype(BF16) for kv in range(N_KV_A)]
    scores = [_mm_nt(_stack_heads(q, kv, lo), _dup_half(kk, lo2, kv == 0).astype(BF16))
              for kv in range(N_KV_A)]
    yield

    bc = conv_silu(slice(D_INNER, CONV_DIM))
    bgs = [bc[:, g * D_STATE:(g + 1) * D_STATE].astype(BF16) for g in range(N_GROUPS_M)]
    cgs = [bc[:, (N_GROUPS_M + g) * D_STATE:(N_GROUPS_M + g + 1) * D_STATE].astype(BF16)
           for g in range(N_GROUPS_M)]
    bgts = [bc[:, g * D_STATE:(g + 1) * D_STATE].T.astype(BF16) for g in range(N_GROUPS_M)]
    yield

    es0, inv0 = softmax(scores[0], 0)
    yield

    es1, inv1 = softmax(scores[1], 1)
    cbs = [_mm_nt(cgs[g], bgs[g]) for g in range(N_GROUPS_M)]
    h_prevs = [reset(carry.ht[:, gcols(g)]) for g in range(N_GROUPS_M)]
    y_offs = [_mm(cgs[g], h_prevs[g]) for g in range(N_GROUPS_M)]
    o0 = _mm(es0, vds[0])
    cs = jnp.dot(jnp.where(tril, 1.0, 0.0).astype(BF16), da_split, preferred_element_type=F32)
    yield

    a_cs = cs[:, :LANES] + cs[:, LANES:2 * LANES] + cs[:, 2 * LANES:]
    a_cs_t = a_cs.T
    xs_groups = [conv_silu(gcols(0))]
    o1 = _mm(es1, vds[1])
    yield

    xs_groups.append(conv_silu(gcols(1)))
    mixed.attn[rows, 0:2 * LANES] = _unstack_heads(o0 * inv0, T, lo).astype(BF16)
    yield

    def prep(j):
        g = j // pairs_per_group
        dt = carry.dt_buf[...]
        ws, colbs, dtbs = [], [], []
        for h in (2 * j, 2 * j + 1):
            colb = jnp.broadcast_to(a_cs[:, h:h + 1], (T, T))
            rowb = jnp.broadcast_to(a_cs_t[h:h + 1, :], (T, T))
            seg = jnp.where(tril, jnp.exp(colb - rowb), 0.0)
            ws.append((cbs[g] * seg).astype(BF16))
            colbs.append(colb)
            dtbs.append(jnp.broadcast_to(dt[:, h:h + 1], (T, LANES)))
        dt_e = jnp.where(lo, dtbs[0], dtbs[1])
        acs_e = jnp.where(lo, colbs[0], colbs[1])
        xs_j = xs_groups[g][:, (j % pairs_per_group) * LANES:(j % pairs_per_group + 1) * LANES]
        xdt = xs_j * dt_e
        zero = jnp.zeros_like(xdt)
        rhs = jnp.concatenate([jnp.where(lo, xdt, zero), jnp.where(lo, zero, xdt)], axis=0).astype(BF16)
        alast = acs_e[T - 1:T, :]
        return dict(lhs=jnp.concatenate(ws, axis=1), rhs=rhs, xs=xs_j, e_acs=jnp.exp(acs_e),
                    xd=(xdt * jnp.exp(alast - acs_e)).astype(BF16), decay=jnp.exp(alast))

    def finish(j, p, y_diag):
        g, jj = divmod(j, pairs_per_group)
        sl = slice(j * LANES, (j + 1) * LANES)
        y = y_diag + y_offs[g][:, jj * LANES:(jj + 1) * LANES] * p["e_acs"] + p["xs"] * w.dskip[:, sl]
        return y * _silu(proj.z[rows, sl])

    mixed.attn[rows, 2 * LANES:4 * LANES] = _unstack_heads(o1 * inv1, T, lo).astype(BF16)
    preps = {0: prep(0)}
    yield

    ys, y_diags = [], {}
    for j in range(n_pairs):
        if j + 1 < n_pairs:
            preps[j + 1] = prep(j + 1)
        y_diags[j] = _mm(preps[j]["lhs"], preps[j]["rhs"])
        if j >= 1:
            ys.append(finish(j - 1, preps[j - 1], y_diags.pop(j - 1)))
        if j % pairs_per_group == pairs_per_group - 1:
            g = j // pairs_per_group
            grp = [preps[i] for i in range(g * pairs_per_group, (g + 1) * pairs_per_group)]
            carry.ht[:, gcols(g)] = (h_prevs[g] * jnp.concatenate([p["decay"] for p in grp], axis=1)
                                     + _mm(bgts[g], jnp.concatenate([p["xd"] for p in grp], axis=1)))
        yield
    ys.append(finish(n_pairs - 1, preps[n_pairs - 1], y_diags.pop(n_pairs - 1)))
    yield

    m_slabs = []
    for g in range(N_GROUPS_M):
        grp = ys[g * pairs_per_group:(g + 1) * pairs_per_group]
        ssq = grp[0] * grp[0]
        for y in grp[1:]:
            ssq = ssq + y * y
        scale = lax.rsqrt(jnp.sum(ssq, axis=-1, keepdims=True) * (1.0 / GROUP_WIDTH) + EPS)
        m_slabs.extend([y * scale for y in grp])
    mixed.ssd[rows, :] = (jnp.concatenate(m_slabs, axis=1) * w.ssmn[...]).astype(BF16)


def _chain(*gens):
    for g in gens:
        yield from g


def _prompt_mixer_kernel(sinks_ref, x_in_ref, x_res_ref, *refs, chunks_per_seq):
    refs = list(refs)
    take = lambda n: [refs.pop(0) for _ in range(n)]
    w = MixerWeights(*take(len(MixerWeights._fields)))
    mlp_w_f32 = take(2)
    x1_ref, = take(1)
    outs = StateOuts(*take(4))
    mlp_w_bf16 = take(2)
    proj = (ProjBufs(*take(4)), ProjBufs(*take(4)))
    mixed = (MixedBufs(*take(2)), MixedBufs(*take(2)))
    hnorm = take(2)
    carry = Carry(*take(len(Carry._fields)))
    assert not refs
    T = CHUNK
    s = pl.program_id(0)
    every = slice(None)

    @pl.when(s == 0)
    def _():
        for ref in list(proj[1] + mixed[0] + mixed[1] + carry) + hnorm:
            ref[...] = jnp.zeros_like(ref)

    first = (s - 1) % (chunks_per_seq // 2) == 0

    def cast_mlp_weights():
        for src, dst in zip(mlp_w_f32, mlp_w_bf16):
            dst[...] = src[...].astype(BF16)
            yield

    def step(cur):
        prv = 1 - cur
        dense = _alternate(_stage_out_proj(x_res_ref, every, mixed[cur], w, x1_ref, hnorm[cur]),
                           _stage_in_proj(x_in_ref, every, w, proj[cur], hnorm[cur]), cast_mlp_weights())
        mix = _chain(_stage_mix(proj[prv], mixed[prv], slice(0, T), carry, w, sinks_ref, first),
                     _stage_mix(proj[prv], mixed[prv], slice(T, 2 * T), carry, w, sinks_ref, None))
        _trace_interleaved(mix, dense)

    for parity in (0, 1):
        pl.when(s % 2 == parity)(functools.partial(step, parity))

    @pl.when((s >= 1) & (s % (chunks_per_seq // 2) == 0))
    def _():
        outs.k[0] = carry.kprev[...]
        outs.v[0] = carry.vprev[...]
        outs.conv[0] = carry.conv_tail[SUBLANES - (CONV_W - 1):, :]
        outs.ssm[0] = carry.ht[...].T


def _const_spec(shape):
    return pl.BlockSpec(shape, lambda *_: (0,) * len(shape), pipeline_mode=pl.Buffered(1))


def _prompt_mixer(x, sinks, weights, w_up, w_down):
    nb, seq, _ = x.shape
    assert seq % (2 * CHUNK) == 0
    chunks_per_seq = seq // CHUNK
    n_pairs = nb * chunks_per_seq // 2
    n_steps = n_pairs + 2
    pair = 2 * CHUNK
    consts = tuple(weights)
    cast_steps = 1 << ((n_steps // 2).bit_length() - 1)
    ff_slice = D_FF // cast_steps
    assert ff_slice % LANES == 0
    up_block = lambda s: (0, jnp.minimum(s, cast_steps - 1))
    down_block = lambda s: (jnp.clip(s - cast_steps, 0, cast_steps - 1), 0)
    seq_of_mix = lambda s: jnp.clip((2 * s - 1) // chunks_per_seq, 0, nb - 1)
    in_specs = ([pl.BlockSpec(memory_space=pltpu.SMEM),
                 pl.BlockSpec((pair, D_MODEL), lambda s: (jnp.minimum(s, n_pairs - 1), 0)),
                 pl.BlockSpec((pair, D_MODEL), lambda s: (jnp.maximum(s - 2, 0), 0))]
                + [_const_spec(a.shape) for a in consts]
                + [pl.BlockSpec((D_MODEL, ff_slice), up_block), pl.BlockSpec((ff_slice, D_MODEL), down_block)])
    out_shape = (jax.ShapeDtypeStruct((nb * seq, D_MODEL), F32),
                 jax.ShapeDtypeStruct((nb, WINDOW, KV_WIDTH), F32),
                 jax.ShapeDtypeStruct((nb, WINDOW, KV_WIDTH), F32),
                 jax.ShapeDtypeStruct((nb, CONV_W - 1, CONV_DIM), F32),
                 jax.ShapeDtypeStruct((nb, D_INNER, D_STATE), F32),
                 jax.ShapeDtypeStruct(w_up.shape, BF16), jax.ShapeDtypeStruct(w_down.shape, BF16))
    out_specs = (pl.BlockSpec((pair, D_MODEL), lambda s: (jnp.maximum(s - 2, 0), 0)),
                 pl.BlockSpec((1, WINDOW, KV_WIDTH), lambda s: (seq_of_mix(s), 0, 0)),
                 pl.BlockSpec((1, WINDOW, KV_WIDTH), lambda s: (seq_of_mix(s), 0, 0)),
                 pl.BlockSpec((1, CONV_W - 1, CONV_DIM), lambda s: (seq_of_mix(s), 0, 0)),
                 pl.BlockSpec((1, D_INNER, D_STATE), lambda s: (seq_of_mix(s), 0, 0)),
                 pl.BlockSpec((D_MODEL, ff_slice), up_block), pl.BlockSpec((ff_slice, D_MODEL), down_block))
    proj_bufs = [pltpu.VMEM((pair, QKV_WIDTH), F32), pltpu.VMEM((pair, D_INNER), F32),
                 pltpu.VMEM((pair, CONV_DIM), F32), pltpu.VMEM((pair, LANES), F32)]
    mixed_bufs = [pltpu.VMEM((pair, ATTN_WIDTH), BF16), pltpu.VMEM((pair, D_INNER), BF16)]
    hnorm_bufs = [pltpu.VMEM((pair, D_MODEL), BF16)] * 2
    carry = [pltpu.VMEM((CHUNK, KV_WIDTH), F32), pltpu.VMEM((CHUNK, KV_WIDTH), F32),
             pltpu.VMEM((SUBLANES, CONV_DIM), F32), pltpu.VMEM((D_STATE, D_INNER), F32),
             pltpu.VMEM((CHUNK, LANES), F32)]
    x_rows = x.reshape(nb * seq, D_MODEL)
    return pl.pallas_call(
        functools.partial(_prompt_mixer_kernel, chunks_per_seq=chunks_per_seq),
        grid=(n_steps,), in_specs=in_specs, out_specs=out_specs,
        out_shape=out_shape, scratch_shapes=proj_bufs * 2 + mixed_bufs * 2 + hnorm_bufs + carry,
        name="prompt_mixer",
        compiler_params=pltpu.CompilerParams(dimension_semantics=("arbitrary",),
                                             vmem_limit_bytes=VMEM_LIMIT),
    )(sinks, x_rows, x_rows, *consts, w_up, w_down)


MLP_TILE = 1024
FF_BLOCK = 1024


def _mlp_kernel(x_ref, norm2_ref, wup_ref, wdown_ref, fnorm_ref, y_ref):
    x = x_ref[...]
    hm = _rms(x, norm2_ref[...]).astype(BF16)
    acc = x
    for j in range(D_FF // FF_BLOCK):
        h = _mm(hm, wup_ref[:, j * FF_BLOCK:(j + 1) * FF_BLOCK])
        h = jnp.square(jnp.maximum(h, 0.0))
        acc = acc + _mm(h, wdown_ref[j * FF_BLOCK:(j + 1) * FF_BLOCK, :])
    y_ref[...] = _rms(acc, fnorm_ref[...])


def _mlp(x, norm2, wup, wdown, fnorm):
    rows = x.shape[0]
    tile = min(MLP_TILE, rows)
    assert rows % tile == 0 and tile % SUBLANES == 0
    consts = (norm2, wup, wdown, fnorm)
    return pl.pallas_call(
        _mlp_kernel, grid=(rows // tile,),
        in_specs=[pl.BlockSpec((tile, D_MODEL), lambda i: (i, 0))] + [_const_spec(a.shape) for a in consts],
        out_specs=pl.BlockSpec((tile, D_MODEL), lambda i: (i, 0)),
        out_shape=jax.ShapeDtypeStruct((rows, D_MODEL), F32), name="mlp",
        compiler_params=pltpu.CompilerParams(dimension_semantics=("arbitrary",),
                                             vmem_limit_bytes=VMEM_LIMIT),
    )(x, *consts)


IN_PROJ_ROWS = 256


def _in_proj_kernel(x_ref, norm1_ref, win_ref, wdt_ref, u_ref, dt_ref, hnorm_ref):
    hb = _rms(x_ref[...], norm1_ref[...]).astype(BF16)
    hnorm_ref[...] = hb
    for p in range(win_ref.shape[0]):
        u_ref[:, _piece_cols(p)] = _mm(hb, win_ref[p])
    dt_ref[...] = _mm(hb, wdt_ref[...])


def _in_proj(x, norm1, win, wdt):
    rows = x.shape[0]
    assert rows % IN_PROJ_ROWS == 0
    outs = ((win.shape[0] * PIECE, F32), (LANES, F32), (D_MODEL, BF16))
    row_spec = lambda width: pl.BlockSpec((IN_PROJ_ROWS, width), lambda i: (i, 0))
    return pl.pallas_call(
        _in_proj_kernel, grid=(rows // IN_PROJ_ROWS,),
        in_specs=[row_spec(D_MODEL), _const_spec(norm1.shape), _const_spec(win.shape), _const_spec(wdt.shape)],
        out_specs=tuple(row_spec(width) for width, _ in outs),
        out_shape=tuple(jax.ShapeDtypeStruct((rows, width), dtype) for width, dtype in outs),
        name="sample_in_proj",
        compiler_params=pltpu.CompilerParams(dimension_semantics=("arbitrary",),
                                             vmem_limit_bytes=VMEM_LIMIT),
    )(x, norm1, win, wdt)


def _tile_roll(x, shift):
    rows, width = x.shape
    tiles = x.reshape(rows // SUBLANES, SUBLANES, width)
    return pltpu.roll(tiles, shift % SUBLANES, axis=1).reshape(rows, width)


def _spread_rows(x):
    tiles = []
    for t in range(x.shape[0] // SUBLANES):
        two_seqs = x[t * SUBLANES:(t + 1) * SUBLANES]
        tiles += [two_seqs, pltpu.roll(two_seqs, SUBLANES // 2, axis=0)]
    return jnp.concatenate(tiles, axis=0)


def _gather_rows(y):
    low = lax.broadcasted_iota(jnp.int32, (SUBLANES, 1), 0) < SUBLANES // 2
    tiles = []
    for t in range(y.shape[0] // (2 * SUBLANES)):
        a = y[2 * t * SUBLANES:(2 * t + 1) * SUBLANES]
        b = y[(2 * t + 1) * SUBLANES:(2 * t + 2) * SUBLANES]
        tiles.append(jnp.where(low, a, pltpu.roll(b, SUBLANES // 2, axis=0)))
    return jnp.concatenate(tiles, axis=0)


def _expand_heads(cols, expand):
    terms = []
    for c in cols:
        hi = c.astype(BF16)
        r1 = c - hi.astype(F32)
        mid = r1.astype(BF16)
        lo = (r1 - mid.astype(F32)).astype(BF16)
        terms.extend([hi, mid, lo])
    rows = cols[0].shape[0]
    out = jnp.dot(jnp.concatenate(terms, axis=0), expand, preferred_element_type=F32)
    return [out[(3 * i) * rows:(3 * i + 1) * rows] + out[(3 * i + 1) * rows:(3 * i + 2) * rows]
            + out[(3 * i + 2) * rows:(3 * i + 3) * rows] for i in range(len(cols))]


def _sample_mixer_kernel(sinks_ref, u_ref, dtraw_ref, cprev_ref, kc_ref, vc_ref, st_ref,
                         convw_ref, convb_ref, dtb_ref, alog_ref, dskip_ref, ssmn_ref,
                         aout_ref, mout_ref, nk_ref, nv_ref, nconv_ref, nst_ref):
    R = SAMPLE_GROUP * SAMPLE_PAD
    L = 4
    qkv_ref = u_ref.at[:, QKV_PIECE0 * PIECE:Z_PIECE0 * PIECE]
    z_ref = u_ref.at[:, Z_PIECE0 * PIECE:XBC_PIECE0 * PIECE]
    xbc_ref = u_ref.at[:, XBC_PIECE0 * PIECE:N_IN_PIECES * PIECE]
    rm = lax.broadcasted_iota(jnp.int32, (R, 1), 0) % SAMPLE_PAD

    xr = xbc_raw = _spread_rows(xbc_ref[...])
    with_prev = jnp.where(rm >= SAMPLE_PAD - (CONV_W - 1), cprev_ref[...], xr)
    yc = xr * convw_ref[CONV_W - 1:CONV_W, :]
    for kshift in range(1, CONV_W):
        yc = yc + _tile_roll(with_prev, kshift) * convw_ref[CONV_W - 1 - kshift:CONV_W - kshift, :]
    nconv_ref[...] = _tile_roll(xbc_raw, -1)
    xbc = _silu(yc + convb_ref[...])
    xs = xbc[:, :D_INNER]
    bm = xbc[:, D_INNER:D_INNER + N_GROUPS_M * D_STATE]
    cm = xbc[:, D_INNER + N_GROUPS_M * D_STATE:]

    dt = _softplus(_spread_rows(dtraw_ref[...]) + dtb_ref[...])
    d_a = dt * (-jnp.exp(alog_ref[...]))
    a_cs = d_a
    suf = jnp.zeros_like(d_a)
    for kshift in range(1, L):
        a_cs = a_cs + jnp.where(rm >= kshift, _tile_roll(d_a, kshift), 0.0)
        suf = suf + jnp.where(rm <= L - 1 - kshift, _tile_roll(d_a, -kshift), 0.0)

    expand = (lax.broadcasted_iota(jnp.int32, (LANES, D_INNER), 1) // HEAD_DIM_M
              == lax.broadcasted_iota(jnp.int32, (LANES, D_INNER), 0)).astype(BF16)
    dt_e, acs_e, suf_e = _expand_heads([dt, a_cs, suf], expand)
    xdt = xs * dt_e
    real = rm < L
    xd_t = jnp.where(real, xdt * jnp.exp(suf_e), 0.0).T.astype(BF16)

    y = xs * dskip_ref[...]
    for kshift in range(L):
        bk = bm if kshift == 0 else _tile_roll(bm, kshift)
        cb = cm * bk
        cb_e = jnp.concatenate(
            [jnp.broadcast_to(jnp.sum(cb[:, g * D_STATE:(g + 1) * D_STATE], axis=-1, keepdims=True),
                              (R, GROUP_WIDTH)) for g in range(N_GROUPS_M)], axis=1)
        if kshift == 0:
            y = y + cb_e * xdt
        else:
            seg = jnp.exp(acs_e - _tile_roll(acs_e, kshift))
            y = y + jnp.where(rm >= kshift, cb_e * seg * _tile_roll(xdt, kshift), 0.0)

    qkv = _spread_rows(qkv_ref[...])
    q = qkv[:, :ATTN_WIDTH] * (HEAD_DIM_A ** -0.5)
    q_swapped = jnp.concatenate(
        [pltpu.roll(q[:, s * LANES:(s + 1) * LANES], HEAD_DIM_A, axis=1) for s in range(ATTN_WIDTH // LANES)], axis=1)
    kn = qkv[:, ATTN_WIDTH:ATTN_WIDTH + KV_WIDTH]
    vn = qkv[:, ATTN_WIDTH + KV_WIDTH:]
    lo8 = _lane_lo((SAMPLE_PAD, LANES))
    zero8 = jnp.zeros((SAMPLE_PAD, LANES), F32)
    seq_rows = [slice(i * SAMPLE_PAD, (i + 1) * SAMPLE_PAD) for i in range(SAMPLE_GROUP)]
    HROWS = N_HEADS_A * SAMPLE_PAD

    s_c, s_n = [], []
    for rows in seq_rows:
        pieces = []
        for kv in range(N_KV_A):
            for g in range(Q_PER_KV):
                slab = kv * (Q_PER_KV // 2) + g // 2
                src = q if g % 2 == kv else q_swapped
                piece = src[rows, slab * LANES:(slab + 1) * LANES]
                pieces.append(jnp.where(lo8, piece, zero8) if kv == 0 else jnp.where(lo8, zero8, piece))
        lhs = jnp.concatenate(pieces, axis=0).astype(BF16)
        i = len(s_c)
        s_c.append(_mm(lhs, kc_ref[i]))
        s_n.append(_mm_nt(lhs, kn[rows]))
    s_c = jnp.concatenate(s_c, axis=0)
    s_n = jnp.concatenate(s_n, axis=0)
    n_rows = SAMPLE_GROUP * HROWS
    tok_c = lax.broadcasted_iota(jnp.int32, (n_rows, WINDOW), 0) % SAMPLE_PAD
    s_c = jnp.where(lax.broadcasted_iota(jnp.int32, (n_rows, WINDOW), 1) >= tok_c, s_c, NEG_BIG)
    tok_n = lax.broadcasted_iota(jnp.int32, (n_rows, SAMPLE_PAD), 0) % SAMPLE_PAD
    col_n = lax.broadcasted_iota(jnp.int32, (n_rows, SAMPLE_PAD), 1)
    s_n = jnp.where((col_n <= tok_n) & (col_n < L), s_n, NEG_BIG)
    sk = jnp.concatenate([jnp.full((SAMPLE_PAD, 1), sinks_ref[h], F32) for h in range(N_HEADS_A)] * SAMPLE_GROUP,
                         axis=0)
    m = jnp.maximum(jnp.maximum(jnp.max(s_c, axis=-1, keepdims=True), jnp.max(s_n, axis=-1, keepdims=True)), sk)
    e_c = jnp.exp(s_c - m)
    e_n = jnp.exp(s_n - m)
    inv_den = 1.0 / (jnp.sum(e_c, axis=-1, keepdims=True) + jnp.sum(e_n, axis=-1, keepdims=True) + jnp.exp(sk - m))
    e_c = e_c.astype(BF16)
    e_n = e_n.astype(BF16)
    o = jnp.concatenate([_mm_nt(e_c[i * HROWS:(i + 1) * HROWS], vc_ref[i]) + _mm(e_n[i * HROWS:(i + 1) * HROWS], vn[rows])
                         for i, rows in enumerate(seq_rows)], axis=0) * inv_den
    o_swapped = pltpu.roll(o, HEAD_DIM_A, axis=1)
    a_rows = []
    for i in range(SAMPLE_GROUP):
        blk = lambda arr, kv, g: arr[i * HROWS + (kv * Q_PER_KV + g) * SAMPLE_PAD:
                                     i * HROWS + (kv * Q_PER_KV + g + 1) * SAMPLE_PAD]
        slabs = []
        for kv in range(N_KV_A):
            for j in range(Q_PER_KV // 2):
                first = blk(o if kv == 0 else o_swapped, kv, 2 * j)
                second = blk(o_swapped if kv == 0 else o, kv, 2 * j + 1)
                slabs.append(jnp.where(lo8, first, second))
        a_rows.append(jnp.concatenate(slabs, axis=1))
    aout_ref[...] = _gather_rows(jnp.concatenate(a_rows, axis=0))

    key = lax.broadcasted_iota(jnp.int32, (KV_WIDTH, WINDOW), 1)
    for cache_ref, new, out_ref in ((kc_ref, kn, nk_ref), (vc_ref, vn, nv_ref)):
        new_t = new.T
        for i in range(SAMPLE_GROUP):
            kept = pltpu.roll(cache_ref[i], WINDOW - L, axis=1)
            fresh = pltpu.roll(new_t, (WINDOW - L - i * SAMPLE_PAD) % WINDOW, axis=1)
            out_ref[i] = jnp.where(key >= WINDOW - L, fresh, kept)

    row_r = lax.broadcasted_iota(jnp.int32, (R, D_STATE), 0)
    heads_per_group = N_HEADS_M // N_GROUPS_M
    yoff = []
    for i, rows in enumerate(seq_rows):
        state = st_ref[i]
        state_b = state.astype(BF16)
        a_tot = a_cs[i * SAMPLE_PAD + L - 1:i * SAMPLE_PAD + L, :]
        in_seq = (row_r >= i * SAMPLE_PAD) & (row_r < (i + 1) * SAMPLE_PAD)
        yo = []
        for g in range(N_GROUPS_M):
            gs = slice(g * GROUP_WIDTH, (g + 1) * GROUP_WIDTH)
            yo.append(_mm_nt(cm[rows, g * D_STATE:(g + 1) * D_STATE], state_b[gs]))
            bsel = jnp.where(in_seq, bm[:, g * D_STATE:(g + 1) * D_STATE], 0.0)
            upd = _mm(xd_t[gs, :], bsel)
            for hh in range(heads_per_group):
                h = g * heads_per_group + hh
                hs = slice(h * HEAD_DIM_M, (h + 1) * HEAD_DIM_M)
                decay = jnp.exp(jnp.broadcast_to(a_tot[:, h:h + 1], (HEAD_DIM_M, D_STATE)))
                nst_ref[i, hs, :] = state[hs] * decay + upd[hh * HEAD_DIM_M:(hh + 1) * HEAD_DIM_M]
        yoff.append(jnp.concatenate(yo, axis=1))
    yoff = jnp.concatenate(yoff, axis=0)

    y = (y + yoff * jnp.exp(acs_e)) * _silu(_spread_rows(z_ref[...]))
    outs = []
    for g in range(N_GROUPS_M):
        yg = y[:, g * GROUP_WIDTH:(g + 1) * GROUP_WIDTH]
        outs.append(yg * lax.rsqrt(jnp.mean(yg * yg, axis=-1, keepdims=True) + EPS))
    mout_ref[...] = _gather_rows(jnp.concatenate(outs, axis=1) * ssmn_ref[...])


def _sample_mixer(sinks, u, dtraw, cprev, kc, vc, st, convw, convb, dtb, alog, dskip, ssmn):
    nseq = kc.shape[0]
    assert nseq % SAMPLE_GROUP == 0
    R = SAMPLE_GROUP * SAMPLE_PAD
    T4 = SAMPLE_GROUP * 4
    rows = nseq * SAMPLE_PAD
    consts = (convw, convb, dtb, alog, dskip, ssmn)
    row_spec = lambda w: pl.BlockSpec((R, w), lambda i: (i, 0))
    tok_spec = lambda w: pl.BlockSpec((T4, w), lambda i: (i, 0))
    seq_spec = lambda a, b: pl.BlockSpec((SAMPLE_GROUP, a, b), lambda i: (i, 0, 0))
    in_specs = ([pl.BlockSpec(memory_space=pltpu.SMEM),
                 tok_spec(N_IN_PIECES * PIECE), tok_spec(LANES), row_spec(CONV_DIM),
                 seq_spec(WINDOW, KV_WIDTH), seq_spec(WINDOW, KV_WIDTH), seq_spec(D_INNER, D_STATE)]
                + [_const_spec(a.shape) for a in consts])
    out_shape = (jax.ShapeDtypeStruct((nseq * 4, ATTN_WIDTH), F32), jax.ShapeDtypeStruct((nseq * 4, D_INNER), F32),
                 jax.ShapeDtypeStruct((nseq, WINDOW, KV_WIDTH), F32),
                 jax.ShapeDtypeStruct((nseq, WINDOW, KV_WIDTH), F32),
                 jax.ShapeDtypeStruct((rows, CONV_DIM), F32),
                 jax.ShapeDtypeStruct((nseq, D_INNER, D_STATE), F32))
    out_specs = (tok_spec(ATTN_WIDTH), tok_spec(D_INNER), seq_spec(WINDOW, KV_WIDTH), seq_spec(WINDOW, KV_WIDTH),
                 row_spec(CONV_DIM), seq_spec(D_INNER, D_STATE))
    return pl.pallas_call(
        _sample_mixer_kernel, grid=(nseq // SAMPLE_GROUP,), in_specs=in_specs, out_specs=out_specs,
        out_shape=out_shape, name="sample_mixer",
        compiler_params=pltpu.CompilerParams(dimension_semantics=("arbitrary",),
                                             vmem_limit_bytes=VMEM_LIMIT),
    )(sinks, u, dtraw, cprev, kc, vc, st, *consts)


N_OUT_PIECES = D_MODEL // PIECE
N_FF_BLOCKS = D_FF // FF_BLOCK


def _sample_post_kernel(x_ref, a_ref, m_ref, hnorm_ref, wga_ref, wgb_ref, woa_ref, wob_ref, wo_ref,
                        norm2_ref, wup_ref, wdown_ref, fnorm_ref, y_ref, merged_s, x1_s, hm_s, acc_s):
    s = pl.program_id(0)

    @pl.when(s < N_OUT_PIECES)
    def _():
        hb = hnorm_ref[...]
        piece = (_sigmoid(_mm(hb, wga_ref[0])) * _mm(a_ref[...], woa_ref[0])
                 + _sigmoid(_mm(hb, wgb_ref[0])) * _mm(m_ref[...], wob_ref[0]))
        for p in range(N_OUT_PIECES):
            @pl.when(s == p)
            def _():
                merged_s[:, _piece_cols(p)] = piece.astype(BF16)

    @pl.when((s >= N_OUT_PIECES) & (s < 2 * N_OUT_PIECES))
    def _():
        piece = _mm(merged_s[...], wo_ref[0])
        for p in range(N_OUT_PIECES):
            @pl.when(s == N_OUT_PIECES + p)
            def _():
                x1_s[:, _piece_cols(p)] = x_ref[:, _piece_cols(p)] + piece

    @pl.when(s == 2 * N_OUT_PIECES)
    def _():
        x1 = x1_s[...]
        hm_s[...] = _rms(x1, norm2_ref[...]).astype(BF16)
        acc_s[...] = x1

    @pl.when(s >= 2 * N_OUT_PIECES)
    def _():
        h = jnp.square(jnp.maximum(_mm(hm_s[...], wup_ref[...]), 0.0))
        acc_s[...] += _mm(h, wdown_ref[...])

    @pl.when(s == 2 * N_OUT_PIECES + N_FF_BLOCKS - 1)
    def _():
        y_ref[...] = _rms(acc_s[...], fnorm_ref[...])


def _sample_post(x, a_out, m_out, hnorm, wgate, woa, wob, wo, norm2, wup, wdown, fnorm):
    rows = x.shape[0]
    first = lambda s: jnp.minimum(s, N_OUT_PIECES - 1)
    second = lambda s: jnp.clip(s - N_OUT_PIECES, 0, N_OUT_PIECES - 1)
    ff = lambda s: jnp.clip(s - 2 * N_OUT_PIECES, 0, N_FF_BLOCKS - 1)
    piece_spec = lambda w, index: pl.BlockSpec((1,) + w.shape[1:], lambda s: (index(s), 0, 0))
    in_specs = [_const_spec(x.shape), _const_spec(a_out.shape), _const_spec(m_out.shape), _const_spec(hnorm.shape),
                piece_spec(wgate, first), piece_spec(wgate, lambda s: N_OUT_PIECES + first(s)),
                piece_spec(woa, first), piece_spec(wob, first), piece_spec(wo, second),
                _const_spec(norm2.shape),
                pl.BlockSpec((D_MODEL, FF_BLOCK), lambda s: (0, ff(s))),
                pl.BlockSpec((FF_BLOCK, D_MODEL), lambda s: (ff(s), 0)),
                _const_spec(fnorm.shape)]
    return pl.pallas_call(
        _sample_post_kernel, grid=(2 * N_OUT_PIECES + N_FF_BLOCKS,), in_specs=in_specs,
        out_specs=pl.BlockSpec((rows, D_MODEL), lambda s: (0, 0)),
        out_shape=jax.ShapeDtypeStruct((rows, D_MODEL), F32),
        scratch_shapes=[pltpu.VMEM((rows, D_MODEL), BF16), pltpu.VMEM((rows, D_MODEL), F32),
                        pltpu.VMEM((rows, D_MODEL), BF16), pltpu.VMEM((rows, D_MODEL), F32)],
        name="sample_post",
        compiler_params=pltpu.CompilerParams(dimension_semantics=("arbitrary",),
                                             vmem_limit_bytes=VMEM_LIMIT),
    )(x, a_out, m_out, hnorm, wgate, wgate, woa, wob, wo, norm2, wup, wdown, fnorm)


def _layer_params(norm1, w_in, sinks, conv_w, conv_b, dt_bias, a_log, d_skip, ssm_norm, w_oa, w_ob, w_o,
                  norm2):
    dt0 = QKV_WIDTH + D_INNER + CONV_DIM
    assert dt0 == N_IN_PIECES * PIECE
    pad_lanes = lambda a: jnp.pad(a, ((0, 0), (0, LANES - a.shape[1])))
    w_in_t = w_in.T
    mixer = MixerWeights(
        norm1=norm1[None, :], win=_prep_pieces_t(w_in_t, N_IN_PIECES),
        wgate=_prep_pieces_t_shifted(w_in_t, dt0 + N_HEADS_M, 2 * D_MODEL // PIECE),
        wdt=_prep_dt(w_in_t, dt0),
        convw=conv_w, convb=conv_b[None, :], dtb=pad_lanes(dt_bias[None, :]), alog=pad_lanes(a_log[None, :]),
        dskip=jnp.repeat(d_skip, HEAD_DIM_M)[None, :], ssmn=ssm_norm[None, :],
        woa=_prep_pieces(w_oa), wob=_prep_pieces(w_ob), wo=_prep_pieces(w_o))
    return dict(sinks=sinks.astype(F32), mixer=mixer, norm2=norm2[None, :])


def kernel(x_prompt, x_sample, cache_swa_k, cache_swa_v, state_conv, state_ssm, norm1, w_in, sinks, conv_w,
           conv_b, dt_bias, a_log, d_skip, ssm_norm, w_oa, w_ob, w_o, norm2, w_up, w_down, final_norm):
    depth = w_in.shape[0]
    assert depth == 1
    nb, seq, _ = x_prompt.shape
    ns, ls, _ = x_sample.shape
    assert ls == 4
    p = _layer_params(norm1[0], w_in[0], sinks[0], conv_w[0], conv_b[0], dt_bias[0], a_log[0], d_skip[0],
                      ssm_norm[0], w_oa[0], w_ob[0], w_o[0], norm2[0])
    fnorm = final_norm[None, :]
    mw = p["mixer"]
    mixer_consts = (mw.convw, mw.convb, mw.dtb, mw.alog, mw.dskip, mw.ssmn)

    x1p, pk, pv, pc, pst, wup, wdown = _prompt_mixer(x_prompt, p["sinks"], mw, w_up[0], w_down[0])
    y_prompt = _mlp(x1p, p["norm2"], wup, wdown, fnorm)
    y_prompt = y_prompt.reshape(nb, seq, D_MODEL)

    xs_rows = x_sample.reshape(ns * ls, D_MODEL)
    dim_major = lambda a: jnp.swapaxes(a, 1, 2)
    cprev = jnp.pad(state_conv[0], ((0, 0), (SAMPLE_PAD - (CONV_W - 1), 0), (0, 0)))
    cprev = cprev.reshape(ns * SAMPLE_PAD, CONV_DIM)
    u, dtraw, hnorm = _in_proj(xs_rows, mw.norm1, mw.win, mw.wdt)
    a_out, m_out, sk, sv, sc, sst = _sample_mixer(
        p["sinks"], u, dtraw, cprev,
        dim_major(cache_swa_k[0].reshape(ns, WINDOW, KV_WIDTH)), dim_major(cache_swa_v[0].reshape(ns, WINDOW, KV_WIDTH)),
        state_ssm[0].reshape(ns, D_INNER, D_STATE), *mixer_consts)
    sk, sv = dim_major(sk), dim_major(sv)
    y_sample = _sample_post(xs_rows, a_out, m_out, hnorm, mw.wgate, mw.woa, mw.wob, mw.wo,
                            p["norm2"], wup, wdown, fnorm).reshape(ns, ls, D_MODEL)
    sc = sc.reshape(ns, SAMPLE_PAD, CONV_DIM)[:, :CONV_W - 1]

    kv_shape = (1, -1, WINDOW, N_KV_A, HEAD_DIM_A)
    ssm_shape = (1, -1, N_HEADS_M, HEAD_DIM_M, D_STATE)
    return (y_prompt, y_sample,
            pk.reshape(kv_shape), pv.reshape(kv_shape), pc[None], pst.reshape(ssm_shape),
            sk.reshape(kv_shape), sv.reshape(kv_shape), sc[None], sst.reshape(ssm_shape))
```

```python
import collections
import functools

import jax
import jax.numpy as jnp
from jax import lax
from jax.experimental import pallas as pl
from jax.experimental.pallas import tpu as pltpu

F32 = jnp.float32
BF16 = jnp.bfloat16

D_MODEL = 1024
N_HEADS_A = 8
N_KV_A = 2
Q_PER_KV = N_HEADS_A // N_KV_A
HEAD_DIM_A = 64
WINDOW = 128
ATTN_WIDTH = N_HEADS_A * HEAD_DIM_A
KV_WIDTH = N_KV_A * HEAD_DIM_A
D_INNER = 1024
HEAD_DIM_M = 64
N_HEADS_M = D_INNER // HEAD_DIM_M
N_GROUPS_M = 2
GROUP_WIDTH = D_INNER // N_GROUPS_M
D_STATE = 128
CONV_W = 4
CONV_DIM = D_INNER + 2 * N_GROUPS_M * D_STATE
CHUNK = 128
D_FF = 4 * D_MODEL
EPS = 1e-6

LANES = 128
SUBLANES = 8
QKV_WIDTH = ATTN_WIDTH + 2 * KV_WIDTH
NEG_BIG = -1e30
VMEM_LIMIT = 56 * 1024 * 1024
SAMPLE_PAD = SUBLANES
SAMPLE_GROUP = 16


def _mm(a, b):
    return jnp.dot(a.astype(BF16), b.astype(BF16), preferred_element_type=F32)


def _mm_nt(a, b):
    return lax.dot_general(a.astype(BF16), b.astype(BF16), (((1,), (1,)), ((), ())),
                           preferred_element_type=F32)


def _rms(x, w):
    return x * lax.rsqrt(jnp.mean(x * x, axis=-1, keepdims=True) + EPS) * w


def _sigmoid(x):
    return 0.5 + 0.5 * jnp.tanh(0.5 * x)


def _silu(x):
    h = 0.5 * x
    return h + h * jnp.tanh(h)


def _softplus(x):
    return jnp.maximum(x, 0.0) + jnp.log(1.0 + jnp.exp(-jnp.abs(x)))


def _lane_lo(shape):
    return (lax.broadcasted_iota(jnp.int32, shape, len(shape) - 1) % LANES) < HEAD_DIM_A


def _dup_half(x, lo, first):
    xr = pltpu.roll(x, HEAD_DIM_A, axis=1)
    return jnp.where(lo, x, xr) if first else jnp.where(lo, xr, x)


def _stack_heads(q, kv, lo):
    qa = q[:, kv * 2 * LANES: kv * 2 * LANES + LANES]
    qb = q[:, kv * 2 * LANES + LANES: (kv + 1) * 2 * LANES]
    zero = jnp.zeros_like(qa)
    return jnp.concatenate([jnp.where(lo, qa, zero), jnp.where(lo, zero, qa),
                            jnp.where(lo, qb, zero), jnp.where(lo, zero, qb)], axis=0)


def _unstack_heads(o, rows, lo):
    return jnp.concatenate([jnp.where(lo, o[0:rows], o[rows:2 * rows]),
                            jnp.where(lo, o[2 * rows:3 * rows], o[3 * rows:4 * rows])], axis=1)


ProjBufs = collections.namedtuple("ProjBufs", "qkv z xbc dt")
MixedBufs = collections.namedtuple("MixedBufs", "attn ssd")
Carry = collections.namedtuple("Carry", "kprev vprev conv_tail ht dt_buf")
MixerWeights = collections.namedtuple(
    "MixerWeights", "norm1 win wgate wdt convw convb dtb alog dskip ssmn woa wob wo")

PIECE = 256
QKV_PIECE0 = 0
Z_PIECE0 = QKV_PIECE0 + QKV_WIDTH // PIECE
XBC_PIECE0 = Z_PIECE0 + D_INNER // PIECE
N_IN_PIECES = XBC_PIECE0 + CONV_DIM // PIECE
PREP_PIECES = 4


def _prep_pieces_kernel(w_ref, o_ref):
    for j in range(PREP_PIECES):
        o_ref[j] = w_ref[:, j * PIECE:(j + 1) * PIECE].astype(BF16)


def _prep_pieces(w, n_pieces=None, k_block=None):
    k_dim, n_dim = w.shape
    n_pieces = n_dim // PIECE if n_pieces is None else n_pieces
    k_block = k_dim if k_block is None else k_block
    cols = PREP_PIECES * PIECE
    return pl.pallas_call(
        _prep_pieces_kernel, grid=(pl.cdiv(n_pieces, PREP_PIECES), k_dim // k_block),
        in_specs=[pl.BlockSpec((k_block, cols), lambda p, k: (k, p))],
        out_specs=pl.BlockSpec((PREP_PIECES, k_block, PIECE), lambda p, k: (p, k, 0)),
        out_shape=jax.ShapeDtypeStruct((n_pieces, k_dim, PIECE), BF16), name="prep_weight_pieces",
        compiler_params=pltpu.CompilerParams(dimension_semantics=("arbitrary", "arbitrary"),
                                             vmem_limit_bytes=VMEM_LIMIT),
    )(w)


def _prep_pieces_t_kernel(wt_ref, o_ref):
    for j in range(PREP_PIECES):
        o_ref[j] = wt_ref[j * PIECE:(j + 1) * PIECE, :].T.astype(BF16)


def _prep_pieces_t(wt, n_pieces=None):
    n_dim, k_dim = wt.shape
    n_pieces = n_dim // PIECE if n_pieces is None else n_pieces
    return pl.pallas_call(
        _prep_pieces_t_kernel, grid=(pl.cdiv(n_pieces, PREP_PIECES),),
        in_specs=[pl.BlockSpec((PREP_PIECES * PIECE, k_dim), lambda p: (p, 0))],
        out_specs=pl.BlockSpec((PREP_PIECES, k_dim, PIECE), lambda p: (p, 0, 0)),
        out_shape=jax.ShapeDtypeStruct((n_pieces, k_dim, PIECE), BF16), name="prep_weight_pieces_t",
        compiler_params=pltpu.CompilerParams(dimension_semantics=("arbitrary",),
                                             vmem_limit_bytes=VMEM_LIMIT),
    )(wt)


def _prep_pieces_t_shifted_kernel(a_ref, b_ref, o_ref, *, shift):
    rows = jnp.concatenate([a_ref[shift:, :], b_ref[:shift, :]], axis=0)
    o_ref[0] = rows.T.astype(BF16)


def _prep_pieces_t_shifted(wt, row0, n_pieces):
    k_dim = wt.shape[1]
    block0, shift = divmod(row0, PIECE)
    assert shift % SUBLANES == 0 and shift > 0
    return pl.pallas_call(
        functools.partial(_prep_pieces_t_shifted_kernel, shift=shift), grid=(n_pieces,),
        in_specs=[pl.BlockSpec((PIECE, k_dim), lambda p: (block0 + p, 0)),
                  pl.BlockSpec((PIECE, k_dim), lambda p: (block0 + p + 1, 0))],
        out_specs=pl.BlockSpec((1, k_dim, PIECE), lambda p: (p, 0, 0)),
        out_shape=jax.ShapeDtypeStruct((n_pieces, k_dim, PIECE), BF16), name="prep_weight_pieces_t_shifted",
        compiler_params=pltpu.CompilerParams(dimension_semantics=("arbitrary",),
                                             vmem_limit_bytes=VMEM_LIMIT),
    )(wt, wt)


def _prep_dt_kernel(wt_ref, o_ref):
    rows = wt_ref[...]
    keep = lax.broadcasted_iota(jnp.int32, rows.shape, 0) < N_HEADS_M
    o_ref[...] = jnp.where(keep, rows, 0.0).T.astype(BF16)


def _prep_dt(wt, row0):
    k_dim = wt.shape[1]
    assert row0 % LANES == 0
    return pl.pallas_call(
        _prep_dt_kernel, grid=(1,),
        in_specs=[pl.BlockSpec((LANES, k_dim), lambda i: (row0 // LANES, 0))],
        out_specs=pl.BlockSpec((k_dim, LANES), lambda i: (0, 0)),
        out_shape=jax.ShapeDtypeStruct((k_dim, LANES), BF16), name="prep_weight_dt",
        compiler_params=pltpu.CompilerParams(dimension_semantics=("arbitrary",), vmem_limit_bytes=VMEM_LIMIT),
    )(wt)


StateOuts = collections.namedtuple("StateOuts", "k v conv ssm")


def _split3_bf16(x):
    hi = x.astype(BF16)
    r1 = x - hi.astype(F32)
    mid = r1.astype(BF16)
    return hi, mid, (r1 - mid.astype(F32)).astype(BF16)


def _piece_cols(p):
    return slice(p * PIECE, (p + 1) * PIECE)


def _in_proj_pieces(hb, w, dsts):
    for dst, first in zip(dsts, (QKV_PIECE0, Z_PIECE0, XBC_PIECE0)):
        for p in range(dst.shape[1] // PIECE):
            dst[:, _piece_cols(p)] = _mm(hb, w.win[first + p])
            yield


def _out_proj_pieces(x_ref, rows, hb, attn_ref, ssd_ref, w, out_ref):
    merged = []
    for half, (src_ref, w_ref) in enumerate(((attn_ref, w.woa), (ssd_ref, w.wob))):
        parts = []
        for p in range(D_MODEL // PIECE):
            gate = _sigmoid(_mm(hb, w.wgate[half * (D_MODEL // PIECE) + p]))
            yield
            parts.append(gate * _mm(src_ref[...], w_ref[p]))
            yield
        merged.append(jnp.concatenate(parts, axis=1))
    merged = (merged[0] + merged[1]).astype(BF16)
    for p in range(D_MODEL // PIECE):
        out_ref[rows, _piece_cols(p)] = x_ref[rows, _piece_cols(p)] + _mm(merged, w.wo[p])
        yield


def _stage_in_proj(x_ref, rows, w, proj, hnorm_ref, other=None):
    x = x_ref[rows, :] if other is None else jnp.where(other[0], other[1][rows, :], x_ref[rows, :])
    hb = _rms(x, w.norm1[...]).astype(BF16)
    hnorm_ref[...] = hb
    yield
    yield from _in_proj_pieces(hb, w, (proj.qkv, proj.z, proj.xbc))
    proj.dt[...] = _mm(hb, w.wdt[...])
    yield


def _stage_out_proj(x_ref, rows, mixed, w, x1_ref, hnorm_ref):
    hb = hnorm_ref[...]
    yield
    yield from _out_proj_pieces(x_ref, rows, hb, mixed.attn, mixed.ssd, w, x1_ref)


def _alternate(*gens):
    gens = list(gens)
    while gens:
        for g in list(gens):
            try:
                next(g)
                yield
            except StopIteration:
                gens.remove(g)


def _trace_interleaved(primary, filler):
    for _ in primary:
        next(filler, None)
    for _ in filler:
        pass


def _stage_mix(proj, mixed, rows, carry, w, sinks_ref, first):
    T = CHUNK
    reset = (lambda a: a) if first is None else (lambda a: jnp.where(first, 0.0, a))
    lo = _lane_lo((T, LANES))
    lo2 = _lane_lo((2 * T, LANES))
    rm = lax.broadcasted_iota(jnp.int32, (T, 1), 0) % SUBLANES
    rr = lax.broadcasted_iota(jnp.int32, (T, T), 0)
    cc = lax.broadcasted_iota(jnp.int32, (T, T), 1)
    tril = cc <= rr
    pairs_per_group = N_HEADS_M // 2 // N_GROUPS_M
    n_pairs = N_HEADS_M // 2
    gcols = lambda g: slice(g * GROUP_WIDTH, (g + 1) * GROUP_WIDTH)

    def conv_silu(cols):
        xr = proj.xbc[rows, cols]
        prev_tile = jnp.concatenate([reset(carry.conv_tail[:, cols]), xr[:T - SUBLANES]], axis=0)
        carry.conv_tail[:, cols] = xr[T - SUBLANES:]
        yc = xr * w.convw[CONV_W - 1:CONV_W, cols]
        for shift in range(1, CONV_W):
            shifted = _tile_roll(jnp.where(rm >= SUBLANES - shift, prev_tile, xr), shift)
            yc = yc + shifted * w.convw[CONV_W - 1 - shift:CONV_W - shift, cols]
        return _silu(yc + w.convb[:, cols])

    def softmax(s, kv):
        r = lax.broadcasted_iota(jnp.int32, (T, 2 * T), 0)
        col = lax.broadcasted_iota(jnp.int32, (T, 2 * T), 1)
        valid = (col >= r) & (col <= r + WINDOW)
        if first is not None:
            valid = valid & (col >= jnp.where(first, T, 0))
        es, inv = [], []
        for g in range(Q_PER_KV):
            sk = sinks_ref[kv * Q_PER_KV + g]
            sg = jnp.where(valid, s[g * T:(g + 1) * T], NEG_BIG)
            m = jnp.maximum(jnp.max(sg, axis=-1, keepdims=True), sk)
            e = jnp.exp(sg - m)
            es.append(e.astype(BF16))
            inv.append(1.0 / (jnp.sum(e, axis=-1, keepdims=True) + jnp.exp(sk - m)))
        return jnp.concatenate(es, axis=0), jnp.concatenate(inv, axis=0)

    qkv = proj.qkv[rows, :]
    q = qkv[:, :ATTN_WIDTH] * (HEAD_DIM_A ** -0.5)
    k = qkv[:, ATTN_WIDTH:ATTN_WIDTH + KV_WIDTH]
    v = qkv[:, ATTN_WIDTH + KV_WIDTH:]
    kk = jnp.concatenate([carry.kprev[...], k], axis=0)
    vv = jnp.concatenate([carry.vprev[...], v], axis=0)
    carry.kprev[...] = k
    carry.vprev[...] = v
    carry.dt_buf[...] = _softplus(proj.dt[rows, :] + w.dtb[...])
    da_split = jnp.concatenate(_split3_bf16(carry.dt_buf[...] * -jnp.exp(w.alog[...])), axis=1)
    vds = [_dup_half(vv, lo2, kv == 0).astype(BF16) for kv in range(N_KV_A)]
    scores = [_mm_nt(_stack_heads(q, kv, lo), _dup_half(kk, lo2, kv == 0).astype(BF16))
              for kv in range(N_KV_A)]
    yield

    bc = conv_silu(slice(D_INNER, CONV_DIM))
    bgs = [bc[:, g * D_STATE:(g + 1) * D_STATE].astype(BF16) for g in range(N_GROUPS_M)]
    cgs = [bc[:, (N_GROUPS_M + g) * D_STATE:(N_GROUPS_M + g + 1) * D_STATE].astype(BF16)
           for g in range(N_GROUPS_M)]
    bgts = [bc[:, g * D_STATE:(g + 1) * D_STATE].T.astype(BF16) for g in range(N_GROUPS_M)]
    yield

    es0, inv0 = softmax(scores[0], 0)
    yield

    es1, inv1 = softmax(scores[1], 1)
    cbs = [_mm_nt(cgs[g], bgs[g]) for g in range(N_GROUPS_M)]
    h_prevs = [reset(carry.ht[:, gcols(g)]) for g in range(N_GROUPS_M)]
    y_offs = [_mm(cgs[g], h_prevs[g]) for g in range(N_GROUPS_M)]
    o0 = _mm(es0, vds[0])
    cs = jnp.dot(jnp.where(tril, 1.0, 0.0).astype(BF16), da_split, preferred_element_type=F32)
    yield

    a_cs = cs[:, :LANES] + cs[:, LANES:2 * LANES] + cs[:, 2 * LANES:]
    a_cs_t = a_cs.T
    xs_groups = [conv_silu(gcols(0))]
    o1 = _mm(es1, vds[1])
    yield

    xs_groups.append(conv_silu(gcols(1)))
    mixed.attn[rows, 0:2 * LANES] = _unstack_heads(o0 * inv0, T, lo).astype(BF16)
    yield

    def prep(j):
        g = j // pairs_per_group
        dt = carry.dt_buf[...]
        ws, colbs, dtbs = [], [], []
        for h in (2 * j, 2 * j + 1):
            colb = jnp.broadcast_to(a_cs[:, h:h + 1], (T, T))
            rowb = jnp.broadcast_to(a_cs_t[h:h + 1, :], (T, T))
            seg = jnp.where(tril, jnp.exp(colb - rowb), 0.0)
            ws.append((cbs[g] * seg).astype(BF16))
            colbs.append(colb)
            dtbs.append(jnp.broadcast_to(dt[:, h:h + 1], (T, LANES)))
        dt_e = jnp.where(lo, dtbs[0], dtbs[1])
        acs_e = jnp.where(lo, colbs[0], colbs[1])
        xs_j = xs_groups[g][:, (j % pairs_per_group) * LANES:(j % pairs_per_group + 1) * LANES]
        xdt = xs_j * dt_e
        zero = jnp.zeros_like(xdt)
        rhs = jnp.concatenate([jnp.where(lo, xdt, zero), jnp.where(lo, zero, xdt)], axis=0).astype(BF16)
        alast = acs_e[T - 1:T, :]
        return dict(lhs=jnp.concatenate(ws, axis=1), rhs=rhs, xs=xs_j, e_acs=jnp.exp(acs_e),
                    xd=(xdt * jnp.exp(alast - acs_e)).astype(BF16), decay=jnp.exp(alast))

    def finish(j, p, y_diag):
        g, jj = divmod(j, pairs_per_group)
        sl = slice(j * LANES, (j + 1) * LANES)
        y = y_diag + y_offs[g][:, jj * LANES:(jj + 1) * LANES] * p["e_acs"] + p["xs"] * w.dskip[:, sl]
        return y * _silu(proj.z[rows, sl])

    mixed.attn[rows, 2 * LANES:4 * LANES] = _unstack_heads(o1 * inv1, T, lo).astype(BF16)
    preps = {0: prep(0)}
    yield

    ys, y_diags = [], {}
    for j in range(n_pairs):
        if j + 1 < n_pairs:
            preps[j + 1] = prep(j + 1)
        y_diags[j] = _mm(preps[j]["lhs"], preps[j]["rhs"])
        if j >= 1:
            ys.append(finish(j - 1, preps[j - 1], y_diags.pop(j - 1)))
        if j % pairs_per_group == pairs_per_group - 1:
            g = j // pairs_per_group
            grp = [preps[i] for i in range(g * pairs_per_group, (g + 1) * pairs_per_group)]
            carry.ht[:, gcols(g)] = (h_prevs[g] * jnp.concatenate([p["decay"] for p in grp], axis=1)
                                     + _mm(bgts[g], jnp.concatenate([p["xd"] for p in grp], axis=1)))
        yield
    ys.append(finish(n_pairs - 1, preps[n_pairs - 1], y_diags.pop(n_pairs - 1)))
    yield

    m_slabs = []
    for g in range(N_GROUPS_M):
        grp = ys[g * pairs_per_group:(g + 1) * pairs_per_group]
        ssq = grp[0] * grp[0]
        for y in grp[1:]:
            ssq = ssq + y * y
        scale = lax.rsqrt(jnp.sum(ssq, axis=-1, keepdims=True) * (1.0 / GROUP_WIDTH) + EPS)
        m_slabs.extend([y * scale for y in grp])
    mixed.ssd[rows, :] = (jnp.concatenate(m_slabs, axis=1) * w.ssmn[...]).astype(BF16)


def _chain(*gens):
    for g in gens:
        yield from g


def _prompt_mixer_kernel(sinks_ref, x_in_ref, x_res_ref, x_extra_ref, *refs, chunks_per_seq, n_pairs):
    refs = list(refs)
    take = lambda n: [refs.pop(0) for _ in range(n)]
    w = MixerWeights(*take(len(MixerWeights._fields)))
    mlp_w_f32 = take(2)
    x1_ref, = take(1)
    outs = StateOuts(*take(4))
    mlp_w_bf16 = take(2)
    extra_u_ref, extra_dt_ref, extra_hnorm_ref = take(3)
    proj = (ProjBufs(*take(4)), ProjBufs(*take(4)))
    mixed = (MixedBufs(*take(2)), MixedBufs(*take(2)))
    hnorm = take(2)
    carry = Carry(*take(len(Carry._fields)))
    assert not refs
    T = CHUNK
    s = pl.program_id(0)
    every = slice(None)

    @pl.when(s == 0)
    def _():
        for ref in list(proj[1] + mixed[0] + mixed[1] + carry) + hnorm:
            ref[...] = jnp.zeros_like(ref)

    first = (s - 1) % (chunks_per_seq // 2) == 0
    draining = s >= n_pairs

    def cast_mlp_weights():
        for src, dst in zip(mlp_w_f32, mlp_w_bf16):
            dst[...] = src[...].astype(BF16)
            yield

    def step(cur):
        prv = 1 - cur
        dense = _alternate(_stage_out_proj(x_res_ref, every, mixed[cur], w, x1_ref, hnorm[cur]),
                           _stage_in_proj(x_in_ref, every, w, proj[cur], hnorm[cur], (draining, x_extra_ref)),
                           cast_mlp_weights())
        mix = _chain(_stage_mix(proj[prv], mixed[prv], slice(0, T), carry, w, sinks_ref, first),
                     _stage_mix(proj[prv], mixed[prv], slice(T, 2 * T), carry, w, sinks_ref, None))
        _trace_interleaved(mix, dense)

    for parity in (0, 1):
        pl.when(s % 2 == parity)(functools.partial(step, parity))

    @pl.when((s >= 1) & (s % (chunks_per_seq // 2) == 0))
    def _():
        outs.k[0] = carry.kprev[...]
        outs.v[0] = carry.vprev[...]
        outs.conv[0] = carry.conv_tail[SUBLANES - (CONV_W - 1):, :]
        outs.ssm[0] = carry.ht[...].T

    for parity in (0, 1):
        @pl.when(draining & (s % 2 == parity))
        def _():
            for src, first_piece in ((proj[parity].qkv, QKV_PIECE0), (proj[parity].z, Z_PIECE0),
                                     (proj[parity].xbc, XBC_PIECE0)):
                extra_u_ref[:, first_piece * PIECE:first_piece * PIECE + src.shape[1]] = src[...]
            extra_dt_ref[...] = proj[parity].dt[...]
            extra_hnorm_ref[...] = hnorm[parity][...]


def _const_spec(shape):
    return pl.BlockSpec(shape, lambda *_: (0,) * len(shape), pipeline_mode=pl.Buffered(1))


DRAIN_STEPS = 2


def _prompt_mixer(x, sinks, weights, w_up, w_down, x_extra):
    nb, seq, _ = x.shape
    assert seq % (2 * CHUNK) == 0
    chunks_per_seq = seq // CHUNK
    n_pairs = nb * chunks_per_seq // 2
    n_steps = n_pairs + DRAIN_STEPS
    pair = 2 * CHUNK
    assert x_extra.shape == (DRAIN_STEPS * pair, D_MODEL)
    extra_block = lambda s: (jnp.clip(s - n_pairs, 0, DRAIN_STEPS - 1), 0)
    extra_outs = ((N_IN_PIECES * PIECE, F32), (LANES, F32), (D_MODEL, BF16))
    consts = tuple(weights)
    cast_steps = 1 << ((n_steps // 2).bit_length() - 1)
    ff_slice = D_FF // cast_steps
    assert ff_slice % LANES == 0
    up_block = lambda s: (0, jnp.minimum(s, cast_steps - 1))
    down_block = lambda s: (jnp.clip(s - cast_steps, 0, cast_steps - 1), 0)
    seq_of_mix = lambda s: jnp.clip((2 * s - 1) // chunks_per_seq, 0, nb - 1)
    in_specs = ([pl.BlockSpec(memory_space=pltpu.SMEM),
                 pl.BlockSpec((pair, D_MODEL), lambda s: (jnp.minimum(s, n_pairs - 1), 0)),
                 pl.BlockSpec((pair, D_MODEL), lambda s: (jnp.maximum(s - 2, 0), 0)),
                 pl.BlockSpec((pair, D_MODEL), extra_block)]
                + [_const_spec(a.shape) for a in consts]
                + [pl.BlockSpec((D_MODEL, ff_slice), up_block), pl.BlockSpec((ff_slice, D_MODEL), down_block)])
    out_shape = (jax.ShapeDtypeStruct((nb * seq, D_MODEL), F32),
                 jax.ShapeDtypeStruct((nb, WINDOW, KV_WIDTH), F32),
                 jax.ShapeDtypeStruct((nb, WINDOW, KV_WIDTH), F32),
                 jax.ShapeDtypeStruct((nb, CONV_W - 1, CONV_DIM), F32),
                 jax.ShapeDtypeStruct((nb, D_INNER, D_STATE), F32),
                 jax.ShapeDtypeStruct(w_up.shape, BF16), jax.ShapeDtypeStruct(w_down.shape, BF16),
                 *[jax.ShapeDtypeStruct((x_extra.shape[0], width), dtype) for width, dtype in extra_outs])
    out_specs = (pl.BlockSpec((pair, D_MODEL), lambda s: (jnp.maximum(s - 2, 0), 0)),
                 pl.BlockSpec((1, WINDOW, KV_WIDTH), lambda s: (seq_of_mix(s), 0, 0)),
                 pl.BlockSpec((1, WINDOW, KV_WIDTH), lambda s: (seq_of_mix(s), 0, 0)),
                 pl.BlockSpec((1, CONV_W - 1, CONV_DIM), lambda s: (seq_of_mix(s), 0, 0)),
                 pl.BlockSpec((1, D_INNER, D_STATE), lambda s: (seq_of_mix(s), 0, 0)),
                 pl.BlockSpec((D_MODEL, ff_slice), up_block), pl.BlockSpec((ff_slice, D_MODEL), down_block),
                 *[pl.BlockSpec((pair, width), extra_block) for width, _ in extra_outs])
    proj_bufs = [pltpu.VMEM((pair, QKV_WIDTH), F32), pltpu.VMEM((pair, D_INNER), F32),
                 pltpu.VMEM((pair, CONV_DIM), F32), pltpu.VMEM((pair, LANES), F32)]
    mixed_bufs = [pltpu.VMEM((pair, ATTN_WIDTH), BF16), pltpu.VMEM((pair, D_INNER), BF16)]
    hnorm_bufs = [pltpu.VMEM((pair, D_MODEL), BF16)] * 2
    carry = [pltpu.VMEM((CHUNK, KV_WIDTH), F32), pltpu.VMEM((CHUNK, KV_WIDTH), F32),
             pltpu.VMEM((SUBLANES, CONV_DIM), F32), pltpu.VMEM((D_STATE, D_INNER), F32),
             pltpu.VMEM((CHUNK, LANES), F32)]
    x_rows = x.reshape(nb * seq, D_MODEL)
    return pl.pallas_call(
        functools.partial(_prompt_mixer_kernel, chunks_per_seq=chunks_per_seq, n_pairs=n_pairs),
        grid=(n_steps,), in_specs=in_specs, out_specs=out_specs,
        out_shape=out_shape, scratch_shapes=proj_bufs * 2 + mixed_bufs * 2 + hnorm_bufs + carry,
        name="prompt_mixer",
        compiler_params=pltpu.CompilerParams(dimension_semantics=("arbitrary",),
                                             vmem_limit_bytes=VMEM_LIMIT),
    )(sinks, x_rows, x_rows, x_extra, *consts, w_up, w_down)


MLP_TILE = 1024
FF_BLOCK = 1024


def _mlp_kernel(x_ref, norm2_ref, wup_ref, wdown_ref, fnorm_ref, y_ref):
    x = x_ref[...]
    hm = _rms(x, norm2_ref[...]).astype(BF16)
    acc = x
    for j in range(D_FF // FF_BLOCK):
        h = _mm(hm, wup_ref[:, j * FF_BLOCK:(j + 1) * FF_BLOCK])
        h = jnp.square(jnp.maximum(h, 0.0))
        acc = acc + _mm(h, wdown_ref[j * FF_BLOCK:(j + 1) * FF_BLOCK, :])
    y_ref[...] = _rms(acc, fnorm_ref[...])


def _mlp(x, norm2, wup, wdown, fnorm):
    rows = x.shape[0]
    tile = min(MLP_TILE, rows)
    assert rows % tile == 0 and tile % SUBLANES == 0
    consts = (norm2, wup, wdown, fnorm)
    return pl.pallas_call(
        _mlp_kernel, grid=(rows // tile,),
        in_specs=[pl.BlockSpec((tile, D_MODEL), lambda i: (i, 0))] + [_const_spec(a.shape) for a in consts],
        out_specs=pl.BlockSpec((tile, D_MODEL), lambda i: (i, 0)),
        out_shape=jax.ShapeDtypeStruct((rows, D_MODEL), F32), name="mlp",
        compiler_params=pltpu.CompilerParams(dimension_semantics=("arbitrary",),
                                             vmem_limit_bytes=VMEM_LIMIT),
    )(x, *consts)


def _tile_roll(x, shift):
    rows, width = x.shape
    tiles = x.reshape(rows // SUBLANES, SUBLANES, width)
    return pltpu.roll(tiles, shift % SUBLANES, axis=1).reshape(rows, width)


def _spread_rows(x):
    tiles = []
    for t in range(x.shape[0] // SUBLANES):
        two_seqs = x[t * SUBLANES:(t + 1) * SUBLANES]
        tiles += [two_seqs, pltpu.roll(two_seqs, SUBLANES // 2, axis=0)]
    return jnp.concatenate(tiles, axis=0)


def _gather_rows(y):
    low = lax.broadcasted_iota(jnp.int32, (SUBLANES, 1), 0) < SUBLANES // 2
    tiles = []
    for t in range(y.shape[0] // (2 * SUBLANES)):
        a = y[2 * t * SUBLANES:(2 * t + 1) * SUBLANES]
        b = y[(2 * t + 1) * SUBLANES:(2 * t + 2) * SUBLANES]
        tiles.append(jnp.where(low, a, pltpu.roll(b, SUBLANES // 2, axis=0)))
    return jnp.concatenate(tiles, axis=0)


def _expand_heads(cols, expand):
    terms = []
    for c in cols:
        hi = c.astype(BF16)
        r1 = c - hi.astype(F32)
        mid = r1.astype(BF16)
        lo = (r1 - mid.astype(F32)).astype(BF16)
        terms.extend([hi, mid, lo])
    rows = cols[0].shape[0]
    out = jnp.dot(jnp.concatenate(terms, axis=0), expand, preferred_element_type=F32)
    return [out[(3 * i) * rows:(3 * i + 1) * rows] + out[(3 * i + 1) * rows:(3 * i + 2) * rows]
            + out[(3 * i + 2) * rows:(3 * i + 3) * rows] for i in range(len(cols))]


def _sample_mixer_kernel(sinks_ref, u_ref, dtraw_ref, cprev_ref, kc_ref, vc_ref, st_ref,
                         convw_ref, convb_ref, dtb_ref, alog_ref, dskip_ref, ssmn_ref,
                         aout_ref, mout_ref, nk_ref, nv_ref, nconv_ref, nst_ref):
    R = SAMPLE_GROUP * SAMPLE_PAD
    L = 4
    qkv_ref = u_ref.at[:, QKV_PIECE0 * PIECE:Z_PIECE0 * PIECE]
    z_ref = u_ref.at[:, Z_PIECE0 * PIECE:XBC_PIECE0 * PIECE]
    xbc_ref = u_ref.at[:, XBC_PIECE0 * PIECE:N_IN_PIECES * PIECE]
    rm = lax.broadcasted_iota(jnp.int32, (R, 1), 0) % SAMPLE_PAD

    xr = xbc_raw = _spread_rows(xbc_ref[...])
    with_prev = jnp.where(rm >= SAMPLE_PAD - (CONV_W - 1), cprev_ref[...], xr)
    yc = xr * convw_ref[CONV_W - 1:CONV_W, :]
    for kshift in range(1, CONV_W):
        yc = yc + _tile_roll(with_prev, kshift) * convw_ref[CONV_W - 1 - kshift:CONV_W - kshift, :]
    nconv_ref[...] = _tile_roll(xbc_raw, -1)
    xbc = _silu(yc + convb_ref[...])
    xs = xbc[:, :D_INNER]
    bm = xbc[:, D_INNER:D_INNER + N_GROUPS_M * D_STATE]
    cm = xbc[:, D_INNER + N_GROUPS_M * D_STATE:]

    dt = _softplus(_spread_rows(dtraw_ref[...]) + dtb_ref[...])
    d_a = dt * (-jnp.exp(alog_ref[...]))
    a_cs = d_a
    suf = jnp.zeros_like(d_a)
    for kshift in range(1, L):
        a_cs = a_cs + jnp.where(rm >= kshift, _tile_roll(d_a, kshift), 0.0)
        suf = suf + jnp.where(rm <= L - 1 - kshift, _tile_roll(d_a, -kshift), 0.0)

    expand = (lax.broadcasted_iota(jnp.int32, (LANES, D_INNER), 1) // HEAD_DIM_M
              == lax.broadcasted_iota(jnp.int32, (LANES, D_INNER), 0)).astype(BF16)
    dt_e, acs_e, suf_e = _expand_heads([dt, a_cs, suf], expand)
    xdt = xs * dt_e
    real = rm < L
    xd_t = jnp.where(real, xdt * jnp.exp(suf_e), 0.0).T.astype(BF16)

    y = xs * dskip_ref[...]
    for kshift in range(L):
        bk = bm if kshift == 0 else _tile_roll(bm, kshift)
        cb = cm * bk
        cb_e = jnp.concatenate(
            [jnp.broadcast_to(jnp.sum(cb[:, g * D_STATE:(g + 1) * D_STATE], axis=-1, keepdims=True),
                              (R, GROUP_WIDTH)) for g in range(N_GROUPS_M)], axis=1)
        if kshift == 0:
            y = y + cb_e * xdt
        else:
            seg = jnp.exp(acs_e - _tile_roll(acs_e, kshift))
            y = y + jnp.where(rm >= kshift, cb_e * seg * _tile_roll(xdt, kshift), 0.0)

    qkv = _spread_rows(qkv_ref[...])
    q = qkv[:, :ATTN_WIDTH] * (HEAD_DIM_A ** -0.5)
    q_swapped = jnp.concatenate(
        [pltpu.roll(q[:, s * LANES:(s + 1) * LANES], HEAD_DIM_A, axis=1) for s in range(ATTN_WIDTH // LANES)], axis=1)
    kn = qkv[:, ATTN_WIDTH:ATTN_WIDTH + KV_WIDTH]
    vn = qkv[:, ATTN_WIDTH + KV_WIDTH:]
    lo8 = _lane_lo((SAMPLE_PAD, LANES))
    zero8 = jnp.zeros((SAMPLE_PAD, LANES), F32)
    seq_rows = [slice(i * SAMPLE_PAD, (i + 1) * SAMPLE_PAD) for i in range(SAMPLE_GROUP)]
    HROWS = N_HEADS_A * SAMPLE_PAD

    s_c, s_n = [], []
    for rows in seq_rows:
        pieces = []
        for kv in range(N_KV_A):
            for g in range(Q_PER_KV):
                slab = kv * (Q_PER_KV // 2) + g // 2
                src = q if g % 2 == kv else q_swapped
                piece = src[rows, slab * LANES:(slab + 1) * LANES]
                pieces.append(jnp.where(lo8, piece, zero8) if kv == 0 else jnp.where(lo8, zero8, piece))
        lhs = jnp.concatenate(pieces, axis=0).astype(BF16)
        i = len(s_c)
        s_c.append(_mm(lhs, kc_ref[i]))
        s_n.append(_mm_nt(lhs, kn[rows]))
    s_c = jnp.concatenate(s_c, axis=0)
    s_n = jnp.concatenate(s_n, axis=0)
    n_rows = SAMPLE_GROUP * HROWS
    tok_c = lax.broadcasted_iota(jnp.int32, (n_rows, WINDOW), 0) % SAMPLE_PAD
    s_c = jnp.where(lax.broadcasted_iota(jnp.int32, (n_rows, WINDOW), 1) >= tok_c, s_c, NEG_BIG)
    tok_n = lax.broadcasted_iota(jnp.int32, (n_rows, SAMPLE_PAD), 0) % SAMPLE_PAD
    col_n = lax.broadcasted_iota(jnp.int32, (n_rows, SAMPLE_PAD), 1)
    s_n = jnp.where((col_n <= tok_n) & (col_n < L), s_n, NEG_BIG)
    sk = jnp.concatenate([jnp.full((SAMPLE_PAD, 1), sinks_ref[h], F32) for h in range(N_HEADS_A)] * SAMPLE_GROUP,
                         axis=0)
    m = jnp.maximum(jnp.maximum(jnp.max(s_c, axis=-1, keepdims=True), jnp.max(s_n, axis=-1, keepdims=True)), sk)
    e_c = jnp.exp(s_c - m)
    e_n = jnp.exp(s_n - m)
    inv_den = 1.0 / (jnp.sum(e_c, axis=-1, keepdims=True) + jnp.sum(e_n, axis=-1, keepdims=True) + jnp.exp(sk - m))
    e_c = e_c.astype(BF16)
    e_n = e_n.astype(BF16)
    o = jnp.concatenate([_mm_nt(e_c[i * HROWS:(i + 1) * HROWS], vc_ref[i]) + _mm(e_n[i * HROWS:(i + 1) * HROWS], vn[rows])
                         for i, rows in enumerate(seq_rows)], axis=0) * inv_den
    o_swapped = pltpu.roll(o, HEAD_DIM_A, axis=1)
    a_rows = []
    for i in range(SAMPLE_GROUP):
        blk = lambda arr, kv, g: arr[i * HROWS + (kv * Q_PER_KV + g) * SAMPLE_PAD:
                                     i * HROWS + (kv * Q_PER_KV + g + 1) * SAMPLE_PAD]
        slabs = []
        for kv in range(N_KV_A):
            for j in range(Q_PER_KV // 2):
                first = blk(o if kv == 0 else o_swapped, kv, 2 * j)
                second = blk(o_swapped if kv == 0 else o, kv, 2 * j + 1)
                slabs.append(jnp.where(lo8, first, second))
        a_rows.append(jnp.concatenate(slabs, axis=1))
    aout_ref[...] = _gather_rows(jnp.concatenate(a_rows, axis=0))

    key = lax.broadcasted_iota(jnp.int32, (KV_WIDTH, WINDOW), 1)
    for cache_ref, new, out_ref in ((kc_ref, kn, nk_ref), (vc_ref, vn, nv_ref)):
        new_t = new.T
        for i in range(SAMPLE_GROUP):
            kept = pltpu.roll(cache_ref[i], WINDOW - L, axis=1)
            fresh = pltpu.roll(new_t, (WINDOW - L - i * SAMPLE_PAD) % WINDOW, axis=1)
            out_ref[i] = jnp.where(key >= WINDOW - L, fresh, kept)

    row_r = lax.broadcasted_iota(jnp.int32, (R, D_STATE), 0)
    heads_per_group = N_HEADS_M // N_GROUPS_M
    yoff = []
    for i, rows in enumerate(seq_rows):
        state = st_ref[i]
        state_b = state.astype(BF16)
        a_tot = a_cs[i * SAMPLE_PAD + L - 1:i * SAMPLE_PAD + L, :]
        in_seq = (row_r >= i * SAMPLE_PAD) & (row_r < (i + 1) * SAMPLE_PAD)
        yo = []
        for g in range(N_GROUPS_M):
            gs = slice(g * GROUP_WIDTH, (g + 1) * GROUP_WIDTH)
            yo.append(_mm_nt(cm[rows, g * D_STATE:(g + 1) * D_STATE], state_b[gs]))
            bsel = jnp.where(in_seq, bm[:, g * D_STATE:(g + 1) * D_STATE], 0.0)
            upd = _mm(xd_t[gs, :], bsel)
            for hh in range(heads_per_group):
                h = g * heads_per_group + hh
                hs = slice(h * HEAD_DIM_M, (h + 1) * HEAD_DIM_M)
                decay = jnp.exp(jnp.broadcast_to(a_tot[:, h:h + 1], (HEAD_DIM_M, D_STATE)))
                nst_ref[i, hs, :] = state[hs] * decay + upd[hh * HEAD_DIM_M:(hh + 1) * HEAD_DIM_M]
        yoff.append(jnp.concatenate(yo, axis=1))
    yoff = jnp.concatenate(yoff, axis=0)

    y = (y + yoff * jnp.exp(acs_e)) * _silu(_spread_rows(z_ref[...]))
    outs = []
    for g in range(N_GROUPS_M):
        yg = y[:, g * GROUP_WIDTH:(g + 1) * GROUP_WIDTH]
        outs.append(yg * lax.rsqrt(jnp.mean(yg * yg, axis=-1, keepdims=True) + EPS))
    mout_ref[...] = _gather_rows(jnp.concatenate(outs, axis=1) * ssmn_ref[...])


def _sample_mixer(sinks, u, dtraw, cprev, kc, vc, st, convw, convb, dtb, alog, dskip, ssmn):
    nseq = kc.shape[0]
    assert nseq % SAMPLE_GROUP == 0
    R = SAMPLE_GROUP * SAMPLE_PAD
    T4 = SAMPLE_GROUP * 4
    rows = nseq * SAMPLE_PAD
    consts = (convw, convb, dtb, alog, dskip, ssmn)
    row_spec = lambda w: pl.BlockSpec((R, w), lambda i: (i, 0))
    tok_spec = lambda w: pl.BlockSpec((T4, w), lambda i: (i, 0))
    seq_spec = lambda a, b: pl.BlockSpec((SAMPLE_GROUP, a, b), lambda i: (i, 0, 0))
    in_specs = ([pl.BlockSpec(memory_space=pltpu.SMEM),
                 tok_spec(N_IN_PIECES * PIECE), tok_spec(LANES), row_spec(CONV_DIM),
                 seq_spec(WINDOW, KV_WIDTH), seq_spec(WINDOW, KV_WIDTH), seq_spec(D_INNER, D_STATE)]
                + [_const_spec(a.shape) for a in consts])
    out_shape = (jax.ShapeDtypeStruct((nseq * 4, ATTN_WIDTH), F32), jax.ShapeDtypeStruct((nseq * 4, D_INNER), F32),
                 jax.ShapeDtypeStruct((nseq, WINDOW, KV_WIDTH), F32),
                 jax.ShapeDtypeStruct((nseq, WINDOW, KV_WIDTH), F32),
                 jax.ShapeDtypeStruct((rows, CONV_DIM), F32),
                 jax.ShapeDtypeStruct((nseq, D_INNER, D_STATE), F32))
    out_specs = (tok_spec(ATTN_WIDTH), tok_spec(D_INNER), seq_spec(WINDOW, KV_WIDTH), seq_spec(WINDOW, KV_WIDTH),
                 row_spec(CONV_DIM), seq_spec(D_INNER, D_STATE))
    return pl.pallas_call(
        _sample_mixer_kernel, grid=(nseq // SAMPLE_GROUP,), in_specs=in_specs, out_specs=out_specs,
        out_shape=out_shape, name="sample_mixer",
        compiler_params=pltpu.CompilerParams(dimension_semantics=("arbitrary",),
                                             vmem_limit_bytes=VMEM_LIMIT),
    )(sinks, u, dtraw, cprev, kc, vc, st, *consts)


N_OUT_PIECES = D_MODEL // PIECE
N_FF_BLOCKS = D_FF // FF_BLOCK


def _sample_post_kernel(x_ref, a_ref, m_ref, hnorm_ref, wga_ref, wgb_ref, woa_ref, wob_ref, wo_ref,
                        norm2_ref, wup_ref, wdown_ref, fnorm_ref, y_ref, merged_s, x1_s, hm_s, acc_s):
    s = pl.program_id(0)

    @pl.when(s < N_OUT_PIECES)
    def _():
        hb = hnorm_ref[...]
        piece = (_sigmoid(_mm(hb, wga_ref[0])) * _mm(a_ref[...], woa_ref[0])
                 + _sigmoid(_mm(hb, wgb_ref[0])) * _mm(m_ref[...], wob_ref[0]))
        for p in range(N_OUT_PIECES):
            @pl.when(s == p)
            def _():
                merged_s[:, _piece_cols(p)] = piece.astype(BF16)

    @pl.when((s >= N_OUT_PIECES) & (s < 2 * N_OUT_PIECES))
    def _():
        piece = _mm(merged_s[...], wo_ref[0])
        for p in range(N_OUT_PIECES):
            @pl.when(s == N_OUT_PIECES + p)
            def _():
                x1_s[:, _piece_cols(p)] = x_ref[:, _piece_cols(p)] + piece

    @pl.when(s == 2 * N_OUT_PIECES)
    def _():
        x1 = x1_s[...]
        hm_s[...] = _rms(x1, norm2_ref[...]).astype(BF16)
        acc_s[...] = x1

    @pl.when(s >= 2 * N_OUT_PIECES)
    def _():
        h = jnp.square(jnp.maximum(_mm(hm_s[...], wup_ref[...]), 0.0))
        acc_s[...] += _mm(h, wdown_ref[...])

    @pl.when(s == 2 * N_OUT_PIECES + N_FF_BLOCKS - 1)
    def _():
        y_ref[...] = _rms(acc_s[...], fnorm_ref[...])


def _sample_post(x, a_out, m_out, hnorm, wgate, woa, wob, wo, norm2, wup, wdown, fnorm):
    rows = x.shape[0]
    first = lambda s: jnp.minimum(s, N_OUT_PIECES - 1)
    second = lambda s: jnp.clip(s - N_OUT_PIECES, 0, N_OUT_PIECES - 1)
    ff = lambda s: jnp.clip(s - 2 * N_OUT_PIECES, 0, N_FF_BLOCKS - 1)
    piece_spec = lambda w, index: pl.BlockSpec((1,) + w.shape[1:], lambda s: (index(s), 0, 0))
    in_specs = [_const_spec(x.shape), _const_spec(a_out.shape), _const_spec(m_out.shape), _const_spec(hnorm.shape),
                piece_spec(wgate, first), piece_spec(wgate, lambda s: N_OUT_PIECES + first(s)),
                piece_spec(woa, first), piece_spec(wob, first), piece_spec(wo, second),
                _const_spec(norm2.shape),
                pl.BlockSpec((D_MODEL, FF_BLOCK), lambda s: (0, ff(s))),
                pl.BlockSpec((FF_BLOCK, D_MODEL), lambda s: (ff(s), 0)),
                _const_spec(fnorm.shape)]
    return pl.pallas_call(
        _sample_post_kernel, grid=(2 * N_OUT_PIECES + N_FF_BLOCKS,), in_specs=in_specs,
        out_specs=pl.BlockSpec((rows, D_MODEL), lambda s: (0, 0)),
        out_shape=jax.ShapeDtypeStruct((rows, D_MODEL), F32),
        scratch_shapes=[pltpu.VMEM((rows, D_MODEL), BF16), pltpu.VMEM((rows, D_MODEL), F32),
                        pltpu.VMEM((rows, D_MODEL), BF16), pltpu.VMEM((rows, D_MODEL), F32)],
        name="sample_post",
        compiler_params=pltpu.CompilerParams(dimension_semantics=("arbitrary",),
                                             vmem_limit_bytes=VMEM_LIMIT),
    )(x, a_out, m_out, hnorm, wgate, wgate, woa, wob, wo, norm2, wup, wdown, fnorm)


def _layer_params(norm1, w_in, sinks, conv_w, conv_b, dt_bias, a_log, d_skip, ssm_norm, w_oa, w_ob, w_o,
                  norm2):
    dt0 = QKV_WIDTH + D_INNER + CONV_DIM
    assert dt0 == N_IN_PIECES * PIECE
    pad_lanes = lambda a: jnp.pad(a, ((0, 0), (0, LANES - a.shape[1])))
    w_in_t = w_in.T
    mixer = MixerWeights(
        norm1=norm1[None, :], win=_prep_pieces_t(w_in_t, N_IN_PIECES),
        wgate=_prep_pieces_t_shifted(w_in_t, dt0 + N_HEADS_M, 2 * D_MODEL // PIECE),
        wdt=_prep_dt(w_in_t, dt0),
        convw=conv_w, convb=conv_b[None, :], dtb=pad_lanes(dt_bias[None, :]), alog=pad_lanes(a_log[None, :]),
        dskip=jnp.repeat(d_skip, HEAD_DIM_M)[None, :], ssmn=ssm_norm[None, :],
        woa=_prep_pieces(w_oa), wob=_prep_pieces(w_ob), wo=_prep_pieces(w_o))
    return dict(sinks=sinks.astype(F32), mixer=mixer, norm2=norm2[None, :])


def kernel(x_prompt, x_sample, cache_swa_k, cache_swa_v, state_conv, state_ssm, norm1, w_in, sinks, conv_w,
           conv_b, dt_bias, a_log, d_skip, ssm_norm, w_oa, w_ob, w_o, norm2, w_up, w_down, final_norm):
    depth = w_in.shape[0]
    assert depth == 1
    nb, seq, _ = x_prompt.shape
    ns, ls, _ = x_sample.shape
    assert ls == 4
    p = _layer_params(norm1[0], w_in[0], sinks[0], conv_w[0], conv_b[0], dt_bias[0], a_log[0], d_skip[0],
                      ssm_norm[0], w_oa[0], w_ob[0], w_o[0], norm2[0])
    fnorm = final_norm[None, :]
    mw = p["mixer"]
    mixer_consts = (mw.convw, mw.convb, mw.dtb, mw.alog, mw.dskip, mw.ssmn)

    xs_rows = x_sample.reshape(ns * ls, D_MODEL)
    x1p, pk, pv, pc, pst, wup, wdown, u, dtraw, hnorm = _prompt_mixer(
        x_prompt, p["sinks"], mw, w_up[0], w_down[0], xs_rows)
    y_prompt = _mlp(x1p, p["norm2"], wup, wdown, fnorm)
    y_prompt = y_prompt.reshape(nb, seq, D_MODEL)

    dim_major = lambda a: jnp.swapaxes(a, 1, 2)
    cprev = jnp.pad(state_conv[0], ((0, 0), (SAMPLE_PAD - (CONV_W - 1), 0), (0, 0)))
    cprev = cprev.reshape(ns * SAMPLE_PAD, CONV_DIM)
    a_out, m_out, sk, sv, sc, sst = _sample_mixer(
        p["sinks"], u, dtraw, cprev,
        dim_major(cache_swa_k[0].reshape(ns, WINDOW, KV_WIDTH)), dim_major(cache_swa_v[0].reshape(ns, WINDOW, KV_WIDTH)),
        state_ssm[0].reshape(ns, D_INNER, D_STATE), *mixer_consts)
    sk, sv = dim_major(sk), dim_major(sv)
    y_sample = _sample_post(xs_rows, a_out, m_out, hnorm, mw.wgate, mw.woa, mw.wob, mw.wo,
                            p["norm2"], wup, wdown, fnorm).reshape(ns, ls, D_MODEL)
    sc = sc.reshape(ns, SAMPLE_PAD, CONV_DIM)[:, :CONV_W - 1]

    kv_shape = (1, -1, WINDOW, N_KV_A, HEAD_DIM_A)
    ssm_shape = (1, -1, N_HEADS_M, HEAD_DIM_M, D_STATE)
    return (y_prompt, y_sample,
            pk.reshape(kv_shape), pv.reshape(kv_shape), pc[None], pst.reshape(ssm_shape),
            sk.reshape(kv_shape), sv.reshape(kv_shape), sc[None], sst.reshape(ssm_shape))
```

```python
import collections
import functools

import jax
import jax.numpy as jnp
from jax import lax
from jax.experimental import pallas as pl
from jax.experimental.pallas import tpu as pltpu

F32 = jnp.float32
BF16 = jnp.bfloat16

D_MODEL = 1024
N_HEADS_A = 8
N_KV_A = 2
Q_PER_KV = N_HEADS_A // N_KV_A
HEAD_DIM_A = 64
WINDOW = 128
ATTN_WIDTH = N_HEADS_A * HEAD_DIM_A
KV_WIDTH = N_KV_A * HEAD_DIM_A
D_INNER = 1024
HEAD_DIM_M = 64
N_HEADS_M = D_INNER // HEAD_DIM_M
N_GROUPS_M = 2
GROUP_WIDTH = D_INNER // N_GROUPS_M
D_STATE = 128
CONV_W = 4
CONV_DIM = D_INNER + 2 * N_GROUPS_M * D_STATE
CHUNK = 128
D_FF = 4 * D_MODEL
EPS = 1e-6

LANES = 128
SUBLANES = 8
QKV_WIDTH = ATTN_WIDTH + 2 * KV_WIDTH
NEG_BIG = -1e30
VMEM_LIMIT = 56 * 1024 * 1024
SAMPLE_PAD = SUBLANES
SAMPLE_GROUP = 16


def _mm(a, b):
    return jnp.dot(a.astype(BF16), b.astype(BF16), preferred_element_type=F32)


def _mm_nt(a, b):
    return lax.dot_general(a.astype(BF16), b.astype(BF16), (((1,), (1,)), ((), ())),
                           preferred_element_type=F32)


def _rms(x, w):
    return x * lax.rsqrt(jnp.mean(x * x, axis=-1, keepdims=True) + EPS) * w


def _sigmoid(x):
    return 0.5 + 0.5 * jnp.tanh(0.5 * x)


def _silu(x):
    h = 0.5 * x
    return h + h * jnp.tanh(h)


def _softplus(x):
    return jnp.maximum(x, 0.0) + jnp.log(1.0 + jnp.exp(-jnp.abs(x)))


def _lane_lo(shape):
    return (lax.broadcasted_iota(jnp.int32, shape, len(shape) - 1) % LANES) < HEAD_DIM_A


def _dup_half(x, lo, first):
    xr = pltpu.roll(x, HEAD_DIM_A, axis=1)
    return jnp.where(lo, x, xr) if first else jnp.where(lo, xr, x)


def _stack_heads(q, kv, lo):
    qa = q[:, kv * 2 * LANES: kv * 2 * LANES + LANES]
    qb = q[:, kv * 2 * LANES + LANES: (kv + 1) * 2 * LANES]
    zero = jnp.zeros_like(qa)
    return jnp.concatenate([jnp.where(lo, qa, zero), jnp.where(lo, zero, qa),
                            jnp.where(lo, qb, zero), jnp.where(lo, zero, qb)], axis=0)


def _unstack_heads(o, rows, lo):
    return jnp.concatenate([jnp.where(lo, o[0:rows], o[rows:2 * rows]),
                            jnp.where(lo, o[2 * rows:3 * rows], o[3 * rows:4 * rows])], axis=1)


ProjBufs = collections.namedtuple("ProjBufs", "qkv z xbc dt")
MixedBufs = collections.namedtuple("MixedBufs", "attn ssd")
Carry = collections.namedtuple("Carry", "kprev vprev conv_tail ht dt_buf")
MixerWeights = collections.namedtuple(
    "MixerWeights", "norm1 win wgate wdt convw convb dtb alog dskip ssmn woa wob wo")

PIECE = 256
QKV_PIECE0 = 0
Z_PIECE0 = QKV_PIECE0 + QKV_WIDTH // PIECE
XBC_PIECE0 = Z_PIECE0 + D_INNER // PIECE
N_IN_PIECES = XBC_PIECE0 + CONV_DIM // PIECE
PREP_PIECES = 4


def _prep_pieces_kernel(w_ref, o_ref):
    for j in range(PREP_PIECES):
        o_ref[j] = w_ref[:, j * PIECE:(j + 1) * PIECE].astype(BF16)


def _prep_pieces(w, n_pieces=None, k_block=None):
    k_dim, n_dim = w.shape
    n_pieces = n_dim // PIECE if n_pieces is None else n_pieces
    k_block = k_dim if k_block is None else k_block
    cols = PREP_PIECES * PIECE
    return pl.pallas_call(
        _prep_pieces_kernel, grid=(pl.cdiv(n_pieces, PREP_PIECES), k_dim // k_block),
        in_specs=[pl.BlockSpec((k_block, cols), lambda p, k: (k, p))],
        out_specs=pl.BlockSpec((PREP_PIECES, k_block, PIECE), lambda p, k: (p, k, 0)),
        out_shape=jax.ShapeDtypeStruct((n_pieces, k_dim, PIECE), BF16), name="prep_weight_pieces",
        compiler_params=pltpu.CompilerParams(dimension_semantics=("arbitrary", "arbitrary"),
                                             vmem_limit_bytes=VMEM_LIMIT),
    )(w)


def _prep_pieces_t_kernel(wt_ref, o_ref):
    for j in range(PREP_PIECES):
        o_ref[j] = wt_ref[j * PIECE:(j + 1) * PIECE, :].T.astype(BF16)


def _prep_pieces_t(wt, n_pieces=None):
    n_dim, k_dim = wt.shape
    n_pieces = n_dim // PIECE if n_pieces is None else n_pieces
    return pl.pallas_call(
        _prep_pieces_t_kernel, grid=(pl.cdiv(n_pieces, PREP_PIECES),),
        in_specs=[pl.BlockSpec((PREP_PIECES * PIECE, k_dim), lambda p: (p, 0))],
        out_specs=pl.BlockSpec((PREP_PIECES, k_dim, PIECE), lambda p: (p, 0, 0)),
        out_shape=jax.ShapeDtypeStruct((n_pieces, k_dim, PIECE), BF16), name="prep_weight_pieces_t",
        compiler_params=pltpu.CompilerParams(dimension_semantics=("arbitrary",),
                                             vmem_limit_bytes=VMEM_LIMIT),
    )(wt)


def _prep_pieces_t_shifted_kernel(a_ref, b_ref, o_ref, *, shift):
    rows = jnp.concatenate([a_ref[shift:, :], b_ref[:shift, :]], axis=0)
    o_ref[0] = rows.T.astype(BF16)


def _prep_pieces_t_shifted(wt, row0, n_pieces):
    k_dim = wt.shape[1]
    block0, shift = divmod(row0, PIECE)
    assert shift % SUBLANES == 0 and shift > 0
    return pl.pallas_call(
        functools.partial(_prep_pieces_t_shifted_kernel, shift=shift), grid=(n_pieces,),
        in_specs=[pl.BlockSpec((PIECE, k_dim), lambda p: (block0 + p, 0)),
                  pl.BlockSpec((PIECE, k_dim), lambda p: (block0 + p + 1, 0))],
        out_specs=pl.BlockSpec((1, k_dim, PIECE), lambda p: (p, 0, 0)),
        out_shape=jax.ShapeDtypeStruct((n_pieces, k_dim, PIECE), BF16), name="prep_weight_pieces_t_shifted",
        compiler_params=pltpu.CompilerParams(dimension_semantics=("arbitrary",),
                                             vmem_limit_bytes=VMEM_LIMIT),
    )(wt, wt)


def _prep_dt_kernel(wt_ref, o_ref):
    rows = wt_ref[...]
    keep = lax.broadcasted_iota(jnp.int32, rows.shape, 0) < N_HEADS_M
    o_ref[...] = jnp.where(keep, rows, 0.0).T.astype(BF16)


def _prep_dt(wt, row0):
    k_dim = wt.shape[1]
    assert row0 % LANES == 0
    return pl.pallas_call(
        _prep_dt_kernel, grid=(1,),
        in_specs=[pl.BlockSpec((LANES, k_dim), lambda i: (row0 // LANES, 0))],
        out_specs=pl.BlockSpec((k_dim, LANES), lambda i: (0, 0)),
        out_shape=jax.ShapeDtypeStruct((k_dim, LANES), BF16), name="prep_weight_dt",
        compiler_params=pltpu.CompilerParams(dimension_semantics=("arbitrary",), vmem_limit_bytes=VMEM_LIMIT),
    )(wt)


StateOuts = collections.namedtuple("StateOuts", "k v conv ssm")


def _split3_bf16(x):
    hi = x.astype(BF16)
    r1 = x - hi.astype(F32)
    mid = r1.astype(BF16)
    return hi, mid, (r1 - mid.astype(F32)).astype(BF16)


def _piece_cols(p):
    return slice(p * PIECE, (p + 1) * PIECE)


def _in_proj_pieces(hb, w, dsts):
    for dst, first in zip(dsts, (QKV_PIECE0, Z_PIECE0, XBC_PIECE0)):
        for p in range(dst.shape[1] // PIECE):
            dst[:, _piece_cols(p)] = _mm(hb, w.win[first + p])
            yield


def _out_proj_pieces(x_ref, rows, hb, attn_ref, ssd_ref, w, out_ref):
    merged = []
    for half, (src_ref, w_ref) in enumerate(((attn_ref, w.woa), (ssd_ref, w.wob))):
        parts = []
        for p in range(D_MODEL // PIECE):
            gate = _sigmoid(_mm(hb, w.wgate[half * (D_MODEL // PIECE) + p]))
            yield
            parts.append(gate * _mm(src_ref[...], w_ref[p]))
            yield
        merged.append(jnp.concatenate(parts, axis=1))
    merged = (merged[0] + merged[1]).astype(BF16)
    for p in range(D_MODEL // PIECE):
        out_ref[rows, _piece_cols(p)] = x_ref[rows, _piece_cols(p)] + _mm(merged, w.wo[p])
        yield


def _stage_in_proj(x_ref, rows, w, proj, hnorm_ref):
    hb = _rms(x_ref[rows, :], w.norm1[...]).astype(BF16)
    hnorm_ref[...] = hb
    yield
    yield from _in_proj_pieces(hb, w, (proj.qkv, proj.z, proj.xbc))
    proj.dt[...] = _mm(hb, w.wdt[...])
    yield


def _stage_out_proj(x_ref, rows, mixed, w, x1_ref, hnorm_ref):
    hb = hnorm_ref[...]
    yield
    yield from _out_proj_pieces(x_ref, rows, hb, mixed.attn, mixed.ssd, w, x1_ref)


def _alternate(*gens):
    gens = list(gens)
    while gens:
        for g in list(gens):
            try:
                next(g)
                yield
            except StopIteration:
                gens.remove(g)


def _trace_interleaved(primary, filler):
    for _ in primary:
        next(filler, None)
    for _ in filler:
        pass


def _stage_mix(proj, mixed, rows, carry, w, sinks_ref, first):
    T = CHUNK
    reset = (lambda a: a) if first is None else (lambda a: jnp.where(first, 0.0, a))
    lo = _lane_lo((T, LANES))
    lo2 = _lane_lo((2 * T, LANES))
    rm = lax.broadcasted_iota(jnp.int32, (T, 1), 0) % SUBLANES
    rr = lax.broadcasted_iota(jnp.int32, (T, T), 0)
    cc = lax.broadcasted_iota(jnp.int32, (T, T), 1)
    tril = cc <= rr
    pairs_per_group = N_HEADS_M // 2 // N_GROUPS_M
    n_pairs = N_HEADS_M // 2
    gcols = lambda g: slice(g * GROUP_WIDTH, (g + 1) * GROUP_WIDTH)

    def conv_silu(cols):
        xr = proj.xbc[rows, cols]
        prev_tile = jnp.concatenate([reset(carry.conv_tail[:, cols]), xr[:T - SUBLANES]], axis=0)
        carry.conv_tail[:, cols] = xr[T - SUBLANES:]
        yc = xr * w.convw[CONV_W - 1:CONV_W, cols]
        for shift in range(1, CONV_W):
            shifted = _tile_roll(jnp.where(rm >= SUBLANES - shift, prev_tile, xr), shift)
            yc = yc + shifted * w.convw[CONV_W - 1 - shift:CONV_W - shift, cols]
        return _silu(yc + w.convb[:, cols])

    def softmax(s, kv):
        r = lax.broadcasted_iota(jnp.int32, (T, 2 * T), 0)
        col = lax.broadcasted_iota(jnp.int32, (T, 2 * T), 1)
        valid = (col >= r) & (col <= r + WINDOW)
        if first is not None:
            valid = valid & (col >= jnp.where(first, T, 0))
        es, inv = [], []
        for g in range(Q_PER_KV):
            sk = sinks_ref[kv * Q_PER_KV + g]
            sg = jnp.where(valid, s[g * T:(g + 1) * T], NEG_BIG)
            m = jnp.maximum(jnp.max(sg, axis=-1, keepdims=True), sk)
            e = jnp.exp(sg - m)
            es.append(e.astype(BF16))
            inv.append(1.0 / (jnp.sum(e, axis=-1, keepdims=True) + jnp.exp(sk - m)))
        return jnp.concatenate(es, axis=0), jnp.concatenate(inv, axis=0)

    qkv = proj.qkv[rows, :]
    q = qkv[:, :ATTN_WIDTH] * (HEAD_DIM_A ** -0.5)
    k = qkv[:, ATTN_WIDTH:ATTN_WIDTH + KV_WIDTH]
    v = qkv[:, ATTN_WIDTH + KV_WIDTH:]
    kk = jnp.concatenate([carry.kprev[...], k], axis=0)
    vv = jnp.concatenate([carry.vprev[...], v], axis=0)
    carry.kprev[...] = k
    carry.vprev[...] = v
    carry.dt_buf[...] = _softplus(proj.dt[rows, :] + w.dtb[...])
    da_split = jnp.concatenate(_split3_bf16(carry.dt_buf[...] * -jnp.exp(w.alog[...])), axis=1)
    vds = [_dup_half(vv, lo2, kv == 0).astype(BF16) for kv in range(N_KV_A)]
    scores = [_mm_nt(_stack_heads(q, kv, lo), _dup_half(kk, lo2, kv == 0).astype(BF16))
              for kv in range(N_KV_A)]
    yield

    bc = conv_silu(slice(D_INNER, CONV_DIM))
    bgs = [bc[:, g * D_STATE:(g + 1) * D_STATE].astype(BF16) for g in range(N_GROUPS_M)]
    cgs = [bc[:, (N_GROUPS_M + g) * D_STATE:(N_GROUPS_M + g + 1) * D_STATE].astype(BF16)
           for g in range(N_GROUPS_M)]
    bgts = [bc[:, g * D_STATE:(g + 1) * D_STATE].T.astype(BF16) for g in range(N_GROUPS_M)]
    yield

    es0, inv0 = softmax(scores[0], 0)
    yield

    es1, inv1 = softmax(scores[1], 1)
    cbs = [_mm_nt(cgs[g], bgs[g]) for g in range(N_GROUPS_M)]
    h_prevs = [reset(carry.ht[:, gcols(g)]) for g in range(N_GROUPS_M)]
    y_offs = [_mm(cgs[g], h_prevs[g]) for g in range(N_GROUPS_M)]
    o0 = _mm(es0, vds[0])
    cs = jnp.dot(jnp.where(tril, 1.0, 0.0).astype(BF16), da_split, preferred_element_type=F32)
    yield

    a_cs = cs[:, :LANES] + cs[:, LANES:2 * LANES] + cs[:, 2 * LANES:]
    a_cs_t = a_cs.T
    xs_groups = [conv_silu(gcols(0))]
    o1 = _mm(es1, vds[1])
    yield

    xs_groups.append(conv_silu(gcols(1)))
    mixed.attn[rows, 0:2 * LANES] = _unstack_heads(o0 * inv0, T, lo).astype(BF16)
    yield

    def prep(j):
        g = j // pairs_per_group
        dt = carry.dt_buf[...]
        ws, colbs, dtbs = [], [], []
        for h in (2 * j, 2 * j + 1):
            colb = jnp.broadcast_to(a_cs[:, h:h + 1], (T, T))
            rowb = jnp.broadcast_to(a_cs_t[h:h + 1, :], (T, T))
            seg = jnp.where(tril, jnp.exp(colb - rowb), 0.0)
            ws.append((cbs[g] * seg).astype(BF16))
            colbs.append(colb)
            dtbs.append(jnp.broadcast_to(dt[:, h:h + 1], (T, LANES)))
        dt_e = jnp.where(lo, dtbs[0], dtbs[1])
        acs_e = jnp.where(lo, colbs[0], colbs[1])
        xs_j = xs_groups[g][:, (j % pairs_per_group) * LANES:(j % pairs_per_group + 1) * LANES]
        xdt = xs_j * dt_e
        zero = jnp.zeros_like(xdt)
        rhs = jnp.concatenate([jnp.where(lo, xdt, zero), jnp.where(lo, zero, xdt)], axis=0).astype(BF16)
        alast = acs_e[T - 1:T, :]
        return dict(lhs=jnp.concatenate(ws, axis=1), rhs=rhs, xs=xs_j, e_acs=jnp.exp(acs_e),
                    xd=(xdt * jnp.exp(alast - acs_e)).astype(BF16), decay=jnp.exp(alast))

    def finish(j, p, y_diag):
        g, jj = divmod(j, pairs_per_group)
        sl = slice(j * LANES, (j + 1) * LANES)
        y = y_diag + y_offs[g][:, jj * LANES:(jj + 1) * LANES] * p["e_acs"] + p["xs"] * w.dskip[:, sl]
        return y * _silu(proj.z[rows, sl])

    mixed.attn[rows, 2 * LANES:4 * LANES] = _unstack_heads(o1 * inv1, T, lo).astype(BF16)
    preps = {0: prep(0)}
    yield

    ys, y_diags = [], {}
    for j in range(n_pairs):
        if j + 1 < n_pairs:
            preps[j + 1] = prep(j + 1)
        y_diags[j] = _mm(preps[j]["lhs"], preps[j]["rhs"])
        if j >= 1:
            ys.append(finish(j - 1, preps[j - 1], y_diags.pop(j - 1)))
        if j % pairs_per_group == pairs_per_group - 1:
            g = j // pairs_per_group
            grp = [preps[i] for i in range(g * pairs_per_group, (g + 1) * pairs_per_group)]
            carry.ht[:, gcols(g)] = (h_prevs[g] * jnp.concatenate([p["decay"] for p in grp], axis=1)
                                     + _mm(bgts[g], jnp.concatenate([p["xd"] for p in grp], axis=1)))
        yield
    ys.append(finish(n_pairs - 1, preps[n_pairs - 1], y_diags.pop(n_pairs - 1)))
    yield

    m_slabs = []
    for g in range(N_GROUPS_M):
        grp = ys[g * pairs_per_group:(g + 1) * pairs_per_group]
        ssq = grp[0] * grp[0]
        for y in grp[1:]:
            ssq = ssq + y * y
        scale = lax.rsqrt(jnp.sum(ssq, axis=-1, keepdims=True) * (1.0 / GROUP_WIDTH) + EPS)
        m_slabs.extend([y * scale for y in grp])
    mixed.ssd[rows, :] = (jnp.concatenate(m_slabs, axis=1) * w.ssmn[...]).astype(BF16)


def _chain(*gens):
    for g in gens:
        yield from g


def _prompt_mixer_kernel(sinks_ref, x_in_ref, x_res_ref, *refs, chunks_per_seq, n_steps):
    refs = list(refs)
    take = lambda n: [refs.pop(0) for _ in range(n)]
    w = MixerWeights(*take(len(MixerWeights._fields)))
    mlp_w_f32 = take(2)
    x1_ref, = take(1)
    outs = StateOuts(*take(4))
    mlp_w_bf16 = take(2)
    proj = (ProjBufs(*take(4)), ProjBufs(*take(4)))
    mixed = (MixedBufs(*take(2)), MixedBufs(*take(2)))
    hnorm = take(2)
    carry = Carry(*take(len(Carry._fields)))
    assert not refs
    T = CHUNK
    s = pl.program_id(0)
    every = slice(None)

    @pl.when(s == 0)
    def _():
        for ref in list(proj[1] + mixed[0] + mixed[1] + carry) + hnorm:
            ref[...] = jnp.zeros_like(ref)

    first = (s - 1) % (chunks_per_seq // 2) == 0

    def cast_mlp_weights():
        for src, dst in zip(mlp_w_f32, mlp_w_bf16):
            dst[...] = src[...].astype(BF16)
            yield

    def step(cur):
        prv = 1 - cur
        dense = _alternate(_stage_out_proj(x_res_ref, every, mixed[cur], w, x1_ref, hnorm[cur]),
                           _stage_in_proj(x_in_ref, every, w, proj[cur], hnorm[cur]), cast_mlp_weights())
        mix = _chain(_stage_mix(proj[prv], mixed[prv], slice(0, T), carry, w, sinks_ref, first),
                     _stage_mix(proj[prv], mixed[prv], slice(T, 2 * T), carry, w, sinks_ref, None))
        _trace_interleaved(mix, dense)

    def drain_all(*gens):
        for _ in _alternate(*gens):
            pass

    last = n_steps - 1
    pl.when(s == 0)(lambda: drain_all(_stage_in_proj(x_in_ref, every, w, proj[0], hnorm[0]), cast_mlp_weights()))
    pl.when(s == last)(lambda: drain_all(
        _stage_out_proj(x_res_ref, every, mixed[last % 2], w, x1_ref, hnorm[last % 2]), cast_mlp_weights()))
    for parity in (0, 1):
        pl.when((s > 0) & (s < last) & (s % 2 == parity))(functools.partial(step, parity))

    @pl.when((s >= 1) & (s % (chunks_per_seq // 2) == 0))
    def _():
        outs.k[0] = carry.kprev[...]
        outs.v[0] = carry.vprev[...]
        outs.conv[0] = carry.conv_tail[SUBLANES - (CONV_W - 1):, :]
        outs.ssm[0] = carry.ht[...].T


def _const_spec(shape):
    return pl.BlockSpec(shape, lambda *_: (0,) * len(shape), pipeline_mode=pl.Buffered(1))


def _prompt_mixer(x, sinks, weights, w_up, w_down):
    nb, seq, _ = x.shape
    assert seq % (2 * CHUNK) == 0
    chunks_per_seq = seq // CHUNK
    n_pairs = nb * chunks_per_seq // 2
    n_steps = n_pairs + 2
    pair = 2 * CHUNK
    consts = tuple(weights)
    cast_steps = 1 << ((n_steps // 2).bit_length() - 1)
    ff_slice = D_FF // cast_steps
    assert ff_slice % LANES == 0
    up_block = lambda s: (0, jnp.minimum(s, cast_steps - 1))
    down_block = lambda s: (jnp.clip(s - cast_steps, 0, cast_steps - 1), 0)
    seq_of_mix = lambda s: jnp.clip((2 * s - 1) // chunks_per_seq, 0, nb - 1)
    in_specs = ([pl.BlockSpec(memory_space=pltpu.SMEM),
                 pl.BlockSpec((pair, D_MODEL), lambda s: (jnp.minimum(s, n_pairs - 1), 0)),
                 pl.BlockSpec((pair, D_MODEL), lambda s: (jnp.maximum(s - 2, 0), 0))]
                + [_const_spec(a.shape) for a in consts]
                + [pl.BlockSpec((D_MODEL, ff_slice), up_block), pl.BlockSpec((ff_slice, D_MODEL), down_block)])
    out_shape = (jax.ShapeDtypeStruct((nb * seq, D_MODEL), F32),
                 jax.ShapeDtypeStruct((nb, WINDOW, KV_WIDTH), F32),
                 jax.ShapeDtypeStruct((nb, WINDOW, KV_WIDTH), F32),
                 jax.ShapeDtypeStruct((nb, CONV_W - 1, CONV_DIM), F32),
                 jax.ShapeDtypeStruct((nb, D_INNER, D_STATE), F32),
                 jax.ShapeDtypeStruct(w_up.shape, BF16), jax.ShapeDtypeStruct(w_down.shape, BF16))
    out_specs = (pl.BlockSpec((pair, D_MODEL), lambda s: (jnp.maximum(s - 2, 0), 0)),
                 pl.BlockSpec((1, WINDOW, KV_WIDTH), lambda s: (seq_of_mix(s), 0, 0)),
                 pl.BlockSpec((1, WINDOW, KV_WIDTH), lambda s: (seq_of_mix(s), 0, 0)),
                 pl.BlockSpec((1, CONV_W - 1, CONV_DIM), lambda s: (seq_of_mix(s), 0, 0)),
                 pl.BlockSpec((1, D_INNER, D_STATE), lambda s: (seq_of_mix(s), 0, 0)),
                 pl.BlockSpec((D_MODEL, ff_slice), up_block), pl.BlockSpec((ff_slice, D_MODEL), down_block))
    proj_bufs = [pltpu.VMEM((pair, QKV_WIDTH), F32), pltpu.VMEM((pair, D_INNER), F32),
                 pltpu.VMEM((pair, CONV_DIM), F32), pltpu.VMEM((pair, LANES), F32)]
    mixed_bufs = [pltpu.VMEM((pair, ATTN_WIDTH), BF16), pltpu.VMEM((pair, D_INNER), BF16)]
    hnorm_bufs = [pltpu.VMEM((pair, D_MODEL), BF16)] * 2
    carry = [pltpu.VMEM((CHUNK, KV_WIDTH), F32), pltpu.VMEM((CHUNK, KV_WIDTH), F32),
             pltpu.VMEM((SUBLANES, CONV_DIM), F32), pltpu.VMEM((D_STATE, D_INNER), F32),
             pltpu.VMEM((CHUNK, LANES), F32)]
    x_rows = x.reshape(nb * seq, D_MODEL)
    return pl.pallas_call(
        functools.partial(_prompt_mixer_kernel, chunks_per_seq=chunks_per_seq, n_steps=n_steps),
        grid=(n_steps,), in_specs=in_specs, out_specs=out_specs,
        out_shape=out_shape, scratch_shapes=proj_bufs * 2 + mixed_bufs * 2 + hnorm_bufs + carry,
        name="prompt_mixer",
        compiler_params=pltpu.CompilerParams(dimension_semantics=("arbitrary",),
                                             vmem_limit_bytes=VMEM_LIMIT),
    )(sinks, x_rows, x_rows, *consts, w_up, w_down)


MLP_TILE = 1024
FF_BLOCK = 1024


def _mlp_kernel(x_ref, norm2_ref, wup_ref, wdown_ref, fnorm_ref, y_ref):
    x = x_ref[...]
    hm = _rms(x, norm2_ref[...]).astype(BF16)
    acc = x
    for j in range(D_FF // FF_BLOCK):
        h = _mm(hm, wup_ref[:, j * FF_BLOCK:(j + 1) * FF_BLOCK])
        h = jnp.square(jnp.maximum(h, 0.0))
        acc = acc + _mm(h, wdown_ref[j * FF_BLOCK:(j + 1) * FF_BLOCK, :])
    y_ref[...] = _rms(acc, fnorm_ref[...])


def _mlp(x, norm2, wup, wdown, fnorm):
    rows = x.shape[0]
    tile = min(MLP_TILE, rows)
    assert rows % tile == 0 and tile % SUBLANES == 0
    consts = (norm2, wup, wdown, fnorm)
    return pl.pallas_call(
        _mlp_kernel, grid=(rows // tile,),
        in_specs=[pl.BlockSpec((tile, D_MODEL), lambda i: (i, 0))] + [_const_spec(a.shape) for a in consts],
        out_specs=pl.BlockSpec((tile, D_MODEL), lambda i: (i, 0)),
        out_shape=jax.ShapeDtypeStruct((rows, D_MODEL), F32), name="mlp",
        compiler_params=pltpu.CompilerParams(dimension_semantics=("arbitrary",),
                                             vmem_limit_bytes=VMEM_LIMIT),
    )(x, *consts)


IN_PROJ_ROWS = 256


def _in_proj_kernel(x_ref, norm1_ref, win_ref, wdt_ref, u_ref, dt_ref, hnorm_ref):
    hb = _rms(x_ref[...], norm1_ref[...]).astype(BF16)
    hnorm_ref[...] = hb
    for p in range(win_ref.shape[0]):
        u_ref[:, _piece_cols(p)] = _mm(hb, win_ref[p])
    dt_ref[...] = _mm(hb, wdt_ref[...])


def _in_proj(x, norm1, win, wdt):
    rows = x.shape[0]
    assert rows % IN_PROJ_ROWS == 0
    outs = ((win.shape[0] * PIECE, F32), (LANES, F32), (D_MODEL, BF16))
    row_spec = lambda width: pl.BlockSpec((IN_PROJ_ROWS, width), lambda i: (i, 0))
    return pl.pallas_call(
        _in_proj_kernel, grid=(rows // IN_PROJ_ROWS,),
        in_specs=[row_spec(D_MODEL), _const_spec(norm1.shape), _const_spec(win.shape), _const_spec(wdt.shape)],
        out_specs=tuple(row_spec(width) for width, _ in outs),
        out_shape=tuple(jax.ShapeDtypeStruct((rows, width), dtype) for width, dtype in outs),
        name="sample_in_proj",
        compiler_params=pltpu.CompilerParams(dimension_semantics=("arbitrary",),
                                             vmem_limit_bytes=VMEM_LIMIT),
    )(x, norm1, win, wdt)


def _tile_roll(x, shift):
    rows, width = x.shape
    tiles = x.reshape(rows // SUBLANES, SUBLANES, width)
    return pltpu.roll(tiles, shift % SUBLANES, axis=1).reshape(rows, width)


def _spread_rows(x):
    tiles = []
    for t in range(x.shape[0] // SUBLANES):
        two_seqs = x[t * SUBLANES:(t + 1) * SUBLANES]
        tiles += [two_seqs, pltpu.roll(two_seqs, SUBLANES // 2, axis=0)]
    return jnp.concatenate(tiles, axis=0)


def _gather_rows(y):
    low = lax.broadcasted_iota(jnp.int32, (SUBLANES, 1), 0) < SUBLANES // 2
    tiles = []
    for t in range(y.shape[0] // (2 * SUBLANES)):
        a = y[2 * t * SUBLANES:(2 * t + 1) * SUBLANES]
        b = y[(2 * t + 1) * SUBLANES:(2 * t + 2) * SUBLANES]
        tiles.append(jnp.where(low, a, pltpu.roll(b, SUBLANES // 2, axis=0)))
    return jnp.concatenate(tiles, axis=0)


def _expand_heads(cols, expand):
    terms = []
    for c in cols:
        hi = c.astype(BF16)
        r1 = c - hi.astype(F32)
        mid = r1.astype(BF16)
        lo = (r1 - mid.astype(F32)).astype(BF16)
        terms.extend([hi, mid, lo])
    rows = cols[0].shape[0]
    out = jnp.dot(jnp.concatenate(terms, axis=0), expand, preferred_element_type=F32)
    return [out[(3 * i) * rows:(3 * i + 1) * rows] + out[(3 * i + 1) * rows:(3 * i + 2) * rows]
            + out[(3 * i + 2) * rows:(3 * i + 3) * rows] for i in range(len(cols))]


def _sample_mixer_kernel(sinks_ref, u_ref, dtraw_ref, cprev_ref, kc_ref, vc_ref, st_ref,
                         convw_ref, convb_ref, dtb_ref, alog_ref, dskip_ref, ssmn_ref,
                         aout_ref, mout_ref, nk_ref, nv_ref, nconv_ref, nst_ref):
    R = SAMPLE_GROUP * SAMPLE_PAD
    L = 4
    qkv_ref = u_ref.at[:, QKV_PIECE0 * PIECE:Z_PIECE0 * PIECE]
    z_ref = u_ref.at[:, Z_PIECE0 * PIECE:XBC_PIECE0 * PIECE]
    xbc_ref = u_ref.at[:, XBC_PIECE0 * PIECE:N_IN_PIECES * PIECE]
    rm = lax.broadcasted_iota(jnp.int32, (R, 1), 0) % SAMPLE_PAD

    xr = xbc_raw = _spread_rows(xbc_ref[...])
    with_prev = jnp.where(rm >= SAMPLE_PAD - (CONV_W - 1), cprev_ref[...], xr)
    yc = xr * convw_ref[CONV_W - 1:CONV_W, :]
    for kshift in range(1, CONV_W):
        yc = yc + _tile_roll(with_prev, kshift) * convw_ref[CONV_W - 1 - kshift:CONV_W - kshift, :]
    nconv_ref[...] = _tile_roll(xbc_raw, -1)
    xbc = _silu(yc + convb_ref[...])
    xs = xbc[:, :D_INNER]
    bm = xbc[:, D_INNER:D_INNER + N_GROUPS_M * D_STATE]
    cm = xbc[:, D_INNER + N_GROUPS_M * D_STATE:]

    dt = _softplus(_spread_rows(dtraw_ref[...]) + dtb_ref[...])
    d_a = dt * (-jnp.exp(alog_ref[...]))
    a_cs = d_a
    suf = jnp.zeros_like(d_a)
    for kshift in range(1, L):
        a_cs = a_cs + jnp.where(rm >= kshift, _tile_roll(d_a, kshift), 0.0)
        suf = suf + jnp.where(rm <= L - 1 - kshift, _tile_roll(d_a, -kshift), 0.0)

    expand = (lax.broadcasted_iota(jnp.int32, (LANES, D_INNER), 1) // HEAD_DIM_M
              == lax.broadcasted_iota(jnp.int32, (LANES, D_INNER), 0)).astype(BF16)
    dt_e, acs_e, suf_e = _expand_heads([dt, a_cs, suf], expand)
    xdt = xs * dt_e
    real = rm < L
    xd_t = jnp.where(real, xdt * jnp.exp(suf_e), 0.0).T.astype(BF16)

    y = xs * dskip_ref[...]
    for kshift in range(L):
        bk = bm if kshift == 0 else _tile_roll(bm, kshift)
        cb = cm * bk
        cb_e = jnp.concatenate(
            [jnp.broadcast_to(jnp.sum(cb[:, g * D_STATE:(g + 1) * D_STATE], axis=-1, keepdims=True),
                              (R, GROUP_WIDTH)) for g in range(N_GROUPS_M)], axis=1)
        if kshift == 0:
            y = y + cb_e * xdt
        else:
            seg = jnp.exp(acs_e - _tile_roll(acs_e, kshift))
            y = y + jnp.where(rm >= kshift, cb_e * seg * _tile_roll(xdt, kshift), 0.0)

    qkv = _spread_rows(qkv_ref[...])
    q = qkv[:, :ATTN_WIDTH] * (HEAD_DIM_A ** -0.5)
    q_swapped = jnp.concatenate(
        [pltpu.roll(q[:, s * LANES:(s + 1) * LANES], HEAD_DIM_A, axis=1) for s in range(ATTN_WIDTH // LANES)], axis=1)
    kn = qkv[:, ATTN_WIDTH:ATTN_WIDTH + KV_WIDTH]
    vn = qkv[:, ATTN_WIDTH + KV_WIDTH:]
    lo8 = _lane_lo((SAMPLE_PAD, LANES))
    zero8 = jnp.zeros((SAMPLE_PAD, LANES), F32)
    seq_rows = [slice(i * SAMPLE_PAD, (i + 1) * SAMPLE_PAD) for i in range(SAMPLE_GROUP)]
    HROWS = N_HEADS_A * SAMPLE_PAD

    s_c, s_n = [], []
    for rows in seq_rows:
        pieces = []
        for kv in range(N_KV_A):
            for g in range(Q_PER_KV):
                slab = kv * (Q_PER_KV // 2) + g // 2
                src = q if g % 2 == kv else q_swapped
                piece = src[rows, slab * LANES:(slab + 1) * LANES]
                pieces.append(jnp.where(lo8, piece, zero8) if kv == 0 else jnp.where(lo8, zero8, piece))
        lhs = jnp.concatenate(pieces, axis=0).astype(BF16)
        i = len(s_c)
        s_c.append(_mm(lhs, kc_ref[i]))
        s_n.append(_mm_nt(lhs, kn[rows]))
    s_c = jnp.concatenate(s_c, axis=0)
    s_n = jnp.concatenate(s_n, axis=0)
    n_rows = SAMPLE_GROUP * HROWS
    tok_c = lax.broadcasted_iota(jnp.int32, (n_rows, WINDOW), 0) % SAMPLE_PAD
    s_c = jnp.where(lax.broadcasted_iota(jnp.int32, (n_rows, WINDOW), 1) >= tok_c, s_c, NEG_BIG)
    tok_n = lax.broadcasted_iota(jnp.int32, (n_rows, SAMPLE_PAD), 0) % SAMPLE_PAD
    col_n = lax.broadcasted_iota(jnp.int32, (n_rows, SAMPLE_PAD), 1)
    s_n = jnp.where((col_n <= tok_n) & (col_n < L), s_n, NEG_BIG)
    sk = jnp.concatenate([jnp.full((SAMPLE_PAD, 1), sinks_ref[h], F32) for h in range(N_HEADS_A)] * SAMPLE_GROUP,
                         axis=0)
    m = jnp.maximum(jnp.maximum(jnp.max(s_c, axis=-1, keepdims=True), jnp.max(s_n, axis=-1, keepdims=True)), sk)
    e_c = jnp.exp(s_c - m)
    e_n = jnp.exp(s_n - m)
    inv_den = 1.0 / (jnp.sum(e_c, axis=-1, keepdims=True) + jnp.sum(e_n, axis=-1, keepdims=True) + jnp.exp(sk - m))
    e_c = e_c.astype(BF16)
    e_n = e_n.astype(BF16)
    o = jnp.concatenate([_mm_nt(e_c[i * HROWS:(i + 1) * HROWS], vc_ref[i]) + _mm(e_n[i * HROWS:(i + 1) * HROWS], vn[rows])
                         for i, rows in enumerate(seq_rows)], axis=0) * inv_den
    o_swapped = pltpu.roll(o, HEAD_DIM_A, axis=1)
    a_rows = []
    for i in range(SAMPLE_GROUP):
        blk = lambda arr, kv, g: arr[i * HROWS + (kv * Q_PER_KV + g) * SAMPLE_PAD:
                                     i * HROWS + (kv * Q_PER_KV + g + 1) * SAMPLE_PAD]
        slabs = []
        for kv in range(N_KV_A):
            for j in range(Q_PER_KV // 2):
                first = blk(o if kv == 0 else o_swapped, kv, 2 * j)
                second = blk(o_swapped if kv == 0 else o, kv, 2 * j + 1)
                slabs.append(jnp.where(lo8, first, second))
        a_rows.append(jnp.concatenate(slabs, axis=1))
    aout_ref[...] = _gather_rows(jnp.concatenate(a_rows, axis=0))

    key = lax.broadcasted_iota(jnp.int32, (KV_WIDTH, WINDOW), 1)
    for cache_ref, new, out_ref in ((kc_ref, kn, nk_ref), (vc_ref, vn, nv_ref)):
        new_t = new.T
        for i in range(SAMPLE_GROUP):
            kept = pltpu.roll(cache_ref[i], WINDOW - L, axis=1)
            fresh = pltpu.roll(new_t, (WINDOW - L - i * SAMPLE_PAD) % WINDOW, axis=1)
            out_ref[i] = jnp.where(key >= WINDOW - L, fresh, kept)

    row_r = lax.broadcasted_iota(jnp.int32, (R, D_STATE), 0)
    heads_per_group = N_HEADS_M // N_GROUPS_M
    yoff = []
    for i, rows in enumerate(seq_rows):
        state = st_ref[i]
        state_b = state.astype(BF16)
        a_tot = a_cs[i * SAMPLE_PAD + L - 1:i * SAMPLE_PAD + L, :]
        in_seq = (row_r >= i * SAMPLE_PAD) & (row_r < (i + 1) * SAMPLE_PAD)
        yo = []
        for g in range(N_GROUPS_M):
            gs = slice(g * GROUP_WIDTH, (g + 1) * GROUP_WIDTH)
            yo.append(_mm_nt(cm[rows, g * D_STATE:(g + 1) * D_STATE], state_b[gs]))
            bsel = jnp.where(in_seq, bm[:, g * D_STATE:(g + 1) * D_STATE], 0.0)
            upd = _mm(xd_t[gs, :], bsel)
            for hh in range(heads_per_group):
                h = g * heads_per_group + hh
                hs = slice(h * HEAD_DIM_M, (h + 1) * HEAD_DIM_M)
                decay = jnp.exp(jnp.broadcast_to(a_tot[:, h:h + 1], (HEAD_DIM_M, D_STATE)))
                nst_ref[i, hs, :] = state[hs] * decay + upd[hh * HEAD_DIM_M:(hh + 1) * HEAD_DIM_M]
        yoff.append(jnp.concatenate(yo, axis=1))
    yoff = jnp.concatenate(yoff, axis=0)

    y = (y + yoff * jnp.exp(acs_e)) * _silu(_spread_rows(z_ref[...]))
    outs = []
    for g in range(N_GROUPS_M):
        yg = y[:, g * GROUP_WIDTH:(g + 1) * GROUP_WIDTH]
        outs.append(yg * lax.rsqrt(jnp.mean(yg * yg, axis=-1, keepdims=True) + EPS))
    mout_ref[...] = _gather_rows(jnp.concatenate(outs, axis=1) * ssmn_ref[...])


def _sample_mixer(sinks, u, dtraw, cprev, kc, vc, st, convw, convb, dtb, alog, dskip, ssmn):
    nseq = kc.shape[0]
    assert nseq % SAMPLE_GROUP == 0
    R = SAMPLE_GROUP * SAMPLE_PAD
    T4 = SAMPLE_GROUP * 4
    rows = nseq * SAMPLE_PAD
    consts = (convw, convb, dtb, alog, dskip, ssmn)
    row_spec = lambda w: pl.BlockSpec((R, w), lambda i: (i, 0))
    tok_spec = lambda w: pl.BlockSpec((T4, w), lambda i: (i, 0))
    seq_spec = lambda a, b: pl.BlockSpec((SAMPLE_GROUP, a, b), lambda i: (i, 0, 0))
    in_specs = ([pl.BlockSpec(memory_space=pltpu.SMEM),
                 tok_spec(N_IN_PIECES * PIECE), tok_spec(LANES), row_spec(CONV_DIM),
                 seq_spec(WINDOW, KV_WIDTH), seq_spec(WINDOW, KV_WIDTH), seq_spec(D_INNER, D_STATE)]
                + [_const_spec(a.shape) for a in consts])
    out_shape = (jax.ShapeDtypeStruct((nseq * 4, ATTN_WIDTH), F32), jax.ShapeDtypeStruct((nseq * 4, D_INNER), F32),
                 jax.ShapeDtypeStruct((nseq, WINDOW, KV_WIDTH), F32),
                 jax.ShapeDtypeStruct((nseq, WINDOW, KV_WIDTH), F32),
                 jax.ShapeDtypeStruct((rows, CONV_DIM), F32),
                 jax.ShapeDtypeStruct((nseq, D_INNER, D_STATE), F32))
    out_specs = (tok_spec(ATTN_WIDTH), tok_spec(D_INNER), seq_spec(WINDOW, KV_WIDTH), seq_spec(WINDOW, KV_WIDTH),
                 row_spec(CONV_DIM), seq_spec(D_INNER, D_STATE))
    return pl.pallas_call(
        _sample_mixer_kernel, grid=(nseq // SAMPLE_GROUP,), in_specs=in_specs, out_specs=out_specs,
        out_shape=out_shape, name="sample_mixer",
        compiler_params=pltpu.CompilerParams(dimension_semantics=("arbitrary",),
                                             vmem_limit_bytes=VMEM_LIMIT),
    )(sinks, u, dtraw, cprev, kc, vc, st, *consts)


N_OUT_PIECES = D_MODEL // PIECE
N_FF_BLOCKS = D_FF // FF_BLOCK


def _sample_post_kernel(x_ref, a_ref, m_ref, hnorm_ref, wga_ref, wgb_ref, woa_ref, wob_ref, wo_ref,
                        norm2_ref, wup_ref, wdown_ref, fnorm_ref, y_ref, merged_s, x1_s, hm_s, acc_s):
    s = pl.program_id(0)

    @pl.when(s < N_OUT_PIECES)
    def _():
        hb = hnorm_ref[...]
        piece = (_sigmoid(_mm(hb, wga_ref[0])) * _mm(a_ref[...], woa_ref[0])
                 + _sigmoid(_mm(hb, wgb_ref[0])) * _mm(m_ref[...], wob_ref[0]))
        for p in range(N_OUT_PIECES):
            @pl.when(s == p)
            def _():
                merged_s[:, _piece_cols(p)] = piece.astype(BF16)

    @pl.when((s >= N_OUT_PIECES) & (s < 2 * N_OUT_PIECES))
    def _():
        piece = _mm(merged_s[...], wo_ref[0])
        for p in range(N_OUT_PIECES):
            @pl.when(s == N_OUT_PIECES + p)
            def _():
                x1_s[:, _piece_cols(p)] = x_ref[:, _piece_cols(p)] + piece

    @pl.when(s == 2 * N_OUT_PIECES)
    def _():
        x1 = x1_s[...]
        hm_s[...] = _rms(x1, norm2_ref[...]).astype(BF16)
        acc_s[...] = x1

    @pl.when(s >= 2 * N_OUT_PIECES)
    def _():
        h = jnp.square(jnp.maximum(_mm(hm_s[...], wup_ref[...]), 0.0))
        acc_s[...] += _mm(h, wdown_ref[...])

    @pl.when(s == 2 * N_OUT_PIECES + N_FF_BLOCKS - 1)
    def _():
        y_ref[...] = _rms(acc_s[...], fnorm_ref[...])


def _sample_post(x, a_out, m_out, hnorm, wgate, woa, wob, wo, norm2, wup, wdown, fnorm):
    rows = x.shape[0]
    first = lambda s: jnp.minimum(s, N_OUT_PIECES - 1)
    second = lambda s: jnp.clip(s - N_OUT_PIECES, 0, N_OUT_PIECES - 1)
    ff = lambda s: jnp.clip(s - 2 * N_OUT_PIECES, 0, N_FF_BLOCKS - 1)
    piece_spec = lambda w, index: pl.BlockSpec((1,) + w.shape[1:], lambda s: (index(s), 0, 0))
    in_specs = [_const_spec(x.shape), _const_spec(a_out.shape), _const_spec(m_out.shape), _const_spec(hnorm.shape),
                piece_spec(wgate, first), piece_spec(wgate, lambda s: N_OUT_PIECES + first(s)),
                piece_spec(woa, first), piece_spec(wob, first), piece_spec(wo, second),
                _const_spec(norm2.shape),
                pl.BlockSpec((D_MODEL, FF_BLOCK), lambda s: (0, ff(s))),
                pl.BlockSpec((FF_BLOCK, D_MODEL), lambda s: (ff(s), 0)),
                _const_spec(fnorm.shape)]
    return pl.pallas_call(
        _sample_post_kernel, grid=(2 * N_OUT_PIECES + N_FF_BLOCKS,), in_specs=in_specs,
        out_specs=pl.BlockSpec((rows, D_MODEL), lambda s: (0, 0)),
        out_shape=jax.ShapeDtypeStruct((rows, D_MODEL), F32),
        scratch_shapes=[pltpu.VMEM((rows, D_MODEL), BF16), pltpu.VMEM((rows, D_MODEL), F32),
                        pltpu.VMEM((rows, D_MODEL), BF16), pltpu.VMEM((rows, D_MODEL), F32)],
        name="sample_post",
        compiler_params=pltpu.CompilerParams(dimension_semantics=("arbitrary",),
                                             vmem_limit_bytes=VMEM_LIMIT),
    )(x, a_out, m_out, hnorm, wgate, wgate, woa, wob, wo, norm2, wup, wdown, fnorm)


def _layer_params(norm1, w_in, sinks, conv_w, conv_b, dt_bias, a_log, d_skip, ssm_norm, w_oa, w_ob, w_o,
                  norm2):
    dt0 = QKV_WIDTH + D_INNER + CONV_DIM
    assert dt0 == N_IN_PIECES * PIECE
    pad_lanes = lambda a: jnp.pad(a, ((0, 0), (0, LANES - a.shape[1])))
    w_in_t = w_in.T
    mixer = MixerWeights(
        norm1=norm1[None, :], win=_prep_pieces_t(w_in_t, N_IN_PIECES),
        wgate=_prep_pieces_t_shifted(w_in_t, dt0 + N_HEADS_M, 2 * D_MODEL // PIECE),
        wdt=_prep_dt(w_in_t, dt0),
        convw=conv_w, convb=conv_b[None, :], dtb=pad_lanes(dt_bias[None, :]), alog=pad_lanes(a_log[None, :]),
        dskip=jnp.repeat(d_skip, HEAD_DIM_M)[None, :], ssmn=ssm_norm[None, :],
        woa=_prep_pieces(w_oa), wob=_prep_pieces(w_ob), wo=_prep_pieces(w_o))
    return dict(sinks=sinks.astype(F32), mixer=mixer, norm2=norm2[None, :])


def kernel(x_prompt, x_sample, cache_swa_k, cache_swa_v, state_conv, state_ssm, norm1, w_in, sinks, conv_w,
           conv_b, dt_bias, a_log, d_skip, ssm_norm, w_oa, w_ob, w_o, norm2, w_up, w_down, final_norm):
    depth = w_in.shape[0]
    assert depth == 1
    nb, seq, _ = x_prompt.shape
    ns, ls, _ = x_sample.shape
    assert ls == 4
    p = _layer_params(norm1[0], w_in[0], sinks[0], conv_w[0], conv_b[0], dt_bias[0], a_log[0], d_skip[0],
                      ssm_norm[0], w_oa[0], w_ob[0], w_o[0], norm2[0])
    fnorm = final_norm[None, :]
    mw = p["mixer"]
    mixer_consts = (mw.convw, mw.convb, mw.dtb, mw.alog, mw.dskip, mw.ssmn)

    x1p, pk, pv, pc, pst, wup, wdown = _prompt_mixer(x_prompt, p["sinks"], mw, w_up[0], w_down[0])
    y_prompt = _mlp(x1p, p["norm2"], wup, wdown, fnorm)
    y_prompt = y_prompt.reshape(nb, seq, D_MODEL)

    xs_rows = x_sample.reshape(ns * ls, D_MODEL)
    dim_major = lambda a: jnp.swapaxes(a, 1, 2)
    cprev = jnp.pad(state_conv[0], ((0, 0), (SAMPLE_PAD - (CONV_W - 1), 0), (0, 0)))
    cprev = cprev.reshape(ns * SAMPLE_PAD, CONV_DIM)
    u, dtraw, hnorm = _in_proj(xs_rows, mw.norm1, mw.win, mw.wdt)
    a_out, m_out, sk, sv, sc, sst = _sample_mixer(
        p["sinks"], u, dtraw, cprev,
        dim_major(cache_swa_k[0].reshape(ns, WINDOW, KV_WIDTH)), dim_major(cache_swa_v[0].reshape(ns, WINDOW, KV_WIDTH)),
        state_ssm[0].reshape(ns, D_INNER, D_STATE), *mixer_consts)
    sk, sv = dim_major(sk), dim_major(sv)
    y_sample = _sample_post(xs_rows, a_out, m_out, hnorm, mw.wgate, mw.woa, mw.wob, mw.wo,
                            p["norm2"], wup, wdown, fnorm).reshape(ns, ls, D_MODEL)
    sc = sc.reshape(ns, SAMPLE_PAD, CONV_DIM)[:, :CONV_W - 1]

    kv_shape = (1, -1, WINDOW, N_KV_A, HEAD_DIM_A)
    ssm_shape = (1, -1, N_HEADS_M, HEAD_DIM_M, D_STATE)
    return (y_prompt, y_sample,
            pk.reshape(kv_shape), pv.reshape(kv_shape), pc[None], pst.reshape(ssm_shape),
            sk.reshape(kv_shape), sv.reshape(kv_shape), sc[None], sst.reshape(ssm_shape))
```

```python
import collections
import functools

import jax
import jax.numpy as jnp
from jax import lax
from jax.experimental import pallas as pl
from jax.experimental.pallas import tpu as pltpu

F32 = jnp.float32
BF16 = jnp.bfloat16

D_MODEL = 1024
N_HEADS_A = 8
N_KV_A = 2
Q_PER_KV = N_HEADS_A // N_KV_A
HEAD_DIM_A = 64
WINDOW = 128
ATTN_WIDTH = N_HEADS_A * HEAD_DIM_A
KV_WIDTH = N_KV_A * HEAD_DIM_A
D_INNER = 1024
HEAD_DIM_M = 64
N_HEADS_M = D_INNER // HEAD_DIM_M
N_GROUPS_M = 2
GROUP_WIDTH = D_INNER // N_GROUPS_M
D_STATE = 128
CONV_W = 4
CONV_DIM = D_INNER + 2 * N_GROUPS_M * D_STATE
CHUNK = 128
D_FF = 4 * D_MODEL
EPS = 1e-6

LANES = 128
SUBLANES = 8
QKV_WIDTH = ATTN_WIDTH + 2 * KV_WIDTH
NEG_BIG = -1e30
VMEM_LIMIT = 56 * 1024 * 1024
SAMPLE_PAD = SUBLANES
SAMPLE_GROUP = 16


def _mm(a, b):
    return jnp.dot(a.astype(BF16), b.astype(BF16), preferred_element_type=F32)


def _mm_nt(a, b):
    return lax.dot_general(a.astype(BF16), b.astype(BF16), (((1,), (1,)), ((), ())),
                           preferred_element_type=F32)


def _rms(x, w):
    return x * lax.rsqrt(jnp.mean(x * x, axis=-1, keepdims=True) + EPS) * w


def _sigmoid(x):
    return 0.5 + 0.5 * jnp.tanh(0.5 * x)


def _silu(x):
    h = 0.5 * x
    return h + h * jnp.tanh(h)


def _softplus(x):
    return jnp.maximum(x, 0.0) + jnp.log(1.0 + jnp.exp(-jnp.abs(x)))


def _lane_lo(shape):
    return (lax.broadcasted_iota(jnp.int32, shape, len(shape) - 1) % LANES) < HEAD_DIM_A


def _dup_half(x, lo, first):
    xr = pltpu.roll(x, HEAD_DIM_A, axis=1)
    return jnp.where(lo, x, xr) if first else jnp.where(lo, xr, x)


def _stack_heads(q, kv, lo):
    qa = q[:, kv * 2 * LANES: kv * 2 * LANES + LANES]
    qb = q[:, kv * 2 * LANES + LANES: (kv + 1) * 2 * LANES]
    zero = jnp.zeros_like(qa)
    return jnp.concatenate([jnp.where(lo, qa, zero), jnp.where(lo, zero, qa),
                            jnp.where(lo, qb, zero), jnp.where(lo, zero, qb)], axis=0)


def _unstack_heads(o, rows, lo):
    return jnp.concatenate([jnp.where(lo, o[0:rows], o[rows:2 * rows]),
                            jnp.where(lo, o[2 * rows:3 * rows], o[3 * rows:4 * rows])], axis=1)


ProjBufs = collections.namedtuple("ProjBufs", "qkv z xbc dt")
MixedBufs = collections.namedtuple("MixedBufs", "attn ssd")
Carry = collections.namedtuple("Carry", "kprev vprev conv_tail ht dt_buf")
MixerWeights = collections.namedtuple(
    "MixerWeights", "norm1 win wgate wdt convw convb dtb alog dskip ssmn woa wob wo")

PIECE = 256
QKV_PIECE0 = 0
Z_PIECE0 = QKV_PIECE0 + QKV_WIDTH // PIECE
XBC_PIECE0 = Z_PIECE0 + D_INNER // PIECE
N_IN_PIECES = XBC_PIECE0 + CONV_DIM // PIECE
PREP_PIECES = 4


def _prep_pieces_kernel(*refs):
    n = len(refs) // 2
    for w_ref, o_ref in zip(refs[:n], refs[n:]):
        for j in range(o_ref.shape[0]):
            o_ref[j] = w_ref[:, j * PIECE:(j + 1) * PIECE].astype(BF16)


def _prep_pieces(*ws):
    return pl.pallas_call(
        _prep_pieces_kernel, grid=(1,),
        in_specs=[_const_spec(w.shape) for w in ws],
        out_specs=tuple(pl.BlockSpec((w.shape[1] // PIECE, w.shape[0], PIECE), lambda i: (0, 0, 0)) for w in ws),
        out_shape=tuple(jax.ShapeDtypeStruct((w.shape[1] // PIECE, w.shape[0], PIECE), BF16) for w in ws),
        name="prep_weight_pieces",
        compiler_params=pltpu.CompilerParams(dimension_semantics=("arbitrary",), vmem_limit_bytes=VMEM_LIMIT),
    )(*ws)


def _prep_pieces_t_kernel(wt_ref, o_ref):
    for j in range(PREP_PIECES):
        o_ref[j] = wt_ref[j * PIECE:(j + 1) * PIECE, :].T.astype(BF16)


def _prep_pieces_t(wt, n_pieces=None):
    n_dim, k_dim = wt.shape
    n_pieces = n_dim // PIECE if n_pieces is None else n_pieces
    return pl.pallas_call(
        _prep_pieces_t_kernel, grid=(pl.cdiv(n_pieces, PREP_PIECES),),
        in_specs=[pl.BlockSpec((PREP_PIECES * PIECE, k_dim), lambda p: (p, 0))],
        out_specs=pl.BlockSpec((PREP_PIECES, k_dim, PIECE), lambda p: (p, 0, 0)),
        out_shape=jax.ShapeDtypeStruct((n_pieces, k_dim, PIECE), BF16), name="prep_weight_pieces_t",
        compiler_params=pltpu.CompilerParams(dimension_semantics=("arbitrary",),
                                             vmem_limit_bytes=VMEM_LIMIT),
    )(wt)


def _prep_pieces_t_shifted_kernel(blk_ref, o_ref, prev_s, *, shift):
    @pl.when(pl.program_id(0) > 0)
    def _():
        rows = jnp.concatenate([prev_s[shift:, :], blk_ref[:shift, :]], axis=0)
        o_ref[0] = rows.T.astype(BF16)

    prev_s[...] = blk_ref[...]


def _prep_pieces_t_shifted(wt, row0, n_pieces):
    k_dim = wt.shape[1]
    block0, shift = divmod(row0, PIECE)
    assert shift % SUBLANES == 0 and shift > 0
    return pl.pallas_call(
        functools.partial(_prep_pieces_t_shifted_kernel, shift=shift), grid=(n_pieces + 1,),
        in_specs=[pl.BlockSpec((PIECE, k_dim), lambda t: (block0 + t, 0))],
        out_specs=pl.BlockSpec((1, k_dim, PIECE), lambda t: (jnp.maximum(t - 1, 0), 0, 0)),
        out_shape=jax.ShapeDtypeStruct((n_pieces, k_dim, PIECE), BF16),
        scratch_shapes=[pltpu.VMEM((PIECE, k_dim), F32)], name="prep_weight_pieces_t_shifted",
        compiler_params=pltpu.CompilerParams(dimension_semantics=("arbitrary",),
                                             vmem_limit_bytes=VMEM_LIMIT),
    )(wt)


def _prep_dt_kernel(wt_ref, o_ref):
    rows = wt_ref[...]
    keep = lax.broadcasted_iota(jnp.int32, rows.shape, 0) < N_HEADS_M
    o_ref[...] = jnp.where(keep, rows, 0.0).T.astype(BF16)


def _prep_dt(wt, row0):
    k_dim = wt.shape[1]
    assert row0 % LANES == 0
    return pl.pallas_call(
        _prep_dt_kernel, grid=(1,),
        in_specs=[pl.BlockSpec((LANES, k_dim), lambda i: (row0 // LANES, 0))],
        out_specs=pl.BlockSpec((k_dim, LANES), lambda i: (0, 0)),
        out_shape=jax.ShapeDtypeStruct((k_dim, LANES), BF16), name="prep_weight_dt",
        compiler_params=pltpu.CompilerParams(dimension_semantics=("arbitrary",), vmem_limit_bytes=VMEM_LIMIT),
    )(wt)


StateOuts = collections.namedtuple("StateOuts", "k v conv ssm")


def _split3_bf16(x):
    hi = x.astype(BF16)
    r1 = x - hi.astype(F32)
    mid = r1.astype(BF16)
    return hi, mid, (r1 - mid.astype(F32)).astype(BF16)


def _piece_cols(p):
    return slice(p * PIECE, (p + 1) * PIECE)


def _in_proj_pieces(hb, w, dsts):
    for dst, first in zip(dsts, (QKV_PIECE0, Z_PIECE0, XBC_PIECE0)):
        for p in range(dst.shape[1] // PIECE):
            dst[:, _piece_cols(p)] = _mm(hb, w.win[first + p])
            yield


def _out_proj_pieces(x_ref, rows, hb, attn_ref, ssd_ref, w, out_ref):
    merged = []
    for half, (src_ref, w_ref) in enumerate(((attn_ref, w.woa), (ssd_ref, w.wob))):
        parts = []
        for p in range(D_MODEL // PIECE):
            gate = _sigmoid(_mm(hb, w.wgate[half * (D_MODEL // PIECE) + p]))
            yield
            parts.append(gate * _mm(src_ref[...], w_ref[p]))
            yield
        merged.append(jnp.concatenate(parts, axis=1))
    merged = (merged[0] + merged[1]).astype(BF16)
    for p in range(D_MODEL // PIECE):
        out_ref[rows, _piece_cols(p)] = x_ref[rows, _piece_cols(p)] + _mm(merged, w.wo[p])
        yield


def _stage_in_proj(x_ref, rows, w, proj, hnorm_ref):
    hb = _rms(x_ref[rows, :], w.norm1[...]).astype(BF16)
    hnorm_ref[...] = hb
    yield
    yield from _in_proj_pieces(hb, w, (proj.qkv, proj.z, proj.xbc))
    proj.dt[...] = _mm(hb, w.wdt[...])
    yield


def _stage_out_proj(x_ref, rows, mixed, w, x1_ref, hnorm_ref):
    hb = hnorm_ref[...]
    yield
    yield from _out_proj_pieces(x_ref, rows, hb, mixed.attn, mixed.ssd, w, x1_ref)


def _alternate(*gens):
    gens = list(gens)
    while gens:
        for g in list(gens):
            try:
                next(g)
                yield
            except StopIteration:
                gens.remove(g)


def _trace_interleaved(primary, filler):
    for _ in primary:
        next(filler, None)
    for _ in filler:
        pass


def _stage_mix(proj, mixed, rows, carry, w, sinks_ref, first):
    T = CHUNK
    reset = (lambda a: a) if first is None else (lambda a: jnp.where(first, 0.0, a))
    lo = _lane_lo((T, LANES))
    lo2 = _lane_lo((2 * T, LANES))
    rm = lax.broadcasted_iota(jnp.int32, (T, 1), 0) % SUBLANES
    rr = lax.broadcasted_iota(jnp.int32, (T, T), 0)
    cc = lax.broadcasted_iota(jnp.int32, (T, T), 1)
    tril = cc <= rr
    pairs_per_group = N_HEADS_M // 2 // N_GROUPS_M
    n_pairs = N_HEADS_M // 2
    gcols = lambda g: slice(g * GROUP_WIDTH, (g + 1) * GROUP_WIDTH)

    def conv_silu(cols):
        xr = proj.xbc[rows, cols]
        prev_tile = jnp.concatenate([reset(carry.conv_tail[:, cols]), xr[:T - SUBLANES]], axis=0)
        carry.conv_tail[:, cols] = xr[T - SUBLANES:]
        yc = xr * w.convw[CONV_W - 1:CONV_W, cols]
        for shift in range(1, CONV_W):
            shifted = _tile_roll(jnp.where(rm >= SUBLANES - shift, prev_tile, xr), shift)
            yc = yc + shifted * w.convw[CONV_W - 1 - shift:CONV_W - shift, cols]
        return _silu(yc + w.convb[:, cols])

    def softmax(s, kv):
        r = lax.broadcasted_iota(jnp.int32, (T, 2 * T), 0)
        col = lax.broadcasted_iota(jnp.int32, (T, 2 * T), 1)
        valid = (col >= r) & (col <= r + WINDOW)
        if first is not None:
            valid = valid & (col >= jnp.where(first, T, 0))
        es, inv = [], []
        for g in range(Q_PER_KV):
            sk = sinks_ref[kv * Q_PER_KV + g]
            sg = jnp.where(valid, s[g * T:(g + 1) * T], NEG_BIG)
            m = jnp.maximum(jnp.max(sg, axis=-1, keepdims=True), sk)
            e = jnp.exp(sg - m)
            es.append(e.astype(BF16))
            inv.append(1.0 / (jnp.sum(e, axis=-1, keepdims=True) + jnp.exp(sk - m)))
        return jnp.concatenate(es, axis=0), jnp.concatenate(inv, axis=0)

    qkv = proj.qkv[rows, :]
    q = qkv[:, :ATTN_WIDTH] * (HEAD_DIM_A ** -0.5)
    k = qkv[:, ATTN_WIDTH:ATTN_WIDTH + KV_WIDTH]
    v = qkv[:, ATTN_WIDTH + KV_WIDTH:]
    kk = jnp.concatenate([carry.kprev[...], k], axis=0)
    vv = jnp.concatenate([carry.vprev[...], v], axis=0)
    carry.kprev[...] = k
    carry.vprev[...] = v
    carry.dt_buf[...] = _softplus(proj.dt[rows, :] + w.dtb[...])
    da_split = jnp.concatenate(_split3_bf16(carry.dt_buf[...] * -jnp.exp(w.alog[...])), axis=1)
    vds = [_dup_half(vv, lo2, kv == 0).astype(BF16) for kv in range(N_KV_A)]
    scores = [_mm_nt(_stack_heads(q, kv, lo), _dup_half(kk, lo2, kv == 0).astype(BF16))
              for kv in range(N_KV_A)]
    yield

    bc = conv_silu(slice(D_INNER, CONV_DIM))
    bgs = [bc[:, g * D_STATE:(g + 1) * D_STATE].astype(BF16) for g in range(N_GROUPS_M)]
    cgs = [bc[:, (N_GROUPS_M + g) * D_STATE:(N_GROUPS_M + g + 1) * D_STATE].astype(BF16)
           for g in range(N_GROUPS_M)]
    bgts = [bc[:, g * D_STATE:(g + 1) * D_STATE].T.astype(BF16) for g in range(N_GROUPS_M)]
    yield

    es0, inv0 = softmax(scores[0], 0)
    yield

    es1, inv1 = softmax(scores[1], 1)
    cbs = [_mm_nt(cgs[g], bgs[g]) for g in range(N_GROUPS_M)]
    h_prevs = [reset(carry.ht[:, gcols(g)]) for g in range(N_GROUPS_M)]
    y_offs = [_mm(cgs[g], h_prevs[g]) for g in range(N_GROUPS_M)]
    o0 = _mm(es0, vds[0])
    cs = jnp.dot(jnp.where(tril, 1.0, 0.0).astype(BF16), da_split, preferred_element_type=F32)
    yield

    a_cs = cs[:, :LANES] + cs[:, LANES:2 * LANES] + cs[:, 2 * LANES:]
    a_cs_t = a_cs.T
    xs_groups = [conv_silu(gcols(0))]
    o1 = _mm(es1, vds[1])
    yield

    xs_groups.append(conv_silu(gcols(1)))
    mixed.attn[rows, 0:2 * LANES] = _unstack_heads(o0 * inv0, T, lo).astype(BF16)
    yield

    def prep(j):
        g = j // pairs_per_group
        dt = carry.dt_buf[...]
        ws, colbs, dtbs = [], [], []
        for h in (2 * j, 2 * j + 1):
            colb = jnp.broadcast_to(a_cs[:, h:h + 1], (T, T))
            rowb = jnp.broadcast_to(a_cs_t[h:h + 1, :], (T, T))
            seg = jnp.where(tril, jnp.exp(colb - rowb), 0.0)
            ws.append((cbs[g] * seg).astype(BF16))
            colbs.append(colb)
            dtbs.append(jnp.broadcast_to(dt[:, h:h + 1], (T, LANES)))
        dt_e = jnp.where(lo, dtbs[0], dtbs[1])
        acs_e = jnp.where(lo, colbs[0], colbs[1])
        xs_j = xs_groups[g][:, (j % pairs_per_group) * LANES:(j % pairs_per_group + 1) * LANES]
        xdt = xs_j * dt_e
        zero = jnp.zeros_like(xdt)
        rhs = jnp.concatenate([jnp.where(lo, xdt, zero), jnp.where(lo, zero, xdt)], axis=0).astype(BF16)
        alast = acs_e[T - 1:T, :]
        return dict(lhs=jnp.concatenate(ws, axis=1), rhs=rhs, xs=xs_j, e_acs=jnp.exp(acs_e),
                    xd=(xdt * jnp.exp(alast - acs_e)).astype(BF16), decay=jnp.exp(alast))

    def finish(j, p, y_diag):
        g, jj = divmod(j, pairs_per_group)
        sl = slice(j * LANES, (j + 1) * LANES)
        y = y_diag + y_offs[g][:, jj * LANES:(jj + 1) * LANES] * p["e_acs"] + p["xs"] * w.dskip[:, sl]
        return y * _silu(proj.z[rows, sl])

    mixed.attn[rows, 2 * LANES:4 * LANES] = _unstack_heads(o1 * inv1, T, lo).astype(BF16)
    preps = {0: prep(0)}
    yield

    ys, y_diags = [], {}
    for j in range(n_pairs):
        if j + 1 < n_pairs:
            preps[j + 1] = prep(j + 1)
        y_diags[j] = _mm(preps[j]["lhs"], preps[j]["rhs"])
        if j >= 1:
            ys.append(finish(j - 1, preps[j - 1], y_diags.pop(j - 1)))
        if j % pairs_per_group == pairs_per_group - 1:
            g = j // pairs_per_group
            grp = [preps[i] for i in range(g * pairs_per_group, (g + 1) * pairs_per_group)]
            carry.ht[:, gcols(g)] = (h_prevs[g] * jnp.concatenate([p["decay"] for p in grp], axis=1)
                                     + _mm(bgts[g], jnp.concatenate([p["xd"] for p in grp], axis=1)))
        yield
    ys.append(finish(n_pairs - 1, preps[n_pairs - 1], y_diags.pop(n_pairs - 1)))
    yield

    m_slabs = []
    for g in range(N_GROUPS_M):
        grp = ys[g * pairs_per_group:(g + 1) * pairs_per_group]
        ssq = grp[0] * grp[0]
        for y in grp[1:]:
            ssq = ssq + y * y
        scale = lax.rsqrt(jnp.sum(ssq, axis=-1, keepdims=True) * (1.0 / GROUP_WIDTH) + EPS)
        m_slabs.extend([y * scale for y in grp])
    mixed.ssd[rows, :] = (jnp.concatenate(m_slabs, axis=1) * w.ssmn[...]).astype(BF16)


def _chain(*gens):
    for g in gens:
        yield from g


def _prompt_mixer_kernel(sinks_ref, x_in_ref, x_res_ref, *refs, chunks_per_seq, n_steps):
    refs = list(refs)
    take = lambda n: [refs.pop(0) for _ in range(n)]
    w = MixerWeights(*take(len(MixerWeights._fields)))
    mlp_w_f32 = take(2)
    x1_ref, = take(1)
    outs = StateOuts(*take(4))
    mlp_w_bf16 = take(2)
    proj = (ProjBufs(*take(4)), ProjBufs(*take(4)))
    mixed = (MixedBufs(*take(2)), MixedBufs(*take(2)))
    hnorm = take(2)
    carry = Carry(*take(len(Carry._fields)))
    assert not refs
    T = CHUNK
    s = pl.program_id(0)
    every = slice(None)

    @pl.when(s == 0)
    def _():
        for ref in list(proj[1] + mixed[0] + mixed[1] + carry) + hnorm:
            ref[...] = jnp.zeros_like(ref)

    first = (s - 1) % (chunks_per_seq // 2) == 0

    def cast_mlp_weights():
        for src, dst in zip(mlp_w_f32, mlp_w_bf16):
            dst[...] = src[...].astype(BF16)
            yield

    def step(cur):
        prv = 1 - cur
        dense = _alternate(_stage_out_proj(x_res_ref, every, mixed[cur], w, x1_ref, hnorm[cur]),
                           _stage_in_proj(x_in_ref, every, w, proj[cur], hnorm[cur]), cast_mlp_weights())
        mix = _chain(_stage_mix(proj[prv], mixed[prv], slice(0, T), carry, w, sinks_ref, first),
                     _stage_mix(proj[prv], mixed[prv], slice(T, 2 * T), carry, w, sinks_ref, None))
        _trace_interleaved(mix, dense)

    def drain_all(*gens):
        for _ in _alternate(*gens):
            pass

    last = n_steps - 1
    pl.when(s == 0)(lambda: drain_all(_stage_in_proj(x_in_ref, every, w, proj[0], hnorm[0]), cast_mlp_weights()))
    pl.when(s == last)(lambda: drain_all(
        _stage_out_proj(x_res_ref, every, mixed[last % 2], w, x1_ref, hnorm[last % 2]), cast_mlp_weights()))
    for parity in (0, 1):
        pl.when((s > 0) & (s < last) & (s % 2 == parity))(functools.partial(step, parity))

    @pl.when((s >= 1) & (s % (chunks_per_seq // 2) == 0))
    def _():
        outs.k[0] = carry.kprev[...]
        outs.v[0] = carry.vprev[...]
        outs.conv[0] = carry.conv_tail[SUBLANES - (CONV_W - 1):, :]
        outs.ssm[0] = carry.ht[...].T


def _const_spec(shape):
    return pl.BlockSpec(shape, lambda *_: (0,) * len(shape), pipeline_mode=pl.Buffered(1))


def _prompt_mixer(x, sinks, weights, w_up, w_down):
    nb, seq, _ = x.shape
    assert seq % (2 * CHUNK) == 0
    chunks_per_seq = seq // CHUNK
    n_pairs = nb * chunks_per_seq // 2
    n_steps = n_pairs + 2
    pair = 2 * CHUNK
    consts = tuple(weights)
    cast_steps = 1 << ((n_steps // 2).bit_length() - 1)
    ff_slice = D_FF // cast_steps
    assert ff_slice % LANES == 0
    up_block = lambda s: (0, jnp.minimum(s, cast_steps - 1))
    down_block = lambda s: (jnp.clip(s - cast_steps, 0, cast_steps - 1), 0)
    seq_of_mix = lambda s: jnp.clip((2 * s - 1) // chunks_per_seq, 0, nb - 1)
    in_specs = ([pl.BlockSpec(memory_space=pltpu.SMEM),
                 pl.BlockSpec((pair, D_MODEL), lambda s: (jnp.minimum(s, n_pairs - 1), 0)),
                 pl.BlockSpec((pair, D_MODEL), lambda s: (jnp.maximum(s - 2, 0), 0))]
                + [_const_spec(a.shape) for a in consts]
                + [pl.BlockSpec((D_MODEL, ff_slice), up_block), pl.BlockSpec((ff_slice, D_MODEL), down_block)])
    out_shape = (jax.ShapeDtypeStruct((nb * seq, D_MODEL), F32),
                 jax.ShapeDtypeStruct((nb, WINDOW, KV_WIDTH), F32),
                 jax.ShapeDtypeStruct((nb, WINDOW, KV_WIDTH), F32),
                 jax.ShapeDtypeStruct((nb, CONV_W - 1, CONV_DIM), F32),
                 jax.ShapeDtypeStruct((nb, D_INNER, D_STATE), F32),
                 jax.ShapeDtypeStruct(w_up.shape, BF16), jax.ShapeDtypeStruct(w_down.shape, BF16))
    out_specs = (pl.BlockSpec((pair, D_MODEL), lambda s: (jnp.maximum(s - 2, 0), 0)),
                 pl.BlockSpec((1, WINDOW, KV_WIDTH), lambda s: (seq_of_mix(s), 0, 0)),
                 pl.BlockSpec((1, WINDOW, KV_WIDTH), lambda s: (seq_of_mix(s), 0, 0)),
                 pl.BlockSpec((1, CONV_W - 1, CONV_DIM), lambda s: (seq_of_mix(s), 0, 0)),
                 pl.BlockSpec((1, D_INNER, D_STATE), lambda s: (seq_of_mix(s), 0, 0)),
                 pl.BlockSpec((D_MODEL, ff_slice), up_block), pl.BlockSpec((ff_slice, D_MODEL), down_block))
    proj_bufs = [pltpu.VMEM((pair, QKV_WIDTH), F32), pltpu.VMEM((pair, D_INNER), F32),
                 pltpu.VMEM((pair, CONV_DIM), F32), pltpu.VMEM((pair, LANES), F32)]
    mixed_bufs = [pltpu.VMEM((pair, ATTN_WIDTH), BF16), pltpu.VMEM((pair, D_INNER), BF16)]
    hnorm_bufs = [pltpu.VMEM((pair, D_MODEL), BF16)] * 2
    carry = [pltpu.VMEM((CHUNK, KV_WIDTH), F32), pltpu.VMEM((CHUNK, KV_WIDTH), F32),
             pltpu.VMEM((SUBLANES, CONV_DIM), F32), pltpu.VMEM((D_STATE, D_INNER), F32),
             pltpu.VMEM((CHUNK, LANES), F32)]
    x_rows = x.reshape(nb * seq, D_MODEL)
    return pl.pallas_call(
        functools.partial(_prompt_mixer_kernel, chunks_per_seq=chunks_per_seq, n_steps=n_steps),
        grid=(n_steps,), in_specs=in_specs, out_specs=out_specs,
        out_shape=out_shape, scratch_shapes=proj_bufs * 2 + mixed_bufs * 2 + hnorm_bufs + carry,
        name="prompt_mixer",
        compiler_params=pltpu.CompilerParams(dimension_semantics=("arbitrary",),
                                             vmem_limit_bytes=VMEM_LIMIT),
    )(sinks, x_rows, x_rows, *consts, w_up, w_down)


MLP_TILE = 1024
FF_BLOCK = 1024


def _mlp_kernel(x_ref, norm2_ref, wup_ref, wdown_ref, fnorm_ref, y_ref):
    x = x_ref[...]
    hm = _rms(x, norm2_ref[...]).astype(BF16)
    acc = x
    for j in range(D_FF // FF_BLOCK):
        h = _mm(hm, wup_ref[:, j * FF_BLOCK:(j + 1) * FF_BLOCK])
        h = jnp.square(jnp.maximum(h, 0.0))
        acc = acc + _mm(h, wdown_ref[j * FF_BLOCK:(j + 1) * FF_BLOCK, :])
    y_ref[...] = _rms(acc, fnorm_ref[...])


def _mlp(x, norm2, wup, wdown, fnorm):
    rows = x.shape[0]
    tile = min(MLP_TILE, rows)
    assert rows % tile == 0 and tile % SUBLANES == 0
    consts = (norm2, wup, wdown, fnorm)
    return pl.pallas_call(
        _mlp_kernel, grid=(rows // tile,),
        in_specs=[pl.BlockSpec((tile, D_MODEL), lambda i: (i, 0))] + [_const_spec(a.shape) for a in consts],
        out_specs=pl.BlockSpec((tile, D_MODEL), lambda i: (i, 0)),
        out_shape=jax.ShapeDtypeStruct((rows, D_MODEL), F32), name="mlp",
        compiler_params=pltpu.CompilerParams(dimension_semantics=("arbitrary",),
                                             vmem_limit_bytes=VMEM_LIMIT),
    )(x, *consts)


IN_PROJ_ROWS = 256


def _in_proj_kernel(x_ref, norm1_ref, win_ref, wdt_ref, u_ref, dt_ref, hnorm_ref):
    hb = _rms(x_ref[...], norm1_ref[...]).astype(BF16)
    hnorm_ref[...] = hb
    for p in range(win_ref.shape[0]):
        u_ref[:, _piece_cols(p)] = _mm(hb, win_ref[p])
    dt_ref[...] = _mm(hb, wdt_ref[...])


def _in_proj(x, norm1, win, wdt):
    rows = x.shape[0]
    assert rows % IN_PROJ_ROWS == 0
    outs = ((win.shape[0] * PIECE, F32), (LANES, F32), (D_MODEL, BF16))
    row_spec = lambda width: pl.BlockSpec((IN_PROJ_ROWS, width), lambda i: (i, 0))
    return pl.pallas_call(
        _in_proj_kernel, grid=(rows // IN_PROJ_ROWS,),
        in_specs=[row_spec(D_MODEL), _const_spec(norm1.shape), _const_spec(win.shape), _const_spec(wdt.shape)],
        out_specs=tuple(row_spec(width) for width, _ in outs),
        out_shape=tuple(jax.ShapeDtypeStruct((rows, width), dtype) for width, dtype in outs),
        name="sample_in_proj",
        compiler_params=pltpu.CompilerParams(dimension_semantics=("arbitrary",),
                                             vmem_limit_bytes=VMEM_LIMIT),
    )(x, norm1, win, wdt)


def _tile_roll(x, shift):
    rows, width = x.shape
    tiles = x.reshape(rows // SUBLANES, SUBLANES, width)
    return pltpu.roll(tiles, shift % SUBLANES, axis=1).reshape(rows, width)


def _spread_rows(x):
    tiles = []
    for t in range(x.shape[0] // SUBLANES):
        two_seqs = x[t * SUBLANES:(t + 1) * SUBLANES]
        tiles += [two_seqs, pltpu.roll(two_seqs, SUBLANES // 2, axis=0)]
    return jnp.concatenate(tiles, axis=0)


def _gather_rows(y):
    low = lax.broadcasted_iota(jnp.int32, (SUBLANES, 1), 0) < SUBLANES // 2
    tiles = []
    for t in range(y.shape[0] // (2 * SUBLANES)):
        a = y[2 * t * SUBLANES:(2 * t + 1) * SUBLANES]
        b = y[(2 * t + 1) * SUBLANES:(2 * t + 2) * SUBLANES]
        tiles.append(jnp.where(low, a, pltpu.roll(b, SUBLANES // 2, axis=0)))
    return jnp.concatenate(tiles, axis=0)


def _expand_heads(cols, expand):
    terms = []
    for c in cols:
        hi = c.astype(BF16)
        r1 = c - hi.astype(F32)
        mid = r1.astype(BF16)
        lo = (r1 - mid.astype(F32)).astype(BF16)
        terms.extend([hi, mid, lo])
    rows = cols[0].shape[0]
    out = jnp.dot(jnp.concatenate(terms, axis=0), expand, preferred_element_type=F32)
    return [out[(3 * i) * rows:(3 * i + 1) * rows] + out[(3 * i + 1) * rows:(3 * i + 2) * rows]
            + out[(3 * i + 2) * rows:(3 * i + 3) * rows] for i in range(len(cols))]


def _sample_mixer_kernel(sinks_ref, u_ref, dtraw_ref, cprev_ref, kc_ref, vc_ref, st_ref,
                         convw_ref, convb_ref, dtb_ref, alog_ref, dskip_ref, ssmn_ref,
                         aout_ref, mout_ref, nk_ref, nv_ref, nconv_ref, nst_ref):
    R = SAMPLE_GROUP * SAMPLE_PAD
    L = 4
    qkv_ref = u_ref.at[:, QKV_PIECE0 * PIECE:Z_PIECE0 * PIECE]
    z_ref = u_ref.at[:, Z_PIECE0 * PIECE:XBC_PIECE0 * PIECE]
    xbc_ref = u_ref.at[:, XBC_PIECE0 * PIECE:N_IN_PIECES * PIECE]
    rm = lax.broadcasted_iota(jnp.int32, (R, 1), 0) % SAMPLE_PAD

    xr = xbc_raw = _spread_rows(xbc_ref[...])
    with_prev = jnp.where(rm >= SAMPLE_PAD - (CONV_W - 1), cprev_ref[...], xr)
    yc = xr * convw_ref[CONV_W - 1:CONV_W, :]
    for kshift in range(1, CONV_W):
        yc = yc + _tile_roll(with_prev, kshift) * convw_ref[CONV_W - 1 - kshift:CONV_W - kshift, :]
    nconv_ref[...] = _tile_roll(xbc_raw, -1)
    xbc = _silu(yc + convb_ref[...])
    xs = xbc[:, :D_INNER]
    bm = xbc[:, D_INNER:D_INNER + N_GROUPS_M * D_STATE]
    cm = xbc[:, D_INNER + N_GROUPS_M * D_STATE:]

    dt = _softplus(_spread_rows(dtraw_ref[...]) + dtb_ref[...])
    d_a = dt * (-jnp.exp(alog_ref[...]))
    a_cs = d_a
    suf = jnp.zeros_like(d_a)
    for kshift in range(1, L):
        a_cs = a_cs + jnp.where(rm >= kshift, _tile_roll(d_a, kshift), 0.0)
        suf = suf + jnp.where(rm <= L - 1 - kshift, _tile_roll(d_a, -kshift), 0.0)

    expand = (lax.broadcasted_iota(jnp.int32, (LANES, D_INNER), 1) // HEAD_DIM_M
              == lax.broadcasted_iota(jnp.int32, (LANES, D_INNER), 0)).astype(BF16)
    dt_e, acs_e, suf_e = _expand_heads([dt, a_cs, suf], expand)
    xdt = xs * dt_e
    real = rm < L
    xd_t = jnp.where(real, xdt * jnp.exp(suf_e), 0.0).T.astype(BF16)

    y = xs * dskip_ref[...]
    for kshift in range(L):
        bk = bm if kshift == 0 else _tile_roll(bm, kshift)
        cb = cm * bk
        cb_e = jnp.concatenate(
            [jnp.broadcast_to(jnp.sum(cb[:, g * D_STATE:(g + 1) * D_STATE], axis=-1, keepdims=True),
                              (R, GROUP_WIDTH)) for g in range(N_GROUPS_M)], axis=1)
        if kshift == 0:
            y = y + cb_e * xdt
        else:
            seg = jnp.exp(acs_e - _tile_roll(acs_e, kshift))
            y = y + jnp.where(rm >= kshift, cb_e * seg * _tile_roll(xdt, kshift), 0.0)

    qkv = _spread_rows(qkv_ref[...])
    q = qkv[:, :ATTN_WIDTH] * (HEAD_DIM_A ** -0.5)
    q_swapped = jnp.concatenate(
        [pltpu.roll(q[:, s * LANES:(s + 1) * LANES], HEAD_DIM_A, axis=1) for s in range(ATTN_WIDTH // LANES)], axis=1)
    kn = qkv[:, ATTN_WIDTH:ATTN_WIDTH + KV_WIDTH]
    vn = qkv[:, ATTN_WIDTH + KV_WIDTH:]
    lo8 = _lane_lo((SAMPLE_PAD, LANES))
    zero8 = jnp.zeros((SAMPLE_PAD, LANES), F32)
    seq_rows = [slice(i * SAMPLE_PAD, (i + 1) * SAMPLE_PAD) for i in range(SAMPLE_GROUP)]
    HROWS = N_HEADS_A * SAMPLE_PAD

    s_c, s_n = [], []
    for rows in seq_rows:
        pieces = []
        for kv in range(N_KV_A):
            for g in range(Q_PER_KV):
                slab = kv * (Q_PER_KV // 2) + g // 2
                src = q if g % 2 == kv else q_swapped
                piece = src[rows, slab * LANES:(slab + 1) * LANES]
                pieces.append(jnp.where(lo8, piece, zero8) if kv == 0 else jnp.where(lo8, zero8, piece))
        lhs = jnp.concatenate(pieces, axis=0).astype(BF16)
        i = len(s_c)
        s_c.append(_mm(lhs, kc_ref[i]))
        s_n.append(_mm_nt(lhs, kn[rows]))
    s_c = jnp.concatenate(s_c, axis=0)
    s_n = jnp.concatenate(s_n, axis=0)
    n_rows = SAMPLE_GROUP * HROWS
    tok_c = lax.broadcasted_iota(jnp.int32, (n_rows, WINDOW), 0) % SAMPLE_PAD
    s_c = jnp.where(lax.broadcasted_iota(jnp.int32, (n_rows, WINDOW), 1) >= tok_c, s_c, NEG_BIG)
    tok_n = lax.broadcasted_iota(jnp.int32, (n_rows, SAMPLE_PAD), 0) % SAMPLE_PAD
    col_n = lax.broadcasted_iota(jnp.int32, (n_rows, SAMPLE_PAD), 1)
    s_n = jnp.where((col_n <= tok_n) & (col_n < L), s_n, NEG_BIG)
    sk = jnp.concatenate([jnp.full((SAMPLE_PAD, 1), sinks_ref[h], F32) for h in range(N_HEADS_A)] * SAMPLE_GROUP,
                         axis=0)
    m = jnp.maximum(jnp.maximum(jnp.max(s_c, axis=-1, keepdims=True), jnp.max(s_n, axis=-1, keepdims=True)), sk)
    e_c = jnp.exp(s_c - m)
    e_n = jnp.exp(s_n - m)
    inv_den = 1.0 / (jnp.sum(e_c, axis=-1, keepdims=True) + jnp.sum(e_n, axis=-1, keepdims=True) + jnp.exp(sk - m))
    e_c = e_c.astype(BF16)
    e_n = e_n.astype(BF16)
    o = jnp.concatenate([_mm_nt(e_c[i * HROWS:(i + 1) * HROWS], vc_ref[i]) + _mm(e_n[i * HROWS:(i + 1) * HROWS], vn[rows])
                         for i, rows in enumerate(seq_rows)], axis=0) * inv_den
    o_swapped = pltpu.roll(o, HEAD_DIM_A, axis=1)
    a_rows = []
    for i in range(SAMPLE_GROUP):
        blk = lambda arr, kv, g: arr[i * HROWS + (kv * Q_PER_KV + g) * SAMPLE_PAD:
                                     i * HROWS + (kv * Q_PER_KV + g + 1) * SAMPLE_PAD]
        slabs = []
        for kv in range(N_KV_A):
            for j in range(Q_PER_KV // 2):
                first = blk(o if kv == 0 else o_swapped, kv, 2 * j)
                second = blk(o_swapped if kv == 0 else o, kv, 2 * j + 1)
                slabs.append(jnp.where(lo8, first, second))
        a_rows.append(jnp.concatenate(slabs, axis=1))
    aout_ref[...] = _gather_rows(jnp.concatenate(a_rows, axis=0))

    key = lax.broadcasted_iota(jnp.int32, (KV_WIDTH, WINDOW), 1)
    for cache_ref, new, out_ref in ((kc_ref, kn, nk_ref), (vc_ref, vn, nv_ref)):
        new_t = new.T
        for i in range(SAMPLE_GROUP):
            kept = pltpu.roll(cache_ref[i], WINDOW - L, axis=1)
            fresh = pltpu.roll(new_t, (WINDOW - L - i * SAMPLE_PAD) % WINDOW, axis=1)
            out_ref[i] = jnp.where(key >= WINDOW - L, fresh, kept)

    row_r = lax.broadcasted_iota(jnp.int32, (R, D_STATE), 0)
    heads_per_group = N_HEADS_M // N_GROUPS_M
    yoff = []
    for i, rows in enumerate(seq_rows):
        state = st_ref[i]
        state_b = state.astype(BF16)
        a_tot = a_cs[i * SAMPLE_PAD + L - 1:i * SAMPLE_PAD + L, :]
        in_seq = (row_r >= i * SAMPLE_PAD) & (row_r < (i + 1) * SAMPLE_PAD)
        yo = []
        for g in range(N_GROUPS_M):
            gs = slice(g * GROUP_WIDTH, (g + 1) * GROUP_WIDTH)
            yo.append(_mm_nt(cm[rows, g * D_STATE:(g + 1) * D_STATE], state_b[gs]))
            bsel = jnp.where(in_seq, bm[:, g * D_STATE:(g + 1) * D_STATE], 0.0)
            upd = _mm(xd_t[gs, :], bsel)
            for hh in range(heads_per_group):
                h = g * heads_per_group + hh
                hs = slice(h * HEAD_DIM_M, (h + 1) * HEAD_DIM_M)
                decay = jnp.exp(jnp.broadcast_to(a_tot[:, h:h + 1], (HEAD_DIM_M, D_STATE)))
                nst_ref[i, hs, :] = state[hs] * decay + upd[hh * HEAD_DIM_M:(hh + 1) * HEAD_DIM_M]
        yoff.append(jnp.concatenate(yo, axis=1))
    yoff = jnp.concatenate(yoff, axis=0)

    y = (y + yoff * jnp.exp(acs_e)) * _silu(_spread_rows(z_ref[...]))
    outs = []
    for g in range(N_GROUPS_M):
        yg = y[:, g * GROUP_WIDTH:(g + 1) * GROUP_WIDTH]
        outs.append(yg * lax.rsqrt(jnp.mean(yg * yg, axis=-1, keepdims=True) + EPS))
    mout_ref[...] = _gather_rows(jnp.concatenate(outs, axis=1) * ssmn_ref[...])


def _sample_mixer(sinks, u, dtraw, cprev, kc, vc, st, convw, convb, dtb, alog, dskip, ssmn):
    nseq = kc.shape[0]
    assert nseq % SAMPLE_GROUP == 0
    R = SAMPLE_GROUP * SAMPLE_PAD
    T4 = SAMPLE_GROUP * 4
    rows = nseq * SAMPLE_PAD
    consts = (convw, convb, dtb, alog, dskip, ssmn)
    row_spec = lambda w: pl.BlockSpec((R, w), lambda i: (i, 0))
    tok_spec = lambda w: pl.BlockSpec((T4, w), lambda i: (i, 0))
    seq_spec = lambda a, b: pl.BlockSpec((SAMPLE_GROUP, a, b), lambda i: (i, 0, 0))
    in_specs = ([pl.BlockSpec(memory_space=pltpu.SMEM),
                 tok_spec(N_IN_PIECES * PIECE), tok_spec(LANES), row_spec(CONV_DIM),
                 seq_spec(WINDOW, KV_WIDTH), seq_spec(WINDOW, KV_WIDTH), seq_spec(D_INNER, D_STATE)]
                + [_const_spec(a.shape) for a in consts])
    out_shape = (jax.ShapeDtypeStruct((nseq * 4, ATTN_WIDTH), F32), jax.ShapeDtypeStruct((nseq * 4, D_INNER), F32),
                 jax.ShapeDtypeStruct((nseq, WINDOW, KV_WIDTH), F32),
                 jax.ShapeDtypeStruct((nseq, WINDOW, KV_WIDTH), F32),
                 jax.ShapeDtypeStruct((rows, CONV_DIM), F32),
                 jax.ShapeDtypeStruct((nseq, D_INNER, D_STATE), F32))
    out_specs = (tok_spec(ATTN_WIDTH), tok_spec(D_INNER), seq_spec(WINDOW, KV_WIDTH), seq_spec(WINDOW, KV_WIDTH),
                 row_spec(CONV_DIM), seq_spec(D_INNER, D_STATE))
    return pl.pallas_call(
        _sample_mixer_kernel, grid=(nseq // SAMPLE_GROUP,), in_specs=in_specs, out_specs=out_specs,
        out_shape=out_shape, name="sample_mixer",
        compiler_params=pltpu.CompilerParams(dimension_semantics=("arbitrary",),
                                             vmem_limit_bytes=VMEM_LIMIT),
    )(sinks, u, dtraw, cprev, kc, vc, st, *consts)


N_OUT_PIECES = D_MODEL // PIECE
N_FF_BLOCKS = D_FF // FF_BLOCK


def _sample_post_kernel(x_ref, a_ref, m_ref, hnorm_ref, wga_ref, wgb_ref, woa_ref, wob_ref, wo_ref,
                        norm2_ref, wup_ref, wdown_ref, fnorm_ref, y_ref, merged_s, x1_s, hm_s, acc_s):
    s = pl.program_id(0)

    @pl.when(s < N_OUT_PIECES)
    def _():
        hb = hnorm_ref[...]
        piece = (_sigmoid(_mm(hb, wga_ref[0])) * _mm(a_ref[...], woa_ref[0])
                 + _sigmoid(_mm(hb, wgb_ref[0])) * _mm(m_ref[...], wob_ref[0]))
        for p in range(N_OUT_PIECES):
            @pl.when(s == p)
            def _():
                merged_s[:, _piece_cols(p)] = piece.astype(BF16)

    @pl.when((s >= N_OUT_PIECES) & (s < 2 * N_OUT_PIECES))
    def _():
        piece = _mm(merged_s[...], wo_ref[0])
        for p in range(N_OUT_PIECES):
            @pl.when(s == N_OUT_PIECES + p)
            def _():
                x1_s[:, _piece_cols(p)] = x_ref[:, _piece_cols(p)] + piece

    @pl.when(s == 2 * N_OUT_PIECES)
    def _():
        x1 = x1_s[...]
        hm_s[...] = _rms(x1, norm2_ref[...]).astype(BF16)
        acc_s[...] = x1

    @pl.when(s >= 2 * N_OUT_PIECES)
    def _():
        h = jnp.square(jnp.maximum(_mm(hm_s[...], wup_ref[...]), 0.0))
        acc_s[...] += _mm(h, wdown_ref[...])

    @pl.when(s == 2 * N_OUT_PIECES + N_FF_BLOCKS - 1)
    def _():
        y_ref[...] = _rms(acc_s[...], fnorm_ref[...])


def _sample_post(x, a_out, m_out, hnorm, wgate, woa, wob, wo, norm2, wup, wdown, fnorm):
    rows = x.shape[0]
    first = lambda s: jnp.minimum(s, N_OUT_PIECES - 1)
    second = lambda s: jnp.clip(s - N_OUT_PIECES, 0, N_OUT_PIECES - 1)
    ff = lambda s: jnp.clip(s - 2 * N_OUT_PIECES, 0, N_FF_BLOCKS - 1)
    piece_spec = lambda w, index: pl.BlockSpec((1,) + w.shape[1:], lambda s: (index(s), 0, 0))
    in_specs = [_const_spec(x.shape), _const_spec(a_out.shape), _const_spec(m_out.shape), _const_spec(hnorm.shape),
                piece_spec(wgate, first), piece_spec(wgate, lambda s: N_OUT_PIECES + first(s)),
                piece_spec(woa, first), piece_spec(wob, first), piece_spec(wo, second),
                _const_spec(norm2.shape),
                pl.BlockSpec((D_MODEL, FF_BLOCK), lambda s: (0, ff(s))),
                pl.BlockSpec((FF_BLOCK, D_MODEL), lambda s: (ff(s), 0)),
                _const_spec(fnorm.shape)]
    return pl.pallas_call(
        _sample_post_kernel, grid=(2 * N_OUT_PIECES + N_FF_BLOCKS,), in_specs=in_specs,
        out_specs=pl.BlockSpec((rows, D_MODEL), lambda s: (0, 0)),
        out_shape=jax.ShapeDtypeStruct((rows, D_MODEL), F32),
        scratch_shapes=[pltpu.VMEM((rows, D_MODEL), BF16), pltpu.VMEM((rows, D_MODEL), F32),
                        pltpu.VMEM((rows, D_MODEL), BF16), pltpu.VMEM((rows, D_MODEL), F32)],
        name="sample_post",
        compiler_params=pltpu.CompilerParams(dimension_semantics=("arbitrary",),
                                             vmem_limit_bytes=VMEM_LIMIT),
    )(x, a_out, m_out, hnorm, wgate, wgate, woa, wob, wo, norm2, wup, wdown, fnorm)


def _layer_params(norm1, w_in, sinks, conv_w, conv_b, dt_bias, a_log, d_skip, ssm_norm, w_oa, w_ob, w_o,
                  norm2):
    dt0 = QKV_WIDTH + D_INNER + CONV_DIM
    assert dt0 == N_IN_PIECES * PIECE
    pad_lanes = lambda a: jnp.pad(a, ((0, 0), (0, LANES - a.shape[1])))
    w_in_t = w_in.T
    woa, wob, wo = _prep_pieces(w_oa, w_ob, w_o)
    mixer = MixerWeights(
        norm1=norm1[None, :], win=_prep_pieces_t(w_in_t, N_IN_PIECES),
        wgate=_prep_pieces_t_shifted(w_in_t, dt0 + N_HEADS_M, 2 * D_MODEL // PIECE),
        wdt=_prep_dt(w_in_t, dt0),
        convw=conv_w, convb=conv_b[None, :], dtb=pad_lanes(dt_bias[None, :]), alog=pad_lanes(a_log[None, :]),
        dskip=jnp.repeat(d_skip, HEAD_DIM_M)[None, :], ssmn=ssm_norm[None, :],
        woa=woa, wob=wob, wo=wo)
    return dict(sinks=sinks.astype(F32), mixer=mixer, norm2=norm2[None, :])


def kernel(x_prompt, x_sample, cache_swa_k, cache_swa_v, state_conv, state_ssm, norm1, w_in, sinks, conv_w,
           conv_b, dt_bias, a_log, d_skip, ssm_norm, w_oa, w_ob, w_o, norm2, w_up, w_down, final_norm):
    depth = w_in.shape[0]
    assert depth == 1
    nb, seq, _ = x_prompt.shape
    ns, ls, _ = x_sample.shape
    assert ls == 4
    p = _layer_params(norm1[0], w_in[0], sinks[0], conv_w[0], conv_b[0], dt_bias[0], a_log[0], d_skip[0],
                      ssm_norm[0], w_oa[0], w_ob[0], w_o[0], norm2[0])
    fnorm = final_norm[None, :]
    mw = p["mixer"]
    mixer_consts = (mw.convw, mw.convb, mw.dtb, mw.alog, mw.dskip, mw.ssmn)

    x1p, pk, pv, pc, pst, wup, wdown = _prompt_mixer(x_prompt, p["sinks"], mw, w_up[0], w_down[0])
    y_prompt = _mlp(x1p, p["norm2"], wup, wdown, fnorm)
    y_prompt = y_prompt.reshape(nb, seq, D_MODEL)

    xs_rows = x_sample.reshape(ns * ls, D_MODEL)
    dim_major = lambda a: jnp.swapaxes(a, 1, 2)
    cprev = jnp.pad(state_conv[0], ((0, 0), (SAMPLE_PAD - (CONV_W - 1), 0), (0, 0)))
    cprev = cprev.reshape(ns * SAMPLE_PAD, CONV_DIM)
    u, dtraw, hnorm = _in_proj(xs_rows, mw.norm1, mw.win, mw.wdt)
    a_out, m_out, sk, sv, sc, sst = _sample_mixer(
        p["sinks"], u, dtraw, cprev,
        dim_major(cache_swa_k[0].reshape(ns, WINDOW, KV_WIDTH)), dim_major(cache_swa_v[0].reshape(ns, WINDOW, KV_WIDTH)),
        state_ssm[0].reshape(ns, D_INNER, D_STATE), *mixer_consts)
    sk, sv = dim_major(sk), dim_major(sv)
    y_sample = _sample_post(xs_rows, a_out, m_out, hnorm, mw.wgate, mw.woa, mw.wob, mw.wo,
                            p["norm2"], wup, wdown, fnorm).reshape(ns, ls, D_MODEL)
    sc = sc.reshape(ns, SAMPLE_PAD, CONV_DIM)[:, :CONV_W - 1]

    kv_shape = (1, -1, WINDOW, N_KV_A, HEAD_DIM_A)
    ssm_shape = (1, -1, N_HEADS_M, HEAD_DIM_M, D_STATE)
    return (y_prompt, y_sample,
            pk.reshape(kv_shape), pv.reshape(kv_shape), pc[None], pst.reshape(ssm_shape),
            sk.reshape(kv_shape), sv.reshape(kv_shape), sc[None], sst.reshape(ssm_shape))
```

```python
import collections
import functools

import jax
import jax.numpy as jnp
from jax import lax
from jax.experimental import pallas as pl
from jax.experimental.pallas import tpu as pltpu

F32 = jnp.float32
BF16 = jnp.bfloat16

D_MODEL = 1024
N_HEADS_A = 8
N_KV_A = 2
Q_PER_KV = N_HEADS_A // N_KV_A
HEAD_DIM_A = 64
WINDOW = 128
ATTN_WIDTH = N_HEADS_A * HEAD_DIM_A
KV_WIDTH = N_KV_A * HEAD_DIM_A
D_INNER = 1024
HEAD_DIM_M = 64
N_HEADS_M = D_INNER // HEAD_DIM_M
N_GROUPS_M = 2
GROUP_WIDTH = D_INNER // N_GROUPS_M
D_STATE = 128
CONV_W = 4
CONV_DIM = D_INNER + 2 * N_GROUPS_M * D_STATE
CHUNK = 128
D_FF = 4 * D_MODEL
EPS = 1e-6

LANES = 128
SUBLANES = 8
QKV_WIDTH = ATTN_WIDTH + 2 * KV_WIDTH
NEG_BIG = -1e30
VMEM_LIMIT = 56 * 1024 * 1024
SAMPLE_PAD = SUBLANES
SAMPLE_GROUP = 16


def _mm(a, b):
    return jnp.dot(a.astype(BF16), b.astype(BF16), preferred_element_type=F32)


def _mm_nt(a, b):
    return lax.dot_general(a.astype(BF16), b.astype(BF16), (((1,), (1,)), ((), ())),
                           preferred_element_type=F32)


def _rms(x, w):
    return x * lax.rsqrt(jnp.mean(x * x, axis=-1, keepdims=True) + EPS) * w


def _sigmoid(x):
    return 0.5 + 0.5 * jnp.tanh(0.5 * x)


def _silu(x):
    h = 0.5 * x
    return h + h * jnp.tanh(h)


def _softplus(x):
    return jnp.maximum(x, 0.0) + jnp.log(1.0 + jnp.exp(-jnp.abs(x)))


def _lane_lo(shape):
    return (lax.broadcasted_iota(jnp.int32, shape, len(shape) - 1) % LANES) < HEAD_DIM_A


def _dup_half(x, lo, first):
    xr = pltpu.roll(x, HEAD_DIM_A, axis=1)
    return jnp.where(lo, x, xr) if first else jnp.where(lo, xr, x)


def _stack_heads(q, kv, lo):
    qa = q[:, kv * 2 * LANES: kv * 2 * LANES + LANES]
    qb = q[:, kv * 2 * LANES + LANES: (kv + 1) * 2 * LANES]
    zero = jnp.zeros_like(qa)
    return jnp.concatenate([jnp.where(lo, qa, zero), jnp.where(lo, zero, qa),
                            jnp.where(lo, qb, zero), jnp.where(lo, zero, qb)], axis=0)


def _unstack_heads(o, rows, lo):
    return jnp.concatenate([jnp.where(lo, o[0:rows], o[rows:2 * rows]),
                            jnp.where(lo, o[2 * rows:3 * rows], o[3 * rows:4 * rows])], axis=1)


ProjBufs = collections.namedtuple("ProjBufs", "qkv z xbc dt")
MixedBufs = collections.namedtuple("MixedBufs", "attn ssd")
Carry = collections.namedtuple("Carry", "kprev vprev conv_tail ht dt_buf")
MixerWeights = collections.namedtuple(
    "MixerWeights", "norm1 win wgate wdt convw convb dtb alog dskip ssmn woa wob wo")

PIECE = 256
QKV_PIECE0 = 0
Z_PIECE0 = QKV_PIECE0 + QKV_WIDTH // PIECE
XBC_PIECE0 = Z_PIECE0 + D_INNER // PIECE
N_IN_PIECES = XBC_PIECE0 + CONV_DIM // PIECE
PREP_PIECES = 4


def _prep_pieces_kernel(*refs):
    n = len(refs) // 2
    for w_ref, o_ref in zip(refs[:n], refs[n:]):
        for j in range(o_ref.shape[0]):
            o_ref[j] = w_ref[:, j * PIECE:(j + 1) * PIECE].astype(BF16)


def _prep_pieces(*ws):
    return pl.pallas_call(
        _prep_pieces_kernel, grid=(1,),
        in_specs=[_const_spec(w.shape) for w in ws],
        out_specs=tuple(pl.BlockSpec((w.shape[1] // PIECE, w.shape[0], PIECE), lambda i: (0, 0, 0)) for w in ws),
        out_shape=tuple(jax.ShapeDtypeStruct((w.shape[1] // PIECE, w.shape[0], PIECE), BF16) for w in ws),
        name="prep_weight_pieces",
        compiler_params=pltpu.CompilerParams(dimension_semantics=("arbitrary",), vmem_limit_bytes=VMEM_LIMIT),
    )(*ws)


def _prep_pieces_t_kernel(wt_ref, o_ref, dt_ref, *, dt_step, dt_row):
    for j in range(PREP_PIECES):
        o_ref[j] = wt_ref[j * PIECE:(j + 1) * PIECE, :].T.astype(BF16)

    @pl.when(pl.program_id(0) == dt_step)
    def _():
        rows = wt_ref[dt_row:dt_row + LANES, :]
        keep = lax.broadcasted_iota(jnp.int32, rows.shape, 0) < N_HEADS_M
        dt_ref[...] = jnp.where(keep, rows, 0.0).T.astype(BF16)


def _prep_pieces_t(wt, n_pieces, dt_row0):
    k_dim = wt.shape[1]
    block_rows = PREP_PIECES * PIECE
    n_blocks = pl.cdiv(n_pieces, PREP_PIECES)
    dt_step, dt_row = divmod(dt_row0, block_rows)
    assert dt_step < n_blocks and dt_row % SUBLANES == 0 and dt_row + LANES <= block_rows
    return pl.pallas_call(
        functools.partial(_prep_pieces_t_kernel, dt_step=dt_step, dt_row=dt_row), grid=(n_blocks,),
        in_specs=[pl.BlockSpec((block_rows, k_dim), lambda p: (p, 0))],
        out_specs=(pl.BlockSpec((PREP_PIECES, k_dim, PIECE), lambda p: (p, 0, 0)),
                   pl.BlockSpec((k_dim, LANES), lambda p: (0, 0))),
        out_shape=(jax.ShapeDtypeStruct((n_pieces, k_dim, PIECE), BF16),
                   jax.ShapeDtypeStruct((k_dim, LANES), BF16)), name="prep_weight_pieces_t",
        compiler_params=pltpu.CompilerParams(dimension_semantics=("arbitrary",),
                                             vmem_limit_bytes=VMEM_LIMIT),
    )(wt)


def _prep_pieces_t_shifted_kernel(blk_ref, o_ref, prev_s, *, shift):
    @pl.when(pl.program_id(0) > 0)
    def _():
        rows = jnp.concatenate([prev_s[shift:, :], blk_ref[:shift, :]], axis=0)
        o_ref[0] = rows.T.astype(BF16)

    prev_s[...] = blk_ref[...]


def _prep_pieces_t_shifted(wt, row0, n_pieces):
    k_dim = wt.shape[1]
    block0, shift = divmod(row0, PIECE)
    assert shift % SUBLANES == 0 and shift > 0
    return pl.pallas_call(
        functools.partial(_prep_pieces_t_shifted_kernel, shift=shift), grid=(n_pieces + 1,),
        in_specs=[pl.BlockSpec((PIECE, k_dim), lambda t: (block0 + t, 0))],
        out_specs=pl.BlockSpec((1, k_dim, PIECE), lambda t: (jnp.maximum(t - 1, 0), 0, 0)),
        out_shape=jax.ShapeDtypeStruct((n_pieces, k_dim, PIECE), BF16),
        scratch_shapes=[pltpu.VMEM((PIECE, k_dim), F32)], name="prep_weight_pieces_t_shifted",
        compiler_params=pltpu.CompilerParams(dimension_semantics=("arbitrary",),
                                             vmem_limit_bytes=VMEM_LIMIT),
    )(wt)


StateOuts = collections.namedtuple("StateOuts", "k v conv ssm")


def _split3_bf16(x):
    hi = x.astype(BF16)
    r1 = x - hi.astype(F32)
    mid = r1.astype(BF16)
    return hi, mid, (r1 - mid.astype(F32)).astype(BF16)


def _piece_cols(p):
    return slice(p * PIECE, (p + 1) * PIECE)


def _in_proj_pieces(hb, w, dsts):
    for dst, first in zip(dsts, (QKV_PIECE0, Z_PIECE0, XBC_PIECE0)):
        for p in range(dst.shape[1] // PIECE):
            dst[:, _piece_cols(p)] = _mm(hb, w.win[first + p])
            yield


def _out_proj_pieces(x_ref, rows, hb, attn_ref, ssd_ref, w, out_ref):
    merged = []
    for half, (src_ref, w_ref) in enumerate(((attn_ref, w.woa), (ssd_ref, w.wob))):
        parts = []
        for p in range(D_MODEL // PIECE):
            gate = _sigmoid(_mm(hb, w.wgate[half * (D_MODEL // PIECE) + p]))
            yield
            parts.append(gate * _mm(src_ref[...], w_ref[p]))
            yield
        merged.append(jnp.concatenate(parts, axis=1))
    merged = (merged[0] + merged[1]).astype(BF16)
    for p in range(D_MODEL // PIECE):
        out_ref[rows, _piece_cols(p)] = x_ref[rows, _piece_cols(p)] + _mm(merged, w.wo[p])
        yield


def _stage_in_proj(x_ref, rows, w, proj, hnorm_ref):
    hb = _rms(x_ref[rows, :], w.norm1[...]).astype(BF16)
    hnorm_ref[...] = hb
    yield
    yield from _in_proj_pieces(hb, w, (proj.qkv, proj.z, proj.xbc))
    proj.dt[...] = _mm(hb, w.wdt[...])
    yield


def _stage_out_proj(x_ref, rows, mixed, w, x1_ref, hnorm_ref):
    hb = hnorm_ref[...]
    yield
    yield from _out_proj_pieces(x_ref, rows, hb, mixed.attn, mixed.ssd, w, x1_ref)


def _alternate(*gens):
    gens = list(gens)
    while gens:
        for g in list(gens):
            try:
                next(g)
                yield
            except StopIteration:
                gens.remove(g)


def _trace_interleaved(primary, filler):
    for _ in primary:
        next(filler, None)
    for _ in filler:
        pass


def _stage_mix(proj, mixed, rows, carry, w, sinks_ref, first):
    T = CHUNK
    reset = (lambda a: a) if first is None else (lambda a: jnp.where(first, 0.0, a))
    lo = _lane_lo((T, LANES))
    lo2 = _lane_lo((2 * T, LANES))
    rm = lax.broadcasted_iota(jnp.int32, (T, 1), 0) % SUBLANES
    rr = lax.broadcasted_iota(jnp.int32, (T, T), 0)
    cc = lax.broadcasted_iota(jnp.int32, (T, T), 1)
    tril = cc <= rr
    pairs_per_group = N_HEADS_M // 2 // N_GROUPS_M
    n_pairs = N_HEADS_M // 2
    gcols = lambda g: slice(g * GROUP_WIDTH, (g + 1) * GROUP_WIDTH)

    def conv_silu(cols):
        xr = proj.xbc[rows, cols]
        prev_tile = jnp.concatenate([reset(carry.conv_tail[:, cols]), xr[:T - SUBLANES]], axis=0)
        carry.conv_tail[:, cols] = xr[T - SUBLANES:]
        yc = xr * w.convw[CONV_W - 1:CONV_W, cols]
        for shift in range(1, CONV_W):
            shifted = _tile_roll(jnp.where(rm >= SUBLANES - shift, prev_tile, xr), shift)
            yc = yc + shifted * w.convw[CONV_W - 1 - shift:CONV_W - shift, cols]
        return _silu(yc + w.convb[:, cols])

    def softmax(s, kv):
        r = lax.broadcasted_iota(jnp.int32, (T, 2 * T), 0)
        col = lax.broadcasted_iota(jnp.int32, (T, 2 * T), 1)
        valid = (col >= r) & (col <= r + WINDOW)
        if first is not None:
            valid = valid & (col >= jnp.where(first, T, 0))
        es, inv = [], []
        for g in range(Q_PER_KV):
            sk = sinks_ref[kv * Q_PER_KV + g]
            sg = jnp.where(valid, s[g * T:(g + 1) * T], NEG_BIG)
            m = jnp.maximum(jnp.max(sg, axis=-1, keepdims=True), sk)
            e = jnp.exp(sg - m)
            es.append(e.astype(BF16))
            inv.append(1.0 / (jnp.sum(e, axis=-1, keepdims=True) + jnp.exp(sk - m)))
        return jnp.concatenate(es, axis=0), jnp.concatenate(inv, axis=0)

    qkv = proj.qkv[rows, :]
    q = qkv[:, :ATTN_WIDTH] * (HEAD_DIM_A ** -0.5)
    k = qkv[:, ATTN_WIDTH:ATTN_WIDTH + KV_WIDTH]
    v = qkv[:, ATTN_WIDTH + KV_WIDTH:]
    kk = jnp.concatenate([carry.kprev[...], k], axis=0)
    vv = jnp.concatenate([carry.vprev[...], v], axis=0)
    carry.kprev[...] = k
    carry.vprev[...] = v
    carry.dt_buf[...] = _softplus(proj.dt[rows, :] + w.dtb[...])
    da_split = jnp.concatenate(_split3_bf16(carry.dt_buf[...] * -jnp.exp(w.alog[...])), axis=1)
    vds = [_dup_half(vv, lo2, kv == 0).astype(BF16) for kv in range(N_KV_A)]
    scores = [_mm_nt(_stack_heads(q, kv, lo), _dup_half(kk, lo2, kv == 0).astype(BF16))
              for kv in range(N_KV_A)]
    yield

    bc = conv_silu(slice(D_INNER, CONV_DIM))
    bgs = [bc[:, g * D_STATE:(g + 1) * D_STATE].astype(BF16) for g in range(N_GROUPS_M)]
    cgs = [bc[:, (N_GROUPS_M + g) * D_STATE:(N_GROUPS_M + g + 1) * D_STATE].astype(BF16)
           for g in range(N_GROUPS_M)]
    bgts = [bc[:, g * D_STATE:(g + 1) * D_STATE].T.astype(BF16) for g in range(N_GROUPS_M)]
    yield

    es0, inv0 = softmax(scores[0], 0)
    yield

    es1, inv1 = softmax(scores[1], 1)
    cbs = [_mm_nt(cgs[g], bgs[g]) for g in range(N_GROUPS_M)]
    h_prevs = [reset(carry.ht[:, gcols(g)]) for g in range(N_GROUPS_M)]
    y_offs = [_mm(cgs[g], h_prevs[g]) for g in range(N_GROUPS_M)]
    o0 = _mm(es0, vds[0])
    cs = jnp.dot(jnp.where(tril, 1.0, 0.0).astype(BF16), da_split, preferred_element_type=F32)
    yield

    a_cs = cs[:, :LANES] + cs[:, LANES:2 * LANES] + cs[:, 2 * LANES:]
    a_cs_t = a_cs.T
    xs_groups = [conv_silu(gcols(0))]
    o1 = _mm(es1, vds[1])
    yield

    xs_groups.append(conv_silu(gcols(1)))
    mixed.attn[rows, 0:2 * LANES] = _unstack_heads(o0 * inv0, T, lo).astype(BF16)
    yield

    def prep(j):
        g = j // pairs_per_group
        dt = carry.dt_buf[...]
        ws, colbs, dtbs = [], [], []
        for h in (2 * j, 2 * j + 1):
            colb = jnp.broadcast_to(a_cs[:, h:h + 1], (T, T))
            rowb = jnp.broadcast_to(a_cs_t[h:h + 1, :], (T, T))
            seg = jnp.where(tril, jnp.exp(colb - rowb), 0.0)
            ws.append((cbs[g] * seg).astype(BF16))
            colbs.append(colb)
            dtbs.append(jnp.broadcast_to(dt[:, h:h + 1], (T, LANES)))
        dt_e = jnp.where(lo, dtbs[0], dtbs[1])
        acs_e = jnp.where(lo, colbs[0], colbs[1])
        xs_j = xs_groups[g][:, (j % pairs_per_group) * LANES:(j % pairs_per_group + 1) * LANES]
        xdt = xs_j * dt_e
        zero = jnp.zeros_like(xdt)
        rhs = jnp.concatenate([jnp.where(lo, xdt, zero), jnp.where(lo, zero, xdt)], axis=0).astype(BF16)
        alast = acs_e[T - 1:T, :]
        return dict(lhs=jnp.concatenate(ws, axis=1), rhs=rhs, xs=xs_j, e_acs=jnp.exp(acs_e),
                    xd=(xdt * jnp.exp(alast - acs_e)).astype(BF16), decay=jnp.exp(alast))

    def finish(j, p, y_diag):
        g, jj = divmod(j, pairs_per_group)
        sl = slice(j * LANES, (j + 1) * LANES)
        y = y_diag + y_offs[g][:, jj * LANES:(jj + 1) * LANES] * p["e_acs"] + p["xs"] * w.dskip[:, sl]
        return y * _silu(proj.z[rows, sl])

    mixed.attn[rows, 2 * LANES:4 * LANES] = _unstack_heads(o1 * inv1, T, lo).astype(BF16)
    preps = {0: prep(0)}
    yield

    ys, y_diags = [], {}
    for j in range(n_pairs):
        if j + 1 < n_pairs:
            preps[j + 1] = prep(j + 1)
        y_diags[j] = _mm(preps[j]["lhs"], preps[j]["rhs"])
        if j >= 1:
            ys.append(finish(j - 1, preps[j - 1], y_diags.pop(j - 1)))
        if j % pairs_per_group == pairs_per_group - 1:
            g = j // pairs_per_group
            grp = [preps[i] for i in range(g * pairs_per_group, (g + 1) * pairs_per_group)]
            carry.ht[:, gcols(g)] = (h_prevs[g] * jnp.concatenate([p["decay"] for p in grp], axis=1)
                                     + _mm(bgts[g], jnp.concatenate([p["xd"] for p in grp], axis=1)))
        yield
    ys.append(finish(n_pairs - 1, preps[n_pairs - 1], y_diags.pop(n_pairs - 1)))
    yield

    m_slabs = []
    for g in range(N_GROUPS_M):
        grp = ys[g * pairs_per_group:(g + 1) * pairs_per_group]
        ssq = grp[0] * grp[0]
        for y in grp[1:]:
            ssq = ssq + y * y
        scale = lax.rsqrt(jnp.sum(ssq, axis=-1, keepdims=True) * (1.0 / GROUP_WIDTH) + EPS)
        m_slabs.extend([y * scale for y in grp])
    mixed.ssd[rows, :] = (jnp.concatenate(m_slabs, axis=1) * w.ssmn[...]).astype(BF16)


def _chain(*gens):
    for g in gens:
        yield from g


def _prompt_mixer_kernel(sinks_ref, x_in_ref, x_res_ref, *refs, chunks_per_seq, n_steps):
    refs = list(refs)
    take = lambda n: [refs.pop(0) for _ in range(n)]
    w = MixerWeights(*take(len(MixerWeights._fields)))
    mlp_w_f32 = take(2)
    x1_ref, = take(1)
    outs = StateOuts(*take(4))
    mlp_w_bf16 = take(2)
    proj = (ProjBufs(*take(4)), ProjBufs(*take(4)))
    mixed = (MixedBufs(*take(2)), MixedBufs(*take(2)))
    hnorm = take(2)
    carry = Carry(*take(len(Carry._fields)))
    assert not refs
    T = CHUNK
    s = pl.program_id(0)
    every = slice(None)

    @pl.when(s == 0)
    def _():
        for ref in list(proj[1] + mixed[0] + mixed[1] + carry) + hnorm:
            ref[...] = jnp.zeros_like(ref)

    first = (s - 1) % (chunks_per_seq // 2) == 0

    def cast_mlp_weights():
        for src, dst in zip(mlp_w_f32, mlp_w_bf16):
            dst[...] = src[...].astype(BF16)
            yield

    def step(cur):
        prv = 1 - cur
        dense = _alternate(_stage_out_proj(x_res_ref, every, mixed[cur], w, x1_ref, hnorm[cur]),
                           _stage_in_proj(x_in_ref, every, w, proj[cur], hnorm[cur]), cast_mlp_weights())
        mix = _chain(_stage_mix(proj[prv], mixed[prv], slice(0, T), carry, w, sinks_ref, first),
                     _stage_mix(proj[prv], mixed[prv], slice(T, 2 * T), carry, w, sinks_ref, None))
        _trace_interleaved(mix, dense)

    def drain_all(*gens):
        for _ in _alternate(*gens):
            pass

    last = n_steps - 1
    pl.when(s == 0)(lambda: drain_all(_stage_in_proj(x_in_ref, every, w, proj[0], hnorm[0]), cast_mlp_weights()))
    pl.when(s == last)(lambda: drain_all(
        _stage_out_proj(x_res_ref, every, mixed[last % 2], w, x1_ref, hnorm[last % 2]), cast_mlp_weights()))
    for parity in (0, 1):
        pl.when((s > 0) & (s < last) & (s % 2 == parity))(functools.partial(step, parity))

    @pl.when((s >= 1) & (s % (chunks_per_seq // 2) == 0))
    def _():
        outs.k[0] = carry.kprev[...]
        outs.v[0] = carry.vprev[...]
        outs.conv[0] = carry.conv_tail[SUBLANES - (CONV_W - 1):, :]
        outs.ssm[0] = carry.ht[...].T


def _const_spec(shape):
    return pl.BlockSpec(shape, lambda *_: (0,) * len(shape), pipeline_mode=pl.Buffered(1))


def _prompt_mixer(x, sinks, weights, w_up, w_down):
    nb, seq, _ = x.shape
    assert seq % (2 * CHUNK) == 0
    chunks_per_seq = seq // CHUNK
    n_pairs = nb * chunks_per_seq // 2
    n_steps = n_pairs + 2
    pair = 2 * CHUNK
    consts = tuple(weights)
    cast_steps = 1 << ((n_steps // 2).bit_length() - 1)
    ff_slice = D_FF // cast_steps
    assert ff_slice % LANES == 0
    up_block = lambda s: (0, jnp.minimum(s, cast_steps - 1))
    down_block = lambda s: (jnp.clip(s - cast_steps, 0, cast_steps - 1), 0)
    seq_of_mix = lambda s: jnp.clip((2 * s - 1) // chunks_per_seq, 0, nb - 1)
    in_specs = ([pl.BlockSpec(memory_space=pltpu.SMEM),
                 pl.BlockSpec((pair, D_MODEL), lambda s: (jnp.minimum(s, n_pairs - 1), 0)),
                 pl.BlockSpec((pair, D_MODEL), lambda s: (jnp.maximum(s - 2, 0), 0))]
                + [_const_spec(a.shape) for a in consts]
                + [pl.BlockSpec((D_MODEL, ff_slice), up_block), pl.BlockSpec((ff_slice, D_MODEL), down_block)])
    out_shape = (jax.ShapeDtypeStruct((nb * seq, D_MODEL), F32),
                 jax.ShapeDtypeStruct((nb, WINDOW, KV_WIDTH), F32),
                 jax.ShapeDtypeStruct((nb, WINDOW, KV_WIDTH), F32),
                 jax.ShapeDtypeStruct((nb, CONV_W - 1, CONV_DIM), F32),
                 jax.ShapeDtypeStruct((nb, D_INNER, D_STATE), F32),
                 jax.ShapeDtypeStruct(w_up.shape, BF16), jax.ShapeDtypeStruct(w_down.shape, BF16))
    out_specs = (pl.BlockSpec((pair, D_MODEL), lambda s: (jnp.maximum(s - 2, 0), 0)),
                 pl.BlockSpec((1, WINDOW, KV_WIDTH), lambda s: (seq_of_mix(s), 0, 0)),
                 pl.BlockSpec((1, WINDOW, KV_WIDTH), lambda s: (seq_of_mix(s), 0, 0)),
                 pl.BlockSpec((1, CONV_W - 1, CONV_DIM), lambda s: (seq_of_mix(s), 0, 0)),
                 pl.BlockSpec((1, D_INNER, D_STATE), lambda s: (seq_of_mix(s), 0, 0)),
                 pl.BlockSpec((D_MODEL, ff_slice), up_block), pl.BlockSpec((ff_slice, D_MODEL), down_block))
    proj_bufs = [pltpu.VMEM((pair, QKV_WIDTH), F32), pltpu.VMEM((pair, D_INNER), F32),
                 pltpu.VMEM((pair, CONV_DIM), F32), pltpu.VMEM((pair, LANES), F32)]
    mixed_bufs = [pltpu.VMEM((pair, ATTN_WIDTH), BF16), pltpu.VMEM((pair, D_INNER), BF16)]
    hnorm_bufs = [pltpu.VMEM((pair, D_MODEL), BF16)] * 2
    carry = [pltpu.VMEM((CHUNK, KV_WIDTH), F32), pltpu.VMEM((CHUNK, KV_WIDTH), F32),
             pltpu.VMEM((SUBLANES, CONV_DIM), F32), pltpu.VMEM((D_STATE, D_INNER), F32),
             pltpu.VMEM((CHUNK, LANES), F32)]
    x_rows = x.reshape(nb * seq, D_MODEL)
    return pl.pallas_call(
        functools.partial(_prompt_mixer_kernel, chunks_per_seq=chunks_per_seq, n_steps=n_steps),
        grid=(n_steps,), in_specs=in_specs, out_specs=out_specs,
        out_shape=out_shape, scratch_shapes=proj_bufs * 2 + mixed_bufs * 2 + hnorm_bufs + carry,
        name="prompt_mixer",
        compiler_params=pltpu.CompilerParams(dimension_semantics=("arbitrary",),
                                             vmem_limit_bytes=VMEM_LIMIT),
    )(sinks, x_rows, x_rows, *consts, w_up, w_down)


MLP_TILE = 1024
FF_BLOCK = 1024


def _mlp_kernel(x_ref, norm2_ref, wup_ref, wdown_ref, fnorm_ref, y_ref):
    x = x_ref[...]
    hm = _rms(x, norm2_ref[...]).astype(BF16)
    acc = x
    for j in range(D_FF // FF_BLOCK):
        h = _mm(hm, wup_ref[:, j * FF_BLOCK:(j + 1) * FF_BLOCK])
        h = jnp.square(jnp.maximum(h, 0.0))
        acc = acc + _mm(h, wdown_ref[j * FF_BLOCK:(j + 1) * FF_BLOCK, :])
    y_ref[...] = _rms(acc, fnorm_ref[...])


def _mlp(x, norm2, wup, wdown, fnorm):
    rows = x.shape[0]
    tile = min(MLP_TILE, rows)
    assert rows % tile == 0 and tile % SUBLANES == 0
    consts = (norm2, wup, wdown, fnorm)
    return pl.pallas_call(
        _mlp_kernel, grid=(rows // tile,),
        in_specs=[pl.BlockSpec((tile, D_MODEL), lambda i: (i, 0))] + [_const_spec(a.shape) for a in consts],
        out_specs=pl.BlockSpec((tile, D_MODEL), lambda i: (i, 0)),
        out_shape=jax.ShapeDtypeStruct((rows, D_MODEL), F32), name="mlp",
        compiler_params=pltpu.CompilerParams(dimension_semantics=("arbitrary",),
                                             vmem_limit_bytes=VMEM_LIMIT),
    )(x, *consts)


IN_PROJ_ROWS = 256


def _in_proj_kernel(x_ref, norm1_ref, win_ref, wdt_ref, u_ref, dt_ref, hnorm_ref):
    hb = _rms(x_ref[...], norm1_ref[...]).astype(BF16)
    hnorm_ref[...] = hb
    for p in range(win_ref.shape[0]):
        u_ref[:, _piece_cols(p)] = _mm(hb, win_ref[p])
    dt_ref[...] = _mm(hb, wdt_ref[...])


def _in_proj(x, norm1, win, wdt):
    rows = x.shape[0]
    assert rows % IN_PROJ_ROWS == 0
    outs = ((win.shape[0] * PIECE, F32), (LANES, F32), (D_MODEL, BF16))
    row_spec = lambda width: pl.BlockSpec((IN_PROJ_ROWS, width), lambda i: (i, 0))
    return pl.pallas_call(
        _in_proj_kernel, grid=(rows // IN_PROJ_ROWS,),
        in_specs=[row_spec(D_MODEL), _const_spec(norm1.shape), _const_spec(win.shape), _const_spec(wdt.shape)],
        out_specs=tuple(row_spec(width) for width, _ in outs),
        out_shape=tuple(jax.ShapeDtypeStruct((rows, width), dtype) for width, dtype in outs),
        name="sample_in_proj",
        compiler_params=pltpu.CompilerParams(dimension_semantics=("arbitrary",),
                                             vmem_limit_bytes=VMEM_LIMIT),
    )(x, norm1, win, wdt)


def _tile_roll(x, shift):
    rows, width = x.shape
    tiles = x.reshape(rows // SUBLANES, SUBLANES, width)
    return pltpu.roll(tiles, shift % SUBLANES, axis=1).reshape(rows, width)


def _spread_rows(x):
    tiles = []
    for t in range(x.shape[0] // SUBLANES):
        two_seqs = x[t * SUBLANES:(t + 1) * SUBLANES]
        tiles += [two_seqs, pltpu.roll(two_seqs, SUBLANES // 2, axis=0)]
    return jnp.concatenate(tiles, axis=0)


def _gather_rows(y):
    low = lax.broadcasted_iota(jnp.int32, (SUBLANES, 1), 0) < SUBLANES // 2
    tiles = []
    for t in range(y.shape[0] // (2 * SUBLANES)):
        a = y[2 * t * SUBLANES:(2 * t + 1) * SUBLANES]
        b = y[(2 * t + 1) * SUBLANES:(2 * t + 2) * SUBLANES]
        tiles.append(jnp.where(low, a, pltpu.roll(b, SUBLANES // 2, axis=0)))
    return jnp.concatenate(tiles, axis=0)


def _expand_heads(cols, expand):
    terms = []
    for c in cols:
        hi = c.astype(BF16)
        r1 = c - hi.astype(F32)
        mid = r1.astype(BF16)
        lo = (r1 - mid.astype(F32)).astype(BF16)
        terms.extend([hi, mid, lo])
    rows = cols[0].shape[0]
    out = jnp.dot(jnp.concatenate(terms, axis=0), expand, preferred_element_type=F32)
    return [out[(3 * i) * rows:(3 * i + 1) * rows] + out[(3 * i + 1) * rows:(3 * i + 2) * rows]
            + out[(3 * i + 2) * rows:(3 * i + 3) * rows] for i in range(len(cols))]


def _sample_mixer_kernel(sinks_ref, u_ref, dtraw_ref, cprev_ref, kc_ref, vc_ref, st_ref,
                         convw_ref, convb_ref, dtb_ref, alog_ref, dskip_ref, ssmn_ref,
                         aout_ref, mout_ref, nk_ref, nv_ref, nconv_ref, nst_ref):
    R = SAMPLE_GROUP * SAMPLE_PAD
    L = 4
    qkv_ref = u_ref.at[:, QKV_PIECE0 * PIECE:Z_PIECE0 * PIECE]
    z_ref = u_ref.at[:, Z_PIECE0 * PIECE:XBC_PIECE0 * PIECE]
    xbc_ref = u_ref.at[:, XBC_PIECE0 * PIECE:N_IN_PIECES * PIECE]
    rm = lax.broadcasted_iota(jnp.int32, (R, 1), 0) % SAMPLE_PAD

    xr = xbc_raw = _spread_rows(xbc_ref[...])
    with_prev = jnp.where(rm >= SAMPLE_PAD - (CONV_W - 1), cprev_ref[...], xr)
    yc = xr * convw_ref[CONV_W - 1:CONV_W, :]
    for kshift in range(1, CONV_W):
        yc = yc + _tile_roll(with_prev, kshift) * convw_ref[CONV_W - 1 - kshift:CONV_W - kshift, :]
    nconv_ref[...] = _tile_roll(xbc_raw, -1)
    xbc = _silu(yc + convb_ref[...])
    xs = xbc[:, :D_INNER]
    bm = xbc[:, D_INNER:D_INNER + N_GROUPS_M * D_STATE]
    cm = xbc[:, D_INNER + N_GROUPS_M * D_STATE:]

    dt = _softplus(_spread_rows(dtraw_ref[...]) + dtb_ref[...])
    d_a = dt * (-jnp.exp(alog_ref[...]))
    a_cs = d_a
    suf = jnp.zeros_like(d_a)
    for kshift in range(1, L):
        a_cs = a_cs + jnp.where(rm >= kshift, _tile_roll(d_a, kshift), 0.0)
        suf = suf + jnp.where(rm <= L - 1 - kshift, _tile_roll(d_a, -kshift), 0.0)

    expand = (lax.broadcasted_iota(jnp.int32, (LANES, D_INNER), 1) // HEAD_DIM_M
              == lax.broadcasted_iota(jnp.int32, (LANES, D_INNER), 0)).astype(BF16)
    dt_e, acs_e, suf_e = _expand_heads([dt, a_cs, suf], expand)
    xdt = xs * dt_e
    real = rm < L
    xd_t = jnp.where(real, xdt * jnp.exp(suf_e), 0.0).T.astype(BF16)

    y = xs * dskip_ref[...]
    for kshift in range(L):
        bk = bm if kshift == 0 else _tile_roll(bm, kshift)
        cb = cm * bk
        cb_e = jnp.concatenate(
            [jnp.broadcast_to(jnp.sum(cb[:, g * D_STATE:(g + 1) * D_STATE], axis=-1, keepdims=True),
                              (R, GROUP_WIDTH)) for g in range(N_GROUPS_M)], axis=1)
        if kshift == 0:
            y = y + cb_e * xdt
        else:
            seg = jnp.exp(acs_e - _tile_roll(acs_e, kshift))
            y = y + jnp.where(rm >= kshift, cb_e * seg * _tile_roll(xdt, kshift), 0.0)

    qkv = _spread_rows(qkv_ref[...])
    q = qkv[:, :ATTN_WIDTH] * (HEAD_DIM_A ** -0.5)
    q_swapped = jnp.concatenate(
        [pltpu.roll(q[:, s * LANES:(s + 1) * LANES], HEAD_DIM_A, axis=1) for s in range(ATTN_WIDTH // LANES)], axis=1)
    kn = qkv[:, ATTN_WIDTH:ATTN_WIDTH + KV_WIDTH]
    vn = qkv[:, ATTN_WIDTH + KV_WIDTH:]
    lo8 = _lane_lo((SAMPLE_PAD, LANES))
    zero8 = jnp.zeros((SAMPLE_PAD, LANES), F32)
    seq_rows = [slice(i * SAMPLE_PAD, (i + 1) * SAMPLE_PAD) for i in range(SAMPLE_GROUP)]
    HROWS = N_HEADS_A * SAMPLE_PAD

    s_c, s_n = [], []
    for rows in seq_rows:
        pieces = []
        for kv in range(N_KV_A):
            for g in range(Q_PER_KV):
                slab = kv * (Q_PER_KV // 2) + g // 2
                src = q if g % 2 == kv else q_swapped
                piece = src[rows, slab * LANES:(slab + 1) * LANES]
                pieces.append(jnp.where(lo8, piece, zero8) if kv == 0 else jnp.where(lo8, zero8, piece))
        lhs = jnp.concatenate(pieces, axis=0).astype(BF16)
        i = len(s_c)
        s_c.append(_mm(lhs, kc_ref[i]))
        s_n.append(_mm_nt(lhs, kn[rows]))
    s_c = jnp.concatenate(s_c, axis=0)
    s_n = jnp.concatenate(s_n, axis=0)
    n_rows = SAMPLE_GROUP * HROWS
    tok_c = lax.broadcasted_iota(jnp.int32, (n_rows, WINDOW), 0) % SAMPLE_PAD
    s_c = jnp.where(lax.broadcasted_iota(jnp.int32, (n_rows, WINDOW), 1) >= tok_c, s_c, NEG_BIG)
    tok_n = lax.broadcasted_iota(jnp.int32, (n_rows, SAMPLE_PAD), 0) % SAMPLE_PAD
    col_n = lax.broadcasted_iota(jnp.int32, (n_rows, SAMPLE_PAD), 1)
    s_n = jnp.where((col_n <= tok_n) & (col_n < L), s_n, NEG_BIG)
    sk = jnp.concatenate([jnp.full((SAMPLE_PAD, 1), sinks_ref[h], F32) for h in range(N_HEADS_A)] * SAMPLE_GROUP,
                         axis=0)
    m = jnp.maximum(jnp.maximum(jnp.max(s_c, axis=-1, keepdims=True), jnp.max(s_n, axis=-1, keepdims=True)), sk)
    e_c = jnp.exp(s_c - m)
    e_n = jnp.exp(s_n - m)
    inv_den = 1.0 / (jnp.sum(e_c, axis=-1, keepdims=True) + jnp.sum(e_n, axis=-1, keepdims=True) + jnp.exp(sk - m))
    e_c = e_c.astype(BF16)
    e_n = e_n.astype(BF16)
    o = jnp.concatenate([_mm_nt(e_c[i * HROWS:(i + 1) * HROWS], vc_ref[i]) + _mm(e_n[i * HROWS:(i + 1) * HROWS], vn[rows])
                         for i, rows in enumerate(seq_rows)], axis=0) * inv_den
    o_swapped = pltpu.roll(o, HEAD_DIM_A, axis=1)
    a_rows = []
    for i in range(SAMPLE_GROUP):
        blk = lambda arr, kv, g: arr[i * HROWS + (kv * Q_PER_KV + g) * SAMPLE_PAD:
                                     i * HROWS + (kv * Q_PER_KV + g + 1) * SAMPLE_PAD]
        slabs = []
        for kv in range(N_KV_A):
            for j in range(Q_PER_KV // 2):
                first = blk(o if kv == 0 else o_swapped, kv, 2 * j)
                second = blk(o_swapped if kv == 0 else o, kv, 2 * j + 1)
                slabs.append(jnp.where(lo8, first, second))
        a_rows.append(jnp.concatenate(slabs, axis=1))
    aout_ref[...] = _gather_rows(jnp.concatenate(a_rows, axis=0))

    key = lax.broadcasted_iota(jnp.int32, (KV_WIDTH, WINDOW), 1)
    for cache_ref, new, out_ref in ((kc_ref, kn, nk_ref), (vc_ref, vn, nv_ref)):
        new_t = new.T
        for i in range(SAMPLE_GROUP):
            kept = pltpu.roll(cache_ref[i], WINDOW - L, axis=1)
            fresh = pltpu.roll(new_t, (WINDOW - L - i * SAMPLE_PAD) % WINDOW, axis=1)
            out_ref[i] = jnp.where(key >= WINDOW - L, fresh, kept)

    row_r = lax.broadcasted_iota(jnp.int32, (R, D_STATE), 0)
    heads_per_group = N_HEADS_M // N_GROUPS_M
    yoff = []
    for i, rows in enumerate(seq_rows):
        state = st_ref[i]
        state_b = state.astype(BF16)
        a_tot = a_cs[i * SAMPLE_PAD + L - 1:i * SAMPLE_PAD + L, :]
        in_seq = (row_r >= i * SAMPLE_PAD) & (row_r < (i + 1) * SAMPLE_PAD)
        yo = []
        for g in range(N_GROUPS_M):
            gs = slice(g * GROUP_WIDTH, (g + 1) * GROUP_WIDTH)
            yo.append(_mm_nt(cm[rows, g * D_STATE:(g + 1) * D_STATE], state_b[gs]))
            bsel = jnp.where(in_seq, bm[:, g * D_STATE:(g + 1) * D_STATE], 0.0)
            upd = _mm(xd_t[gs, :], bsel)
            for hh in range(heads_per_group):
                h = g * heads_per_group + hh
                hs = slice(h * HEAD_DIM_M, (h + 1) * HEAD_DIM_M)
                decay = jnp.exp(jnp.broadcast_to(a_tot[:, h:h + 1], (HEAD_DIM_M, D_STATE)))
                nst_ref[i, hs, :] = state[hs] * decay + upd[hh * HEAD_DIM_M:(hh + 1) * HEAD_DIM_M]
        yoff.append(jnp.concatenate(yo, axis=1))
    yoff = jnp.concatenate(yoff, axis=0)

    y = (y + yoff * jnp.exp(acs_e)) * _silu(_spread_rows(z_ref[...]))
    outs = []
    for g in range(N_GROUPS_M):
        yg = y[:, g * GROUP_WIDTH:(g + 1) * GROUP_WIDTH]
        outs.append(yg * lax.rsqrt(jnp.mean(yg * yg, axis=-1, keepdims=True) + EPS))
    mout_ref[...] = _gather_rows(jnp.concatenate(outs, axis=1) * ssmn_ref[...])


def _sample_mixer(sinks, u, dtraw, cprev, kc, vc, st, convw, convb, dtb, alog, dskip, ssmn):
    nseq = kc.shape[0]
    assert nseq % SAMPLE_GROUP == 0
    R = SAMPLE_GROUP * SAMPLE_PAD
    T4 = SAMPLE_GROUP * 4
    rows = nseq * SAMPLE_PAD
    consts = (convw, convb, dtb, alog, dskip, ssmn)
    row_spec = lambda w: pl.BlockSpec((R, w), lambda i: (i, 0))
    tok_spec = lambda w: pl.BlockSpec((T4, w), lambda i: (i, 0))
    seq_spec = lambda a, b: pl.BlockSpec((SAMPLE_GROUP, a, b), lambda i: (i, 0, 0))
    in_specs = ([pl.BlockSpec(memory_space=pltpu.SMEM),
                 tok_spec(N_IN_PIECES * PIECE), tok_spec(LANES), row_spec(CONV_DIM),
                 seq_spec(WINDOW, KV_WIDTH), seq_spec(WINDOW, KV_WIDTH), seq_spec(D_INNER, D_STATE)]
                + [_const_spec(a.shape) for a in consts])
    out_shape = (jax.ShapeDtypeStruct((nseq * 4, ATTN_WIDTH), F32), jax.ShapeDtypeStruct((nseq * 4, D_INNER), F32),
                 jax.ShapeDtypeStruct((nseq, WINDOW, KV_WIDTH), F32),
                 jax.ShapeDtypeStruct((nseq, WINDOW, KV_WIDTH), F32),
                 jax.ShapeDtypeStruct((rows, CONV_DIM), F32),
                 jax.ShapeDtypeStruct((nseq, D_INNER, D_STATE), F32))
    out_specs = (tok_spec(ATTN_WIDTH), tok_spec(D_INNER), seq_spec(WINDOW, KV_WIDTH), seq_spec(WINDOW, KV_WIDTH),
                 row_spec(CONV_DIM), seq_spec(D_INNER, D_STATE))
    return pl.pallas_call(
        _sample_mixer_kernel, grid=(nseq // SAMPLE_GROUP,), in_specs=in_specs, out_specs=out_specs,
        out_shape=out_shape, name="sample_mixer",
        compiler_params=pltpu.CompilerParams(dimension_semantics=("arbitrary",),
                                             vmem_limit_bytes=VMEM_LIMIT),
    )(sinks, u, dtraw, cprev, kc, vc, st, *consts)


N_OUT_PIECES = D_MODEL // PIECE
N_FF_BLOCKS = D_FF // FF_BLOCK


def _sample_post_kernel(x_ref, a_ref, m_ref, hnorm_ref, wga_ref, wgb_ref, woa_ref, wob_ref, wo_ref,
                        norm2_ref, wup_ref, wdown_ref, fnorm_ref, y_ref, merged_s, x1_s, hm_s, acc_s):
    s = pl.program_id(0)

    @pl.when(s < N_OUT_PIECES)
    def _():
        hb = hnorm_ref[...]
        piece = (_sigmoid(_mm(hb, wga_ref[0])) * _mm(a_ref[...], woa_ref[0])
                 + _sigmoid(_mm(hb, wgb_ref[0])) * _mm(m_ref[...], wob_ref[0]))
        for p in range(N_OUT_PIECES):
            @pl.when(s == p)
            def _():
                merged_s[:, _piece_cols(p)] = piece.astype(BF16)

    @pl.when((s >= N_OUT_PIECES) & (s < 2 * N_OUT_PIECES))
    def _():
        piece = _mm(merged_s[...], wo_ref[0])
        for p in range(N_OUT_PIECES):
            @pl.when(s == N_OUT_PIECES + p)
            def _():
                x1_s[:, _piece_cols(p)] = x_ref[:, _piece_cols(p)] + piece

    @pl.when(s == 2 * N_OUT_PIECES)
    def _():
        x1 = x1_s[...]
        hm_s[...] = _rms(x1, norm2_ref[...]).astype(BF16)
        acc_s[...] = x1

    @pl.when(s >= 2 * N_OUT_PIECES)
    def _():
        h = jnp.square(jnp.maximum(_mm(hm_s[...], wup_ref[...]), 0.0))
        acc_s[...] += _mm(h, wdown_ref[...])

    @pl.when(s == 2 * N_OUT_PIECES + N_FF_BLOCKS - 1)
    def _():
        y_ref[...] = _rms(acc_s[...], fnorm_ref[...])


def _sample_post(x, a_out, m_out, hnorm, wgate, woa, wob, wo, norm2, wup, wdown, fnorm):
    rows = x.shape[0]
    first = lambda s: jnp.minimum(s, N_OUT_PIECES - 1)
    second = lambda s: jnp.clip(s - N_OUT_PIECES, 0, N_OUT_PIECES - 1)
    ff = lambda s: jnp.clip(s - 2 * N_OUT_PIECES, 0, N_FF_BLOCKS - 1)
    piece_spec = lambda w, index: pl.BlockSpec((1,) + w.shape[1:], lambda s: (index(s), 0, 0))
    in_specs = [_const_spec(x.shape), _const_spec(a_out.shape), _const_spec(m_out.shape), _const_spec(hnorm.shape),
                piece_spec(wgate, first), piece_spec(wgate, lambda s: N_OUT_PIECES + first(s)),
                piece_spec(woa, first), piece_spec(wob, first), piece_spec(wo, second),
                _const_spec(norm2.shape),
                pl.BlockSpec((D_MODEL, FF_BLOCK), lambda s: (0, ff(s))),
                pl.BlockSpec((FF_BLOCK, D_MODEL), lambda s: (ff(s), 0)),
                _const_spec(fnorm.shape)]
    return pl.pallas_call(
        _sample_post_kernel, grid=(2 * N_OUT_PIECES + N_FF_BLOCKS,), in_specs=in_specs,
        out_specs=pl.BlockSpec((rows, D_MODEL), lambda s: (0, 0)),
        out_shape=jax.ShapeDtypeStruct((rows, D_MODEL), F32),
        scratch_shapes=[pltpu.VMEM((rows, D_MODEL), BF16), pltpu.VMEM((rows, D_MODEL), F32),
                        pltpu.VMEM((rows, D_MODEL), BF16), pltpu.VMEM((rows, D_MODEL), F32)],
        name="sample_post",
        compiler_params=pltpu.CompilerParams(dimension_semantics=("arbitrary",),
                                             vmem_limit_bytes=VMEM_LIMIT),
    )(x, a_out, m_out, hnorm, wgate, wgate, woa, wob, wo, norm2, wup, wdown, fnorm)


def _layer_params(norm1, w_in, sinks, conv_w, conv_b, dt_bias, a_log, d_skip, ssm_norm, w_oa, w_ob, w_o,
                  norm2):
    dt0 = QKV_WIDTH + D_INNER + CONV_DIM
    assert dt0 == N_IN_PIECES * PIECE
    pad_lanes = lambda a: jnp.pad(a, ((0, 0), (0, LANES - a.shape[1])))
    w_in_t = w_in.T
    woa, wob, wo = _prep_pieces(w_oa, w_ob, w_o)
    win, wdt = _prep_pieces_t(w_in_t, N_IN_PIECES, dt0)
    mixer = MixerWeights(
        norm1=norm1[None, :], win=win,
        wgate=_prep_pieces_t_shifted(w_in_t, dt0 + N_HEADS_M, 2 * D_MODEL // PIECE),
        wdt=wdt,
        convw=conv_w, convb=conv_b[None, :], dtb=pad_lanes(dt_bias[None, :]), alog=pad_lanes(a_log[None, :]),
        dskip=jnp.repeat(d_skip, HEAD_DIM_M)[None, :], ssmn=ssm_norm[None, :],
        woa=woa, wob=wob, wo=wo)
    return dict(sinks=sinks.astype(F32), mixer=mixer, norm2=norm2[None, :])


def kernel(x_prompt, x_sample, cache_swa_k, cache_swa_v, state_conv, state_ssm, norm1, w_in, sinks, conv_w,
           conv_b, dt_bias, a_log, d_skip, ssm_norm, w_oa, w_ob, w_o, norm2, w_up, w_down, final_norm):
    depth = w_in.shape[0]
    assert depth == 1
    nb, seq, _ = x_prompt.shape
    ns, ls, _ = x_sample.shape
    assert ls == 4
    p = _layer_params(norm1[0], w_in[0], sinks[0], conv_w[0], conv_b[0], dt_bias[0], a_log[0], d_skip[0],
                      ssm_norm[0], w_oa[0], w_ob[0], w_o[0], norm2[0])
    fnorm = final_norm[None, :]
    mw = p["mixer"]
    mixer_consts = (mw.convw, mw.convb, mw.dtb, mw.alog, mw.dskip, mw.ssmn)

    x1p, pk, pv, pc, pst, wup, wdown = _prompt_mixer(x_prompt, p["sinks"], mw, w_up[0], w_down[0])
    y_prompt = _mlp(x1p, p["norm2"], wup, wdown, fnorm)
    y_prompt = y_prompt.reshape(nb, seq, D_MODEL)

    xs_rows = x_sample.reshape(ns * ls, D_MODEL)
    dim_major = lambda a: jnp.swapaxes(a, 1, 2)
    cprev = jnp.pad(state_conv[0], ((0, 0), (SAMPLE_PAD - (CONV_W - 1), 0), (0, 0)))
    cprev = cprev.reshape(ns * SAMPLE_PAD, CONV_DIM)
    u, dtraw, hnorm = _in_proj(xs_rows, mw.norm1, mw.win, mw.wdt)
    a_out, m_out, sk, sv, sc, sst = _sample_mixer(
        p["sinks"], u, dtraw, cprev,
        dim_major(cache_swa_k[0].reshape(ns, WINDOW, KV_WIDTH)), dim_major(cache_swa_v[0].reshape(ns, WINDOW, KV_WIDTH)),
        state_ssm[0].reshape(ns, D_INNER, D_STATE), *mixer_consts)
    sk, sv = dim_major(sk), dim_major(sv)
    y_sample = _sample_post(xs_rows, a_out, m_out, hnorm, mw.wgate, mw.woa, mw.wob, mw.wo,
                            p["norm2"], wup, wdown, fnorm).reshape(ns, ls, D_MODEL)
    sc = sc.reshape(ns, SAMPLE_PAD, CONV_DIM)[:, :CONV_W - 1]

    kv_shape = (1, -1, WINDOW, N_KV_A, HEAD_DIM_A)
    ssm_shape = (1, -1, N_HEADS_M, HEAD_DIM_M, D_STATE)
    return (y_prompt, y_sample,
            pk.reshape(kv_shape), pv.reshape(kv_shape), pc[None], pst.reshape(ssm_shape),
            sk.reshape(kv_shape), sv.reshape(kv_shape), sc[None], sst.reshape(ssm_shape))
```

```python
import collections
import functools

import jax
import jax.numpy as jnp
from jax import lax
from jax.experimental import pallas as pl
from jax.experimental.pallas import tpu as pltpu

F32 = jnp.float32
BF16 = jnp.bfloat16

D_MODEL = 1024
N_HEADS_A = 8
N_KV_A = 2
Q_PER_KV = N_HEADS_A // N_KV_A
HEAD_DIM_A = 64
WINDOW = 128
ATTN_WIDTH = N_HEADS_A * HEAD_DIM_A
KV_WIDTH = N_KV_A * HEAD_DIM_A
D_INNER = 1024
HEAD_DIM_M = 64
N_HEADS_M = D_INNER // HEAD_DIM_M
N_GROUPS_M = 2
GROUP_WIDTH = D_INNER // N_GROUPS_M
D_STATE = 128
CONV_W = 4
CONV_DIM = D_INNER + 2 * N_GROUPS_M * D_STATE
CHUNK = 128
D_FF = 4 * D_MODEL
EPS = 1e-6

LANES = 128
SUBLANES = 8
QKV_WIDTH = ATTN_WIDTH + 2 * KV_WIDTH
NEG_BIG = -1e30
VMEM_LIMIT = 56 * 1024 * 1024
SAMPLE_PAD = SUBLANES
SAMPLE_GROUP = 16


def _mm(a, b):
    return jnp.dot(a.astype(BF16), b.astype(BF16), preferred_element_type=F32)


def _mm_nt(a, b):
    return lax.dot_general(a.astype(BF16), b.astype(BF16), (((1,), (1,)), ((), ())),
                           preferred_element_type=F32)


def _rms(x, w):
    return x * lax.rsqrt(jnp.mean(x * x, axis=-1, keepdims=True) + EPS) * w


def _sigmoid(x):
    return 0.5 + 0.5 * jnp.tanh(0.5 * x)


def _silu(x):
    h = 0.5 * x
    return h + h * jnp.tanh(h)


def _softplus(x):
    return jnp.maximum(x, 0.0) + jnp.log(1.0 + jnp.exp(-jnp.abs(x)))


def _lane_lo(shape):
    return (lax.broadcasted_iota(jnp.int32, shape, len(shape) - 1) % LANES) < HEAD_DIM_A


def _dup_half(x, lo, first):
    xr = pltpu.roll(x, HEAD_DIM_A, axis=1)
    return jnp.where(lo, x, xr) if first else jnp.where(lo, xr, x)


def _stack_heads(q, kv, lo):
    qa = q[:, kv * 2 * LANES: kv * 2 * LANES + LANES]
    qb = q[:, kv * 2 * LANES + LANES: (kv + 1) * 2 * LANES]
    zero = jnp.zeros_like(qa)
    return jnp.concatenate([jnp.where(lo, qa, zero), jnp.where(lo, zero, qa),
                            jnp.where(lo, qb, zero), jnp.where(lo, zero, qb)], axis=0)


def _unstack_heads(o, rows, lo):
    return jnp.concatenate([jnp.where(lo, o[0:rows], o[rows:2 * rows]),
                            jnp.where(lo, o[2 * rows:3 * rows], o[3 * rows:4 * rows])], axis=1)


ProjBufs = collections.namedtuple("ProjBufs", "qkv z xbc dt")
MixedBufs = collections.namedtuple("MixedBufs", "attn ssd")
Carry = collections.namedtuple("Carry", "kprev vprev conv_tail ht dt_buf")
MixerWeights = collections.namedtuple(
    "MixerWeights", "norm1 win wgate wdt convw convb dtb alog dskip ssmn woa wob wo")

PIECE = 256
QKV_PIECE0 = 0
Z_PIECE0 = QKV_PIECE0 + QKV_WIDTH // PIECE
XBC_PIECE0 = Z_PIECE0 + D_INNER // PIECE
N_IN_PIECES = XBC_PIECE0 + CONV_DIM // PIECE
PREP_PIECES = 4


def _prep_pieces_kernel(*refs):
    n = len(refs) // 2
    for w_ref, o_ref in zip(refs[:n], refs[n:]):
        for j in range(o_ref.shape[0]):
            o_ref[j] = w_ref[:, j * PIECE:(j + 1) * PIECE].astype(BF16)


def _prep_pieces(*ws):
    return pl.pallas_call(
        _prep_pieces_kernel, grid=(1,),
        in_specs=[_const_spec(w.shape) for w in ws],
        out_specs=tuple(pl.BlockSpec((w.shape[1] // PIECE, w.shape[0], PIECE), lambda i: (0, 0, 0)) for w in ws),
        out_shape=tuple(jax.ShapeDtypeStruct((w.shape[1] // PIECE, w.shape[0], PIECE), BF16) for w in ws),
        name="prep_weight_pieces",
        compiler_params=pltpu.CompilerParams(dimension_semantics=("arbitrary",), vmem_limit_bytes=VMEM_LIMIT),
    )(*ws)


def _prep_pieces_t_kernel(wt_ref, o_ref, dt_ref, *, dt_step, dt_row):
    for j in range(PREP_PIECES):
        o_ref[j] = wt_ref[j * PIECE:(j + 1) * PIECE, :].T.astype(BF16)

    @pl.when(pl.program_id(0) == dt_step)
    def _():
        rows = wt_ref[dt_row:dt_row + LANES, :]
        keep = lax.broadcasted_iota(jnp.int32, rows.shape, 0) < N_HEADS_M
        dt_ref[...] = jnp.where(keep, rows, 0.0).T.astype(BF16)


def _prep_pieces_t(wt, n_pieces, dt_row0):
    k_dim = wt.shape[1]
    block_rows = PREP_PIECES * PIECE
    n_blocks = pl.cdiv(n_pieces, PREP_PIECES)
    dt_step, dt_row = divmod(dt_row0, block_rows)
    assert dt_step < n_blocks and dt_row % SUBLANES == 0 and dt_row + LANES <= block_rows
    return pl.pallas_call(
        functools.partial(_prep_pieces_t_kernel, dt_step=dt_step, dt_row=dt_row), grid=(n_blocks,),
        in_specs=[pl.BlockSpec((block_rows, k_dim), lambda p: (p, 0))],
        out_specs=(pl.BlockSpec((PREP_PIECES, k_dim, PIECE), lambda p: (p, 0, 0)),
                   pl.BlockSpec((k_dim, LANES), lambda p: (0, 0))),
        out_shape=(jax.ShapeDtypeStruct((n_pieces, k_dim, PIECE), BF16),
                   jax.ShapeDtypeStruct((k_dim, LANES), BF16)), name="prep_weight_pieces_t",
        compiler_params=pltpu.CompilerParams(dimension_semantics=("arbitrary",),
                                             vmem_limit_bytes=VMEM_LIMIT),
    )(wt)


def _prep_pieces_t_shifted_kernel(blk_ref, o_ref, prev_s, *, shift):
    @pl.when(pl.program_id(0) > 0)
    def _():
        rows = jnp.concatenate([prev_s[shift:, :], blk_ref[:shift, :]], axis=0)
        o_ref[0] = rows.T.astype(BF16)

    prev_s[...] = blk_ref[...]


def _prep_pieces_t_shifted(wt, row0, n_pieces):
    k_dim = wt.shape[1]
    block0, shift = divmod(row0, PIECE)
    assert shift % SUBLANES == 0 and shift > 0
    return pl.pallas_call(
        functools.partial(_prep_pieces_t_shifted_kernel, shift=shift), grid=(n_pieces + 1,),
        in_specs=[pl.BlockSpec((PIECE, k_dim), lambda t: (block0 + t, 0))],
        out_specs=pl.BlockSpec((1, k_dim, PIECE), lambda t: (jnp.maximum(t - 1, 0), 0, 0)),
        out_shape=jax.ShapeDtypeStruct((n_pieces, k_dim, PIECE), BF16),
        scratch_shapes=[pltpu.VMEM((PIECE, k_dim), F32)], name="prep_weight_pieces_t_shifted",
        compiler_params=pltpu.CompilerParams(dimension_semantics=("arbitrary",),
                                             vmem_limit_bytes=VMEM_LIMIT),
    )(wt)


StateOuts = collections.namedtuple("StateOuts", "k v conv ssm")


def _split3_bf16(x):
    hi = x.astype(BF16)
    r1 = x - hi.astype(F32)
    mid = r1.astype(BF16)
    return hi, mid, (r1 - mid.astype(F32)).astype(BF16)


def _piece_cols(p):
    return slice(p * PIECE, (p + 1) * PIECE)


def _in_proj_pieces(hb, w, dsts):
    for dst, first in zip(dsts, (QKV_PIECE0, Z_PIECE0, XBC_PIECE0)):
        for p in range(dst.shape[1] // PIECE):
            dst[:, _piece_cols(p)] = _mm(hb, w.win[first + p])
            yield


def _out_proj_pieces(x_ref, rows, hb, attn_ref, ssd_ref, w, out_ref):
    merged = []
    for half, (src_ref, w_ref) in enumerate(((attn_ref, w.woa), (ssd_ref, w.wob))):
        parts = []
        for p in range(D_MODEL // PIECE):
            gate = _sigmoid(_mm(hb, w.wgate[half * (D_MODEL // PIECE) + p]))
            yield
            parts.append(gate * _mm(src_ref[...], w_ref[p]))
            yield
        merged.append(jnp.concatenate(parts, axis=1))
    merged = (merged[0] + merged[1]).astype(BF16)
    for p in range(D_MODEL // PIECE):
        out_ref[rows, _piece_cols(p)] = x_ref[rows, _piece_cols(p)] + _mm(merged, w.wo[p])
        yield


def _stage_in_proj(x_ref, rows, w, proj, hnorm_ref):
    hb = _rms(x_ref[rows, :], w.norm1[...]).astype(BF16)
    hnorm_ref[...] = hb
    yield
    yield from _in_proj_pieces(hb, w, (proj.qkv, proj.z, proj.xbc))
    proj.dt[...] = _mm(hb, w.wdt[...])
    yield


def _stage_out_proj(x_ref, rows, mixed, w, x1_ref, hnorm_ref):
    hb = hnorm_ref[...]
    yield
    yield from _out_proj_pieces(x_ref, rows, hb, mixed.attn, mixed.ssd, w, x1_ref)


def _alternate(*gens):
    gens = list(gens)
    while gens:
        for g in list(gens):
            try:
                next(g)
                yield
            except StopIteration:
                gens.remove(g)


def _trace_interleaved(primary, filler):
    for _ in primary:
        next(filler, None)
    for _ in filler:
        pass


def _stage_mix(proj, mixed, rows, carry, w, sinks_ref, first):
    T = CHUNK
    reset = (lambda a: a) if first is None else (lambda a: jnp.where(first, 0.0, a))
    lo = _lane_lo((T, LANES))
    lo2 = _lane_lo((2 * T, LANES))
    rm = lax.broadcasted_iota(jnp.int32, (T, 1), 0) % SUBLANES
    rr = lax.broadcasted_iota(jnp.int32, (T, T), 0)
    cc = lax.broadcasted_iota(jnp.int32, (T, T), 1)
    tril = cc <= rr
    pairs_per_group = N_HEADS_M // 2 // N_GROUPS_M
    n_pairs = N_HEADS_M // 2
    gcols = lambda g: slice(g * GROUP_WIDTH, (g + 1) * GROUP_WIDTH)

    def conv_silu(cols):
        xr = proj.xbc[rows, cols]
        prev_tile = jnp.concatenate([reset(carry.conv_tail[:, cols]), xr[:T - SUBLANES]], axis=0)
        carry.conv_tail[:, cols] = xr[T - SUBLANES:]
        yc = xr * w.convw[CONV_W - 1:CONV_W, cols]
        for shift in range(1, CONV_W):
            shifted = _tile_roll(jnp.where(rm >= SUBLANES - shift, prev_tile, xr), shift)
            yc = yc + shifted * w.convw[CONV_W - 1 - shift:CONV_W - shift, cols]
        return _silu(yc + w.convb[:, cols])

    def softmax(s, kv):
        r = lax.broadcasted_iota(jnp.int32, (T, 2 * T), 0)
        col = lax.broadcasted_iota(jnp.int32, (T, 2 * T), 1)
        valid = (col >= r) & (col <= r + WINDOW)
        if first is not None:
            valid = valid & (col >= jnp.where(first, T, 0))
        es, inv = [], []
        for g in range(Q_PER_KV):
            sk = sinks_ref[kv * Q_PER_KV + g]
            sg = jnp.where(valid, s[g * T:(g + 1) * T], NEG_BIG)
            m = jnp.maximum(jnp.max(sg, axis=-1, keepdims=True), sk)
            e = jnp.exp(sg - m)
            es.append(e.astype(BF16))
            inv.append(1.0 / (jnp.sum(e, axis=-1, keepdims=True) + jnp.exp(sk - m)))
        return jnp.concatenate(es, axis=0), jnp.concatenate(inv, axis=0)

    qkv = proj.qkv[rows, :]
    q = qkv[:, :ATTN_WIDTH] * (HEAD_DIM_A ** -0.5)
    k = qkv[:, ATTN_WIDTH:ATTN_WIDTH + KV_WIDTH]
    v = qkv[:, ATTN_WIDTH + KV_WIDTH:]
    kk = jnp.concatenate([carry.kprev[...], k], axis=0)
    vv = jnp.concatenate([carry.vprev[...], v], axis=0)
    carry.kprev[...] = k
    carry.vprev[...] = v
    carry.dt_buf[...] = _softplus(proj.dt[rows, :] + w.dtb[...])
    da_split = jnp.concatenate(_split3_bf16(carry.dt_buf[...] * -jnp.exp(w.alog[...])), axis=1)
    vds = [_dup_half(vv, lo2, kv == 0).astype(BF16) for kv in range(N_KV_A)]
    scores = [_mm_nt(_stack_heads(q, kv, lo), _dup_half(kk, lo2, kv == 0).astype(BF16))
              for kv in range(N_KV_A)]
    yield

    bc = conv_silu(slice(D_INNER, CONV_DIM))
    bgs = [bc[:, g * D_STATE:(g + 1) * D_STATE].astype(BF16) for g in range(N_GROUPS_M)]
    cgs = [bc[:, (N_GROUPS_M + g) * D_STATE:(N_GROUPS_M + g + 1) * D_STATE].astype(BF16)
           for g in range(N_GROUPS_M)]
    bgts = [bc[:, g * D_STATE:(g + 1) * D_STATE].T.astype(BF16) for g in range(N_GROUPS_M)]
    yield

    es0, inv0 = softmax(scores[0], 0)
    yield

    es1, inv1 = softmax(scores[1], 1)
    cbs = [_mm_nt(cgs[g], bgs[g]) for g in range(N_GROUPS_M)]
    h_prevs = [reset(carry.ht[:, gcols(g)]) for g in range(N_GROUPS_M)]
    y_offs = [_mm(cgs[g], h_prevs[g]) for g in range(N_GROUPS_M)]
    o0 = _mm(es0, vds[0])
    cs = jnp.dot(jnp.where(tril, 1.0, 0.0).astype(BF16), da_split, preferred_element_type=F32)
    yield

    a_cs = cs[:, :LANES] + cs[:, LANES:2 * LANES] + cs[:, 2 * LANES:]
    a_cs_t = a_cs.T
    xs_groups = [conv_silu(gcols(0))]
    o1 = _mm(es1, vds[1])
    yield

    xs_groups.append(conv_silu(gcols(1)))
    mixed.attn[rows, 0:2 * LANES] = _unstack_heads(o0 * inv0, T, lo).astype(BF16)
    yield

    def prep(j):
        g = j // pairs_per_group
        dt = carry.dt_buf[...]
        ws, colbs, dtbs = [], [], []
        for h in (2 * j, 2 * j + 1):
            colb = jnp.broadcast_to(a_cs[:, h:h + 1], (T, T))
            rowb = jnp.broadcast_to(a_cs_t[h:h + 1, :], (T, T))
            seg = jnp.where(tril, jnp.exp(colb - rowb), 0.0)
            ws.append((cbs[g] * seg).astype(BF16))
            colbs.append(colb)
            dtbs.append(jnp.broadcast_to(dt[:, h:h + 1], (T, LANES)))
        dt_e = jnp.where(lo, dtbs[0], dtbs[1])
        acs_e = jnp.where(lo, colbs[0], colbs[1])
        xs_j = xs_groups[g][:, (j % pairs_per_group) * LANES:(j % pairs_per_group + 1) * LANES]
        xdt = xs_j * dt_e
        zero = jnp.zeros_like(xdt)
        rhs = jnp.concatenate([jnp.where(lo, xdt, zero), jnp.where(lo, zero, xdt)], axis=0).astype(BF16)
        alast = acs_e[T - 1:T, :]
        return dict(lhs=jnp.concatenate(ws, axis=1), rhs=rhs, xs=xs_j, e_acs=jnp.exp(acs_e),
                    xd=(xdt * jnp.exp(alast - acs_e)).astype(BF16), decay=jnp.exp(alast))

    def finish(j, p, y_diag):
        g, jj = divmod(j, pairs_per_group)
        sl = slice(j * LANES, (j + 1) * LANES)
        y = y_diag + y_offs[g][:, jj * LANES:(jj + 1) * LANES] * p["e_acs"] + p["xs"] * w.dskip[:, sl]
        return y * _silu(proj.z[rows, sl])

    mixed.attn[rows, 2 * LANES:4 * LANES] = _unstack_heads(o1 * inv1, T, lo).astype(BF16)
    preps = {0: prep(0)}
    yield

    ys, y_diags = [], {}
    for j in range(n_pairs):
        if j + 1 < n_pairs:
            preps[j + 1] = prep(j + 1)
        y_diags[j] = _mm(preps[j]["lhs"], preps[j]["rhs"])
        if j >= 1:
            ys.append(finish(j - 1, preps[j - 1], y_diags.pop(j - 1)))
        if j % pairs_per_group == pairs_per_group - 1:
            g = j // pairs_per_group
            grp = [preps[i] for i in range(g * pairs_per_group, (g + 1) * pairs_per_group)]
            carry.ht[:, gcols(g)] = (h_prevs[g] * jnp.concatenate([p["decay"] for p in grp], axis=1)
                                     + _mm(bgts[g], jnp.concatenate([p["xd"] for p in grp], axis=1)))
        yield
    ys.append(finish(n_pairs - 1, preps[n_pairs - 1], y_diags.pop(n_pairs - 1)))
    yield

    m_slabs = []
    for g in range(N_GROUPS_M):
        grp = ys[g * pairs_per_group:(g + 1) * pairs_per_group]
        ssq = grp[0] * grp[0]
        for y in grp[1:]:
            ssq = ssq + y * y
        scale = lax.rsqrt(jnp.sum(ssq, axis=-1, keepdims=True) * (1.0 / GROUP_WIDTH) + EPS)
        m_slabs.extend([y * scale for y in grp])
    mixed.ssd[rows, :] = (jnp.concatenate(m_slabs, axis=1) * w.ssmn[...]).astype(BF16)


def _chain(*gens):
    for g in gens:
        yield from g


def _prompt_mixer_kernel(sinks_ref, x_in_ref, x_res_ref, *refs, chunks_per_seq, n_steps):
    refs = list(refs)
    take = lambda n: [refs.pop(0) for _ in range(n)]
    w = MixerWeights(*take(len(MixerWeights._fields)))
    mlp_w_f32 = take(2)
    x1_ref, = take(1)
    outs = StateOuts(*take(4))
    mlp_w_bf16 = take(2)
    proj = (ProjBufs(*take(4)), ProjBufs(*take(4)))
    mixed = (MixedBufs(*take(2)), MixedBufs(*take(2)))
    hnorm = take(2)
    carry = Carry(*take(len(Carry._fields)))
    assert not refs
    T = CHUNK
    s = pl.program_id(0)
    every = slice(None)

    @pl.when(s == 0)
    def _():
        for ref in list(proj[1] + mixed[0] + mixed[1] + carry) + hnorm:
            ref[...] = jnp.zeros_like(ref)

    first = (s - 1) % (chunks_per_seq // 2) == 0

    def cast_mlp_weights():
        for src, dst in zip(mlp_w_f32, mlp_w_bf16):
            dst[...] = src[...].astype(BF16)
            yield

    def step(cur):
        prv = 1 - cur
        dense = _alternate(_stage_out_proj(x_res_ref, every, mixed[cur], w, x1_ref, hnorm[cur]),
                           _stage_in_proj(x_in_ref, every, w, proj[cur], hnorm[cur]), cast_mlp_weights())
        mix = _chain(_stage_mix(proj[prv], mixed[prv], slice(0, T), carry, w, sinks_ref, first),
                     _stage_mix(proj[prv], mixed[prv], slice(T, 2 * T), carry, w, sinks_ref, None))
        _trace_interleaved(mix, dense)

    def drain_all(*gens):
        for _ in _alternate(*gens):
            pass

    last = n_steps - 1
    pl.when(s == 0)(lambda: drain_all(_stage_in_proj(x_in_ref, every, w, proj[0], hnorm[0]), cast_mlp_weights()))
    pl.when(s == last)(lambda: drain_all(
        _stage_out_proj(x_res_ref, every, mixed[last % 2], w, x1_ref, hnorm[last % 2]), cast_mlp_weights()))
    for parity in (0, 1):
        pl.when((s > 0) & (s < last) & (s % 2 == parity))(functools.partial(step, parity))

    @pl.when((s >= 1) & (s % (chunks_per_seq // 2) == 0))
    def _():
        outs.k[0] = carry.kprev[...]
        outs.v[0] = carry.vprev[...]
        outs.conv[0] = carry.conv_tail[SUBLANES - (CONV_W - 1):, :]
        outs.ssm[0] = carry.ht[...].T


def _const_spec(shape):
    return pl.BlockSpec(shape, lambda *_: (0,) * len(shape), pipeline_mode=pl.Buffered(1))


def _prompt_mixer(x, sinks, weights, w_up, w_down):
    nb, seq, _ = x.shape
    assert seq % (2 * CHUNK) == 0
    chunks_per_seq = seq // CHUNK
    n_pairs = nb * chunks_per_seq // 2
    n_steps = n_pairs + 2
    pair = 2 * CHUNK
    consts = tuple(weights)
    cast_steps = 1 << ((n_steps // 2).bit_length() - 1)
    ff_slice = D_FF // cast_steps
    assert ff_slice % LANES == 0
    up_block = lambda s: (0, jnp.minimum(s, cast_steps - 1))
    down_block = lambda s: (jnp.clip(s - cast_steps, 0, cast_steps - 1), 0)
    seq_of_mix = lambda s: jnp.clip((2 * s - 1) // chunks_per_seq, 0, nb - 1)
    in_specs = ([pl.BlockSpec(memory_space=pltpu.SMEM),
                 pl.BlockSpec((pair, D_MODEL), lambda s: (jnp.minimum(s, n_pairs - 1), 0)),
                 pl.BlockSpec((pair, D_MODEL), lambda s: (jnp.maximum(s - 2, 0), 0))]
                + [_const_spec(a.shape) for a in consts]
                + [pl.BlockSpec((D_MODEL, ff_slice), up_block), pl.BlockSpec((ff_slice, D_MODEL), down_block)])
    out_shape = (jax.ShapeDtypeStruct((nb * seq, D_MODEL), F32),
                 jax.ShapeDtypeStruct((nb, WINDOW, KV_WIDTH), F32),
                 jax.ShapeDtypeStruct((nb, WINDOW, KV_WIDTH), F32),
                 jax.ShapeDtypeStruct((nb, CONV_W - 1, CONV_DIM), F32),
                 jax.ShapeDtypeStruct((nb, D_INNER, D_STATE), F32),
                 jax.ShapeDtypeStruct(w_up.shape, BF16), jax.ShapeDtypeStruct(w_down.shape, BF16))
    out_specs = (pl.BlockSpec((pair, D_MODEL), lambda s: (jnp.maximum(s - 2, 0), 0)),
                 pl.BlockSpec((1, WINDOW, KV_WIDTH), lambda s: (seq_of_mix(s), 0, 0)),
                 pl.BlockSpec((1, WINDOW, KV_WIDTH), lambda s: (seq_of_mix(s), 0, 0)),
                 pl.BlockSpec((1, CONV_W - 1, CONV_DIM), lambda s: (seq_of_mix(s), 0, 0)),
                 pl.BlockSpec((1, D_INNER, D_STATE), lambda s: (seq_of_mix(s), 0, 0)),
                 pl.BlockSpec((D_MODEL, ff_slice), up_block), pl.BlockSpec((ff_slice, D_MODEL), down_block))
    proj_bufs = [pltpu.VMEM((pair, QKV_WIDTH), F32), pltpu.VMEM((pair, D_INNER), F32),
                 pltpu.VMEM((pair, CONV_DIM), F32), pltpu.VMEM((pair, LANES), F32)]
    mixed_bufs = [pltpu.VMEM((pair, ATTN_WIDTH), BF16), pltpu.VMEM((pair, D_INNER), BF16)]
    hnorm_bufs = [pltpu.VMEM((pair, D_MODEL), BF16)] * 2
    carry = [pltpu.VMEM((CHUNK, KV_WIDTH), F32), pltpu.VMEM((CHUNK, KV_WIDTH), F32),
             pltpu.VMEM((SUBLANES, CONV_DIM), F32), pltpu.VMEM((D_STATE, D_INNER), F32),
             pltpu.VMEM((CHUNK, LANES), F32)]
    x_rows = x.reshape(nb * seq, D_MODEL)
    return pl.pallas_call(
        functools.partial(_prompt_mixer_kernel, chunks_per_seq=chunks_per_seq, n_steps=n_steps),
        grid=(n_steps,), in_specs=in_specs, out_specs=out_specs,
        out_shape=out_shape, scratch_shapes=proj_bufs * 2 + mixed_bufs * 2 + hnorm_bufs + carry,
        name="prompt_mixer",
        compiler_params=pltpu.CompilerParams(dimension_semantics=("arbitrary",),
                                             vmem_limit_bytes=VMEM_LIMIT),
    )(sinks, x_rows, x_rows, *consts, w_up, w_down)


MLP_TILE = 1024
FF_BLOCK = 1024


def _mlp_kernel(x_ref, norm2_ref, wup_ref, wdown_ref, fnorm_ref, y_ref):
    x = x_ref[...]
    hm = _rms(x, norm2_ref[...]).astype(BF16)
    acc = x
    for j in range(D_FF // FF_BLOCK):
        h = _mm(hm, wup_ref[:, j * FF_BLOCK:(j + 1) * FF_BLOCK])
        h = jnp.square(jnp.maximum(h, 0.0))
        acc = acc + _mm(h, wdown_ref[j * FF_BLOCK:(j + 1) * FF_BLOCK, :])
    y_ref[...] = _rms(acc, fnorm_ref[...])


def _mlp(x, norm2, wup, wdown, fnorm):
    rows = x.shape[0]
    tile = min(MLP_TILE, rows)
    assert rows % tile == 0 and tile % SUBLANES == 0
    consts = (norm2, wup, wdown, fnorm)
    return pl.pallas_call(
        _mlp_kernel, grid=(rows // tile,),
        in_specs=[pl.BlockSpec((tile, D_MODEL), lambda i: (i, 0))] + [_const_spec(a.shape) for a in consts],
        out_specs=pl.BlockSpec((tile, D_MODEL), lambda i: (i, 0)),
        out_shape=jax.ShapeDtypeStruct((rows, D_MODEL), F32), name="mlp",
        compiler_params=pltpu.CompilerParams(dimension_semantics=("arbitrary",),
                                             vmem_limit_bytes=VMEM_LIMIT),
    )(x, *consts)


IN_PROJ_ROWS = 256


def _in_proj_kernel(x_ref, norm1_ref, win_ref, wdt_ref, u_ref, dt_ref, hnorm_ref):
    hb = _rms(x_ref[...], norm1_ref[...]).astype(BF16)
    hnorm_ref[...] = hb
    for p in range(win_ref.shape[0]):
        u_ref[:, _piece_cols(p)] = _mm(hb, win_ref[p])
    dt_ref[...] = _mm(hb, wdt_ref[...])


def _in_proj(x, norm1, win, wdt):
    rows = x.shape[0]
    assert rows % IN_PROJ_ROWS == 0
    outs = ((win.shape[0] * PIECE, F32), (LANES, F32), (D_MODEL, BF16))
    row_spec = lambda width: pl.BlockSpec((IN_PROJ_ROWS, width), lambda i: (i, 0))
    return pl.pallas_call(
        _in_proj_kernel, grid=(rows // IN_PROJ_ROWS,),
        in_specs=[row_spec(D_MODEL), _const_spec(norm1.shape), _const_spec(win.shape), _const_spec(wdt.shape)],
        out_specs=tuple(row_spec(width) for width, _ in outs),
        out_shape=tuple(jax.ShapeDtypeStruct((rows, width), dtype) for width, dtype in outs),
        name="sample_in_proj",
        compiler_params=pltpu.CompilerParams(dimension_semantics=("arbitrary",),
                                             vmem_limit_bytes=VMEM_LIMIT),
    )(x, norm1, win, wdt)


def _tile_roll(x, shift):
    rows, width = x.shape
    tiles = x.reshape(rows // SUBLANES, SUBLANES, width)
    return pltpu.roll(tiles, shift % SUBLANES, axis=1).reshape(rows, width)


def _spread_rows(x):
    tiles = []
    for t in range(x.shape[0] // SUBLANES):
        two_seqs = x[t * SUBLANES:(t + 1) * SUBLANES]
        tiles += [two_seqs, pltpu.roll(two_seqs, SUBLANES // 2, axis=0)]
    return jnp.concatenate(tiles, axis=0)


def _gather_rows(y):
    low = lax.broadcasted_iota(jnp.int32, (SUBLANES, 1), 0) < SUBLANES // 2
    tiles = []
    for t in range(y.shape[0] // (2 * SUBLANES)):
        a = y[2 * t * SUBLANES:(2 * t + 1) * SUBLANES]
        b = y[(2 * t + 1) * SUBLANES:(2 * t + 2) * SUBLANES]
        tiles.append(jnp.where(low, a, pltpu.roll(b, SUBLANES // 2, axis=0)))
    return jnp.concatenate(tiles, axis=0)


def _expand_heads(cols, expand):
    terms = []
    for c in cols:
        hi = c.astype(BF16)
        r1 = c - hi.astype(F32)
        mid = r1.astype(BF16)
        lo = (r1 - mid.astype(F32)).astype(BF16)
        terms.extend([hi, mid, lo])
    rows = cols[0].shape[0]
    out = jnp.dot(jnp.concatenate(terms, axis=0), expand, preferred_element_type=F32)
    return [out[(3 * i) * rows:(3 * i + 1) * rows] + out[(3 * i + 1) * rows:(3 * i + 2) * rows]
            + out[(3 * i + 2) * rows:(3 * i + 3) * rows] for i in range(len(cols))]


def _sample_mixer_kernel(sinks_ref, u_ref, dtraw_ref, cprev_ref, kc_ref, vc_ref, st_ref,
                         convw_ref, convb_ref, dtb_ref, alog_ref, dskip_ref, ssmn_ref,
                         aout_ref, mout_ref, nk_ref, nv_ref, nconv_ref, nst_ref):
    R = SAMPLE_GROUP * SAMPLE_PAD
    L = 4
    qkv_ref = u_ref.at[:, QKV_PIECE0 * PIECE:Z_PIECE0 * PIECE]
    z_ref = u_ref.at[:, Z_PIECE0 * PIECE:XBC_PIECE0 * PIECE]
    xbc_ref = u_ref.at[:, XBC_PIECE0 * PIECE:N_IN_PIECES * PIECE]
    rm = lax.broadcasted_iota(jnp.int32, (R, 1), 0) % SAMPLE_PAD

    xr = xbc_raw = _spread_rows(xbc_ref[...])
    with_prev = jnp.where(rm >= SAMPLE_PAD - (CONV_W - 1), cprev_ref[...], xr)
    yc = xr * convw_ref[CONV_W - 1:CONV_W, :]
    for kshift in range(1, CONV_W):
        yc = yc + _tile_roll(with_prev, kshift) * convw_ref[CONV_W - 1 - kshift:CONV_W - kshift, :]
    for i in range(SAMPLE_GROUP):
        nconv_ref[i] = xbc_raw[i * SAMPLE_PAD + 1:i * SAMPLE_PAD + CONV_W, :]
    xbc = _silu(yc + convb_ref[...])
    xs = xbc[:, :D_INNER]
    bm = xbc[:, D_INNER:D_INNER + N_GROUPS_M * D_STATE]
    cm = xbc[:, D_INNER + N_GROUPS_M * D_STATE:]

    dt = _softplus(_spread_rows(dtraw_ref[...]) + dtb_ref[...])
    d_a = dt * (-jnp.exp(alog_ref[...]))
    a_cs = d_a
    suf = jnp.zeros_like(d_a)
    for kshift in range(1, L):
        a_cs = a_cs + jnp.where(rm >= kshift, _tile_roll(d_a, kshift), 0.0)
        suf = suf + jnp.where(rm <= L - 1 - kshift, _tile_roll(d_a, -kshift), 0.0)

    expand = (lax.broadcasted_iota(jnp.int32, (LANES, D_INNER), 1) // HEAD_DIM_M
              == lax.broadcasted_iota(jnp.int32, (LANES, D_INNER), 0)).astype(BF16)
    dt_e, acs_e, suf_e = _expand_heads([dt, a_cs, suf], expand)
    xdt = xs * dt_e
    real = rm < L
    xd_t = jnp.where(real, xdt * jnp.exp(suf_e), 0.0).T.astype(BF16)

    y = xs * dskip_ref[...]
    for kshift in range(L):
        bk = bm if kshift == 0 else _tile_roll(bm, kshift)
        cb = cm * bk
        cb_e = jnp.concatenate(
            [jnp.broadcast_to(jnp.sum(cb[:, g * D_STATE:(g + 1) * D_STATE], axis=-1, keepdims=True),
                              (R, GROUP_WIDTH)) for g in range(N_GROUPS_M)], axis=1)
        if kshift == 0:
            y = y + cb_e * xdt
        else:
            seg = jnp.exp(acs_e - _tile_roll(acs_e, kshift))
            y = y + jnp.where(rm >= kshift, cb_e * seg * _tile_roll(xdt, kshift), 0.0)

    qkv = _spread_rows(qkv_ref[...])
    q = qkv[:, :ATTN_WIDTH] * (HEAD_DIM_A ** -0.5)
    q_swapped = jnp.concatenate(
        [pltpu.roll(q[:, s * LANES:(s + 1) * LANES], HEAD_DIM_A, axis=1) for s in range(ATTN_WIDTH // LANES)], axis=1)
    kn = qkv[:, ATTN_WIDTH:ATTN_WIDTH + KV_WIDTH]
    vn = qkv[:, ATTN_WIDTH + KV_WIDTH:]
    lo8 = _lane_lo((SAMPLE_PAD, LANES))
    zero8 = jnp.zeros((SAMPLE_PAD, LANES), F32)
    seq_rows = [slice(i * SAMPLE_PAD, (i + 1) * SAMPLE_PAD) for i in range(SAMPLE_GROUP)]
    HROWS = N_HEADS_A * SAMPLE_PAD

    s_c, s_n = [], []
    for rows in seq_rows:
        pieces = []
        for kv in range(N_KV_A):
            for g in range(Q_PER_KV):
                slab = kv * (Q_PER_KV // 2) + g // 2
                src = q if g % 2 == kv else q_swapped
                piece = src[rows, slab * LANES:(slab + 1) * LANES]
                pieces.append(jnp.where(lo8, piece, zero8) if kv == 0 else jnp.where(lo8, zero8, piece))
        lhs = jnp.concatenate(pieces, axis=0).astype(BF16)
        i = len(s_c)
        s_c.append(_mm(lhs, kc_ref[i]))
        s_n.append(_mm_nt(lhs, kn[rows]))
    s_c = jnp.concatenate(s_c, axis=0)
    s_n = jnp.concatenate(s_n, axis=0)
    n_rows = SAMPLE_GROUP * HROWS
    tok_c = lax.broadcasted_iota(jnp.int32, (n_rows, WINDOW), 0) % SAMPLE_PAD
    s_c = jnp.where(lax.broadcasted_iota(jnp.int32, (n_rows, WINDOW), 1) >= tok_c, s_c, NEG_BIG)
    tok_n = lax.broadcasted_iota(jnp.int32, (n_rows, SAMPLE_PAD), 0) % SAMPLE_PAD
    col_n = lax.broadcasted_iota(jnp.int32, (n_rows, SAMPLE_PAD), 1)
    s_n = jnp.where((col_n <= tok_n) & (col_n < L), s_n, NEG_BIG)
    sk = jnp.concatenate([jnp.full((SAMPLE_PAD, 1), sinks_ref[h], F32) for h in range(N_HEADS_A)] * SAMPLE_GROUP,
                         axis=0)
    m = jnp.maximum(jnp.maximum(jnp.max(s_c, axis=-1, keepdims=True), jnp.max(s_n, axis=-1, keepdims=True)), sk)
    e_c = jnp.exp(s_c - m)
    e_n = jnp.exp(s_n - m)
    inv_den = 1.0 / (jnp.sum(e_c, axis=-1, keepdims=True) + jnp.sum(e_n, axis=-1, keepdims=True) + jnp.exp(sk - m))
    e_c = e_c.astype(BF16)
    e_n = e_n.astype(BF16)
    o = jnp.concatenate([_mm_nt(e_c[i * HROWS:(i + 1) * HROWS], vc_ref[i]) + _mm(e_n[i * HROWS:(i + 1) * HROWS], vn[rows])
                         for i, rows in enumerate(seq_rows)], axis=0) * inv_den
    o_swapped = pltpu.roll(o, HEAD_DIM_A, axis=1)
    a_rows = []
    for i in range(SAMPLE_GROUP):
        blk = lambda arr, kv, g: arr[i * HROWS + (kv * Q_PER_KV + g) * SAMPLE_PAD:
                                     i * HROWS + (kv * Q_PER_KV + g + 1) * SAMPLE_PAD]
        slabs = []
        for kv in range(N_KV_A):
            for j in range(Q_PER_KV // 2):
                first = blk(o if kv == 0 else o_swapped, kv, 2 * j)
                second = blk(o_swapped if kv == 0 else o, kv, 2 * j + 1)
                slabs.append(jnp.where(lo8, first, second))
        a_rows.append(jnp.concatenate(slabs, axis=1))
    aout_ref[...] = _gather_rows(jnp.concatenate(a_rows, axis=0))

    key = lax.broadcasted_iota(jnp.int32, (KV_WIDTH, WINDOW), 1)
    for cache_ref, new, out_ref in ((kc_ref, kn, nk_ref), (vc_ref, vn, nv_ref)):
        new_t = new.T
        for i in range(SAMPLE_GROUP):
            kept = pltpu.roll(cache_ref[i], WINDOW - L, axis=1)
            fresh = pltpu.roll(new_t, (WINDOW - L - i * SAMPLE_PAD) % WINDOW, axis=1)
            out_ref[i] = jnp.where(key >= WINDOW - L, fresh, kept)

    row_r = lax.broadcasted_iota(jnp.int32, (R, D_STATE), 0)
    heads_per_group = N_HEADS_M // N_GROUPS_M
    yoff = []
    for i, rows in enumerate(seq_rows):
        state = st_ref[i]
        state_b = state.astype(BF16)
        a_tot = a_cs[i * SAMPLE_PAD + L - 1:i * SAMPLE_PAD + L, :]
        in_seq = (row_r >= i * SAMPLE_PAD) & (row_r < (i + 1) * SAMPLE_PAD)
        yo = []
        for g in range(N_GROUPS_M):
            gs = slice(g * GROUP_WIDTH, (g + 1) * GROUP_WIDTH)
            yo.append(_mm_nt(cm[rows, g * D_STATE:(g + 1) * D_STATE], state_b[gs]))
            bsel = jnp.where(in_seq, bm[:, g * D_STATE:(g + 1) * D_STATE], 0.0)
            upd = _mm(xd_t[gs, :], bsel)
            for hh in range(heads_per_group):
                h = g * heads_per_group + hh
                hs = slice(h * HEAD_DIM_M, (h + 1) * HEAD_DIM_M)
                decay = jnp.exp(jnp.broadcast_to(a_tot[:, h:h + 1], (HEAD_DIM_M, D_STATE)))
                nst_ref[i, hs, :] = state[hs] * decay + upd[hh * HEAD_DIM_M:(hh + 1) * HEAD_DIM_M]
        yoff.append(jnp.concatenate(yo, axis=1))
    yoff = jnp.concatenate(yoff, axis=0)

    y = (y + yoff * jnp.exp(acs_e)) * _silu(_spread_rows(z_ref[...]))
    outs = []
    for g in range(N_GROUPS_M):
        yg = y[:, g * GROUP_WIDTH:(g + 1) * GROUP_WIDTH]
        outs.append(yg * lax.rsqrt(jnp.mean(yg * yg, axis=-1, keepdims=True) + EPS))
    mout_ref[...] = _gather_rows(jnp.concatenate(outs, axis=1) * ssmn_ref[...])


def _sample_mixer(sinks, u, dtraw, cprev, kc, vc, st, convw, convb, dtb, alog, dskip, ssmn):
    nseq = kc.shape[0]
    assert nseq % SAMPLE_GROUP == 0
    R = SAMPLE_GROUP * SAMPLE_PAD
    T4 = SAMPLE_GROUP * 4
    consts = (convw, convb, dtb, alog, dskip, ssmn)
    row_spec = lambda w: pl.BlockSpec((R, w), lambda i: (i, 0))
    tok_spec = lambda w: pl.BlockSpec((T4, w), lambda i: (i, 0))
    seq_spec = lambda a, b: pl.BlockSpec((SAMPLE_GROUP, a, b), lambda i: (i, 0, 0))
    in_specs = ([pl.BlockSpec(memory_space=pltpu.SMEM),
                 tok_spec(N_IN_PIECES * PIECE), tok_spec(LANES), row_spec(CONV_DIM),
                 seq_spec(WINDOW, KV_WIDTH), seq_spec(WINDOW, KV_WIDTH), seq_spec(D_INNER, D_STATE)]
                + [_const_spec(a.shape) for a in consts])
    out_shape = (jax.ShapeDtypeStruct((nseq * 4, ATTN_WIDTH), F32), jax.ShapeDtypeStruct((nseq * 4, D_INNER), F32),
                 jax.ShapeDtypeStruct((nseq, WINDOW, KV_WIDTH), F32),
                 jax.ShapeDtypeStruct((nseq, WINDOW, KV_WIDTH), F32),
                 jax.ShapeDtypeStruct((nseq, CONV_W - 1, CONV_DIM), F32),
                 jax.ShapeDtypeStruct((nseq, D_INNER, D_STATE), F32))
    out_specs = (tok_spec(ATTN_WIDTH), tok_spec(D_INNER), seq_spec(WINDOW, KV_WIDTH), seq_spec(WINDOW, KV_WIDTH),
                 seq_spec(CONV_W - 1, CONV_DIM), seq_spec(D_INNER, D_STATE))
    return pl.pallas_call(
        _sample_mixer_kernel, grid=(nseq // SAMPLE_GROUP,), in_specs=in_specs, out_specs=out_specs,
        out_shape=out_shape, name="sample_mixer",
        compiler_params=pltpu.CompilerParams(dimension_semantics=("arbitrary",),
                                             vmem_limit_bytes=VMEM_LIMIT),
    )(sinks, u, dtraw, cprev, kc, vc, st, *consts)


N_OUT_PIECES = D_MODEL // PIECE
N_FF_BLOCKS = D_FF // FF_BLOCK


def _sample_post_kernel(x_ref, a_ref, m_ref, hnorm_ref, wga_ref, wgb_ref, woa_ref, wob_ref, wo_ref,
                        norm2_ref, wup_ref, wdown_ref, fnorm_ref, y_ref, merged_s, x1_s, hm_s, acc_s):
    s = pl.program_id(0)

    @pl.when(s < N_OUT_PIECES)
    def _():
        hb = hnorm_ref[...]
        piece = (_sigmoid(_mm(hb, wga_ref[0])) * _mm(a_ref[...], woa_ref[0])
                 + _sigmoid(_mm(hb, wgb_ref[0])) * _mm(m_ref[...], wob_ref[0]))
        for p in range(N_OUT_PIECES):
            @pl.when(s == p)
            def _():
                merged_s[:, _piece_cols(p)] = piece.astype(BF16)

    @pl.when((s >= N_OUT_PIECES) & (s < 2 * N_OUT_PIECES))
    def _():
        piece = _mm(merged_s[...], wo_ref[0])
        for p in range(N_OUT_PIECES):
            @pl.when(s == N_OUT_PIECES + p)
            def _():
                x1_s[:, _piece_cols(p)] = x_ref[:, _piece_cols(p)] + piece

    @pl.when(s == 2 * N_OUT_PIECES)
    def _():
        x1 = x1_s[...]
        hm_s[...] = _rms(x1, norm2_ref[...]).astype(BF16)
        acc_s[...] = x1

    @pl.when(s >= 2 * N_OUT_PIECES)
    def _():
        h = jnp.square(jnp.maximum(_mm(hm_s[...], wup_ref[...]), 0.0))
        acc_s[...] += _mm(h, wdown_ref[...])

    @pl.when(s == 2 * N_OUT_PIECES + N_FF_BLOCKS - 1)
    def _():
        y_ref[...] = _rms(acc_s[...], fnorm_ref[...])


def _sample_post(x, a_out, m_out, hnorm, wgate, woa, wob, wo, norm2, wup, wdown, fnorm):
    rows = x.shape[0]
    first = lambda s: jnp.minimum(s, N_OUT_PIECES - 1)
    second = lambda s: jnp.clip(s - N_OUT_PIECES, 0, N_OUT_PIECES - 1)
    ff = lambda s: jnp.clip(s - 2 * N_OUT_PIECES, 0, N_FF_BLOCKS - 1)
    piece_spec = lambda w, index: pl.BlockSpec((1,) + w.shape[1:], lambda s: (index(s), 0, 0))
    in_specs = [_const_spec(x.shape), _const_spec(a_out.shape), _const_spec(m_out.shape), _const_spec(hnorm.shape),
                piece_spec(wgate, first), piece_spec(wgate, lambda s: N_OUT_PIECES + first(s)),
                piece_spec(woa, first), piece_spec(wob, first), piece_spec(wo, second),
                _const_spec(norm2.shape),
                pl.BlockSpec((D_MODEL, FF_BLOCK), lambda s: (0, ff(s))),
                pl.BlockSpec((FF_BLOCK, D_MODEL), lambda s: (ff(s), 0)),
                _const_spec(fnorm.shape)]
    return pl.pallas_call(
        _sample_post_kernel, grid=(2 * N_OUT_PIECES + N_FF_BLOCKS,), in_specs=in_specs,
        out_specs=pl.BlockSpec((rows, D_MODEL), lambda s: (0, 0)),
        out_shape=jax.ShapeDtypeStruct((rows, D_MODEL), F32),
        scratch_shapes=[pltpu.VMEM((rows, D_MODEL), BF16), pltpu.VMEM((rows, D_MODEL), F32),
                        pltpu.VMEM((rows, D_MODEL), BF16), pltpu.VMEM((rows, D_MODEL), F32)],
        name="sample_post",
        compiler_params=pltpu.CompilerParams(dimension_semantics=("arbitrary",),
                                             vmem_limit_bytes=VMEM_LIMIT),
    )(x, a_out, m_out, hnorm, wgate, wgate, woa, wob, wo, norm2, wup, wdown, fnorm)


def _layer_params(norm1, w_in, sinks, conv_w, conv_b, dt_bias, a_log, d_skip, ssm_norm, w_oa, w_ob, w_o,
                  norm2):
    dt0 = QKV_WIDTH + D_INNER + CONV_DIM
    assert dt0 == N_IN_PIECES * PIECE
    pad_lanes = lambda a: jnp.pad(a, ((0, 0), (0, LANES - a.shape[1])))
    w_in_t = w_in.T
    woa, wob, wo = _prep_pieces(w_oa, w_ob, w_o)
    win, wdt = _prep_pieces_t(w_in_t, N_IN_PIECES, dt0)
    mixer = MixerWeights(
        norm1=norm1[None, :], win=win,
        wgate=_prep_pieces_t_shifted(w_in_t, dt0 + N_HEADS_M, 2 * D_MODEL // PIECE),
        wdt=wdt,
        convw=conv_w, convb=conv_b[None, :], dtb=pad_lanes(dt_bias[None, :]), alog=pad_lanes(a_log[None, :]),
        dskip=jnp.repeat(d_skip, HEAD_DIM_M)[None, :], ssmn=ssm_norm[None, :],
        woa=woa, wob=wob, wo=wo)
    return dict(sinks=sinks.astype(F32), mixer=mixer, norm2=norm2[None, :])


def kernel(x_prompt, x_sample, cache_swa_k, cache_swa_v, state_conv, state_ssm, norm1, w_in, sinks, conv_w,
           conv_b, dt_bias, a_log, d_skip, ssm_norm, w_oa, w_ob, w_o, norm2, w_up, w_down, final_norm):
    depth = w_in.shape[0]
    assert depth == 1
    nb, seq, _ = x_prompt.shape
    ns, ls, _ = x_sample.shape
    assert ls == 4
    p = _layer_params(norm1[0], w_in[0], sinks[0], conv_w[0], conv_b[0], dt_bias[0], a_log[0], d_skip[0],
                      ssm_norm[0], w_oa[0], w_ob[0], w_o[0], norm2[0])
    fnorm = final_norm[None, :]
    mw = p["mixer"]
    mixer_consts = (mw.convw, mw.convb, mw.dtb, mw.alog, mw.dskip, mw.ssmn)

    x1p, pk, pv, pc, pst, wup, wdown = _prompt_mixer(x_prompt, p["sinks"], mw, w_up[0], w_down[0])
    y_prompt = _mlp(x1p, p["norm2"], wup, wdown, fnorm)
    y_prompt = y_prompt.reshape(nb, seq, D_MODEL)

    xs_rows = x_sample.reshape(ns * ls, D_MODEL)
    dim_major = lambda a: jnp.swapaxes(a, 1, 2)
    cprev = jnp.pad(state_conv[0], ((0, 0), (SAMPLE_PAD - (CONV_W - 1), 0), (0, 0)))
    cprev = cprev.reshape(ns * SAMPLE_PAD, CONV_DIM)
    u, dtraw, hnorm = _in_proj(xs_rows, mw.norm1, mw.win, mw.wdt)
    a_out, m_out, sk, sv, sc, sst = _sample_mixer(
        p["sinks"], u, dtraw, cprev,
        dim_major(cache_swa_k[0].reshape(ns, WINDOW, KV_WIDTH)), dim_major(cache_swa_v[0].reshape(ns, WINDOW, KV_WIDTH)),
        state_ssm[0].reshape(ns, D_INNER, D_STATE), *mixer_consts)
    sk, sv = dim_major(sk), dim_major(sv)
    y_sample = _sample_post(xs_rows, a_out, m_out, hnorm, mw.wgate, mw.woa, mw.wob, mw.wo,
                            p["norm2"], wup, wdown, fnorm).reshape(ns, ls, D_MODEL)

    kv_shape = (1, -1, WINDOW, N_KV_A, HEAD_DIM_A)
    ssm_shape = (1, -1, N_HEADS_M, HEAD_DIM_M, D_STATE)
    return (y_prompt, y_sample,
            pk.reshape(kv_shape), pv.reshape(kv_shape), pc[None], pst.reshape(ssm_shape),
            sk.reshape(kv_shape), sv.reshape(kv_shape), sc[None], sst.reshape(ssm_shape))
```

```python
import collections
import functools

import jax
import jax.numpy as jnp
from jax import lax
from jax.experimental import pallas as pl
from jax.experimental.pallas import tpu as pltpu

F32 = jnp.float32
BF16 = jnp.bfloat16

D_MODEL = 1024
N_HEADS_A = 8
N_KV_A = 2
Q_PER_KV = N_HEADS_A // N_KV_A
HEAD_DIM_A = 64
WINDOW = 128
ATTN_WIDTH = N_HEADS_A * HEAD_DIM_A
KV_WIDTH = N_KV_A * HEAD_DIM_A
D_INNER = 1024
HEAD_DIM_M = 64
N_HEADS_M = D_INNER // HEAD_DIM_M
N_GROUPS_M = 2
GROUP_WIDTH = D_INNER // N_GROUPS_M
D_STATE = 128
CONV_W = 4
CONV_DIM = D_INNER + 2 * N_GROUPS_M * D_STATE
CHUNK = 128
D_FF = 4 * D_MODEL
EPS = 1e-6

LANES = 128
SUBLANES = 8
QKV_WIDTH = ATTN_WIDTH + 2 * KV_WIDTH
NEG_BIG = -1e30
VMEM_LIMIT = 56 * 1024 * 1024
SAMPLE_PAD = SUBLANES
SAMPLE_GROUP = 16


def _mm(a, b):
    return jnp.dot(a.astype(BF16), b.astype(BF16), preferred_element_type=F32)


def _mm_nt(a, b):
    return lax.dot_general(a.astype(BF16), b.astype(BF16), (((1,), (1,)), ((), ())),
                           preferred_element_type=F32)


def _rms(x, w):
    return x * lax.rsqrt(jnp.mean(x * x, axis=-1, keepdims=True) + EPS) * w


def _sigmoid(x):
    return 0.5 + 0.5 * jnp.tanh(0.5 * x)


def _silu(x):
    h = 0.5 * x
    return h + h * jnp.tanh(h)


def _softplus(x):
    return jnp.maximum(x, 0.0) + jnp.log(1.0 + jnp.exp(-jnp.abs(x)))


def _lane_lo(shape):
    return (lax.broadcasted_iota(jnp.int32, shape, len(shape) - 1) % LANES) < HEAD_DIM_A


def _dup_half(x, lo, first):
    xr = pltpu.roll(x, HEAD_DIM_A, axis=1)
    return jnp.where(lo, x, xr) if first else jnp.where(lo, xr, x)


def _stack_heads(q, kv, lo):
    qa = q[:, kv * 2 * LANES: kv * 2 * LANES + LANES]
    qb = q[:, kv * 2 * LANES + LANES: (kv + 1) * 2 * LANES]
    zero = jnp.zeros_like(qa)
    return jnp.concatenate([jnp.where(lo, qa, zero), jnp.where(lo, zero, qa),
                            jnp.where(lo, qb, zero), jnp.where(lo, zero, qb)], axis=0)


def _unstack_heads(o, rows, lo):
    return jnp.concatenate([jnp.where(lo, o[0:rows], o[rows:2 * rows]),
                            jnp.where(lo, o[2 * rows:3 * rows], o[3 * rows:4 * rows])], axis=1)


ProjBufs = collections.namedtuple("ProjBufs", "qkv z xbc dt")
MixedBufs = collections.namedtuple("MixedBufs", "attn ssd")
Carry = collections.namedtuple("Carry", "kprev vprev conv_tail ht dt_buf")
MixerWeights = collections.namedtuple(
    "MixerWeights", "norm1 win wgate wdt convw convb dtb alog dskip ssmn woa wob wo")

PIECE = 256
QKV_PIECE0 = 0
Z_PIECE0 = QKV_PIECE0 + QKV_WIDTH // PIECE
XBC_PIECE0 = Z_PIECE0 + D_INNER // PIECE
N_IN_PIECES = XBC_PIECE0 + CONV_DIM // PIECE
PREP_PIECES = 4


def _prep_pieces_kernel(*refs):
    n = len(refs) // 2
    for w_ref, o_ref in zip(refs[:n], refs[n:]):
        for j in range(o_ref.shape[0]):
            o_ref[j] = w_ref[:, j * PIECE:(j + 1) * PIECE].astype(BF16)


def _prep_pieces(*ws):
    return pl.pallas_call(
        _prep_pieces_kernel, grid=(1,),
        in_specs=[_const_spec(w.shape) for w in ws],
        out_specs=tuple(pl.BlockSpec((w.shape[1] // PIECE, w.shape[0], PIECE), lambda i: (0, 0, 0)) for w in ws),
        out_shape=tuple(jax.ShapeDtypeStruct((w.shape[1] // PIECE, w.shape[0], PIECE), BF16) for w in ws),
        name="prep_weight_pieces",
        compiler_params=pltpu.CompilerParams(dimension_semantics=("arbitrary",), vmem_limit_bytes=VMEM_LIMIT),
    )(*ws)


def _prep_pieces_t_kernel(wt_ref, o_ref, dt_ref, *, dt_step, dt_row):
    for j in range(PREP_PIECES):
        o_ref[j] = wt_ref[j * PIECE:(j + 1) * PIECE, :].T.astype(BF16)

    @pl.when(pl.program_id(0) == dt_step)
    def _():
        rows = wt_ref[dt_row:dt_row + LANES, :]
        keep = lax.broadcasted_iota(jnp.int32, rows.shape, 0) < N_HEADS_M
        dt_ref[...] = jnp.where(keep, rows, 0.0).T.astype(BF16)


def _prep_pieces_t(wt, n_pieces, dt_row0):
    k_dim = wt.shape[1]
    block_rows = PREP_PIECES * PIECE
    n_blocks = pl.cdiv(n_pieces, PREP_PIECES)
    dt_step, dt_row = divmod(dt_row0, block_rows)
    assert dt_step < n_blocks and dt_row % SUBLANES == 0 and dt_row + LANES <= block_rows
    return pl.pallas_call(
        functools.partial(_prep_pieces_t_kernel, dt_step=dt_step, dt_row=dt_row), grid=(n_blocks,),
        in_specs=[pl.BlockSpec((block_rows, k_dim), lambda p: (p, 0))],
        out_specs=(pl.BlockSpec((PREP_PIECES, k_dim, PIECE), lambda p: (p, 0, 0)),
                   pl.BlockSpec((k_dim, LANES), lambda p: (0, 0))),
        out_shape=(jax.ShapeDtypeStruct((n_pieces, k_dim, PIECE), BF16),
                   jax.ShapeDtypeStruct((k_dim, LANES), BF16)), name="prep_weight_pieces_t",
        compiler_params=pltpu.CompilerParams(dimension_semantics=("arbitrary",),
                                             vmem_limit_bytes=VMEM_LIMIT),
    )(wt)


def _prep_pieces_t_shifted_kernel(blk_ref, o_ref, prev_s, *, shift):
    @pl.when(pl.program_id(0) > 0)
    def _():
        rows = jnp.concatenate([prev_s[shift:, :], blk_ref[:shift, :]], axis=0)
        o_ref[0] = rows.T.astype(BF16)

    prev_s[...] = blk_ref[...]


def _prep_pieces_t_shifted(wt, row0, n_pieces):
    k_dim = wt.shape[1]
    block0, shift = divmod(row0, PIECE)
    assert shift % SUBLANES == 0 and shift > 0
    return pl.pallas_call(
        functools.partial(_prep_pieces_t_shifted_kernel, shift=shift), grid=(n_pieces + 1,),
        in_specs=[pl.BlockSpec((PIECE, k_dim), lambda t: (block0 + t, 0))],
        out_specs=pl.BlockSpec((1, k_dim, PIECE), lambda t: (jnp.maximum(t - 1, 0), 0, 0)),
        out_shape=jax.ShapeDtypeStruct((n_pieces, k_dim, PIECE), BF16),
        scratch_shapes=[pltpu.VMEM((PIECE, k_dim), F32)], name="prep_weight_pieces_t_shifted",
        compiler_params=pltpu.CompilerParams(dimension_semantics=("arbitrary",),
                                             vmem_limit_bytes=VMEM_LIMIT),
    )(wt)


StateOuts = collections.namedtuple("StateOuts", "k v conv ssm")


def _split3_bf16(x):
    hi = x.astype(BF16)
    r1 = x - hi.astype(F32)
    mid = r1.astype(BF16)
    return hi, mid, (r1 - mid.astype(F32)).astype(BF16)


def _piece_cols(p):
    return slice(p * PIECE, (p + 1) * PIECE)


def _in_proj_pieces(hb, w, dsts):
    for dst, first in zip(dsts, (QKV_PIECE0, Z_PIECE0, XBC_PIECE0)):
        for p in range(dst.shape[1] // PIECE):
            dst[:, _piece_cols(p)] = _mm(hb, w.win[first + p])
            yield


def _out_proj_pieces(x_ref, rows, hb, attn_ref, ssd_ref, w, out_ref):
    merged = []
    for half, (src_ref, w_ref) in enumerate(((attn_ref, w.woa), (ssd_ref, w.wob))):
        parts = []
        for p in range(D_MODEL // PIECE):
            gate = _sigmoid(_mm(hb, w.wgate[half * (D_MODEL // PIECE) + p]))
            yield
            parts.append(gate * _mm(src_ref[...], w_ref[p]))
            yield
        merged.append(jnp.concatenate(parts, axis=1))
    merged = (merged[0] + merged[1]).astype(BF16)
    for p in range(D_MODEL // PIECE):
        out_ref[rows, _piece_cols(p)] = x_ref[rows, _piece_cols(p)] + _mm(merged, w.wo[p])
        yield


def _stage_in_proj(x_ref, rows, w, proj, hnorm_ref):
    hb = _rms(x_ref[rows, :], w.norm1[...]).astype(BF16)
    hnorm_ref[...] = hb
    yield
    yield from _in_proj_pieces(hb, w, (proj.qkv, proj.z, proj.xbc))
    proj.dt[...] = _mm(hb, w.wdt[...])
    yield


def _stage_out_proj(x_ref, rows, mixed, w, x1_ref, hnorm_ref):
    hb = hnorm_ref[...]
    yield
    yield from _out_proj_pieces(x_ref, rows, hb, mixed.attn, mixed.ssd, w, x1_ref)


def _alternate(*gens):
    gens = list(gens)
    while gens:
        for g in list(gens):
            try:
                next(g)
                yield
            except StopIteration:
                gens.remove(g)


def _trace_interleaved(primary, filler):
    for _ in primary:
        next(filler, None)
    for _ in filler:
        pass


def _stage_mix(proj, mixed, rows, carry, w, sinks_ref, first):
    T = CHUNK
    reset = (lambda a: a) if first is None else (lambda a: jnp.where(first, 0.0, a))
    lo = _lane_lo((T, LANES))
    lo2 = _lane_lo((2 * T, LANES))
    rm = lax.broadcasted_iota(jnp.int32, (T, 1), 0) % SUBLANES
    rr = lax.broadcasted_iota(jnp.int32, (T, T), 0)
    cc = lax.broadcasted_iota(jnp.int32, (T, T), 1)
    tril = cc <= rr
    pairs_per_group = N_HEADS_M // 2 // N_GROUPS_M
    n_pairs = N_HEADS_M // 2
    gcols = lambda g: slice(g * GROUP_WIDTH, (g + 1) * GROUP_WIDTH)

    def conv_silu(cols):
        xr = proj.xbc[rows, cols]
        prev_tile = jnp.concatenate([reset(carry.conv_tail[:, cols]), xr[:T - SUBLANES]], axis=0)
        carry.conv_tail[:, cols] = xr[T - SUBLANES:]
        yc = xr * w.convw[CONV_W - 1:CONV_W, cols]
        for shift in range(1, CONV_W):
            shifted = _tile_roll(jnp.where(rm >= SUBLANES - shift, prev_tile, xr), shift)
            yc = yc + shifted * w.convw[CONV_W - 1 - shift:CONV_W - shift, cols]
        return _silu(yc + w.convb[:, cols])

    def softmax(s, kv):
        r = lax.broadcasted_iota(jnp.int32, (T, 2 * T), 0)
        col = lax.broadcasted_iota(jnp.int32, (T, 2 * T), 1)
        valid = (col >= r) & (col <= r + WINDOW)
        if first is not None:
            valid = valid & (col >= jnp.where(first, T, 0))
        es, inv = [], []
        for g in range(Q_PER_KV):
            sk = sinks_ref[kv * Q_PER_KV + g]
            sg = jnp.where(valid, s[g * T:(g + 1) * T], NEG_BIG)
            m = jnp.maximum(jnp.max(sg, axis=-1, keepdims=True), sk)
            e = jnp.exp(sg - m)
            es.append(e.astype(BF16))
            inv.append(1.0 / (jnp.sum(e, axis=-1, keepdims=True) + jnp.exp(sk - m)))
        return jnp.concatenate(es, axis=0), jnp.concatenate(inv, axis=0)

    qkv = proj.qkv[rows, :]
    q = qkv[:, :ATTN_WIDTH] * (HEAD_DIM_A ** -0.5)
    k = qkv[:, ATTN_WIDTH:ATTN_WIDTH + KV_WIDTH]
    v = qkv[:, ATTN_WIDTH + KV_WIDTH:]
    kk = jnp.concatenate([carry.kprev[...], k], axis=0)
    vv = jnp.concatenate([carry.vprev[...], v], axis=0)
    carry.kprev[...] = k
    carry.vprev[...] = v
    carry.dt_buf[...] = _softplus(proj.dt[rows, :] + w.dtb[...])
    da_split = jnp.concatenate(_split3_bf16(carry.dt_buf[...] * -jnp.exp(w.alog[...])), axis=1)
    vds = [_dup_half(vv, lo2, kv == 0).astype(BF16) for kv in range(N_KV_A)]
    scores = [_mm_nt(_stack_heads(q, kv, lo), _dup_half(kk, lo2, kv == 0).astype(BF16))
              for kv in range(N_KV_A)]
    yield

    bc = conv_silu(slice(D_INNER, CONV_DIM))
    bgs = [bc[:, g * D_STATE:(g + 1) * D_STATE].astype(BF16) for g in range(N_GROUPS_M)]
    cgs = [bc[:, (N_GROUPS_M + g) * D_STATE:(N_GROUPS_M + g + 1) * D_STATE].astype(BF16)
           for g in range(N_GROUPS_M)]
    bgts = [bc[:, g * D_STATE:(g + 1) * D_STATE].T.astype(BF16) for g in range(N_GROUPS_M)]
    yield

    es0, inv0 = softmax(scores[0], 0)
    yield

    es1, inv1 = softmax(scores[1], 1)
    cbs = [_mm_nt(cgs[g], bgs[g]) for g in range(N_GROUPS_M)]
    h_prevs = [reset(carry.ht[:, gcols(g)]) for g in range(N_GROUPS_M)]
    y_offs = [_mm(cgs[g], h_prevs[g]) for g in range(N_GROUPS_M)]
    o0 = _mm(es0, vds[0])
    cs = jnp.dot(jnp.where(tril, 1.0, 0.0).astype(BF16), da_split, preferred_element_type=F32)
    yield

    a_cs = cs[:, :LANES] + cs[:, LANES:2 * LANES] + cs[:, 2 * LANES:]
    a_cs_t = a_cs.T
    xs_groups = [conv_silu(gcols(0))]
    o1 = _mm(es1, vds[1])
    yield

    xs_groups.append(conv_silu(gcols(1)))
    mixed.attn[rows, 0:2 * LANES] = _unstack_heads(o0 * inv0, T, lo).astype(BF16)
    yield

    def prep(j):
        g = j // pairs_per_group
        dt = carry.dt_buf[...]
        ws, colbs, dtbs = [], [], []
        for h in (2 * j, 2 * j + 1):
            colb = jnp.broadcast_to(a_cs[:, h:h + 1], (T, T))
            rowb = jnp.broadcast_to(a_cs_t[h:h + 1, :], (T, T))
            seg = jnp.where(tril, jnp.exp(colb - rowb), 0.0)
            ws.append((cbs[g] * seg).astype(BF16))
            colbs.append(colb)
            dtbs.append(jnp.broadcast_to(dt[:, h:h + 1], (T, LANES)))
        dt_e = jnp.where(lo, dtbs[0], dtbs[1])
        acs_e = jnp.where(lo, colbs[0], colbs[1])
        xs_j = xs_groups[g][:, (j % pairs_per_group) * LANES:(j % pairs_per_group + 1) * LANES]
        xdt = xs_j * dt_e
        zero = jnp.zeros_like(xdt)
        rhs = jnp.concatenate([jnp.where(lo, xdt, zero), jnp.where(lo, zero, xdt)], axis=0).astype(BF16)
        alast = acs_e[T - 1:T, :]
        return dict(lhs=jnp.concatenate(ws, axis=1), rhs=rhs, xs=xs_j, e_acs=jnp.exp(acs_e),
                    xd=(xdt * jnp.exp(alast - acs_e)).astype(BF16), decay=jnp.exp(alast))

    def finish(j, p, y_diag):
        g, jj = divmod(j, pairs_per_group)
        sl = slice(j * LANES, (j + 1) * LANES)
        y = y_diag + y_offs[g][:, jj * LANES:(jj + 1) * LANES] * p["e_acs"] + p["xs"] * w.dskip[:, sl]
        return y * _silu(proj.z[rows, sl])

    mixed.attn[rows, 2 * LANES:4 * LANES] = _unstack_heads(o1 * inv1, T, lo).astype(BF16)
    preps = {0: prep(0)}
    yield

    ys, y_diags = [], {}
    for j in range(n_pairs):
        if j + 1 < n_pairs:
            preps[j + 1] = prep(j + 1)
        y_diags[j] = _mm(preps[j]["lhs"], preps[j]["rhs"])
        if j >= 1:
            ys.append(finish(j - 1, preps[j - 1], y_diags.pop(j - 1)))
        if j % pairs_per_group == pairs_per_group - 1:
            g = j // pairs_per_group
            grp = [preps[i] for i in range(g * pairs_per_group, (g + 1) * pairs_per_group)]
            carry.ht[:, gcols(g)] = (h_prevs[g] * jnp.concatenate([p["decay"] for p in grp], axis=1)
                                     + _mm(bgts[g], jnp.concatenate([p["xd"] for p in grp], axis=1)))
        yield
    ys.append(finish(n_pairs - 1, preps[n_pairs - 1], y_diags.pop(n_pairs - 1)))
    yield

    m_slabs = []
    for g in range(N_GROUPS_M):
        grp = ys[g * pairs_per_group:(g + 1) * pairs_per_group]
        ssq = grp[0] * grp[0]
        for y in grp[1:]:
            ssq = ssq + y * y
        scale = lax.rsqrt(jnp.sum(ssq, axis=-1, keepdims=True) * (1.0 / GROUP_WIDTH) + EPS)
        m_slabs.extend([y * scale for y in grp])
    mixed.ssd[rows, :] = (jnp.concatenate(m_slabs, axis=1) * w.ssmn[...]).astype(BF16)


def _chain(*gens):
    for g in gens:
        yield from g


def _prompt_mixer_kernel(sinks_ref, x_in_ref, x_res_ref, *refs, chunks_per_seq, n_steps):
    refs = list(refs)
    take = lambda n: [refs.pop(0) for _ in range(n)]
    w = MixerWeights(*take(len(MixerWeights._fields)))
    mlp_w_f32 = take(2)
    x1_ref, = take(1)
    outs = StateOuts(*take(4))
    mlp_w_bf16 = take(2)
    proj = (ProjBufs(*take(4)), ProjBufs(*take(4)))
    mixed = (MixedBufs(*take(2)), MixedBufs(*take(2)))
    hnorm = take(2)
    carry = Carry(*take(len(Carry._fields)))
    assert not refs
    T = CHUNK
    s = pl.program_id(0)
    every = slice(None)

    @pl.when(s == 0)
    def _():
        for ref in list(proj[1] + mixed[0] + mixed[1] + carry) + hnorm:
            ref[...] = jnp.zeros_like(ref)

    first = (s - 1) % (chunks_per_seq // 2) == 0

    def cast_mlp_weights():
        for src, dst in zip(mlp_w_f32, mlp_w_bf16):
            dst[...] = src[...].astype(BF16)
            yield

    def step(cur):
        prv = 1 - cur
        dense = _alternate(_stage_out_proj(x_res_ref, every, mixed[cur], w, x1_ref, hnorm[cur]),
                           _stage_in_proj(x_in_ref, every, w, proj[cur], hnorm[cur]), cast_mlp_weights())
        mix = _chain(_stage_mix(proj[prv], mixed[prv], slice(0, T), carry, w, sinks_ref, first),
                     _stage_mix(proj[prv], mixed[prv], slice(T, 2 * T), carry, w, sinks_ref, None))
        _trace_interleaved(mix, dense)

    def drain_all(*gens):
        for _ in _alternate(*gens):
            pass

    last = n_steps - 1
    pl.when(s == 0)(lambda: drain_all(_stage_in_proj(x_in_ref, every, w, proj[0], hnorm[0]), cast_mlp_weights()))
    pl.when(s == last)(lambda: drain_all(
        _stage_out_proj(x_res_ref, every, mixed[last % 2], w, x1_ref, hnorm[last % 2]), cast_mlp_weights()))
    for parity in (0, 1):
        pl.when((s > 0) & (s < last) & (s % 2 == parity))(functools.partial(step, parity))

    @pl.when((s >= 1) & (s % (chunks_per_seq // 2) == 0))
    def _():
        outs.k[0] = carry.kprev[...]
        outs.v[0] = carry.vprev[...]
        outs.conv[0] = carry.conv_tail[SUBLANES - (CONV_W - 1):, :]
        outs.ssm[0] = carry.ht[...].T


def _const_spec(shape):
    return pl.BlockSpec(shape, lambda *_: (0,) * len(shape), pipeline_mode=pl.Buffered(1))


def _prompt_mixer(x, sinks, weights, w_up, w_down):
    nb, seq, _ = x.shape
    assert seq % (2 * CHUNK) == 0
    chunks_per_seq = seq // CHUNK
    n_pairs = nb * chunks_per_seq // 2
    n_steps = n_pairs + 2
    pair = 2 * CHUNK
    consts = tuple(weights)
    cast_steps = 1 << ((n_steps // 2).bit_length() - 1)
    ff_slice = D_FF // cast_steps
    assert ff_slice % LANES == 0
    up_block = lambda s: (0, jnp.minimum(s, cast_steps - 1))
    down_block = lambda s: (jnp.clip(s - cast_steps, 0, cast_steps - 1), 0)
    seq_of_mix = lambda s: jnp.clip((2 * s - 1) // chunks_per_seq, 0, nb - 1)
    in_specs = ([pl.BlockSpec(memory_space=pltpu.SMEM),
                 pl.BlockSpec((pair, D_MODEL), lambda s: (jnp.minimum(s, n_pairs - 1), 0)),
                 pl.BlockSpec((pair, D_MODEL), lambda s: (jnp.maximum(s - 2, 0), 0))]
                + [_const_spec(a.shape) for a in consts]
                + [pl.BlockSpec((D_MODEL, ff_slice), up_block), pl.BlockSpec((ff_slice, D_MODEL), down_block)])
    out_shape = (jax.ShapeDtypeStruct((nb * seq, D_MODEL), F32),
                 jax.ShapeDtypeStruct((nb, WINDOW, KV_WIDTH), F32),
                 jax.ShapeDtypeStruct((nb, WINDOW, KV_WIDTH), F32),
                 jax.ShapeDtypeStruct((nb, CONV_W - 1, CONV_DIM), F32),
                 jax.ShapeDtypeStruct((nb, D_INNER, D_STATE), F32),
                 jax.ShapeDtypeStruct(w_up.shape, BF16), jax.ShapeDtypeStruct(w_down.shape, BF16))
    out_specs = (pl.BlockSpec((pair, D_MODEL), lambda s: (jnp.maximum(s - 2, 0), 0)),
                 pl.BlockSpec((1, WINDOW, KV_WIDTH), lambda s: (seq_of_mix(s), 0, 0)),
                 pl.BlockSpec((1, WINDOW, KV_WIDTH), lambda s: (seq_of_mix(s), 0, 0)),
                 pl.BlockSpec((1, CONV_W - 1, CONV_DIM), lambda s: (seq_of_mix(s), 0, 0)),
                 pl.BlockSpec((1, D_INNER, D_STATE), lambda s: (seq_of_mix(s), 0, 0)),
                 pl.BlockSpec((D_MODEL, ff_slice), up_block), pl.BlockSpec((ff_slice, D_MODEL), down_block))
    proj_bufs = [pltpu.VMEM((pair, QKV_WIDTH), F32), pltpu.VMEM((pair, D_INNER), F32),
                 pltpu.VMEM((pair, CONV_DIM), F32), pltpu.VMEM((pair, LANES), F32)]
    mixed_bufs = [pltpu.VMEM((pair, ATTN_WIDTH), BF16), pltpu.VMEM((pair, D_INNER), BF16)]
    hnorm_bufs = [pltpu.VMEM((pair, D_MODEL), BF16)] * 2
    carry = [pltpu.VMEM((CHUNK, KV_WIDTH), F32), pltpu.VMEM((CHUNK, KV_WIDTH), F32),
             pltpu.VMEM((SUBLANES, CONV_DIM), F32), pltpu.VMEM((D_STATE, D_INNER), F32),
             pltpu.VMEM((CHUNK, LANES), F32)]
    x_rows = x.reshape(nb * seq, D_MODEL)
    return pl.pallas_call(
        functools.partial(_prompt_mixer_kernel, chunks_per_seq=chunks_per_seq, n_steps=n_steps),
        grid=(n_steps,), in_specs=in_specs, out_specs=out_specs,
        out_shape=out_shape, scratch_shapes=proj_bufs * 2 + mixed_bufs * 2 + hnorm_bufs + carry,
        name="prompt_mixer",
        compiler_params=pltpu.CompilerParams(dimension_semantics=("arbitrary",),
                                             vmem_limit_bytes=VMEM_LIMIT),
    )(sinks, x_rows, x_rows, *consts, w_up, w_down)


MLP_TILE = 1024
FF_BLOCK = 1024


def _mlp_kernel(x_ref, norm2_ref, wup_ref, wdown_ref, fnorm_ref, y_ref):
    x = x_ref[...]
    hm = _rms(x, norm2_ref[...]).astype(BF16)
    acc = x
    for j in range(D_FF // FF_BLOCK):
        h = _mm(hm, wup_ref[:, j * FF_BLOCK:(j + 1) * FF_BLOCK])
        h = jnp.square(jnp.maximum(h, 0.0))
        acc = acc + _mm(h, wdown_ref[j * FF_BLOCK:(j + 1) * FF_BLOCK, :])
    y_ref[...] = _rms(acc, fnorm_ref[...])


def _mlp(x, norm2, wup, wdown, fnorm):
    rows = x.shape[0]
    tile = min(MLP_TILE, rows)
    assert rows % tile == 0 and tile % SUBLANES == 0
    consts = (norm2, wup, wdown, fnorm)
    return pl.pallas_call(
        _mlp_kernel, grid=(rows // tile,),
        in_specs=[pl.BlockSpec((tile, D_MODEL), lambda i: (i, 0))] + [_const_spec(a.shape) for a in consts],
        out_specs=pl.BlockSpec((tile, D_MODEL), lambda i: (i, 0)),
        out_shape=jax.ShapeDtypeStruct((rows, D_MODEL), F32), name="mlp",
        compiler_params=pltpu.CompilerParams(dimension_semantics=("arbitrary",),
                                             vmem_limit_bytes=VMEM_LIMIT),
    )(x, *consts)


IN_PROJ_ROWS = 256


def _in_proj_kernel(x_ref, norm1_ref, win_ref, wdt_ref, u_ref, dt_ref, hnorm_ref):
    hb = _rms(x_ref[...], norm1_ref[...]).astype(BF16)
    hnorm_ref[...] = hb
    for p in range(win_ref.shape[0]):
        u_ref[:, _piece_cols(p)] = _mm(hb, win_ref[p])
    dt_ref[...] = _mm(hb, wdt_ref[...])


def _in_proj(x, norm1, win, wdt):
    rows = x.shape[0]
    assert rows % IN_PROJ_ROWS == 0
    outs = ((win.shape[0] * PIECE, F32), (LANES, F32), (D_MODEL, BF16))
    row_spec = lambda width: pl.BlockSpec((IN_PROJ_ROWS, width), lambda i: (i, 0))
    return pl.pallas_call(
        _in_proj_kernel, grid=(rows // IN_PROJ_ROWS,),
        in_specs=[row_spec(D_MODEL), _const_spec(norm1.shape), _const_spec(win.shape), _const_spec(wdt.shape)],
        out_specs=tuple(row_spec(width) for width, _ in outs),
        out_shape=tuple(jax.ShapeDtypeStruct((rows, width), dtype) for width, dtype in outs),
        name="sample_in_proj",
        compiler_params=pltpu.CompilerParams(dimension_semantics=("arbitrary",),
                                             vmem_limit_bytes=VMEM_LIMIT),
    )(x, norm1, win, wdt)


def _tile_roll(x, shift):
    rows, width = x.shape
    tiles = x.reshape(rows // SUBLANES, SUBLANES, width)
    return pltpu.roll(tiles, shift % SUBLANES, axis=1).reshape(rows, width)


def _spread_rows(x):
    tiles = []
    for t in range(x.shape[0] // SUBLANES):
        two_seqs = x[t * SUBLANES:(t + 1) * SUBLANES]
        tiles += [two_seqs, pltpu.roll(two_seqs, SUBLANES // 2, axis=0)]
    return jnp.concatenate(tiles, axis=0)


def _gather_rows(y):
    low = lax.broadcasted_iota(jnp.int32, (SUBLANES, 1), 0) < SUBLANES // 2
    tiles = []
    for t in range(y.shape[0] // (2 * SUBLANES)):
        a = y[2 * t * SUBLANES:(2 * t + 1) * SUBLANES]
        b = y[(2 * t + 1) * SUBLANES:(2 * t + 2) * SUBLANES]
        tiles.append(jnp.where(low, a, pltpu.roll(b, SUBLANES // 2, axis=0)))
    return jnp.concatenate(tiles, axis=0)


def _expand_heads(cols, expand):
    terms = []
    for c in cols:
        hi = c.astype(BF16)
        r1 = c - hi.astype(F32)
        mid = r1.astype(BF16)
        lo = (r1 - mid.astype(F32)).astype(BF16)
        terms.extend([hi, mid, lo])
    rows = cols[0].shape[0]
    out = jnp.dot(jnp.concatenate(terms, axis=0), expand, preferred_element_type=F32)
    return [out[(3 * i) * rows:(3 * i + 1) * rows] + out[(3 * i + 1) * rows:(3 * i + 2) * rows]
            + out[(3 * i + 2) * rows:(3 * i + 3) * rows] for i in range(len(cols))]


def _sample_mixer_kernel(sinks_ref, u_ref, dtraw_ref, cprev_ref, kc_ref, vc_ref, st_ref,
                         convw_ref, convb_ref, dtb_ref, alog_ref, dskip_ref, ssmn_ref,
                         aout_ref, mout_ref, nk_ref, nv_ref, nconv_ref, nst_ref):
    R = SAMPLE_GROUP * SAMPLE_PAD
    L = 4
    qkv_ref = u_ref.at[:, QKV_PIECE0 * PIECE:Z_PIECE0 * PIECE]
    z_ref = u_ref.at[:, Z_PIECE0 * PIECE:XBC_PIECE0 * PIECE]
    xbc_ref = u_ref.at[:, XBC_PIECE0 * PIECE:N_IN_PIECES * PIECE]
    rm = lax.broadcasted_iota(jnp.int32, (R, 1), 0) % SAMPLE_PAD

    xr = xbc_raw = _spread_rows(xbc_ref[...])
    pad_rows = SAMPLE_PAD - (CONV_W - 1)
    prev = jnp.pad(cprev_ref[...], ((0, 0), (pad_rows, 0), (0, 0))).reshape(R, CONV_DIM)
    with_prev = jnp.where(rm >= pad_rows, prev, xr)
    yc = xr * convw_ref[CONV_W - 1:CONV_W, :]
    for kshift in range(1, CONV_W):
        yc = yc + _tile_roll(with_prev, kshift) * convw_ref[CONV_W - 1 - kshift:CONV_W - kshift, :]
    for i in range(SAMPLE_GROUP):
        nconv_ref[i] = xbc_raw[i * SAMPLE_PAD + 1:i * SAMPLE_PAD + CONV_W, :]
    xbc = _silu(yc + convb_ref[...])
    xs = xbc[:, :D_INNER]
    bm = xbc[:, D_INNER:D_INNER + N_GROUPS_M * D_STATE]
    cm = xbc[:, D_INNER + N_GROUPS_M * D_STATE:]

    dt = _softplus(_spread_rows(dtraw_ref[...]) + dtb_ref[...])
    d_a = dt * (-jnp.exp(alog_ref[...]))
    a_cs = d_a
    suf = jnp.zeros_like(d_a)
    for kshift in range(1, L):
        a_cs = a_cs + jnp.where(rm >= kshift, _tile_roll(d_a, kshift), 0.0)
        suf = suf + jnp.where(rm <= L - 1 - kshift, _tile_roll(d_a, -kshift), 0.0)

    expand = (lax.broadcasted_iota(jnp.int32, (LANES, D_INNER), 1) // HEAD_DIM_M
              == lax.broadcasted_iota(jnp.int32, (LANES, D_INNER), 0)).astype(BF16)
    dt_e, acs_e, suf_e = _expand_heads([dt, a_cs, suf], expand)
    xdt = xs * dt_e
    real = rm < L
    xd_t = jnp.where(real, xdt * jnp.exp(suf_e), 0.0).T.astype(BF16)

    y = xs * dskip_ref[...]
    for kshift in range(L):
        bk = bm if kshift == 0 else _tile_roll(bm, kshift)
        cb = cm * bk
        cb_e = jnp.concatenate(
            [jnp.broadcast_to(jnp.sum(cb[:, g * D_STATE:(g + 1) * D_STATE], axis=-1, keepdims=True),
                              (R, GROUP_WIDTH)) for g in range(N_GROUPS_M)], axis=1)
        if kshift == 0:
            y = y + cb_e * xdt
        else:
            seg = jnp.exp(acs_e - _tile_roll(acs_e, kshift))
            y = y + jnp.where(rm >= kshift, cb_e * seg * _tile_roll(xdt, kshift), 0.0)

    qkv = _spread_rows(qkv_ref[...])
    q = qkv[:, :ATTN_WIDTH] * (HEAD_DIM_A ** -0.5)
    q_swapped = jnp.concatenate(
        [pltpu.roll(q[:, s * LANES:(s + 1) * LANES], HEAD_DIM_A, axis=1) for s in range(ATTN_WIDTH // LANES)], axis=1)
    kn = qkv[:, ATTN_WIDTH:ATTN_WIDTH + KV_WIDTH]
    vn = qkv[:, ATTN_WIDTH + KV_WIDTH:]
    lo8 = _lane_lo((SAMPLE_PAD, LANES))
    zero8 = jnp.zeros((SAMPLE_PAD, LANES), F32)
    seq_rows = [slice(i * SAMPLE_PAD, (i + 1) * SAMPLE_PAD) for i in range(SAMPLE_GROUP)]
    HROWS = N_HEADS_A * SAMPLE_PAD

    s_c, s_n = [], []
    for rows in seq_rows:
        pieces = []
        for kv in range(N_KV_A):
            for g in range(Q_PER_KV):
                slab = kv * (Q_PER_KV // 2) + g // 2
                src = q if g % 2 == kv else q_swapped
                piece = src[rows, slab * LANES:(slab + 1) * LANES]
                pieces.append(jnp.where(lo8, piece, zero8) if kv == 0 else jnp.where(lo8, zero8, piece))
        lhs = jnp.concatenate(pieces, axis=0).astype(BF16)
        i = len(s_c)
        s_c.append(_mm(lhs, kc_ref[i]))
        s_n.append(_mm_nt(lhs, kn[rows]))
    s_c = jnp.concatenate(s_c, axis=0)
    s_n = jnp.concatenate(s_n, axis=0)
    n_rows = SAMPLE_GROUP * HROWS
    tok_c = lax.broadcasted_iota(jnp.int32, (n_rows, WINDOW), 0) % SAMPLE_PAD
    s_c = jnp.where(lax.broadcasted_iota(jnp.int32, (n_rows, WINDOW), 1) >= tok_c, s_c, NEG_BIG)
    tok_n = lax.broadcasted_iota(jnp.int32, (n_rows, SAMPLE_PAD), 0) % SAMPLE_PAD
    col_n = lax.broadcasted_iota(jnp.int32, (n_rows, SAMPLE_PAD), 1)
    s_n = jnp.where((col_n <= tok_n) & (col_n < L), s_n, NEG_BIG)
    sk = jnp.concatenate([jnp.full((SAMPLE_PAD, 1), sinks_ref[h], F32) for h in range(N_HEADS_A)] * SAMPLE_GROUP,
                         axis=0)
    m = jnp.maximum(jnp.maximum(jnp.max(s_c, axis=-1, keepdims=True), jnp.max(s_n, axis=-1, keepdims=True)), sk)
    e_c = jnp.exp(s_c - m)
    e_n = jnp.exp(s_n - m)
    inv_den = 1.0 / (jnp.sum(e_c, axis=-1, keepdims=True) + jnp.sum(e_n, axis=-1, keepdims=True) + jnp.exp(sk - m))
    e_c = e_c.astype(BF16)
    e_n = e_n.astype(BF16)
    o = jnp.concatenate([_mm_nt(e_c[i * HROWS:(i + 1) * HROWS], vc_ref[i]) + _mm(e_n[i * HROWS:(i + 1) * HROWS], vn[rows])
                         for i, rows in enumerate(seq_rows)], axis=0) * inv_den
    o_swapped = pltpu.roll(o, HEAD_DIM_A, axis=1)
    a_rows = []
    for i in range(SAMPLE_GROUP):
        blk = lambda arr, kv, g: arr[i * HROWS + (kv * Q_PER_KV + g) * SAMPLE_PAD:
                                     i * HROWS + (kv * Q_PER_KV + g + 1) * SAMPLE_PAD]
        slabs = []
        for kv in range(N_KV_A):
            for j in range(Q_PER_KV // 2):
                first = blk(o if kv == 0 else o_swapped, kv, 2 * j)
                second = blk(o_swapped if kv == 0 else o, kv, 2 * j + 1)
                slabs.append(jnp.where(lo8, first, second))
        a_rows.append(jnp.concatenate(slabs, axis=1))
    aout_ref[...] = _gather_rows(jnp.concatenate(a_rows, axis=0))

    key = lax.broadcasted_iota(jnp.int32, (KV_WIDTH, WINDOW), 1)
    for cache_ref, new, out_ref in ((kc_ref, kn, nk_ref), (vc_ref, vn, nv_ref)):
        new_t = new.T
        for i in range(SAMPLE_GROUP):
            kept = pltpu.roll(cache_ref[i], WINDOW - L, axis=1)
            fresh = pltpu.roll(new_t, (WINDOW - L - i * SAMPLE_PAD) % WINDOW, axis=1)
            out_ref[i] = jnp.where(key >= WINDOW - L, fresh, kept)

    row_r = lax.broadcasted_iota(jnp.int32, (R, D_STATE), 0)
    heads_per_group = N_HEADS_M // N_GROUPS_M
    yoff = []
    for i, rows in enumerate(seq_rows):
        state = st_ref[i]
        state_b = state.astype(BF16)
        a_tot = a_cs[i * SAMPLE_PAD + L - 1:i * SAMPLE_PAD + L, :]
        in_seq = (row_r >= i * SAMPLE_PAD) & (row_r < (i + 1) * SAMPLE_PAD)
        yo = []
        for g in range(N_GROUPS_M):
            gs = slice(g * GROUP_WIDTH, (g + 1) * GROUP_WIDTH)
            yo.append(_mm_nt(cm[rows, g * D_STATE:(g + 1) * D_STATE], state_b[gs]))
            bsel = jnp.where(in_seq, bm[:, g * D_STATE:(g + 1) * D_STATE], 0.0)
            upd = _mm(xd_t[gs, :], bsel)
            for hh in range(heads_per_group):
                h = g * heads_per_group + hh
                hs = slice(h * HEAD_DIM_M, (h + 1) * HEAD_DIM_M)
                decay = jnp.exp(jnp.broadcast_to(a_tot[:, h:h + 1], (HEAD_DIM_M, D_STATE)))
                nst_ref[i, hs, :] = state[hs] * decay + upd[hh * HEAD_DIM_M:(hh + 1) * HEAD_DIM_M]
        yoff.append(jnp.concatenate(yo, axis=1))
    yoff = jnp.concatenate(yoff, axis=0)

    y = (y + yoff * jnp.exp(acs_e)) * _silu(_spread_rows(z_ref[...]))
    outs = []
    for g in range(N_GROUPS_M):
        yg = y[:, g * GROUP_WIDTH:(g + 1) * GROUP_WIDTH]
        outs.append(yg * lax.rsqrt(jnp.mean(yg * yg, axis=-1, keepdims=True) + EPS))
    mout_ref[...] = _gather_rows(jnp.concatenate(outs, axis=1) * ssmn_ref[...])


def _sample_mixer(sinks, u, dtraw, cprev, kc, vc, st, convw, convb, dtb, alog, dskip, ssmn):
    nseq = kc.shape[0]
    assert nseq % SAMPLE_GROUP == 0
    T4 = SAMPLE_GROUP * 4
    consts = (convw, convb, dtb, alog, dskip, ssmn)
    tok_spec = lambda w: pl.BlockSpec((T4, w), lambda i: (i, 0))
    seq_spec = lambda a, b: pl.BlockSpec((SAMPLE_GROUP, a, b), lambda i: (i, 0, 0))
    in_specs = ([pl.BlockSpec(memory_space=pltpu.SMEM),
                 tok_spec(N_IN_PIECES * PIECE), tok_spec(LANES), seq_spec(CONV_W - 1, CONV_DIM),
                 seq_spec(WINDOW, KV_WIDTH), seq_spec(WINDOW, KV_WIDTH), seq_spec(D_INNER, D_STATE)]
                + [_const_spec(a.shape) for a in consts])
    out_shape = (jax.ShapeDtypeStruct((nseq * 4, ATTN_WIDTH), F32), jax.ShapeDtypeStruct((nseq * 4, D_INNER), F32),
                 jax.ShapeDtypeStruct((nseq, WINDOW, KV_WIDTH), F32),
                 jax.ShapeDtypeStruct((nseq, WINDOW, KV_WIDTH), F32),
                 jax.ShapeDtypeStruct((nseq, CONV_W - 1, CONV_DIM), F32),
                 jax.ShapeDtypeStruct((nseq, D_INNER, D_STATE), F32))
    out_specs = (tok_spec(ATTN_WIDTH), tok_spec(D_INNER), seq_spec(WINDOW, KV_WIDTH), seq_spec(WINDOW, KV_WIDTH),
                 seq_spec(CONV_W - 1, CONV_DIM), seq_spec(D_INNER, D_STATE))
    return pl.pallas_call(
        _sample_mixer_kernel, grid=(nseq // SAMPLE_GROUP,), in_specs=in_specs, out_specs=out_specs,
        out_shape=out_shape, name="sample_mixer",
        compiler_params=pltpu.CompilerParams(dimension_semantics=("arbitrary",),
                                             vmem_limit_bytes=VMEM_LIMIT),
    )(sinks, u, dtraw, cprev, kc, vc, st, *consts)


N_OUT_PIECES = D_MODEL // PIECE
N_FF_BLOCKS = D_FF // FF_BLOCK


def _sample_post_kernel(x_ref, a_ref, m_ref, hnorm_ref, wga_ref, wgb_ref, woa_ref, wob_ref, wo_ref,
                        norm2_ref, wup_ref, wdown_ref, fnorm_ref, y_ref, merged_s, x1_s, hm_s, acc_s):
    s = pl.program_id(0)

    @pl.when(s < N_OUT_PIECES)
    def _():
        hb = hnorm_ref[...]
        piece = (_sigmoid(_mm(hb, wga_ref[0])) * _mm(a_ref[...], woa_ref[0])
                 + _sigmoid(_mm(hb, wgb_ref[0])) * _mm(m_ref[...], wob_ref[0]))
        for p in range(N_OUT_PIECES):
            @pl.when(s == p)
            def _():
                merged_s[:, _piece_cols(p)] = piece.astype(BF16)

    @pl.when((s >= N_OUT_PIECES) & (s < 2 * N_OUT_PIECES))
    def _():
        piece = _mm(merged_s[...], wo_ref[0])
        for p in range(N_OUT_PIECES):
            @pl.when(s == N_OUT_PIECES + p)
            def _():
                x1_s[:, _piece_cols(p)] = x_ref[:, _piece_cols(p)] + piece

    @pl.when(s == 2 * N_OUT_PIECES)
    def _():
        x1 = x1_s[...]
        hm_s[...] = _rms(x1, norm2_ref[...]).astype(BF16)
        acc_s[...] = x1

    @pl.when(s >= 2 * N_OUT_PIECES)
    def _():
        h = jnp.square(jnp.maximum(_mm(hm_s[...], wup_ref[...]), 0.0))
        acc_s[...] += _mm(h, wdown_ref[...])

    @pl.when(s == 2 * N_OUT_PIECES + N_FF_BLOCKS - 1)
    def _():
        y_ref[...] = _rms(acc_s[...], fnorm_ref[...])


def _sample_post(x, a_out, m_out, hnorm, wgate, woa, wob, wo, norm2, wup, wdown, fnorm):
    rows = x.shape[0]
    first = lambda s: jnp.minimum(s, N_OUT_PIECES - 1)
    second = lambda s: jnp.clip(s - N_OUT_PIECES, 0, N_OUT_PIECES - 1)
    ff = lambda s: jnp.clip(s - 2 * N_OUT_PIECES, 0, N_FF_BLOCKS - 1)
    piece_spec = lambda w, index: pl.BlockSpec((1,) + w.shape[1:], lambda s: (index(s), 0, 0))
    in_specs = [_const_spec(x.shape), _const_spec(a_out.shape), _const_spec(m_out.shape), _const_spec(hnorm.shape),
                piece_spec(wgate, first), piece_spec(wgate, lambda s: N_OUT_PIECES + first(s)),
                piece_spec(woa, first), piece_spec(wob, first), piece_spec(wo, second),
                _const_spec(norm2.shape),
                pl.BlockSpec((D_MODEL, FF_BLOCK), lambda s: (0, ff(s))),
                pl.BlockSpec((FF_BLOCK, D_MODEL), lambda s: (ff(s), 0)),
                _const_spec(fnorm.shape)]
    return pl.pallas_call(
        _sample_post_kernel, grid=(2 * N_OUT_PIECES + N_FF_BLOCKS,), in_specs=in_specs,
        out_specs=pl.BlockSpec((rows, D_MODEL), lambda s: (0, 0)),
        out_shape=jax.ShapeDtypeStruct((rows, D_MODEL), F32),
        scratch_shapes=[pltpu.VMEM((rows, D_MODEL), BF16), pltpu.VMEM((rows, D_MODEL), F32),
                        pltpu.VMEM((rows, D_MODEL), BF16), pltpu.VMEM((rows, D_MODEL), F32)],
        name="sample_post",
        compiler_params=pltpu.CompilerParams(dimension_semantics=("arbitrary",),
                                             vmem_limit_bytes=VMEM_LIMIT),
    )(x, a_out, m_out, hnorm, wgate, wgate, woa, wob, wo, norm2, wup, wdown, fnorm)


def _layer_params(norm1, w_in, sinks, conv_w, conv_b, dt_bias, a_log, d_skip, ssm_norm, w_oa, w_ob, w_o,
                  norm2):
    dt0 = QKV_WIDTH + D_INNER + CONV_DIM
    assert dt0 == N_IN_PIECES * PIECE
    pad_lanes = lambda a: jnp.pad(a, ((0, 0), (0, LANES - a.shape[1])))
    w_in_t = w_in.T
    woa, wob, wo = _prep_pieces(w_oa, w_ob, w_o)
    win, wdt = _prep_pieces_t(w_in_t, N_IN_PIECES, dt0)
    mixer = MixerWeights(
        norm1=norm1[None, :], win=win,
        wgate=_prep_pieces_t_shifted(w_in_t, dt0 + N_HEADS_M, 2 * D_MODEL // PIECE),
        wdt=wdt,
        convw=conv_w, convb=conv_b[None, :], dtb=pad_lanes(dt_bias[None, :]), alog=pad_lanes(a_log[None, :]),
        dskip=jnp.repeat(d_skip, HEAD_DIM_M)[None, :], ssmn=ssm_norm[None, :],
        woa=woa, wob=wob, wo=wo)
    return dict(sinks=sinks.astype(F32), mixer=mixer, norm2=norm2[None, :])


def kernel(x_prompt, x_sample, cache_swa_k, cache_swa_v, state_conv, state_ssm, norm1, w_in, sinks, conv_w,
           conv_b, dt_bias, a_log, d_skip, ssm_norm, w_oa, w_ob, w_o, norm2, w_up, w_down, final_norm):
    depth = w_in.shape[0]
    assert depth == 1
    nb, seq, _ = x_prompt.shape
    ns, ls, _ = x_sample.shape
    assert ls == 4
    p = _layer_params(norm1[0], w_in[0], sinks[0], conv_w[0], conv_b[0], dt_bias[0], a_log[0], d_skip[0],
                      ssm_norm[0], w_oa[0], w_ob[0], w_o[0], norm2[0])
    fnorm = final_norm[None, :]
    mw = p["mixer"]
    mixer_consts = (mw.convw, mw.convb, mw.dtb, mw.alog, mw.dskip, mw.ssmn)

    x1p, pk, pv, pc, pst, wup, wdown = _prompt_mixer(x_prompt, p["sinks"], mw, w_up[0], w_down[0])
    y_prompt = _mlp(x1p, p["norm2"], wup, wdown, fnorm)
    y_prompt = y_prompt.reshape(nb, seq, D_MODEL)

    xs_rows = x_sample.reshape(ns * ls, D_MODEL)
    dim_major = lambda a: jnp.swapaxes(a, 1, 2)
    u, dtraw, hnorm = _in_proj(xs_rows, mw.norm1, mw.win, mw.wdt)
    a_out, m_out, sk, sv, sc, sst = _sample_mixer(
        p["sinks"], u, dtraw, state_conv[0],
        dim_major(cache_swa_k[0].reshape(ns, WINDOW, KV_WIDTH)), dim_major(cache_swa_v[0].reshape(ns, WINDOW, KV_WIDTH)),
        state_ssm[0].reshape(ns, D_INNER, D_STATE), *mixer_consts)
    sk, sv = dim_major(sk), dim_major(sv)
    y_sample = _sample_post(xs_rows, a_out, m_out, hnorm, mw.wgate, mw.woa, mw.wob, mw.wo,
                            p["norm2"], wup, wdown, fnorm).reshape(ns, ls, D_MODEL)

    kv_shape = (1, -1, WINDOW, N_KV_A, HEAD_DIM_A)
    ssm_shape = (1, -1, N_HEADS_M, HEAD_DIM_M, D_STATE)
    return (y_prompt, y_sample,
            pk.reshape(kv_shape), pv.reshape(kv_shape), pc[None], pst.reshape(ssm_shape),
            sk.reshape(kv_shape), sv.reshape(kv_shape), sc[None], sst.reshape(ssm_shape))
```

```python
import collections
import functools

import jax
import jax.numpy as jnp
from jax import lax
from jax.experimental import pallas as pl
from jax.experimental.pallas import tpu as pltpu

F32 = jnp.float32
BF16 = jnp.bfloat16

D_MODEL = 1024
N_HEADS_A = 8
N_KV_A = 2
Q_PER_KV = N_HEADS_A // N_KV_A
HEAD_DIM_A = 64
WINDOW = 128
ATTN_WIDTH = N_HEADS_A * HEAD_DIM_A
KV_WIDTH = N_KV_A * HEAD_DIM_A
D_INNER = 1024
HEAD_DIM_M = 64
N_HEADS_M = D_INNER // HEAD_DIM_M
N_GROUPS_M = 2
GROUP_WIDTH = D_INNER // N_GROUPS_M
D_STATE = 128
CONV_W = 4
CONV_DIM = D_INNER + 2 * N_GROUPS_M * D_STATE
CHUNK = 128
D_FF = 4 * D_MODEL
EPS = 1e-6

LANES = 128
SUBLANES = 8
QKV_WIDTH = ATTN_WIDTH + 2 * KV_WIDTH
NEG_BIG = -1e30
VMEM_LIMIT = 56 * 1024 * 1024
SAMPLE_PAD = SUBLANES
SAMPLE_GROUP = 16


def _mm(a, b):
    return jnp.dot(a.astype(BF16), b.astype(BF16), preferred_element_type=F32)


def _mm_nt(a, b):
    return lax.dot_general(a.astype(BF16), b.astype(BF16), (((1,), (1,)), ((), ())),
                           preferred_element_type=F32)


def _rms(x, w):
    return x * lax.rsqrt(jnp.mean(x * x, axis=-1, keepdims=True) + EPS) * w


def _sigmoid(x):
    return 0.5 + 0.5 * jnp.tanh(0.5 * x)


def _silu(x):
    h = 0.5 * x
    return h + h * jnp.tanh(h)


def _softplus(x):
    return jnp.maximum(x, 0.0) + jnp.log(1.0 + jnp.exp(-jnp.abs(x)))


def _lane_lo(shape):
    return (lax.broadcasted_iota(jnp.int32, shape, len(shape) - 1) % LANES) < HEAD_DIM_A


def _dup_half(x, lo, first):
    xr = pltpu.roll(x, HEAD_DIM_A, axis=1)
    return jnp.where(lo, x, xr) if first else jnp.where(lo, xr, x)


def _stack_heads(q, kv, lo):
    qa = q[:, kv * 2 * LANES: kv * 2 * LANES + LANES]
    qb = q[:, kv * 2 * LANES + LANES: (kv + 1) * 2 * LANES]
    zero = jnp.zeros_like(qa)
    return jnp.concatenate([jnp.where(lo, qa, zero), jnp.where(lo, zero, qa),
                            jnp.where(lo, qb, zero), jnp.where(lo, zero, qb)], axis=0)


def _unstack_heads(o, rows, lo):
    return jnp.concatenate([jnp.where(lo, o[0:rows], o[rows:2 * rows]),
                            jnp.where(lo, o[2 * rows:3 * rows], o[3 * rows:4 * rows])], axis=1)


ProjBufs = collections.namedtuple("ProjBufs", "qkv z xbc dt")
MixedBufs = collections.namedtuple("MixedBufs", "attn ssd")
Carry = collections.namedtuple("Carry", "kprev vprev conv_tail ht dt_buf")
MixerWeights = collections.namedtuple(
    "MixerWeights", "norm1 win wgate wdt convw convb dtb alog dskip ssmn woa wob wo")

PIECE = 256
QKV_PIECE0 = 0
Z_PIECE0 = QKV_PIECE0 + QKV_WIDTH // PIECE
XBC_PIECE0 = Z_PIECE0 + D_INNER // PIECE
N_IN_PIECES = XBC_PIECE0 + CONV_DIM // PIECE
PREP_PIECES = 4


def _prep_pieces_kernel(*refs):
    n = len(refs) // 2
    for w_ref, o_ref in zip(refs[:n], refs[n:]):
        for j in range(o_ref.shape[0]):
            o_ref[j] = w_ref[:, j * PIECE:(j + 1) * PIECE].astype(BF16)


def _prep_pieces(*ws):
    return pl.pallas_call(
        _prep_pieces_kernel, grid=(1,),
        in_specs=[_const_spec(w.shape) for w in ws],
        out_specs=tuple(pl.BlockSpec((w.shape[1] // PIECE, w.shape[0], PIECE), lambda i: (0, 0, 0)) for w in ws),
        out_shape=tuple(jax.ShapeDtypeStruct((w.shape[1] // PIECE, w.shape[0], PIECE), BF16) for w in ws),
        name="prep_weight_pieces",
        compiler_params=pltpu.CompilerParams(dimension_semantics=("arbitrary",), vmem_limit_bytes=VMEM_LIMIT),
    )(*ws)


def _prep_pieces_t_kernel(wt_ref, o_ref, dt_ref, *, dt_step, dt_row):
    for j in range(PREP_PIECES):
        o_ref[j] = wt_ref[j * PIECE:(j + 1) * PIECE, :].T.astype(BF16)

    @pl.when(pl.program_id(0) == dt_step)
    def _():
        rows = wt_ref[dt_row:dt_row + LANES, :]
        keep = lax.broadcasted_iota(jnp.int32, rows.shape, 0) < N_HEADS_M
        dt_ref[...] = jnp.where(keep, rows, 0.0).T.astype(BF16)


def _prep_pieces_t(wt, n_pieces, dt_row0):
    k_dim = wt.shape[1]
    block_rows = PREP_PIECES * PIECE
    n_blocks = pl.cdiv(n_pieces, PREP_PIECES)
    dt_step, dt_row = divmod(dt_row0, block_rows)
    assert dt_step < n_blocks and dt_row % SUBLANES == 0 and dt_row + LANES <= block_rows
    return pl.pallas_call(
        functools.partial(_prep_pieces_t_kernel, dt_step=dt_step, dt_row=dt_row), grid=(n_blocks,),
        in_specs=[pl.BlockSpec((block_rows, k_dim), lambda p: (p, 0))],
        out_specs=(pl.BlockSpec((PREP_PIECES, k_dim, PIECE), lambda p: (p, 0, 0)),
                   pl.BlockSpec((k_dim, LANES), lambda p: (0, 0))),
        out_shape=(jax.ShapeDtypeStruct((n_pieces, k_dim, PIECE), BF16),
                   jax.ShapeDtypeStruct((k_dim, LANES), BF16)), name="prep_weight_pieces_t",
        compiler_params=pltpu.CompilerParams(dimension_semantics=("arbitrary",),
                                             vmem_limit_bytes=VMEM_LIMIT),
    )(wt)


def _prep_pieces_t_shifted_kernel(blk_ref, o_ref, prev_s, *, shift):
    @pl.when(pl.program_id(0) > 0)
    def _():
        rows = jnp.concatenate([prev_s[shift:, :], blk_ref[:shift, :]], axis=0)
        o_ref[0] = rows.T.astype(BF16)

    prev_s[...] = blk_ref[...]


def _prep_pieces_t_shifted(wt, row0, n_pieces):
    k_dim = wt.shape[1]
    block0, shift = divmod(row0, PIECE)
    assert shift % SUBLANES == 0 and shift > 0
    return pl.pallas_call(
        functools.partial(_prep_pieces_t_shifted_kernel, shift=shift), grid=(n_pieces + 1,),
        in_specs=[pl.BlockSpec((PIECE, k_dim), lambda t: (block0 + t, 0))],
        out_specs=pl.BlockSpec((1, k_dim, PIECE), lambda t: (jnp.maximum(t - 1, 0), 0, 0)),
        out_shape=jax.ShapeDtypeStruct((n_pieces, k_dim, PIECE), BF16),
        scratch_shapes=[pltpu.VMEM((PIECE, k_dim), F32)], name="prep_weight_pieces_t_shifted",
        compiler_params=pltpu.CompilerParams(dimension_semantics=("arbitrary",),
                                             vmem_limit_bytes=VMEM_LIMIT),
    )(wt)


StateOuts = collections.namedtuple("StateOuts", "k v conv ssm")


def _split3_bf16(x):
    hi = x.astype(BF16)
    r1 = x - hi.astype(F32)
    mid = r1.astype(BF16)
    return hi, mid, (r1 - mid.astype(F32)).astype(BF16)


def _piece_cols(p):
    return slice(p * PIECE, (p + 1) * PIECE)


def _in_proj_pieces(hb, w, dsts):
    for dst, first in zip(dsts, (QKV_PIECE0, Z_PIECE0, XBC_PIECE0)):
        for p in range(dst.shape[1] // PIECE):
            dst[:, _piece_cols(p)] = _mm(hb, w.win[first + p])
            yield


def _out_proj_pieces(x_ref, rows, hb, attn_ref, ssd_ref, w, out_ref):
    merged = []
    for half, (src_ref, w_ref) in enumerate(((attn_ref, w.woa), (ssd_ref, w.wob))):
        parts = []
        for p in range(D_MODEL // PIECE):
            gate = _sigmoid(_mm(hb, w.wgate[half * (D_MODEL // PIECE) + p]))
            yield
            parts.append(gate * _mm(src_ref[...], w_ref[p]))
            yield
        merged.append(jnp.concatenate(parts, axis=1))
    merged = (merged[0] + merged[1]).astype(BF16)
    for p in range(D_MODEL // PIECE):
        out_ref[rows, _piece_cols(p)] = x_ref[rows, _piece_cols(p)] + _mm(merged, w.wo[p])
        yield


def _stage_in_proj(x_ref, rows, w, proj, hnorm_ref):
    hb = _rms(x_ref[rows, :], w.norm1[...]).astype(BF16)
    hnorm_ref[...] = hb
    yield
    yield from _in_proj_pieces(hb, w, (proj.qkv, proj.z, proj.xbc))
    proj.dt[...] = _mm(hb, w.wdt[...])
    yield


def _stage_out_proj(x_ref, rows, mixed, w, x1_ref, hnorm_ref):
    hb = hnorm_ref[...]
    yield
    yield from _out_proj_pieces(x_ref, rows, hb, mixed.attn, mixed.ssd, w, x1_ref)


def _alternate(*gens):
    gens = list(gens)
    while gens:
        for g in list(gens):
            try:
                next(g)
                yield
            except StopIteration:
                gens.remove(g)


def _trace_interleaved(primary, filler):
    for _ in primary:
        next(filler, None)
    for _ in filler:
        pass


def _stage_mix(proj, mixed, rows, carry, w, sinks_ref, first):
    T = CHUNK
    reset = (lambda a: a) if first is None else (lambda a: jnp.where(first, 0.0, a))
    lo = _lane_lo((T, LANES))
    lo2 = _lane_lo((2 * T, LANES))
    rm = lax.broadcasted_iota(jnp.int32, (T, 1), 0) % SUBLANES
    rr = lax.broadcasted_iota(jnp.int32, (T, T), 0)
    cc = lax.broadcasted_iota(jnp.int32, (T, T), 1)
    tril = cc <= rr
    pairs_per_group = N_HEADS_M // 2 // N_GROUPS_M
    n_pairs = N_HEADS_M // 2
    gcols = lambda g: slice(g * GROUP_WIDTH, (g + 1) * GROUP_WIDTH)

    def conv_silu(cols):
        xr = proj.xbc[rows, cols]
        prev_tile = jnp.concatenate([reset(carry.conv_tail[:, cols]), xr[:T - SUBLANES]], axis=0)
        carry.conv_tail[:, cols] = xr[T - SUBLANES:]
        yc = xr * w.convw[CONV_W - 1:CONV_W, cols]
        for shift in range(1, CONV_W):
            shifted = _tile_roll(jnp.where(rm >= SUBLANES - shift, prev_tile, xr), shift)
            yc = yc + shifted * w.convw[CONV_W - 1 - shift:CONV_W - shift, cols]
        return _silu(yc + w.convb[:, cols])

    def softmax(s, kv):
        r = lax.broadcasted_iota(jnp.int32, (T, 2 * T), 0)
        col = lax.broadcasted_iota(jnp.int32, (T, 2 * T), 1)
        valid = (col >= r) & (col <= r + WINDOW)
        if first is not None:
            valid = valid & (col >= jnp.where(first, T, 0))
        es, inv = [], []
        for g in range(Q_PER_KV):
            sk = sinks_ref[kv * Q_PER_KV + g]
            sg = jnp.where(valid, s[g * T:(g + 1) * T], NEG_BIG)
            m = jnp.maximum(jnp.max(sg, axis=-1, keepdims=True), sk)
            e = jnp.exp(sg - m)
            es.append(e.astype(BF16))
            inv.append(1.0 / (jnp.sum(e, axis=-1, keepdims=True) + jnp.exp(sk - m)))
        return jnp.concatenate(es, axis=0), jnp.concatenate(inv, axis=0)

    qkv = proj.qkv[rows, :]
    q = qkv[:, :ATTN_WIDTH] * (HEAD_DIM_A ** -0.5)
    k = qkv[:, ATTN_WIDTH:ATTN_WIDTH + KV_WIDTH]
    v = qkv[:, ATTN_WIDTH + KV_WIDTH:]
    kk = jnp.concatenate([carry.kprev[...], k], axis=0)
    vv = jnp.concatenate([carry.vprev[...], v], axis=0)
    carry.kprev[...] = k
    carry.vprev[...] = v
    carry.dt_buf[...] = _softplus(proj.dt[rows, :] + w.dtb[...])
    da_split = jnp.concatenate(_split3_bf16(carry.dt_buf[...] * -jnp.exp(w.alog[...])), axis=1)
    vds = [_dup_half(vv, lo2, kv == 0).astype(BF16) for kv in range(N_KV_A)]
    scores = [_mm_nt(_stack_heads(q, kv, lo), _dup_half(kk, lo2, kv == 0).astype(BF16))
              for kv in range(N_KV_A)]
    yield

    bc = conv_silu(slice(D_INNER, CONV_DIM))
    bgs = [bc[:, g * D_STATE:(g + 1) * D_STATE].astype(BF16) for g in range(N_GROUPS_M)]
    cgs = [bc[:, (N_GROUPS_M + g) * D_STATE:(N_GROUPS_M + g + 1) * D_STATE].astype(BF16)
           for g in range(N_GROUPS_M)]
    bgts = [bc[:, g * D_STATE:(g + 1) * D_STATE].T.astype(BF16) for g in range(N_GROUPS_M)]
    yield

    es0, inv0 = softmax(scores[0], 0)
    yield

    es1, inv1 = softmax(scores[1], 1)
    cbs = [_mm_nt(cgs[g], bgs[g]) for g in range(N_GROUPS_M)]
    h_prevs = [reset(carry.ht[:, gcols(g)]) for g in range(N_GROUPS_M)]
    y_offs = [_mm(cgs[g], h_prevs[g]) for g in range(N_GROUPS_M)]
    o0 = _mm(es0, vds[0])
    cs = jnp.dot(jnp.where(tril, 1.0, 0.0).astype(BF16), da_split, preferred_element_type=F32)
    yield

    a_cs = cs[:, :LANES] + cs[:, LANES:2 * LANES] + cs[:, 2 * LANES:]
    a_cs_t = a_cs.T
    xs_groups = [conv_silu(gcols(0))]
    o1 = _mm(es1, vds[1])
    yield

    xs_groups.append(conv_silu(gcols(1)))
    mixed.attn[rows, 0:2 * LANES] = _unstack_heads(o0 * inv0, T, lo).astype(BF16)
    yield

    def prep(j):
        g = j // pairs_per_group
        dt = carry.dt_buf[...]
        ws, colbs, dtbs = [], [], []
        for h in (2 * j, 2 * j + 1):
            colb = jnp.broadcast_to(a_cs[:, h:h + 1], (T, T))
            rowb = jnp.broadcast_to(a_cs_t[h:h + 1, :], (T, T))
            seg = jnp.where(tril, jnp.exp(colb - rowb), 0.0)
            ws.append((cbs[g] * seg).astype(BF16))
            colbs.append(colb)
            dtbs.append(jnp.broadcast_to(dt[:, h:h + 1], (T, LANES)))
        dt_e = jnp.where(lo, dtbs[0], dtbs[1])
        acs_e = jnp.where(lo, colbs[0], colbs[1])
        xs_j = xs_groups[g][:, (j % pairs_per_group) * LANES:(j % pairs_per_group + 1) * LANES]
        xdt = xs_j * dt_e
        zero = jnp.zeros_like(xdt)
        rhs = jnp.concatenate([jnp.where(lo, xdt, zero), jnp.where(lo, zero, xdt)], axis=0).astype(BF16)
        alast = acs_e[T - 1:T, :]
        return dict(lhs=jnp.concatenate(ws, axis=1), rhs=rhs, xs=xs_j, e_acs=jnp.exp(acs_e),
                    xd=(xdt * jnp.exp(alast - acs_e)).astype(BF16), decay=jnp.exp(alast))

    def finish(j, p, y_diag):
        g, jj = divmod(j, pairs_per_group)
        sl = slice(j * LANES, (j + 1) * LANES)
        y = y_diag + y_offs[g][:, jj * LANES:(jj + 1) * LANES] * p["e_acs"] + p["xs"] * w.dskip[:, sl]
        return y * _silu(proj.z[rows, sl])

    mixed.attn[rows, 2 * LANES:4 * LANES] = _unstack_heads(o1 * inv1, T, lo).astype(BF16)
    preps = {0: prep(0)}
    yield

    ys, y_diags = [], {}
    for j in range(n_pairs):
        if j + 1 < n_pairs:
            preps[j + 1] = prep(j + 1)
        y_diags[j] = _mm(preps[j]["lhs"], preps[j]["rhs"])
        if j >= 1:
            ys.append(finish(j - 1, preps[j - 1], y_diags.pop(j - 1)))
        if j % pairs_per_group == pairs_per_group - 1:
            g = j // pairs_per_group
            grp = [preps[i] for i in range(g * pairs_per_group, (g + 1) * pairs_per_group)]
            carry.ht[:, gcols(g)] = (h_prevs[g] * jnp.concatenate([p["decay"] for p in grp], axis=1)
                                     + _mm(bgts[g], jnp.concatenate([p["xd"] for p in grp], axis=1)))
        yield
    ys.append(finish(n_pairs - 1, preps[n_pairs - 1], y_diags.pop(n_pairs - 1)))
    yield

    m_slabs = []
    for g in range(N_GROUPS_M):
        grp = ys[g * pairs_per_group:(g + 1) * pairs_per_group]
        ssq = grp[0] * grp[0]
        for y in grp[1:]:
            ssq = ssq + y * y
        scale = lax.rsqrt(jnp.sum(ssq, axis=-1, keepdims=True) * (1.0 / GROUP_WIDTH) + EPS)
        m_slabs.extend([y * scale for y in grp])
    mixed.ssd[rows, :] = (jnp.concatenate(m_slabs, axis=1) * w.ssmn[...]).astype(BF16)


def _chain(*gens):
    for g in gens:
        yield from g


def _prompt_mixer_kernel(sinks_ref, x_in_ref, x_res_ref, *refs, chunks_per_seq, n_steps):
    refs = list(refs)
    take = lambda n: [refs.pop(0) for _ in range(n)]
    w = MixerWeights(*take(len(MixerWeights._fields)))
    mlp_w_f32 = take(2)
    x1_ref, = take(1)
    outs = StateOuts(*take(4))
    mlp_w_bf16 = take(2)
    proj = (ProjBufs(*take(4)), ProjBufs(*take(4)))
    mixed = (MixedBufs(*take(2)), MixedBufs(*take(2)))
    hnorm = take(2)
    carry = Carry(*take(len(Carry._fields)))
    assert not refs
    T = CHUNK
    s = pl.program_id(0)
    every = slice(None)

    @pl.when(s == 0)
    def _():
        for ref in list(proj[1] + mixed[0] + mixed[1] + carry) + hnorm:
            ref[...] = jnp.zeros_like(ref)

    first = (s - 1) % (chunks_per_seq // 2) == 0

    def cast_mlp_weights():
        for src, dst in zip(mlp_w_f32, mlp_w_bf16):
            dst[...] = src[...].astype(BF16)
            yield

    def step(cur):
        prv = 1 - cur
        dense = _alternate(_stage_out_proj(x_res_ref, every, mixed[cur], w, x1_ref, hnorm[cur]),
                           _stage_in_proj(x_in_ref, every, w, proj[cur], hnorm[cur]), cast_mlp_weights())
        mix = _chain(_stage_mix(proj[prv], mixed[prv], slice(0, T), carry, w, sinks_ref, first),
                     _stage_mix(proj[prv], mixed[prv], slice(T, 2 * T), carry, w, sinks_ref, None))
        _trace_interleaved(mix, dense)

    def drain_all(*gens):
        for _ in _alternate(*gens):
            pass

    last = n_steps - 1
    pl.when(s == 0)(lambda: drain_all(_stage_in_proj(x_in_ref, every, w, proj[0], hnorm[0]), cast_mlp_weights()))
    pl.when(s == last)(lambda: drain_all(
        _stage_out_proj(x_res_ref, every, mixed[last % 2], w, x1_ref, hnorm[last % 2]), cast_mlp_weights()))
    for parity in (0, 1):
        pl.when((s > 0) & (s < last) & (s % 2 == parity))(functools.partial(step, parity))

    @pl.when((s >= 1) & (s % (chunks_per_seq // 2) == 0))
    def _():
        outs.k[0] = carry.kprev[...]
        outs.v[0] = carry.vprev[...]
        outs.conv[0] = carry.conv_tail[SUBLANES - (CONV_W - 1):, :]
        outs.ssm[0] = carry.ht[...].T


def _const_spec(shape):
    return pl.BlockSpec(shape, lambda *_: (0,) * len(shape), pipeline_mode=pl.Buffered(1))


def _prompt_mixer(x, sinks, weights, w_up, w_down):
    nb, seq, _ = x.shape
    assert seq % (2 * CHUNK) == 0
    chunks_per_seq = seq // CHUNK
    n_pairs = nb * chunks_per_seq // 2
    n_steps = n_pairs + 2
    pair = 2 * CHUNK
    consts = tuple(weights)
    cast_steps = 1 << ((n_steps // 2).bit_length() - 1)
    ff_slice = D_FF // cast_steps
    assert ff_slice % LANES == 0
    up_block = lambda s: (0, jnp.minimum(s, cast_steps - 1))
    down_block = lambda s: (jnp.clip(s - cast_steps, 0, cast_steps - 1), 0)
    seq_of_mix = lambda s: jnp.clip((2 * s - 1) // chunks_per_seq, 0, nb - 1)
    in_specs = ([pl.BlockSpec(memory_space=pltpu.SMEM),
                 pl.BlockSpec((pair, D_MODEL), lambda s: (jnp.minimum(s, n_pairs - 1), 0)),
                 pl.BlockSpec((pair, D_MODEL), lambda s: (jnp.maximum(s - 2, 0), 0))]
                + [_const_spec(a.shape) for a in consts]
                + [pl.BlockSpec((D_MODEL, ff_slice), up_block), pl.BlockSpec((ff_slice, D_MODEL), down_block)])
    out_shape = (jax.ShapeDtypeStruct((nb * seq, D_MODEL), F32),
                 jax.ShapeDtypeStruct((nb, WINDOW, KV_WIDTH), F32),
                 jax.ShapeDtypeStruct((nb, WINDOW, KV_WIDTH), F32),
                 jax.ShapeDtypeStruct((nb, CONV_W - 1, CONV_DIM), F32),
                 jax.ShapeDtypeStruct((nb, D_INNER, D_STATE), F32),
                 jax.ShapeDtypeStruct(w_up.shape, BF16), jax.ShapeDtypeStruct(w_down.shape, BF16))
    out_specs = (pl.BlockSpec((pair, D_MODEL), lambda s: (jnp.maximum(s - 2, 0), 0)),
                 pl.BlockSpec((1, WINDOW, KV_WIDTH), lambda s: (seq_of_mix(s), 0, 0)),
                 pl.BlockSpec((1, WINDOW, KV_WIDTH), lambda s: (seq_of_mix(s), 0, 0)),
                 pl.BlockSpec((1, CONV_W - 1, CONV_DIM), lambda s: (seq_of_mix(s), 0, 0)),
                 pl.BlockSpec((1, D_INNER, D_STATE), lambda s: (seq_of_mix(s), 0, 0)),
                 pl.BlockSpec((D_MODEL, ff_slice), up_block), pl.BlockSpec((ff_slice, D_MODEL), down_block))
    proj_bufs = [pltpu.VMEM((pair, QKV_WIDTH), F32), pltpu.VMEM((pair, D_INNER), F32),
                 pltpu.VMEM((pair, CONV_DIM), F32), pltpu.VMEM((pair, LANES), F32)]
    mixed_bufs = [pltpu.VMEM((pair, ATTN_WIDTH), BF16), pltpu.VMEM((pair, D_INNER), BF16)]
    hnorm_bufs = [pltpu.VMEM((pair, D_MODEL), BF16)] * 2
    carry = [pltpu.VMEM((CHUNK, KV_WIDTH), F32), pltpu.VMEM((CHUNK, KV_WIDTH), F32),
             pltpu.VMEM((SUBLANES, CONV_DIM), F32), pltpu.VMEM((D_STATE, D_INNER), F32),
             pltpu.VMEM((CHUNK, LANES), F32)]
    x_rows = x.reshape(nb * seq, D_MODEL)
    return pl.pallas_call(
        functools.partial(_prompt_mixer_kernel, chunks_per_seq=chunks_per_seq, n_steps=n_steps),
        grid=(n_steps,), in_specs=in_specs, out_specs=out_specs,
        out_shape=out_shape, scratch_shapes=proj_bufs * 2 + mixed_bufs * 2 + hnorm_bufs + carry,
        name="prompt_mixer",
        compiler_params=pltpu.CompilerParams(dimension_semantics=("arbitrary",),
                                             vmem_limit_bytes=VMEM_LIMIT),
    )(sinks, x_rows, x_rows, *consts, w_up, w_down)


MLP_TILE = 1024
FF_BLOCK = 1024


def _mlp_kernel(x_ref, norm2_ref, wup_ref, wdown_ref, fnorm_ref, y_ref):
    x = x_ref[...]
    hm = _rms(x, norm2_ref[...]).astype(BF16)
    acc = x
    for j in range(D_FF // FF_BLOCK):
        h = _mm(hm, wup_ref[:, j * FF_BLOCK:(j + 1) * FF_BLOCK])
        h = jnp.square(jnp.maximum(h, 0.0))
        acc = acc + _mm(h, wdown_ref[j * FF_BLOCK:(j + 1) * FF_BLOCK, :])
    y_ref[...] = _rms(acc, fnorm_ref[...])


def _mlp(x, norm2, wup, wdown, fnorm):
    rows = x.shape[0]
    tile = min(MLP_TILE, rows)
    assert rows % tile == 0 and tile % SUBLANES == 0
    consts = (norm2, wup, wdown, fnorm)
    return pl.pallas_call(
        _mlp_kernel, grid=(rows // tile,),
        in_specs=[pl.BlockSpec((tile, D_MODEL), lambda i: (i, 0))] + [_const_spec(a.shape) for a in consts],
        out_specs=pl.BlockSpec((tile, D_MODEL), lambda i: (i, 0)),
        out_shape=jax.ShapeDtypeStruct((rows, D_MODEL), F32), name="mlp",
        compiler_params=pltpu.CompilerParams(dimension_semantics=("arbitrary",),
                                             vmem_limit_bytes=VMEM_LIMIT),
    )(x, *consts)


IN_PROJ_ROWS = 512


def _in_proj_kernel(x_ref, norm1_ref, win_ref, wdt_ref, u_ref, dt_ref, hnorm_ref):
    hb = _rms(x_ref[...], norm1_ref[...]).astype(BF16)
    hnorm_ref[...] = hb
    for p in range(win_ref.shape[0]):
        u_ref[:, _piece_cols(p)] = _mm(hb, win_ref[p])
    dt_ref[...] = _mm(hb, wdt_ref[...])


def _in_proj(x, norm1, win, wdt):
    rows = x.shape[0]
    assert rows % IN_PROJ_ROWS == 0
    outs = ((win.shape[0] * PIECE, F32), (LANES, F32), (D_MODEL, BF16))
    row_spec = lambda width: pl.BlockSpec((IN_PROJ_ROWS, width), lambda i: (i, 0))
    return pl.pallas_call(
        _in_proj_kernel, grid=(rows // IN_PROJ_ROWS,),
        in_specs=[row_spec(D_MODEL), _const_spec(norm1.shape), _const_spec(win.shape), _const_spec(wdt.shape)],
        out_specs=tuple(row_spec(width) for width, _ in outs),
        out_shape=tuple(jax.ShapeDtypeStruct((rows, width), dtype) for width, dtype in outs),
        name="sample_in_proj",
        compiler_params=pltpu.CompilerParams(dimension_semantics=("arbitrary",),
                                             vmem_limit_bytes=VMEM_LIMIT),
    )(x, norm1, win, wdt)


def _tile_roll(x, shift):
    rows, width = x.shape
    tiles = x.reshape(rows // SUBLANES, SUBLANES, width)
    return pltpu.roll(tiles, shift % SUBLANES, axis=1).reshape(rows, width)


def _spread_rows(x):
    tiles = []
    for t in range(x.shape[0] // SUBLANES):
        two_seqs = x[t * SUBLANES:(t + 1) * SUBLANES]
        tiles += [two_seqs, pltpu.roll(two_seqs, SUBLANES // 2, axis=0)]
    return jnp.concatenate(tiles, axis=0)


def _gather_rows(y):
    low = lax.broadcasted_iota(jnp.int32, (SUBLANES, 1), 0) < SUBLANES // 2
    tiles = []
    for t in range(y.shape[0] // (2 * SUBLANES)):
        a = y[2 * t * SUBLANES:(2 * t + 1) * SUBLANES]
        b = y[(2 * t + 1) * SUBLANES:(2 * t + 2) * SUBLANES]
        tiles.append(jnp.where(low, a, pltpu.roll(b, SUBLANES // 2, axis=0)))
    return jnp.concatenate(tiles, axis=0)


def _expand_heads(cols, expand):
    terms = []
    for c in cols:
        hi = c.astype(BF16)
        r1 = c - hi.astype(F32)
        mid = r1.astype(BF16)
        lo = (r1 - mid.astype(F32)).astype(BF16)
        terms.extend([hi, mid, lo])
    rows = cols[0].shape[0]
    out = jnp.dot(jnp.concatenate(terms, axis=0), expand, preferred_element_type=F32)
    return [out[(3 * i) * rows:(3 * i + 1) * rows] + out[(3 * i + 1) * rows:(3 * i + 2) * rows]
            + out[(3 * i + 2) * rows:(3 * i + 3) * rows] for i in range(len(cols))]


def _sample_mixer_kernel(sinks_ref, u_ref, dtraw_ref, cprev_ref, kc_ref, vc_ref, st_ref,
                         convw_ref, convb_ref, dtb_ref, alog_ref, dskip_ref, ssmn_ref,
                         aout_ref, mout_ref, nk_ref, nv_ref, nconv_ref, nst_ref):
    R = SAMPLE_GROUP * SAMPLE_PAD
    L = 4
    qkv_ref = u_ref.at[:, QKV_PIECE0 * PIECE:Z_PIECE0 * PIECE]
    z_ref = u_ref.at[:, Z_PIECE0 * PIECE:XBC_PIECE0 * PIECE]
    xbc_ref = u_ref.at[:, XBC_PIECE0 * PIECE:N_IN_PIECES * PIECE]
    rm = lax.broadcasted_iota(jnp.int32, (R, 1), 0) % SAMPLE_PAD

    xr = xbc_raw = _spread_rows(xbc_ref[...])
    pad_rows = SAMPLE_PAD - (CONV_W - 1)
    prev = jnp.pad(cprev_ref[...], ((0, 0), (pad_rows, 0), (0, 0))).reshape(R, CONV_DIM)
    with_prev = jnp.where(rm >= pad_rows, prev, xr)
    yc = xr * convw_ref[CONV_W - 1:CONV_W, :]
    for kshift in range(1, CONV_W):
        yc = yc + _tile_roll(with_prev, kshift) * convw_ref[CONV_W - 1 - kshift:CONV_W - kshift, :]
    for i in range(SAMPLE_GROUP):
        nconv_ref[i] = xbc_raw[i * SAMPLE_PAD + 1:i * SAMPLE_PAD + CONV_W, :]
    xbc = _silu(yc + convb_ref[...])
    xs = xbc[:, :D_INNER]
    bm = xbc[:, D_INNER:D_INNER + N_GROUPS_M * D_STATE]
    cm = xbc[:, D_INNER + N_GROUPS_M * D_STATE:]

    dt = _softplus(_spread_rows(dtraw_ref[...]) + dtb_ref[...])
    d_a = dt * (-jnp.exp(alog_ref[...]))
    a_cs = d_a
    suf = jnp.zeros_like(d_a)
    for kshift in range(1, L):
        a_cs = a_cs + jnp.where(rm >= kshift, _tile_roll(d_a, kshift), 0.0)
        suf = suf + jnp.where(rm <= L - 1 - kshift, _tile_roll(d_a, -kshift), 0.0)

    expand = (lax.broadcasted_iota(jnp.int32, (LANES, D_INNER), 1) // HEAD_DIM_M
              == lax.broadcasted_iota(jnp.int32, (LANES, D_INNER), 0)).astype(BF16)
    dt_e, acs_e, suf_e = _expand_heads([dt, a_cs, suf], expand)
    xdt = xs * dt_e
    real = rm < L
    xd_t = jnp.where(real, xdt * jnp.exp(suf_e), 0.0).T.astype(BF16)

    y = xs * dskip_ref[...]
    for kshift in range(L):
        bk = bm if kshift == 0 else _tile_roll(bm, kshift)
        cb = cm * bk
        cb_e = jnp.concatenate(
            [jnp.broadcast_to(jnp.sum(cb[:, g * D_STATE:(g + 1) * D_STATE], axis=-1, keepdims=True),
                              (R, GROUP_WIDTH)) for g in range(N_GROUPS_M)], axis=1)
        if kshift == 0:
            y = y + cb_e * xdt
        else:
            seg = jnp.exp(acs_e - _tile_roll(acs_e, kshift))
            y = y + jnp.where(rm >= kshift, cb_e * seg * _tile_roll(xdt, kshift), 0.0)

    qkv = _spread_rows(qkv_ref[...])
    q = qkv[:, :ATTN_WIDTH] * (HEAD_DIM_A ** -0.5)
    q_swapped = jnp.concatenate(
        [pltpu.roll(q[:, s * LANES:(s + 1) * LANES], HEAD_DIM_A, axis=1) for s in range(ATTN_WIDTH // LANES)], axis=1)
    kn = qkv[:, ATTN_WIDTH:ATTN_WIDTH + KV_WIDTH]
    vn = qkv[:, ATTN_WIDTH + KV_WIDTH:]
    lo8 = _lane_lo((SAMPLE_PAD, LANES))
    zero8 = jnp.zeros((SAMPLE_PAD, LANES), F32)
    seq_rows = [slice(i * SAMPLE_PAD, (i + 1) * SAMPLE_PAD) for i in range(SAMPLE_GROUP)]
    HROWS = N_HEADS_A * SAMPLE_PAD

    s_c, s_n = [], []
    for rows in seq_rows:
        pieces = []
        for kv in range(N_KV_A):
            for g in range(Q_PER_KV):
                slab = kv * (Q_PER_KV // 2) + g // 2
                src = q if g % 2 == kv else q_swapped
                piece = src[rows, slab * LANES:(slab + 1) * LANES]
                pieces.append(jnp.where(lo8, piece, zero8) if kv == 0 else jnp.where(lo8, zero8, piece))
        lhs = jnp.concatenate(pieces, axis=0).astype(BF16)
        i = len(s_c)
        s_c.append(_mm(lhs, kc_ref[i]))
        s_n.append(_mm_nt(lhs, kn[rows]))
    s_c = jnp.concatenate(s_c, axis=0)
    s_n = jnp.concatenate(s_n, axis=0)
    n_rows = SAMPLE_GROUP * HROWS
    tok_c = lax.broadcasted_iota(jnp.int32, (n_rows, WINDOW), 0) % SAMPLE_PAD
    s_c = jnp.where(lax.broadcasted_iota(jnp.int32, (n_rows, WINDOW), 1) >= tok_c, s_c, NEG_BIG)
    tok_n = lax.broadcasted_iota(jnp.int32, (n_rows, SAMPLE_PAD), 0) % SAMPLE_PAD
    col_n = lax.broadcasted_iota(jnp.int32, (n_rows, SAMPLE_PAD), 1)
    s_n = jnp.where((col_n <= tok_n) & (col_n < L), s_n, NEG_BIG)
    sk = jnp.concatenate([jnp.full((SAMPLE_PAD, 1), sinks_ref[h], F32) for h in range(N_HEADS_A)] * SAMPLE_GROUP,
                         axis=0)
    m = jnp.maximum(jnp.maximum(jnp.max(s_c, axis=-1, keepdims=True), jnp.max(s_n, axis=-1, keepdims=True)), sk)
    e_c = jnp.exp(s_c - m)
    e_n = jnp.exp(s_n - m)
    inv_den = 1.0 / (jnp.sum(e_c, axis=-1, keepdims=True) + jnp.sum(e_n, axis=-1, keepdims=True) + jnp.exp(sk - m))
    e_c = e_c.astype(BF16)
    e_n = e_n.astype(BF16)
    o = jnp.concatenate([_mm_nt(e_c[i * HROWS:(i + 1) * HROWS], vc_ref[i]) + _mm(e_n[i * HROWS:(i + 1) * HROWS], vn[rows])
                         for i, rows in enumerate(seq_rows)], axis=0) * inv_den
    o_swapped = pltpu.roll(o, HEAD_DIM_A, axis=1)
    a_rows = []
    for i in range(SAMPLE_GROUP):
        blk = lambda arr, kv, g: arr[i * HROWS + (kv * Q_PER_KV + g) * SAMPLE_PAD:
                                     i * HROWS + (kv * Q_PER_KV + g + 1) * SAMPLE_PAD]
        slabs = []
        for kv in range(N_KV_A):
            for j in range(Q_PER_KV // 2):
                first = blk(o if kv == 0 else o_swapped, kv, 2 * j)
                second = blk(o_swapped if kv == 0 else o, kv, 2 * j + 1)
                slabs.append(jnp.where(lo8, first, second))
        a_rows.append(jnp.concatenate(slabs, axis=1))
    aout_ref[...] = _gather_rows(jnp.concatenate(a_rows, axis=0))

    key = lax.broadcasted_iota(jnp.int32, (KV_WIDTH, WINDOW), 1)
    for cache_ref, new, out_ref in ((kc_ref, kn, nk_ref), (vc_ref, vn, nv_ref)):
        new_t = new.T
        for i in range(SAMPLE_GROUP):
            kept = pltpu.roll(cache_ref[i], WINDOW - L, axis=1)
            fresh = pltpu.roll(new_t, (WINDOW - L - i * SAMPLE_PAD) % WINDOW, axis=1)
            out_ref[i] = jnp.where(key >= WINDOW - L, fresh, kept)

    row_r = lax.broadcasted_iota(jnp.int32, (R, D_STATE), 0)
    heads_per_group = N_HEADS_M // N_GROUPS_M
    yoff = []
    for i, rows in enumerate(seq_rows):
        state = st_ref[i]
        state_b = state.astype(BF16)
        a_tot = a_cs[i * SAMPLE_PAD + L - 1:i * SAMPLE_PAD + L, :]
        in_seq = (row_r >= i * SAMPLE_PAD) & (row_r < (i + 1) * SAMPLE_PAD)
        yo = []
        for g in range(N_GROUPS_M):
            gs = slice(g * GROUP_WIDTH, (g + 1) * GROUP_WIDTH)
            yo.append(_mm_nt(cm[rows, g * D_STATE:(g + 1) * D_STATE], state_b[gs]))
            bsel = jnp.where(in_seq, bm[:, g * D_STATE:(g + 1) * D_STATE], 0.0)
            upd = _mm(xd_t[gs, :], bsel)
            for hh in range(heads_per_group):
                h = g * heads_per_group + hh
                hs = slice(h * HEAD_DIM_M, (h + 1) * HEAD_DIM_M)
                decay = jnp.exp(jnp.broadcast_to(a_tot[:, h:h + 1], (HEAD_DIM_M, D_STATE)))
                nst_ref[i, hs, :] = state[hs] * decay + upd[hh * HEAD_DIM_M:(hh + 1) * HEAD_DIM_M]
        yoff.append(jnp.concatenate(yo, axis=1))
    yoff = jnp.concatenate(yoff, axis=0)

    y = (y + yoff * jnp.exp(acs_e)) * _silu(_spread_rows(z_ref[...]))
    outs = []
    for g in range(N_GROUPS_M):
        yg = y[:, g * GROUP_WIDTH:(g + 1) * GROUP_WIDTH]
        outs.append(yg * lax.rsqrt(jnp.mean(yg * yg, axis=-1, keepdims=True) + EPS))
    mout_ref[...] = _gather_rows(jnp.concatenate(outs, axis=1) * ssmn_ref[...])


def _sample_mixer(sinks, u, dtraw, cprev, kc, vc, st, convw, convb, dtb, alog, dskip, ssmn):
    nseq = kc.shape[0]
    assert nseq % SAMPLE_GROUP == 0
    T4 = SAMPLE_GROUP * 4
    consts = (convw, convb, dtb, alog, dskip, ssmn)
    tok_spec = lambda w: pl.BlockSpec((T4, w), lambda i: (i, 0))
    seq_spec = lambda a, b: pl.BlockSpec((SAMPLE_GROUP, a, b), lambda i: (i, 0, 0))
    in_specs = ([pl.BlockSpec(memory_space=pltpu.SMEM),
                 tok_spec(N_IN_PIECES * PIECE), tok_spec(LANES), seq_spec(CONV_W - 1, CONV_DIM),
                 seq_spec(WINDOW, KV_WIDTH), seq_spec(WINDOW, KV_WIDTH), seq_spec(D_INNER, D_STATE)]
                + [_const_spec(a.shape) for a in consts])
    out_shape = (jax.ShapeDtypeStruct((nseq * 4, ATTN_WIDTH), F32), jax.ShapeDtypeStruct((nseq * 4, D_INNER), F32),
                 jax.ShapeDtypeStruct((nseq, WINDOW, KV_WIDTH), F32),
                 jax.ShapeDtypeStruct((nseq, WINDOW, KV_WIDTH), F32),
                 jax.ShapeDtypeStruct((nseq, CONV_W - 1, CONV_DIM), F32),
                 jax.ShapeDtypeStruct((nseq, D_INNER, D_STATE), F32))
    out_specs = (tok_spec(ATTN_WIDTH), tok_spec(D_INNER), seq_spec(WINDOW, KV_WIDTH), seq_spec(WINDOW, KV_WIDTH),
                 seq_spec(CONV_W - 1, CONV_DIM), seq_spec(D_INNER, D_STATE))
    return pl.pallas_call(
        _sample_mixer_kernel, grid=(nseq // SAMPLE_GROUP,), in_specs=in_specs, out_specs=out_specs,
        out_shape=out_shape, name="sample_mixer",
        compiler_params=pltpu.CompilerParams(dimension_semantics=("arbitrary",),
                                             vmem_limit_bytes=VMEM_LIMIT),
    )(sinks, u, dtraw, cprev, kc, vc, st, *consts)


N_OUT_PIECES = D_MODEL // PIECE
N_FF_BLOCKS = D_FF // FF_BLOCK


def _sample_post_kernel(x_ref, a_ref, m_ref, hnorm_ref, wga_ref, wgb_ref, woa_ref, wob_ref, wo_ref,
                        norm2_ref, wup_ref, wdown_ref, fnorm_ref, y_ref, merged_s, x1_s, hm_s, acc_s):
    s = pl.program_id(0)

    @pl.when(s < N_OUT_PIECES)
    def _():
        hb = hnorm_ref[...]
        piece = (_sigmoid(_mm(hb, wga_ref[0])) * _mm(a_ref[...], woa_ref[0])
                 + _sigmoid(_mm(hb, wgb_ref[0])) * _mm(m_ref[...], wob_ref[0]))
        for p in range(N_OUT_PIECES):
            @pl.when(s == p)
            def _():
                merged_s[:, _piece_cols(p)] = piece.astype(BF16)

    @pl.when((s >= N_OUT_PIECES) & (s < 2 * N_OUT_PIECES))
    def _():
        piece = _mm(merged_s[...], wo_ref[0])
        for p in range(N_OUT_PIECES):
            @pl.when(s == N_OUT_PIECES + p)
            def _():
                x1_s[:, _piece_cols(p)] = x_ref[:, _piece_cols(p)] + piece

    @pl.when(s == 2 * N_OUT_PIECES)
    def _():
        x1 = x1_s[...]
        hm_s[...] = _rms(x1, norm2_ref[...]).astype(BF16)
        acc_s[...] = x1

    @pl.when(s >= 2 * N_OUT_PIECES)
    def _():
        h = jnp.square(jnp.maximum(_mm(hm_s[...], wup_ref[...]), 0.0))
        acc_s[...] += _mm(h, wdown_ref[...])

    @pl.when(s == 2 * N_OUT_PIECES + N_FF_BLOCKS - 1)
    def _():
        y_ref[...] = _rms(acc_s[...], fnorm_ref[...])


def _sample_post(x, a_out, m_out, hnorm, wgate, woa, wob, wo, norm2, wup, wdown, fnorm):
    rows = x.shape[0]
    first = lambda s: jnp.minimum(s, N_OUT_PIECES - 1)
    second = lambda s: jnp.clip(s - N_OUT_PIECES, 0, N_OUT_PIECES - 1)
    ff = lambda s: jnp.clip(s - 2 * N_OUT_PIECES, 0, N_FF_BLOCKS - 1)
    piece_spec = lambda w, index: pl.BlockSpec((1,) + w.shape[1:], lambda s: (index(s), 0, 0))
    in_specs = [_const_spec(x.shape), _const_spec(a_out.shape), _const_spec(m_out.shape), _const_spec(hnorm.shape),
                piece_spec(wgate, first), piece_spec(wgate, lambda s: N_OUT_PIECES + first(s)),
                piece_spec(woa, first), piece_spec(wob, first), piece_spec(wo, second),
                _const_spec(norm2.shape),
                pl.BlockSpec((D_MODEL, FF_BLOCK), lambda s: (0, ff(s))),
                pl.BlockSpec((FF_BLOCK, D_MODEL), lambda s: (ff(s), 0)),
                _const_spec(fnorm.shape)]
    return pl.pallas_call(
        _sample_post_kernel, grid=(2 * N_OUT_PIECES + N_FF_BLOCKS,), in_specs=in_specs,
        out_specs=pl.BlockSpec((rows, D_MODEL), lambda s: (0, 0)),
        out_shape=jax.ShapeDtypeStruct((rows, D_MODEL), F32),
        scratch_shapes=[pltpu.VMEM((rows, D_MODEL), BF16), pltpu.VMEM((rows, D_MODEL), F32),
                        pltpu.VMEM((rows, D_MODEL), BF16), pltpu.VMEM((rows, D_MODEL), F32)],
        name="sample_post",
        compiler_params=pltpu.CompilerParams(dimension_semantics=("arbitrary",),
                                             vmem_limit_bytes=VMEM_LIMIT),
    )(x, a_out, m_out, hnorm, wgate, wgate, woa, wob, wo, norm2, wup, wdown, fnorm)


def _layer_params(norm1, w_in, sinks, conv_w, conv_b, dt_bias, a_log, d_skip, ssm_norm, w_oa, w_ob, w_o,
                  norm2):
    dt0 = QKV_WIDTH + D_INNER + CONV_DIM
    assert dt0 == N_IN_PIECES * PIECE
    pad_lanes = lambda a: jnp.pad(a, ((0, 0), (0, LANES - a.shape[1])))
    w_in_t = w_in.T
    woa, wob, wo = _prep_pieces(w_oa, w_ob, w_o)
    win, wdt = _prep_pieces_t(w_in_t, N_IN_PIECES, dt0)
    mixer = MixerWeights(
        norm1=norm1[None, :], win=win,
        wgate=_prep_pieces_t_shifted(w_in_t, dt0 + N_HEADS_M, 2 * D_MODEL // PIECE),
        wdt=wdt,
        convw=conv_w, convb=conv_b[None, :], dtb=pad_lanes(dt_bias[None, :]), alog=pad_lanes(a_log[None, :]),
        dskip=jnp.repeat(d_skip, HEAD_DIM_M)[None, :], ssmn=ssm_norm[None, :],
        woa=woa, wob=wob, wo=wo)
    return dict(sinks=sinks.astype(F32), mixer=mixer, norm2=norm2[None, :])


def kernel(x_prompt, x_sample, cache_swa_k, cache_swa_v, state_conv, state_ssm, norm1, w_in, sinks, conv_w,
           conv_b, dt_bias, a_log, d_skip, ssm_norm, w_oa, w_ob, w_o, norm2, w_up, w_down, final_norm):
    depth = w_in.shape[0]
    assert depth == 1
    nb, seq, _ = x_prompt.shape
    ns, ls, _ = x_sample.shape
    assert ls == 4
    p = _layer_params(norm1[0], w_in[0], sinks[0], conv_w[0], conv_b[0], dt_bias[0], a_log[0], d_skip[0],
                      ssm_norm[0], w_oa[0], w_ob[0], w_o[0], norm2[0])
    fnorm = final_norm[None, :]
    mw = p["mixer"]
    mixer_consts = (mw.convw, mw.convb, mw.dtb, mw.alog, mw.dskip, mw.ssmn)

    x1p, pk, pv, pc, pst, wup, wdown = _prompt_mixer(x_prompt, p["sinks"], mw, w_up[0], w_down[0])
    y_prompt = _mlp(x1p, p["norm2"], wup, wdown, fnorm)
    y_prompt = y_prompt.reshape(nb, seq, D_MODEL)

    xs_rows = x_sample.reshape(ns * ls, D_MODEL)
    dim_major = lambda a: jnp.swapaxes(a, 1, 2)
    u, dtraw, hnorm = _in_proj(xs_rows, mw.norm1, mw.win, mw.wdt)
    a_out, m_out, sk, sv, sc, sst = _sample_mixer(
        p["sinks"], u, dtraw, state_conv[0],
        dim_major(cache_swa_k[0].reshape(ns, WINDOW, KV_WIDTH)), dim_major(cache_swa_v[0].reshape(ns, WINDOW, KV_WIDTH)),
        state_ssm[0].reshape(ns, D_INNER, D_STATE), *mixer_consts)
    sk, sv = dim_major(sk), dim_major(sv)
    y_sample = _sample_post(xs_rows, a_out, m_out, hnorm, mw.wgate, mw.woa, mw.wob, mw.wo,
                            p["norm2"], wup, wdown, fnorm).reshape(ns, ls, D_MODEL)

    kv_shape = (1, -1, WINDOW, N_KV_A, HEAD_DIM_A)
    ssm_shape = (1, -1, N_HEADS_M, HEAD_DIM_M, D_STATE)
    return (y_prompt, y_sample,
            pk.reshape(kv_shape), pv.reshape(kv_shape), pc[None], pst.reshape(ssm_shape),
            sk.reshape(kv_shape), sv.reshape(kv_shape), sc[None], sst.reshape(ssm_shape))
```
